```python
import jax, jax.numpy as jnp
from jax import lax
import numpy as np

D_MODEL = 1024
BATCH = 8
SEQ = 8192
DEPTH = 2

N_MEM = 256
HEAD_DIM = 64
N_Q_HEADS = 8
N_KV_HEADS = 2
Q_PER_KV = N_Q_HEADS // N_KV_HEADS
ATTN_WIDTH = N_Q_HEADS * HEAD_DIM
KV_WIDTH = N_KV_HEADS * HEAD_DIM
WINDOW = 128
BLOCK = 128
ROPE_THETA = 10000.0
POOL_WINDOWS = (2, 4, 8, 16)
N_POOL_GROUPS = len(POOL_WINDOWS)
POOL_WIDTH = D_MODEL - ATTN_WIDTH
POOL_GROUP_DIM = POOL_WIDTH // N_POOL_GROUPS
MIX_WIDTH = ATTN_WIDTH + POOL_WIDTH
IN_WIDTH = ATTN_WIDTH + 2 * KV_WIDTH + POOL_WIDTH
X_HEADS = 4
X_HEAD_DIM = D_MODEL // X_HEADS
X_WIDTH = X_HEADS * X_HEAD_DIM
D_FF = 2816
FFN_RES = 0.5
EPS = 1e-6
MAX_POS_OFFSET = 1024
NEG = -1e30

kernel_name = "hymba_swa_sink_pool_macaron_xattn"


def rms_norm(x, g):
    xf = x.astype(jnp.float32)
    y = xf * lax.rsqrt(jnp.mean(xf * xf, axis=-1, keepdims=True) + EPS)
    return (y * g.astype(jnp.float32)).astype(x.dtype)


def swiglu(h, w_gate, w_up, w_down):
    return (jax.nn.silu(h @ w_gate) * (h @ w_up)) @ w_down


def rope_tables(positions):
    inv_freq = ROPE_THETA ** (-jnp.arange(0, HEAD_DIM, 2, dtype=jnp.float32) / HEAD_DIM)
    ang = positions.astype(jnp.float32)[..., None] * inv_freq
    return jnp.cos(ang)[:, :, None, :], jnp.sin(ang)[:, :, None, :]


def apply_rope(t, cos, sin):
    tf = t.astype(jnp.float32)
    t1, t2 = tf[..., : HEAD_DIM // 2], tf[..., HEAD_DIM // 2:]
    return jnp.concatenate([t1 * cos - t2 * sin, t2 * cos + t1 * sin], axis=-1).astype(t.dtype)


def sliding_window_sink_attention(q, k, v, sinks):
    B, S = q.shape[0], q.shape[1]
    nb = S // BLOCK
    qb = q.reshape(B, nb, BLOCK, N_KV_HEADS, Q_PER_KV, HEAD_DIM)

    def with_prev(t):
        tb = t.reshape(B, nb, BLOCK, N_KV_HEADS, HEAD_DIM)
        prev = jnp.pad(tb[:, :-1], ((0, 0), (1, 0), (0, 0), (0, 0), (0, 0)))
        return jnp.concatenate([prev, tb], axis=2)

    kk, vv = with_prev(k), with_prev(v)
    scale = HEAD_DIM ** -0.5
    s = jnp.einsum('bnqkgd,bnjkd->bkgnqj', qb, kk).astype(jnp.float32) * scale
    qi = jnp.arange(BLOCK)[:, None]
    kj = jnp.arange(2 * BLOCK)[None, :]
    diff = qi + BLOCK - kj
    blk = jnp.arange(nb)[:, None, None]
    key_abs = (blk - 1) * BLOCK + kj[None]
    mask = (diff >= 0)[None] & (diff < WINDOW)[None] & (key_abs >= 0)
    s = jnp.where(mask, s, NEG)
    sink = sinks.astype(jnp.float32).reshape(1, N_KV_HEADS, Q_PER_KV, 1, 1, 1)
    sink = jnp.broadcast_to(sink, s.shape[:-1] + (1,))
    p = jax.nn.softmax(jnp.concatenate([s, sink], axis=-1), axis=-1)[..., :-1]
    o = jnp.einsum('bkgnqj,bnjkd->bnqkgd', p.astype(v.dtype), vv)
    return o.reshape(B, S, ATTN_WIDTH)


def multiscale_pool(u, pool_w, pool_scale):
    B, S = u.shape[0], u.shape[1]
    ug = u.reshape(B, S, N_POOL_GROUPS, POOL_GROUP_DIM)
    uf = ug.astype(jnp.float32)
    c = jnp.pad(jnp.cumsum(uf, axis=1), ((0, 0), (1, 0), (0, 0), (0, 0)))
    t = jnp.arange(S)[:, None]
    w = jnp.array(POOL_WINDOWS, dtype=jnp.int32)[None, :]
    start = jnp.maximum(t + 1 - w, 0)
    g = jnp.arange(N_POOL_GROUPS)[None, :]
    window_sum = c[:, 1:] - c[:, start, g]
    count = jnp.minimum(t + 1, w).astype(jnp.float32)[None, :, :, None]
    pooled = (window_sum / count - uf).astype(u.dtype)
    mixed = jnp.einsum('bsgc,gcd->bsgd', pooled, pool_w)
    mixed = mixed * pool_scale.reshape(N_POOL_GROUPS, POOL_GROUP_DIM)
    return mixed.reshape(B, S, POOL_WIDTH)


def memory_cross_attention(h, mem_n, wq, wkv, wo):
    B, S = h.shape[0], h.shape[1]
    q = (h @ wq).reshape(B, S, X_HEADS, X_HEAD_DIM)
    kv = (mem_n @ wkv).reshape(B, mem_n.shape[1], 2, X_HEADS, X_HEAD_DIM)
    k, v = kv[:, :, 0], kv[:, :, 1]
    s = jnp.einsum('bshd,bmhd->bhsm', q, k).astype(jnp.float32) * (X_HEAD_DIM ** -0.5)
    p = jax.nn.softmax(s, axis=-1)
    o = jnp.einsum('bhsm,bmhd->bshd', p.astype(v.dtype), v).reshape(B, S, X_WIDTH)
    return o @ wo


def _fwd_setup_inputs(seed: int = 0) -> dict:
    key = jax.random.key(seed)
    ks = jax.random.split(key, 32)
    f32 = jnp.float32
    L, D = DEPTH, D_MODEL

    def w(k, shape, fan_in):
        return jax.random.normal(k, shape, f32) * (fan_in ** -0.5)

    def gain(k, shape):
        return 1.0 + 0.05 * jax.random.normal(k, shape, f32)

    offset = jax.random.randint(ks[2], (BATCH, 1), 0, MAX_POS_OFFSET, dtype=jnp.int32)
    positions = offset + jnp.arange(SEQ, dtype=jnp.int32)[None, :]
    return {
        "x": jax.random.normal(ks[0], (BATCH, SEQ, D), f32),
        "mem": jax.random.normal(ks[1], (BATCH, N_MEM, D), f32),
        "positions": positions,
        "ffn1_norm": gain(ks[3], (L, D)),
        "ffn1_w_gate": w(ks[4], (L, D, D_FF), D),
        "ffn1_w_up": w(ks[5], (L, D, D_FF), D),
        "ffn1_w_down": w(ks[6], (L, D_FF, D), D_FF),
        "mix_norm": gain(ks[7], (L, D)),
        "w_in": w(ks[8], (L, D, IN_WIDTH), D),
        "attn_sinks": 0.5 * jax.random.normal(ks[9], (L, N_Q_HEADS), f32),
        "pool_w": w(ks[10], (L, N_POOL_GROUPS, POOL_GROUP_DIM, POOL_GROUP_DIM), POOL_GROUP_DIM),
        "pool_scale": gain(ks[11], (L, POOL_WIDTH)),
        "attn_out_norm": gain(ks[12], (L, ATTN_WIDTH)),
        "pool_out_norm": gain(ks[13], (L, POOL_WIDTH)),
        "w_out": w(ks[14], (L, MIX_WIDTH, D), MIX_WIDTH),
        "xattn_norm": gain(ks[15], (L, D)),
        "mem_norm": gain(ks[16], (L, D)),
        "xattn_wq": w(ks[17], (L, D, X_WIDTH), D),
        "xattn_wkv": w(ks[18], (L, D, 2 * X_WIDTH), D),
        "xattn_wo": w(ks[19], (L, X_WIDTH, D), X_WIDTH),
        "ffn2_norm": gain(ks[20], (L, D)),
        "ffn2_w_gate": w(ks[21], (L, D, D_FF), D),
        "ffn2_w_up": w(ks[22], (L, D, D_FF), D),
        "ffn2_w_down": w(ks[23], (L, D_FF, D), D_FF),
        "final_norm": gain(ks[24], (D,)),
    }


def _fwd_reference(x, mem, positions, ffn1_norm, ffn1_w_gate, ffn1_w_up, ffn1_w_down,
              mix_norm, w_in, attn_sinks, pool_w, pool_scale, attn_out_norm, pool_out_norm,
              w_out, xattn_norm, mem_norm, xattn_wq, xattn_wkv, xattn_wo,
              ffn2_norm, ffn2_w_gate, ffn2_w_up, ffn2_w_down, final_norm):
    B, S = x.shape[0], x.shape[1]
    cos, sin = rope_tables(positions)
    for l in range(DEPTH):
        x = x + FFN_RES * swiglu(rms_norm(x, ffn1_norm[l]), ffn1_w_gate[l], ffn1_w_up[l], ffn1_w_down[l])

        h = rms_norm(x, mix_norm[l])
        proj = h @ w_in[l]
        q = proj[..., :ATTN_WIDTH].reshape(B, S, N_Q_HEADS, HEAD_DIM)
        k = proj[..., ATTN_WIDTH:ATTN_WIDTH + KV_WIDTH].reshape(B, S, N_KV_HEADS, HEAD_DIM)
        v = proj[..., ATTN_WIDTH + KV_WIDTH:ATTN_WIDTH + 2 * KV_WIDTH].reshape(B, S, N_KV_HEADS, HEAD_DIM)
        u = proj[..., ATTN_WIDTH + 2 * KV_WIDTH:]
        q = apply_rope(q, cos, sin)
        k = apply_rope(k, cos, sin)
        out_a = sliding_window_sink_attention(q, k, v, attn_sinks[l])
        out_b = multiscale_pool(u, pool_w[l], pool_scale[l])
        merged = jnp.concatenate([rms_norm(out_a, attn_out_norm[l]),
                                  rms_norm(out_b, pool_out_norm[l])], axis=-1)
        x = x + merged @ w_out[l]

        x = x + memory_cross_attention(rms_norm(x, xattn_norm[l]), rms_norm(mem, mem_norm[l]),
                                       xattn_wq[l], xattn_wkv[l], xattn_wo[l])

        x = x + FFN_RES * swiglu(rms_norm(x, ffn2_norm[l]), ffn2_w_gate[l], ffn2_w_up[l], ffn2_w_down[l])
    return rms_norm(x, final_norm)


import jax as _jax
import jax.numpy as _jnp

TWIN_FORMAT = 'train_step'
FWD_PARAMS = ['x', 'mem', 'positions', 'ffn1_norm', 'ffn1_w_gate', 'ffn1_w_up', 'ffn1_w_down', 'mix_norm', 'w_in', 'attn_sinks', 'pool_w', 'pool_scale', 'attn_out_norm', 'pool_out_norm', 'w_out', 'xattn_norm', 'mem_norm', 'xattn_wq', 'xattn_wkv', 'xattn_wo', 'ffn2_norm', 'ffn2_w_gate', 'ffn2_w_up', 'ffn2_w_down', 'final_norm']
TWIN_WEIGHTS = ['ffn1_norm', 'ffn1_w_gate', 'ffn1_w_up', 'ffn1_w_down', 'mix_norm', 'w_in', 'attn_sinks', 'pool_w', 'pool_scale', 'attn_out_norm', 'pool_out_norm', 'w_out', 'xattn_norm', 'mem_norm', 'xattn_wq', 'xattn_wkv', 'xattn_wo', 'ffn2_norm', 'ffn2_w_gate', 'ffn2_w_up', 'ffn2_w_down', 'final_norm']
TWIN_DIFF_INPUT = 'x'
TWIN_INPUTS = ['x', 'mem', 'positions', 'ffn1_norm', 'ffn1_w_gate', 'ffn1_w_up', 'ffn1_w_down', 'mix_norm', 'w_in', 'attn_sinks', 'pool_w', 'pool_scale', 'attn_out_norm', 'pool_out_norm', 'w_out', 'xattn_norm', 'mem_norm', 'xattn_wq', 'xattn_wkv', 'xattn_wo', 'ffn2_norm', 'ffn2_w_gate', 'ffn2_w_up', 'ffn2_w_down', 'final_norm', 'loss_target', 'm_ffn1_norm', 'm_ffn1_w_gate', 'm_ffn1_w_up', 'm_ffn1_w_down', 'm_mix_norm', 'm_w_in', 'm_attn_sinks', 'm_pool_w', 'm_pool_scale', 'm_attn_out_norm', 'm_pool_out_norm', 'm_w_out', 'm_xattn_norm', 'm_mem_norm', 'm_xattn_wq', 'm_xattn_wkv', 'm_xattn_wo', 'm_ffn2_norm', 'm_ffn2_w_gate', 'm_ffn2_w_up', 'm_ffn2_w_down', 'm_final_norm', 'v_ffn1_norm', 'v_ffn1_w_gate', 'v_ffn1_w_up', 'v_ffn1_w_down', 'v_mix_norm', 'v_w_in', 'v_attn_sinks', 'v_pool_w', 'v_pool_scale', 'v_attn_out_norm', 'v_pool_out_norm', 'v_w_out', 'v_xattn_norm', 'v_mem_norm', 'v_xattn_wq', 'v_xattn_wkv', 'v_xattn_wo', 'v_ffn2_norm', 'v_ffn2_w_gate', 'v_ffn2_w_up', 'v_ffn2_w_down', 'v_final_norm']
TWIN_OUTPUTS = ['loss', 'grad_x', 'grad_ffn1_norm', 'grad_ffn1_w_gate', 'grad_ffn1_w_up', 'grad_ffn1_w_down', 'grad_mix_norm', 'grad_w_in', 'grad_attn_sinks', 'grad_pool_w', 'grad_pool_scale', 'grad_attn_out_norm', 'grad_pool_out_norm', 'grad_w_out', 'grad_xattn_norm', 'grad_mem_norm', 'grad_xattn_wq', 'grad_xattn_wkv', 'grad_xattn_wo', 'grad_ffn2_norm', 'grad_ffn2_w_gate', 'grad_ffn2_w_up', 'grad_ffn2_w_down', 'grad_final_norm', 'delta_ffn1_norm', 'delta_ffn1_w_gate', 'delta_ffn1_w_up', 'delta_ffn1_w_down', 'delta_mix_norm', 'delta_w_in', 'delta_attn_sinks', 'delta_pool_w', 'delta_pool_scale', 'delta_attn_out_norm', 'delta_pool_out_norm', 'delta_w_out', 'delta_xattn_norm', 'delta_mem_norm', 'delta_xattn_wq', 'delta_xattn_wkv', 'delta_xattn_wo', 'delta_ffn2_norm', 'delta_ffn2_w_gate', 'delta_ffn2_w_up', 'delta_ffn2_w_down', 'delta_final_norm', 'new_m_ffn1_norm', 'new_m_ffn1_w_gate', 'new_m_ffn1_w_up', 'new_m_ffn1_w_down', 'new_m_mix_norm', 'new_m_w_in', 'new_m_attn_sinks', 'new_m_pool_w', 'new_m_pool_scale', 'new_m_attn_out_norm', 'new_m_pool_out_norm', 'new_m_w_out', 'new_m_xattn_norm', 'new_m_mem_norm', 'new_m_xattn_wq', 'new_m_xattn_wkv', 'new_m_xattn_wo', 'new_m_ffn2_norm', 'new_m_ffn2_w_gate', 'new_m_ffn2_w_up', 'new_m_ffn2_w_down', 'new_m_final_norm', 'new_v_ffn1_norm', 'new_v_ffn1_w_gate', 'new_v_ffn1_w_up', 'new_v_ffn1_w_down', 'new_v_mix_norm', 'new_v_w_in', 'new_v_attn_sinks', 'new_v_pool_w', 'new_v_pool_scale', 'new_v_attn_out_norm', 'new_v_pool_out_norm', 'new_v_w_out', 'new_v_xattn_norm', 'new_v_mem_norm', 'new_v_xattn_wq', 'new_v_xattn_wkv', 'new_v_xattn_wo', 'new_v_ffn2_norm', 'new_v_ffn2_w_gate', 'new_v_ffn2_w_up', 'new_v_ffn2_w_down', 'new_v_final_norm']
TWIN_LEAF_KINDS = {'loss': 'loss', 'grad_x': 'grad_x', 'grad_ffn1_norm': 'grad_w', 'grad_ffn1_w_gate': 'grad_w', 'grad_ffn1_w_up': 'grad_w', 'grad_ffn1_w_down': 'grad_w', 'grad_mix_norm': 'grad_w', 'grad_w_in': 'grad_w', 'grad_attn_sinks': 'grad_w', 'grad_pool_w': 'grad_w', 'grad_pool_scale': 'grad_w', 'grad_attn_out_norm': 'grad_w', 'grad_pool_out_norm': 'grad_w', 'grad_w_out': 'grad_w', 'grad_xattn_norm': 'grad_w', 'grad_mem_norm': 'grad_w', 'grad_xattn_wq': 'grad_w', 'grad_xattn_wkv': 'grad_w', 'grad_xattn_wo': 'grad_w', 'grad_ffn2_norm': 'grad_w', 'grad_ffn2_w_gate': 'grad_w', 'grad_ffn2_w_up': 'grad_w', 'grad_ffn2_w_down': 'grad_w', 'grad_final_norm': 'grad_w', 'delta_ffn1_norm': 'delta_w', 'delta_ffn1_w_gate': 'delta_w', 'delta_ffn1_w_up': 'delta_w', 'delta_ffn1_w_down': 'delta_w', 'delta_mix_norm': 'delta_w', 'delta_w_in': 'delta_w', 'delta_attn_sinks': 'delta_w', 'delta_pool_w': 'delta_w', 'delta_pool_scale': 'delta_w', 'delta_attn_out_norm': 'delta_w', 'delta_pool_out_norm': 'delta_w', 'delta_w_out': 'delta_w', 'delta_xattn_norm': 'delta_w', 'delta_mem_norm': 'delta_w', 'delta_xattn_wq': 'delta_w', 'delta_xattn_wkv': 'delta_w', 'delta_xattn_wo': 'delta_w', 'delta_ffn2_norm': 'delta_w', 'delta_ffn2_w_gate': 'delta_w', 'delta_ffn2_w_up': 'delta_w', 'delta_ffn2_w_down': 'delta_w', 'delta_final_norm': 'delta_w', 'new_m_ffn1_norm': 'new_m', 'new_m_ffn1_w_gate': 'new_m', 'new_m_ffn1_w_up': 'new_m', 'new_m_ffn1_w_down': 'new_m', 'new_m_mix_norm': 'new_m', 'new_m_w_in': 'new_m', 'new_m_attn_sinks': 'new_m', 'new_m_pool_w': 'new_m', 'new_m_pool_scale': 'new_m', 'new_m_attn_out_norm': 'new_m', 'new_m_pool_out_norm': 'new_m', 'new_m_w_out': 'new_m', 'new_m_xattn_norm': 'new_m', 'new_m_mem_norm': 'new_m', 'new_m_xattn_wq': 'new_m', 'new_m_xattn_wkv': 'new_m', 'new_m_xattn_wo': 'new_m', 'new_m_ffn2_norm': 'new_m', 'new_m_ffn2_w_gate': 'new_m', 'new_m_ffn2_w_up': 'new_m', 'new_m_ffn2_w_down': 'new_m', 'new_m_final_norm': 'new_m', 'new_v_ffn1_norm': 'new_v', 'new_v_ffn1_w_gate': 'new_v', 'new_v_ffn1_w_up': 'new_v', 'new_v_ffn1_w_down': 'new_v', 'new_v_mix_norm': 'new_v', 'new_v_w_in': 'new_v', 'new_v_attn_sinks': 'new_v', 'new_v_pool_w': 'new_v', 'new_v_pool_scale': 'new_v', 'new_v_attn_out_norm': 'new_v', 'new_v_pool_out_norm': 'new_v', 'new_v_w_out': 'new_v', 'new_v_xattn_norm': 'new_v', 'new_v_mem_norm': 'new_v', 'new_v_xattn_wq': 'new_v', 'new_v_xattn_wkv': 'new_v', 'new_v_xattn_wo': 'new_v', 'new_v_ffn2_norm': 'new_v', 'new_v_ffn2_w_gate': 'new_v', 'new_v_ffn2_w_up': 'new_v', 'new_v_ffn2_w_down': 'new_v', 'new_v_final_norm': 'new_v'}


def _forward(args):
    return _fwd_reference(*[args[k] for k in FWD_PARAMS])


def _output_shape():
    def fwd():
        inp = _fwd_setup_inputs(0)
        return _fwd_reference(*[inp[k] for k in FWD_PARAMS])
    out = _jax.eval_shape(fwd)
    return out.shape, out.dtype

N_MICROBATCH = 1
ADAM_LR = 0.001
ADAM_B1 = 0.9
ADAM_B2 = 0.999
ADAM_EPS = 1e-08
ADAM_WD = 0.01
ADAM_STEP = 10
PER_EXAMPLE_BATCH_AXIS = {'x': 0, 'mem': 0, 'positions': 0, 'loss_target': 0}
SHARED_INPUTS = []
_WEIGHT_DTYPES = {'ffn1_norm': _jnp.float32, 'ffn1_w_gate': _jnp.float32, 'ffn1_w_up': _jnp.float32, 'ffn1_w_down': _jnp.float32, 'mix_norm': _jnp.float32, 'w_in': _jnp.float32, 'attn_sinks': _jnp.float32, 'pool_w': _jnp.float32, 'pool_scale': _jnp.float32, 'attn_out_norm': _jnp.float32, 'pool_out_norm': _jnp.float32, 'w_out': _jnp.float32, 'xattn_norm': _jnp.float32, 'mem_norm': _jnp.float32, 'xattn_wq': _jnp.float32, 'xattn_wkv': _jnp.float32, 'xattn_wo': _jnp.float32, 'ffn2_norm': _jnp.float32, 'ffn2_w_gate': _jnp.float32, 'ffn2_w_up': _jnp.float32, 'ffn2_w_down': _jnp.float32, 'final_norm': _jnp.float32}
MOMENT_SCALE = {'ffn1_norm': 1.064808e-01, 'ffn1_w_gate': 4.801732e-02, 'ffn1_w_up': 4.678556e-02, 'ffn1_w_down': 7.764782e-02, 'mix_norm': 2.359852e-01, 'w_in': 2.098307e-01, 'attn_sinks': 7.429619e-02, 'pool_w': 1.754970e-01, 'pool_scale': 1.696520e-01, 'attn_out_norm': 2.172359e-01, 'pool_out_norm': 1.691967e-01, 'w_out': 1.952165e-01, 'xattn_norm': 1.750275e-02, 'mem_norm': 2.792570e-02, 'xattn_wq': 1.769206e-02, 'xattn_wkv': 2.054094e-02, 'xattn_wo': 2.280234e-02, 'ffn2_norm': 7.200608e-02, 'ffn2_w_gate': 2.946549e-02, 'ffn2_w_up': 2.893059e-02, 'ffn2_w_down': 4.798507e-02, 'final_norm': 6.411193e+01}


def _to_microbatches(a, axis):
    t = _jnp.moveaxis(a, axis, 0)
    t = t.reshape((N_MICROBATCH, t.shape[0] // N_MICROBATCH) + t.shape[1:])
    return _jnp.moveaxis(t, 1, axis + 1)


def setup_inputs(seed: int = 0) -> dict:
    inp = _fwd_setup_inputs(seed)
    key = _jax.random.fold_in(_jax.random.key(seed), 7919)
    shape, _ = _output_shape()
    out = dict(inp)
    out["loss_target"] = _jax.random.normal(_jax.random.fold_in(key, 0), shape, _jnp.float32)
    for i, name in enumerate(TWIN_WEIGHTS):
        w = inp[name].astype(_jnp.float32)
        if MOMENT_SCALE is None:
            s = _jnp.sqrt(_jnp.mean(_jnp.square(w)) + 1e-30)
        else:
            s = MOMENT_SCALE[name]
        km, kv = _jax.random.split(_jax.random.fold_in(key, i + 1))
        out[name] = w
        out["m_" + name] = s * _jax.random.normal(km, w.shape, _jnp.float32)
        out["v_" + name] = (s * s) * _jax.random.uniform(kv, w.shape, _jnp.float32, 0.5, 1.5)
    if N_MICROBATCH > 1:
        for name, axis in PER_EXAMPLE_BATCH_AXIS.items():
            out[name] = _to_microbatches(out[name], axis)
    return {'x': out['x'], 'mem': out['mem'], 'positions': out['positions'], 'ffn1_norm': out['ffn1_norm'], 'ffn1_w_gate': out['ffn1_w_gate'], 'ffn1_w_up': out['ffn1_w_up'], 'ffn1_w_down': out['ffn1_w_down'], 'mix_norm': out['mix_norm'], 'w_in': out['w_in'], 'attn_sinks': out['attn_sinks'], 'pool_w': out['pool_w'], 'pool_scale': out['pool_scale'], 'attn_out_norm': out['attn_out_norm'], 'pool_out_norm': out['pool_out_norm'], 'w_out': out['w_out'], 'xattn_norm': out['xattn_norm'], 'mem_norm': out['mem_norm'], 'xattn_wq': out['xattn_wq'], 'xattn_wkv': out['xattn_wkv'], 'xattn_wo': out['xattn_wo'], 'ffn2_norm': out['ffn2_norm'], 'ffn2_w_gate': out['ffn2_w_gate'], 'ffn2_w_up': out['ffn2_w_up'], 'ffn2_w_down': out['ffn2_w_down'], 'final_norm': out['final_norm'], 'loss_target': out['loss_target'], 'm_ffn1_norm': out['m_ffn1_norm'], 'm_ffn1_w_gate': out['m_ffn1_w_gate'], 'm_ffn1_w_up': out['m_ffn1_w_up'], 'm_ffn1_w_down': out['m_ffn1_w_down'], 'm_mix_norm': out['m_mix_norm'], 'm_w_in': out['m_w_in'], 'm_attn_sinks': out['m_attn_sinks'], 'm_pool_w': out['m_pool_w'], 'm_pool_scale': out['m_pool_scale'], 'm_attn_out_norm': out['m_attn_out_norm'], 'm_pool_out_norm': out['m_pool_out_norm'], 'm_w_out': out['m_w_out'], 'm_xattn_norm': out['m_xattn_norm'], 'm_mem_norm': out['m_mem_norm'], 'm_xattn_wq': out['m_xattn_wq'], 'm_xattn_wkv': out['m_xattn_wkv'], 'm_xattn_wo': out['m_xattn_wo'], 'm_ffn2_norm': out['m_ffn2_norm'], 'm_ffn2_w_gate': out['m_ffn2_w_gate'], 'm_ffn2_w_up': out['m_ffn2_w_up'], 'm_ffn2_w_down': out['m_ffn2_w_down'], 'm_final_norm': out['m_final_norm'], 'v_ffn1_norm': out['v_ffn1_norm'], 'v_ffn1_w_gate': out['v_ffn1_w_gate'], 'v_ffn1_w_up': out['v_ffn1_w_up'], 'v_ffn1_w_down': out['v_ffn1_w_down'], 'v_mix_norm': out['v_mix_norm'], 'v_w_in': out['v_w_in'], 'v_attn_sinks': out['v_attn_sinks'], 'v_pool_w': out['v_pool_w'], 'v_pool_scale': out['v_pool_scale'], 'v_attn_out_norm': out['v_attn_out_norm'], 'v_pool_out_norm': out['v_pool_out_norm'], 'v_w_out': out['v_w_out'], 'v_xattn_norm': out['v_xattn_norm'], 'v_mem_norm': out['v_mem_norm'], 'v_xattn_wq': out['v_xattn_wq'], 'v_xattn_wkv': out['v_xattn_wkv'], 'v_xattn_wo': out['v_xattn_wo'], 'v_ffn2_norm': out['v_ffn2_norm'], 'v_ffn2_w_gate': out['v_ffn2_w_gate'], 'v_ffn2_w_up': out['v_ffn2_w_up'], 'v_ffn2_w_down': out['v_ffn2_w_down'], 'v_final_norm': out['v_final_norm']}


def _loss(weights, diff, rest, loss_target):
    with _jax.named_scope("forward"):
        args = {**rest, TWIN_DIFF_INPUT: diff, **{k: w.astype(_WEIGHT_DTYPES[k]) for k, w in weights.items()}}
        y = _forward(args)
    with _jax.named_scope("loss_head"):
        err = _jnp.square(y.astype(_jnp.float32) - loss_target)
        return 0.5 * _jnp.sum(_jnp.mean(err, axis=-1)) if err.ndim else 0.5 * err


def _adamw(w, g, m, v):
    m = ADAM_B1 * m + (1.0 - ADAM_B1) * g
    v = ADAM_B2 * v + (1.0 - ADAM_B2) * _jnp.square(g)
    m_hat = m / (1.0 - ADAM_B1 ** ADAM_STEP)
    v_hat = v / (1.0 - ADAM_B2 ** ADAM_STEP)
    delta = -ADAM_LR * (m_hat / (_jnp.sqrt(v_hat) + ADAM_EPS) + ADAM_WD * w)
    return delta, m, v


def reference(x, mem, positions, ffn1_norm, ffn1_w_gate, ffn1_w_up, ffn1_w_down, mix_norm, w_in, attn_sinks, pool_w, pool_scale, attn_out_norm, pool_out_norm, w_out, xattn_norm, mem_norm, xattn_wq, xattn_wkv, xattn_wo, ffn2_norm, ffn2_w_gate, ffn2_w_up, ffn2_w_down, final_norm, loss_target, m_ffn1_norm, m_ffn1_w_gate, m_ffn1_w_up, m_ffn1_w_down, m_mix_norm, m_w_in, m_attn_sinks, m_pool_w, m_pool_scale, m_attn_out_norm, m_pool_out_norm, m_w_out, m_xattn_norm, m_mem_norm, m_xattn_wq, m_xattn_wkv, m_xattn_wo, m_ffn2_norm, m_ffn2_w_gate, m_ffn2_w_up, m_ffn2_w_down, m_final_norm, v_ffn1_norm, v_ffn1_w_gate, v_ffn1_w_up, v_ffn1_w_down, v_mix_norm, v_w_in, v_attn_sinks, v_pool_w, v_pool_scale, v_attn_out_norm, v_pool_out_norm, v_w_out, v_xattn_norm, v_mem_norm, v_xattn_wq, v_xattn_wkv, v_xattn_wo, v_ffn2_norm, v_ffn2_w_gate, v_ffn2_w_up, v_ffn2_w_down, v_final_norm):
    given = dict(x=x, mem=mem, positions=positions, ffn1_norm=ffn1_norm, ffn1_w_gate=ffn1_w_gate, ffn1_w_up=ffn1_w_up, ffn1_w_down=ffn1_w_down, mix_norm=mix_norm, w_in=w_in, attn_sinks=attn_sinks, pool_w=pool_w, pool_scale=pool_scale, attn_out_norm=attn_out_norm, pool_out_norm=pool_out_norm, w_out=w_out, xattn_norm=xattn_norm, mem_norm=mem_norm, xattn_wq=xattn_wq, xattn_wkv=xattn_wkv, xattn_wo=xattn_wo, ffn2_norm=ffn2_norm, ffn2_w_gate=ffn2_w_gate, ffn2_w_up=ffn2_w_up, ffn2_w_down=ffn2_w_down, final_norm=final_norm, loss_target=loss_target, m_ffn1_norm=m_ffn1_norm, m_ffn1_w_gate=m_ffn1_w_gate, m_ffn1_w_up=m_ffn1_w_up, m_ffn1_w_down=m_ffn1_w_down, m_mix_norm=m_mix_norm, m_w_in=m_w_in, m_attn_sinks=m_attn_sinks, m_pool_w=m_pool_w, m_pool_scale=m_pool_scale, m_attn_out_norm=m_attn_out_norm, m_pool_out_norm=m_pool_out_norm, m_w_out=m_w_out, m_xattn_norm=m_xattn_norm, m_mem_norm=m_mem_norm, m_xattn_wq=m_xattn_wq, m_xattn_wkv=m_xattn_wkv, m_xattn_wo=m_xattn_wo, m_ffn2_norm=m_ffn2_norm, m_ffn2_w_gate=m_ffn2_w_gate, m_ffn2_w_up=m_ffn2_w_up, m_ffn2_w_down=m_ffn2_w_down, m_final_norm=m_final_norm, v_ffn1_norm=v_ffn1_norm, v_ffn1_w_gate=v_ffn1_w_gate, v_ffn1_w_up=v_ffn1_w_up, v_ffn1_w_down=v_ffn1_w_down, v_mix_norm=v_mix_norm, v_w_in=v_w_in, v_attn_sinks=v_attn_sinks, v_pool_w=v_pool_w, v_pool_scale=v_pool_scale, v_attn_out_norm=v_attn_out_norm, v_pool_out_norm=v_pool_out_norm, v_w_out=v_w_out, v_xattn_norm=v_xattn_norm, v_mem_norm=v_mem_norm, v_xattn_wq=v_xattn_wq, v_xattn_wkv=v_xattn_wkv, v_xattn_wo=v_xattn_wo, v_ffn2_norm=v_ffn2_norm, v_ffn2_w_gate=v_ffn2_w_gate, v_ffn2_w_up=v_ffn2_w_up, v_ffn2_w_down=v_ffn2_w_down, v_final_norm=v_final_norm)
    weights = {n: given[n] for n in TWIN_WEIGHTS}
    shared = {n: given[n] for n in SHARED_INPUTS}
    per_example = {n: given[n] for n in ['x', 'mem', 'positions']}
    grad_fn = _jax.value_and_grad(_loss, argnums=(0, 1))

    def one_microbatch(ex, loss_target):
        ex = dict(ex)
        diff = ex.pop(TWIN_DIFF_INPUT)
        return grad_fn(weights, diff, {**shared, **ex}, loss_target)

    if N_MICROBATCH == 1:
        loss, (grad_w, grad_x) = one_microbatch(per_example, given["loss_target"])
    else:
        def body(carry, xs):
            loss_sum, grad_sum = carry
            l_k, (gw_k, gx_k) = one_microbatch(xs[0], xs[1])
            with _jax.named_scope("update"):
                return (loss_sum + l_k, _jax.tree.map(_jnp.add, grad_sum, gw_k)), gx_k

        init = (_jnp.zeros((), _jnp.float32), _jax.tree.map(_jnp.zeros_like, weights))
        (loss, grad_w), grad_x = _jax.lax.scan(body, init, (per_example, given["loss_target"]))
    with _jax.named_scope("update"):
        delta_w, new_m, new_v = {}, {}, {}
        for n in TWIN_WEIGHTS:
            delta_w[n], new_m[n], new_v[n] = _adamw(weights[n], grad_w[n], given["m_" + n], given["v_" + n])
    return (loss, grad_x, *[grad_w[n] for n in TWIN_WEIGHTS], *[delta_w[n] for n in TWIN_WEIGHTS],
            *[new_m[n] for n in TWIN_WEIGHTS], *[new_v[n] for n in TWIN_WEIGHTS])
```

```python
import functools

import jax
import jax.numpy as jnp
from jax import lax
from jax.experimental import pallas as pl
from jax.experimental.pallas import tpu as pltpu

F32 = jnp.float32
MXU = jnp.bfloat16

EPS = 1e-6
HEAD_DIM = 64
N_Q_HEADS = 8
N_KV_HEADS = 2
Q_PER_KV = N_Q_HEADS // N_KV_HEADS
ATTN_WIDTH = N_Q_HEADS * HEAD_DIM
KV_WIDTH = N_KV_HEADS * HEAD_DIM
BLOCK = 128
ROPE_THETA = 10000.0
POOL_WINDOWS = (2, 4, 8, 16)
POOL_GROUP = 128
POOL_WIDTH = len(POOL_WINDOWS) * POOL_GROUP
POOL_HALO = 16
X_HEADS = 4
X_HEAD_DIM = 256
FFN_RES = 0.5
NEG = -1e30
ADAM_LR = 0.001
ADAM_B1 = 0.9
ADAM_B2 = 0.999
ADAM_EPS = 1e-08
ADAM_WD = 0.01
ADAM_STEP = 10

N_CHIPS = 4
N_DEV = 8
V7X_VMEM_BYTES = 64 * 1024 * 1024
VMEM_LIMIT = V7X_VMEM_BYTES - 8 * 1024 * 1024
LANES = 128
TOKEN_TILE = 512
MESH = pl.DeviceIdType.MESH

BIG = (("ffn1_w_gate", True), ("ffn1_w_up", True), ("ffn1_w_down", False), ("w_in", True), ("w_out", False),
       ("xattn_wq", False), ("xattn_wkv", True), ("xattn_wo", False),
       ("ffn2_w_gate", True), ("ffn2_w_up", True), ("ffn2_w_down", False))
SMALL = ("ffn1_norm", "mix_norm", "attn_sinks", "pool_w", "pool_scale", "attn_out_norm", "pool_out_norm",
         "xattn_norm", "mem_norm", "ffn2_norm", "final_norm")
WEIGHTS = ("ffn1_norm", "ffn1_w_gate", "ffn1_w_up", "ffn1_w_down", "mix_norm", "w_in", "attn_sinks", "pool_w",
           "pool_scale", "attn_out_norm", "pool_out_norm", "w_out", "xattn_norm", "mem_norm", "xattn_wq",
           "xattn_wkv", "xattn_wo", "ffn2_norm", "ffn2_w_gate", "ffn2_w_up", "ffn2_w_down", "final_norm")


def _S(shape, dtype):
    return jax.ShapeDtypeStruct(tuple(shape), dtype)


def _pcall(body, **kw):
    return pl.pallas_call(body, **kw)


def _cp(*sem):
    return pltpu.CompilerParams(dimension_semantics=tuple(sem), vmem_limit_bytes=VMEM_LIMIT)


def _nt(a, b):
    return lax.dot_general(a, b, (((1,), (1,)), ((), ())), preferred_element_type=F32)


def _nn(a, b):
    return lax.dot_general(a, b, (((1,), (0,)), ((), ())), preferred_element_type=F32)


def _tn(a, b):
    return lax.dot_general(a, b, (((0,), (0,)), ((), ())), preferred_element_type=F32)


def _tile(n, want):
    t = min(n, want)
    assert n % t == 0, (n, want)
    return t


def _resident(shape):
    nd = len(shape)
    return pl.BlockSpec(tuple(shape), lambda *_: (0,) * nd)


def _rms_fwd(x, g, name):
    T, C = x.shape
    tm = _tile(T, TOKEN_TILE)

    def body(x_ref, g_ref, o_ref):
        xv = x_ref[...]
        r = lax.rsqrt(jnp.mean(xv * xv, axis=-1, keepdims=True) + EPS)
        o_ref[...] = (xv * r * g_ref[...]).astype(o_ref.dtype)

    return _pcall(body, name=name, grid=(T // tm,),
                  in_specs=[pl.BlockSpec((tm, C), lambda i: (i, 0)), _resident((1, C))],
                  out_specs=pl.BlockSpec((tm, C), lambda i: (i, 0)),
                  out_shape=_S((T, C), MXU), compiler_params=_cp("parallel"))(x, g)


def _rms_bwd(x, g, dh, name, dres=None, col=0):
    T, C = x.shape
    tm = _tile(T, TOKEN_TILE)

    def body(*refs):
        if dres is None:
            x_ref, g_ref, dh_ref, dx_ref, dg_ref = refs
        else:
            x_ref, g_ref, dh_ref, dres_ref, dx_ref, dg_ref = refs
        xv = x_ref[...]
        r = lax.rsqrt(jnp.mean(xv * xv, axis=-1, keepdims=True) + EPS)
        xh = xv * r
        dhv = dh_ref[...].astype(F32)
        dxn = dhv * g_ref[...]
        dx = r * (dxn - xh * jnp.mean(dxn * xh, axis=-1, keepdims=True))
        if dres is not None:
            dx = dx + dres_ref[...]
        dx_ref[...] = dx

        @pl.when(pl.program_id(0) == 0)
        def _():
            dg_ref[...] = jnp.zeros_like(dg_ref)

        dg_ref[...] += jnp.sum(dhv * xh, axis=0, keepdims=True)

    tok = pl.BlockSpec((tm, C), lambda i: (i, 0))
    in_specs = [tok, _resident((1, C)), pl.BlockSpec((tm, C), lambda i: (i, col))]
    args = [x, g, dh]
    if dres is not None:
        in_specs.append(tok)
        args.append(dres)
    return _pcall(body, name=name, grid=(T // tm,), in_specs=in_specs,
                  out_specs=[tok, _resident((1, C))],
                  out_shape=[_S((T, C), F32), _S((1, C), F32)], compiler_params=_cp("arbitrary"))(*args)


def _mm(pairs, *, nt, out_dtype, name, res=None, res_scale=1.0, a_scale=1.0, tm=TOKEN_TILE):
    M = pairs[0][0].shape[0]
    N = pairs[0][1].shape[0] if nt else pairs[0][1].shape[1]
    tm = _tile(M, tm)
    n = len(pairs)

    def body(*refs):
        a_refs, w_refs = refs[:n], refs[n:2 * n]
        o_ref = refs[-1]
        acc = None
        for a_ref, w_ref in zip(a_refs, w_refs):
            a = a_ref[...]
            if a_scale != 1.0:
                a = a * a_scale
            a = a.astype(MXU)
            p = _nt(a, w_ref[...]) if nt else _nn(a, w_ref[...])
            acc = p if acc is None else acc + p
        if res is not None:
            acc = refs[2 * n][...] + res_scale * acc
        o_ref[...] = acc.astype(o_ref.dtype)

    in_specs = [pl.BlockSpec((tm, a.shape[1]), lambda i: (i, 0)) for a, _ in pairs]
    in_specs += [_resident(w.shape) for _, w in pairs]
    args = [a for a, _ in pairs] + [w for _, w in pairs]
    if res is not None:
        in_specs.append(pl.BlockSpec((tm, N), lambda i: (i, 0)))
        args.append(res)
    return _pcall(body, name=name, grid=(M // tm,), in_specs=in_specs,
                  out_specs=pl.BlockSpec((tm, N), lambda i: (i, 0)),
                  out_shape=_S((M, N), out_dtype), compiler_params=_cp("parallel"))(*args)


def _mm_tn(l, r, name, *, l_scale=1.0, r_scale=1.0, tr=1408, tt=TOKEN_TILE):
    T, R = l.shape
    C = r.shape[1]
    tt = _tile(T, tt)
    tr = tr if R % tr == 0 else (1024 if R % 1024 == 0 and R > 1280 else R)

    def body(l_ref, r_ref, o_ref):
        lv, rv = l_ref[...], r_ref[...]
        if l_scale != 1.0:
            lv = lv * l_scale
        if r_scale != 1.0:
            rv = rv * r_scale
        lv, rv = lv.astype(MXU), rv.astype(MXU)

        @pl.when(pl.program_id(1) == 0)
        def _():
            o_ref[...] = jnp.zeros_like(o_ref)

        o_ref[...] += _tn(lv, rv)

    return _pcall(body, name=name, grid=(R // tr, T // tt),
                  in_specs=[pl.BlockSpec((tt, tr), lambda i, t: (t, i)), pl.BlockSpec((tt, C), lambda i, t: (t, 0))],
                  out_specs=pl.BlockSpec((tr, C), lambda i, t: (i, 0)),
                  out_shape=_S((R, C), F32), compiler_params=_cp("parallel", "arbitrary"))(l, r)


FFN_COL_TILE = 1408


def _ffn_up(h, wgT, wuT, name):
    T, D = h.shape
    Fd = wgT.shape[0]
    tm, tn = _tile(T, TOKEN_TILE), _tile(Fd, FFN_COL_TILE)

    def body(h_ref, wg_ref, wu_ref, a_ref, b_ref, s_ref):
        hv = h_ref[...]
        a = _nt(hv, wg_ref[...])
        b = _nt(hv, wu_ref[...])
        s = a * (1.0 / (1.0 + jnp.exp(-a))) * b
        a_ref[...] = a.astype(a_ref.dtype)
        b_ref[...] = b.astype(b_ref.dtype)
        s_ref[...] = s.astype(s_ref.dtype)

    wspec = pl.BlockSpec((tn, D), lambda j, i: (j, 0))
    ospec = pl.BlockSpec((tm, tn), lambda j, i: (i, j))
    return _pcall(body, name=name, grid=(Fd // tn, T // tm),
                  in_specs=[pl.BlockSpec((tm, D), lambda j, i: (i, 0)), wspec, wspec],
                  out_specs=[ospec, ospec, ospec], out_shape=[_S((T, Fd), MXU)] * 3,
                  compiler_params=_cp("parallel", "parallel"))(h, wgT, wuT)


def _ffn_mid_bwd(dx, wd, a, b, name):
    T, D = dx.shape
    Fd = wd.shape[0]
    tm, tn = _tile(T, TOKEN_TILE), _tile(Fd, FFN_COL_TILE)

    def body(dx_ref, wd_ref, a_ref, b_ref, da_ref, db_ref):
        dy = (dx_ref[...] * FFN_RES).astype(MXU)
        ds = _nt(dy, wd_ref[...])
        av, bv = a_ref[...].astype(F32), b_ref[...].astype(F32)
        sg = 1.0 / (1.0 + jnp.exp(-av))
        da_ref[...] = (ds * bv * (sg * (1.0 + av * (1.0 - sg)))).astype(da_ref.dtype)
        db_ref[...] = (ds * (av * sg)).astype(db_ref.dtype)

    aspec = pl.BlockSpec((tm, tn), lambda j, i: (i, j))
    return _pcall(body, name=name, grid=(Fd // tn, T // tm),
                  in_specs=[pl.BlockSpec((tm, D), lambda j, i: (i, 0)), pl.BlockSpec((tn, D), lambda j, i: (j, 0)),
                            aspec, aspec],
                  out_specs=[aspec, aspec], out_shape=[_S((T, Fd), MXU)] * 2,
                  compiler_params=_cp("parallel", "parallel"))(dx, wd, a, b)


def _swap_halves(t):
    w = t.shape[1]
    lane = lax.broadcasted_iota(jnp.int32, t.shape, 1)
    first = (lane % HEAD_DIM) < (HEAD_DIM // 2)
    return jnp.where(first, pltpu.roll(t, w - HEAD_DIM // 2, 1), pltpu.roll(t, HEAD_DIM // 2, 1))


def _rope(t, cos2, sin2):
    reps = t.shape[1] // LANES
    c = jnp.tile(cos2, (1, reps)) if reps > 1 else cos2
    s = jnp.tile(sin2, (1, reps)) if reps > 1 else sin2
    return t * c + _swap_halves(t) * s


def _rope_bwd(dt, cos2, sin2):
    reps = dt.shape[1] // LANES
    c = jnp.tile(cos2, (1, reps)) if reps > 1 else cos2
    s = jnp.tile(sin2, (1, reps)) if reps > 1 else sin2
    return dt * c + _swap_halves(dt * s)


def _in_proj(h, winT, cos2, sin2, name):
    T, D = h.shape
    tm = _tile(T, TOKEN_TILE)
    qe, ke, ve = ATTN_WIDTH, ATTN_WIDTH + KV_WIDTH, ATTN_WIDTH + 2 * KV_WIDTH

    def body(h_ref, w_ref, c_ref, s_ref, q_ref, k_ref, v_ref, u_ref):
        proj = _nt(h_ref[...], w_ref[...])
        cv, sv = c_ref[...], s_ref[...]
        q_ref[...] = _rope(proj[:, :qe], cv, sv).astype(q_ref.dtype)
        k_ref[...] = _rope(proj[:, qe:ke], cv, sv).astype(k_ref.dtype)
        v_ref[...] = proj[:, ke:ve].astype(v_ref.dtype)
        u_ref[...] = proj[:, ve:]

    def tok(w):
        return pl.BlockSpec((tm, w), lambda i: (i, 0))

    return _pcall(body, name=name, grid=(T // tm,),
                  in_specs=[tok(D), _resident(winT.shape), tok(LANES), tok(LANES)],
                  out_specs=[tok(ATTN_WIDTH), tok(KV_WIDTH), tok(KV_WIDTH), tok(POOL_WIDTH)],
                  out_shape=[_S((T, ATTN_WIDTH), MXU), _S((T, KV_WIDTH), MXU), _S((T, KV_WIDTH), MXU),
                             _S((T, POOL_WIDTH), F32)],
                  compiler_params=_cp("parallel"))(h, winT, cos2, sin2)


def _swa_mask(n):
    rows = Q_PER_KV * BLOCK
    qi = lax.broadcasted_iota(jnp.int32, (rows, 2 * BLOCK), 0) % BLOCK
    kj = lax.broadcasted_iota(jnp.int32, (rows, 2 * BLOCK), 1)
    diff = qi + BLOCK - kj
    return (diff >= 0) & (diff < BLOCK) & ((n - 1) * BLOCK + kj >= 0)


def _swa_probs(qs, kh, sink_col, mask):
    s = _nt(qs, kh) * (HEAD_DIM ** -0.5)
    s = jnp.where(mask, s, NEG)
    m = jnp.maximum(jnp.max(s, axis=1, keepdims=True), sink_col)
    e = jnp.exp(s - m)
    es = jnp.exp(sink_col - m)
    inv = 1.0 / (jnp.sum(e, axis=1, keepdims=True) + es)
    return e * inv, es * inv


def _sink_column(sinks_ref, kv):
    return jnp.concatenate([jnp.full((BLOCK, 1), sinks_ref[0, kv * Q_PER_KV + g], F32) for g in range(Q_PER_KV)], axis=0)


def _stack_heads(t, kv):
    return jnp.concatenate([t[:, (kv * Q_PER_KV + g) * HEAD_DIM:(kv * Q_PER_KV + g + 1) * HEAD_DIM]
                            for g in range(Q_PER_KV)], axis=0)


def _swa_specs(T):
    nb = T // BLOCK
    cur = lambda w: pl.BlockSpec((BLOCK, w), lambda n: (n, 0))
    prev = lambda w: pl.BlockSpec((BLOCK, w), lambda n: (jnp.maximum(n - 1, 0), 0))
    return nb, cur, prev


def _swa_fwd(q, k, v, sinks, name):
    T = q.shape[0]
    nb, cur, prev = _swa_specs(T)

    def body(sinks_ref, q_ref, kc_ref, kp_ref, vc_ref, vp_ref, o_ref):
        mask = _swa_mask(pl.program_id(0))
        qv = q_ref[...]
        kk = jnp.concatenate([kp_ref[...], kc_ref[...]], axis=0)
        vv = jnp.concatenate([vp_ref[...], vc_ref[...]], axis=0)
        for kv in range(N_KV_HEADS):
            hs = slice(kv * HEAD_DIM, (kv + 1) * HEAD_DIM)
            p, _ = _swa_probs(_stack_heads(qv, kv), kk[:, hs], _sink_column(sinks_ref, kv), mask)
            o = _nn(p.astype(MXU), vv[:, hs])
            for g in range(Q_PER_KV):
                c0 = (kv * Q_PER_KV + g) * HEAD_DIM
                o_ref[:, c0:c0 + HEAD_DIM] = o[g * BLOCK:(g + 1) * BLOCK, :]

    return _pcall(body, name=name, grid=(nb,),
                  in_specs=[pl.BlockSpec(memory_space=pltpu.SMEM), cur(ATTN_WIDTH), cur(KV_WIDTH), prev(KV_WIDTH),
                            cur(KV_WIDTH), prev(KV_WIDTH)],
                  out_specs=cur(ATTN_WIDTH), out_shape=_S((T, ATTN_WIDTH), F32),
                  compiler_params=_cp("parallel"))(sinks, q, k, k, v, v)


def _swa_bwd(q, k, v, do, sinks, name):
    T = q.shape[0]
    nb, cur, prev = _swa_specs(T)

    def body(sinks_ref, q_ref, kc_ref, kp_ref, vc_ref, vp_ref, do_ref,
             dq_ref, dko_ref, dkp_ref, dvo_ref, dvp_ref, dsk_ref):
        mask = _swa_mask(pl.program_id(0))
        qv, dov = q_ref[...], do_ref[...].astype(MXU)
        kk = jnp.concatenate([kp_ref[...], kc_ref[...]], axis=0)
        vv = jnp.concatenate([vp_ref[...], vc_ref[...]], axis=0)

        @pl.when(pl.program_id(0) == 0)
        def _():
            dsk_ref[...] = jnp.zeros_like(dsk_ref)

        for kv in range(N_KV_HEADS):
            hs = slice(kv * HEAD_DIM, (kv + 1) * HEAD_DIM)
            qs, dos = _stack_heads(qv, kv), _stack_heads(dov, kv)
            p, ps = _swa_probs(qs, kk[:, hs], _sink_column(sinks_ref, kv), mask)
            dp = _nt(dos, vv[:, hs])
            delta = jnp.sum(p * dp, axis=1, keepdims=True)
            ds = (p * (dp - delta) * (HEAD_DIM ** -0.5)).astype(MXU)
            dq = _nn(ds, kk[:, hs])
            dk = _tn(ds, qs)
            dv = _tn(p.astype(MXU), dos)
            dsink = -ps * delta
            for g in range(Q_PER_KV):
                h = kv * Q_PER_KV + g
                dq_ref[:, h * HEAD_DIM:(h + 1) * HEAD_DIM] = dq[g * BLOCK:(g + 1) * BLOCK, :]
                dsk_ref[h:h + 1, :] += jnp.broadcast_to(
                    jnp.sum(dsink[g * BLOCK:(g + 1) * BLOCK, :], axis=0, keepdims=True), (1, LANES))
            dkp_ref[:, hs] = dk[:BLOCK, :]
            dko_ref[:, hs] = dk[BLOCK:, :]
            dvp_ref[:, hs] = dv[:BLOCK, :]
            dvo_ref[:, hs] = dv[BLOCK:, :]

    kvs = _S((T, KV_WIDTH), F32)
    return _pcall(body, name=name, grid=(nb,),
                  in_specs=[pl.BlockSpec(memory_space=pltpu.SMEM), cur(ATTN_WIDTH), cur(KV_WIDTH), prev(KV_WIDTH),
                            cur(KV_WIDTH), prev(KV_WIDTH), cur(ATTN_WIDTH)],
                  out_specs=[cur(ATTN_WIDTH), cur(KV_WIDTH), cur(KV_WIDTH), cur(KV_WIDTH), cur(KV_WIDTH),
                             _resident((N_Q_HEADS, LANES))],
                  out_shape=[_S((T, ATTN_WIDTH), F32), kvs, kvs, kvs, kvs, _S((N_Q_HEADS, LANES), F32)],
                  compiler_params=_cp("arbitrary"))(sinks, q, k, k, v, v, do)


def _dproj(dq, dko, dkp, dvo, dvp, du, cos2, sin2, name):
    T = dq.shape[0]
    nb = T // BLOCK
    cur = lambda w: pl.BlockSpec((BLOCK, w), lambda n: (n, 0))
    nxt = lambda w: pl.BlockSpec((BLOCK, w), lambda n: (jnp.minimum(n + 1, nb - 1), 0))

    def body(dq_ref, dko_ref, dkp_ref, dvo_ref, dvp_ref, du_ref, c_ref, s_ref, o_ref):
        more = (pl.program_id(0) < nb - 1).astype(F32)
        cv, sv = c_ref[...], s_ref[...]
        dk = dko_ref[...] + more * dkp_ref[...]
        dv = dvo_ref[...] + more * dvp_ref[...]
        o_ref[...] = jnp.concatenate(
            [_rope_bwd(dq_ref[...], cv, sv), _rope_bwd(dk, cv, sv), dv, du_ref[...]], axis=1).astype(o_ref.dtype)

    width = ATTN_WIDTH + 2 * KV_WIDTH + POOL_WIDTH
    return _pcall(body, name=name, grid=(nb,),
                  in_specs=[cur(ATTN_WIDTH), cur(KV_WIDTH), nxt(KV_WIDTH), cur(KV_WIDTH), nxt(KV_WIDTH),
                            cur(POOL_WIDTH), cur(LANES), cur(LANES)],
                  out_specs=cur(width), out_shape=_S((T, width), MXU),
                  compiler_params=_cp("parallel"))(dq, dko, dkp, dvo, dvp, du, cos2, sin2)


def _pool_specs(T):
    tm = _tile(T, TOKEN_TILE)
    hb = tm // POOL_HALO
    nh = T // POOL_HALO
    tok = lambda w: pl.BlockSpec((tm, w), lambda i: (i, 0))
    before = pl.BlockSpec((POOL_HALO, POOL_WIDTH), lambda i: (jnp.maximum(i * hb - 1, 0), 0))
    after = pl.BlockSpec((POOL_HALO, POOL_WIDTH), lambda i: (jnp.minimum((i + 1) * hb, nh - 1), 0))
    return tm, tok, before, after


def _window_counts(i, tm, rows, w):
    t = i * tm + lax.broadcasted_iota(jnp.int32, (rows, 1), 0)
    return jnp.minimum(t + 1, w).astype(F32)


def _pooled(u_ext, i, tm):
    out = []
    for g, w in enumerate(POOL_WINDOWS):
        acc = u_ext[:, g * POOL_GROUP:(g + 1) * POOL_GROUP]
        tok = acc[POOL_HALO:, :]
        sh = 1
        while sh < w:
            acc = acc + pltpu.roll(acc, sh, 0)
            sh *= 2
        out.append(acc[POOL_HALO:, :] / _window_counts(i, tm, tm, w) - tok)
    return out


def _pool_fwd(u, out_a, pool_w, pool_scale, ga, gb, name):
    T = u.shape[0]
    tm, tok, before, _ = _pool_specs(T)

    def body(u_ref, halo_ref, oa_ref, pw_ref, sc_ref, ga_ref, gb_ref, ob_ref, mg_ref):
        i = pl.program_id(0)
        halo = halo_ref[...] * (i > 0).astype(F32)
        pooled = _pooled(jnp.concatenate([halo, u_ref[...]], axis=0), i, tm)
        mixed = [_nn(pooled[g].astype(MXU), pw_ref[g].astype(MXU)) for g in range(len(POOL_WINDOWS))]
        ob = jnp.concatenate(mixed, axis=1) * sc_ref[...]
        ob_ref[...] = ob
        oa = oa_ref[...]
        ra = lax.rsqrt(jnp.mean(oa * oa, axis=-1, keepdims=True) + EPS)
        rb = lax.rsqrt(jnp.mean(ob * ob, axis=-1, keepdims=True) + EPS)
        mg_ref[...] = jnp.concatenate([oa * ra * ga_ref[...], ob * rb * gb_ref[...]], axis=1).astype(mg_ref.dtype)

    vec = _resident((1, POOL_WIDTH))
    return _pcall(body, name=name, grid=(T // tm,),
                  in_specs=[tok(POOL_WIDTH), before, tok(ATTN_WIDTH), _resident(pool_w.shape), vec, vec, vec],
                  out_specs=[tok(POOL_WIDTH), tok(ATTN_WIDTH + POOL_WIDTH)],
                  out_shape=[_S((T, POOL_WIDTH), F32), _S((T, ATTN_WIDTH + POOL_WIDTH), MXU)],
                  compiler_params=_cp("parallel"))(u, u, out_a, pool_w, pool_scale, ga, gb)


def _pool_bwd(u, dob, pool_w, pool_scale, name):
    T = u.shape[0]
    tm, tok, before, after = _pool_specs(T)
    nt = T // tm
    G = len(POOL_WINDOWS)

    def body(u_ref, halo_ref, dob_ref, dnext_ref, pw_ref, sc_ref, du_ref, dpw_ref, dsc_ref):
        i = pl.program_id(0)
        halo = halo_ref[...] * (i > 0).astype(F32)
        pooled = _pooled(jnp.concatenate([halo, u_ref[...]], axis=0), i, tm)
        dnext = dnext_ref[...] * (i < nt - 1).astype(F32)
        dext = jnp.concatenate([dob_ref[...], dnext], axis=0) * sc_ref[...]

        @pl.when(i == 0)
        def _():
            dpw_ref[...] = jnp.zeros_like(dpw_ref)
            dsc_ref[...] = jnp.zeros_like(dsc_ref)

        dus, dscs = [], []
        for g, w in enumerate(POOL_WINDOWS):
            gs = slice(g * POOL_GROUP, (g + 1) * POOL_GROUP)
            pw = pw_ref[g].astype(MXU)
            pg = pooled[g].astype(MXU)
            dmix = dext[:, gs].astype(MXU)
            dscs.append(jnp.sum(dob_ref[:, gs] * _nn(pg, pw), axis=0, keepdims=True))
            dpw_ref[g] += _tn(pg, dmix[:tm, :])
            dpooled = _nt(dmix, pw)
            acc = dpooled / _window_counts(i, tm, tm + POOL_HALO, w)
            sh = 1
            while sh < w:
                acc = acc + pltpu.roll(acc, tm + POOL_HALO - sh, 0)
                sh *= 2
            dus.append(acc[:tm, :] - dpooled[:tm, :])
        du_ref[...] = jnp.concatenate(dus, axis=1)
        dsc_ref[...] += jnp.concatenate(dscs, axis=1)

    vec = _resident((1, POOL_WIDTH))
    return _pcall(body, name=name, grid=(nt,),
                  in_specs=[tok(POOL_WIDTH), before, tok(POOL_WIDTH), after, _resident(pool_w.shape), vec],
                  out_specs=[tok(POOL_WIDTH), _resident(pool_w.shape), vec],
                  out_shape=[_S((T, POOL_WIDTH), F32), _S(pool_w.shape, F32), _S((1, POOL_WIDTH), F32)],
                  compiler_params=_cp("arbitrary"))(u, u, dob, dob, pool_w, pool_scale)


def _xattn_probs(qh, kh):
    s = _nt(qh, kh) * (X_HEAD_DIM ** -0.5)
    e = jnp.exp(s - jnp.max(s, axis=1, keepdims=True))
    return e / jnp.sum(e, axis=1, keepdims=True)


def _xattn_fwd(q, kvm, name):
    T, XW = q.shape
    tm = _tile(T, TOKEN_TILE)

    def body(q_ref, kv_ref, o_ref):
        for h in range(X_HEADS):
            hs = slice(h * X_HEAD_DIM, (h + 1) * X_HEAD_DIM)
            vs = slice(XW + h * X_HEAD_DIM, XW + (h + 1) * X_HEAD_DIM)
            p = _xattn_probs(q_ref[:, hs], kv_ref[:, hs])
            o_ref[:, hs] = _nn(p.astype(MXU), kv_ref[:, vs]).astype(o_ref.dtype)

    return _pcall(body, name=name, grid=(T // tm,),
                  in_specs=[pl.BlockSpec((tm, XW), lambda i: (i, 0)), _resident(kvm.shape)],
                  out_specs=pl.BlockSpec((tm, XW), lambda i: (i, 0)), out_shape=_S((T, XW), MXU),
                  compiler_params=_cp("parallel"))(q, kvm)


def _xattn_bwd(q, kvm, do, name):
    T, XW = q.shape
    tm = _tile(T, TOKEN_TILE)

    def body(q_ref, kv_ref, do_ref, dq_ref, dkv_ref):
        @pl.when(pl.program_id(0) == 0)
        def _():
            dkv_ref[...] = jnp.zeros_like(dkv_ref)

        for h in range(X_HEADS):
            hs = slice(h * X_HEAD_DIM, (h + 1) * X_HEAD_DIM)
            vs = slice(XW + h * X_HEAD_DIM, XW + (h + 1) * X_HEAD_DIM)
            qh, doh = q_ref[:, hs], do_ref[:, hs]
            p = _xattn_probs(qh, kv_ref[:, hs])
            dp = _nt(doh, kv_ref[:, vs])
            ds = (p * (dp - jnp.sum(p * dp, axis=1, keepdims=True)) * (X_HEAD_DIM ** -0.5)).astype(MXU)
            dq_ref[:, hs] = _nn(ds, kv_ref[:, hs]).astype(dq_ref.dtype)
            dkv_ref[:, hs] += _tn(ds, qh)
            dkv_ref[:, vs] += _tn(p.astype(MXU), doh)

    tok = pl.BlockSpec((tm, XW), lambda i: (i, 0))
    return _pcall(body, name=name, grid=(T // tm,),
                  in_specs=[tok, _resident(kvm.shape), tok],
                  out_specs=[tok, _resident(kvm.shape)],
                  out_shape=[_S((T, XW), MXU), _S(kvm.shape, F32)],
                  compiler_params=_cp("arbitrary"))(q, kvm, do)


def _loss_head(x, tgt, g, name):
    T, D = x.shape
    tm = _tile(T, TOKEN_TILE)

    def body(x_ref, t_ref, g_ref, loss_ref, dx_ref, dg_ref):
        xv, gv = x_ref[...], g_ref[...]
        r = lax.rsqrt(jnp.mean(xv * xv, axis=-1, keepdims=True) + EPS)
        xh = xv * r
        e = xh * gv - t_ref[...]
        dy = e * (1.0 / D)
        dxn = dy * gv
        dx_ref[...] = r * (dxn - xh * jnp.mean(dxn * xh, axis=-1, keepdims=True))

        @pl.when(pl.program_id(0) == 0)
        def _():
            loss_ref[...] = jnp.zeros_like(loss_ref)
            dg_ref[...] = jnp.zeros_like(dg_ref)

        part = jnp.sum(jnp.sum(e * e, axis=1, keepdims=True), axis=0, keepdims=True) * (0.5 / D)
        loss_ref[...] += jnp.broadcast_to(part, (1, LANES))
        dg_ref[...] += jnp.sum(dy * xh, axis=0, keepdims=True)

    tok = pl.BlockSpec((tm, D), lambda i: (i, 0))
    return _pcall(body, name=name, grid=(T // tm,),
                  in_specs=[tok, tok, _resident((1, D))],
                  out_specs=[_resident((1, LANES)), tok, _resident((1, D))],
                  out_shape=[_S((1, LANES), F32), _S((T, D), F32), _S((1, D), F32)],
                  compiler_params=_cp("arbitrary"))(x, tgt, g)


def _rows_tile(rows):
    for t in (512, 416, 352, 256, 128, 64, 32, 16, 8):
        if rows % t == 0:
            return t
    return rows


def _pair_sum(own, got, name):
    R, C = own.shape
    tr = _rows_tile(R)

    def body(a_ref, b_ref, s_ref, h_ref):
        s = a_ref[...] + b_ref[...]
        s_ref[...] = s
        h_ref[...] = s.astype(h_ref.dtype)

    spec = pl.BlockSpec((tr, C), lambda i: (i, 0))
    return _pcall(body, name=name, grid=(R // tr,), in_specs=[spec, spec], out_specs=[spec, spec],
                  out_shape=[_S((R, C), F32), _S((R, C), MXU)], compiler_params=_cp("parallel"))(own, got)


def _final_sum(own, got, name):
    R, C = own.shape
    tr = _rows_tile(R)
    nb = R // tr

    def body(a_ref, g0_ref, g1_ref, g2_ref, o_ref):
        o_ref[...] = ((a_ref[...] + g0_ref[...].astype(F32)) + g1_ref[...].astype(F32)) + g2_ref[...].astype(F32)

    spec = pl.BlockSpec((tr, C), lambda i: (i, 0))
    slot = lambda j: pl.BlockSpec((tr, C), lambda i: (j * nb + i, 0))
    return _pcall(body, name=name, grid=(nb,), in_specs=[spec, slot(0), slot(1), slot(2)], out_specs=spec,
                  out_shape=_S((R, C), F32), compiler_params=_cp("parallel"))(own, got, got, got)


def _adamw(w, g, m, v, name):
    R, C = w.shape
    tr = _rows_tile(R)
    c1 = 1.0 / (1.0 - ADAM_B1 ** ADAM_STEP)
    c2 = 1.0 / (1.0 - ADAM_B2 ** ADAM_STEP)

    def body(w_ref, g_ref, m_ref, v_ref, d_ref, nm_ref, nv_ref):
        gv = g_ref[...]
        nm = ADAM_B1 * m_ref[...] + (1.0 - ADAM_B1) * gv
        nv = ADAM_B2 * v_ref[...] + (1.0 - ADAM_B2) * (gv * gv)
        d_ref[...] = -ADAM_LR * ((nm * c1) / (jnp.sqrt(nv * c2) + ADAM_EPS) + ADAM_WD * w_ref[...])
        nm_ref[...] = nm
        nv_ref[...] = nv

    spec = pl.BlockSpec((tr, C), lambda i: (i, 0))
    return _pcall(body, name=name, grid=(R // tr,), in_specs=[spec] * 4, out_specs=[spec] * 3,
                  out_shape=[_S((R, C), F32)] * 3, compiler_params=_cp("parallel"))(w, g, m, v)


ANY = pl.BlockSpec(memory_space=pl.ANY)


def _place():
    x, y, c = lax.axis_index("x"), lax.axis_index("y"), lax.axis_index("c")
    chips = [(1 - x, y), (x, 1 - y), (1 - x, 1 - y)]
    return x, y, c, chips


def _remote(src, dst, send_sem, recv_sem, dev):
    return pltpu.make_async_remote_copy(src_ref=src, dst_ref=dst, send_sem=send_sem, recv_sem=recv_sem,
                                        device_id=dev, device_id_type=MESH)


def _drain(like, send_sem, recv_sem, me, *, send=False, recv=False):
    d = _remote(like, like, send_sem, recv_sem, me)
    if send:
        d.wait_send()
    if recv:
        d.wait_recv()


def _half_offsets(sizes):
    offs, o = [], 0
    for n in sizes:
        offs.append(o)
        o += n // 2
    return offs, o


def _allgather_weights(pack, sizes, name):
    RS, C = pack.shape
    RSH = RS // 2
    offs, tot = _half_offsets(sizes)
    assert tot == RSH
    nw = len(sizes)

    def body(pack_ref, *refs):
        outs = refs[:nw]
        s_ici, r_ici, s_fwd, r_fwd, loc = refs[nw:]
        x, y, c, chips = _place()
        me, sib = (x, y, c), (x, y, 1 - c)
        q_me = 2 * x + y

        def src_rows(w, h):
            return pack_ref.at[pl.ds(h * RSH + offs[w], sizes[w] // 2)]

        def dst_rows(w, q, h):
            return outs[w].at[pl.ds(q * sizes[w] + h * (sizes[w] // 2), sizes[w] // 2)]

        for w in range(nw):
            for h in range(2):
                pltpu.make_async_copy(src_rows(w, h), dst_rows(w, q_me, h), loc).start()
        for j, (px, py) in enumerate(chips):
            for w in range(nw):
                _remote(src_rows(w, c), dst_rows(w, q_me, c), s_ici.at[j], r_ici.at[j], (px, py, c)).start()
        half = pack_ref.at[pl.ds(0, RSH)]
        for j, (px, py) in enumerate(chips):
            _drain(half, s_ici.at[j], r_ici.at[j], me, recv=True)
            for w in range(nw):
                rows = dst_rows(w, 2 * px + py, c)
                _remote(rows, rows, s_fwd.at[j], r_fwd.at[j], sib).start()
        for j in range(3):
            _drain(half, s_fwd.at[j], r_fwd.at[j], me, recv=True)
        for j in range(3):
            _drain(half, s_ici.at[j], r_ici.at[j], me, send=True)
            _drain(half, s_fwd.at[j], r_fwd.at[j], me, send=True)
        pltpu.make_async_copy(pack_ref, pack_ref, loc).wait()

    return _pcall(body, name=name, in_specs=[ANY], out_specs=[ANY] * nw,
                  out_shape=[_S((N_CHIPS * n, C), pack.dtype) for n in sizes],
                  scratch_shapes=[pltpu.SemaphoreType.DMA((3,)), pltpu.SemaphoreType.DMA((3,)),
                                  pltpu.SemaphoreType.DMA((3,)), pltpu.SemaphoreType.DMA((3,)),
                                  pltpu.SemaphoreType.DMA(())],
                  compiler_params=pltpu.CompilerParams(has_side_effects=True))(pack)


def _rs_pair_exchange(grads, sizes, name):
    C = grads[0].shape[1]
    offs, RSH = _half_offsets(sizes)
    nw = len(sizes)

    def body(*refs):
        g = refs[:nw]
        own_ref, got_ref, s_sem, r_sem, loc = refs[nw:]
        x, y, c, _ = _place()
        me, sib = (x, y, c), (x, y, 1 - c)
        for q in range(N_CHIPS):
            for w in range(nw):
                hw = sizes[w] // 2
                slot = pl.ds(q * RSH + offs[w], hw)
                pltpu.make_async_copy(g[w].at[pl.ds(q * sizes[w] + c * hw, hw)], own_ref.at[slot], loc).start()
                _remote(g[w].at[pl.ds(q * sizes[w] + (1 - c) * hw, hw)], got_ref.at[slot], s_sem, r_sem, sib).start()
        _drain(got_ref, s_sem, r_sem, me, send=True, recv=True)
        pltpu.make_async_copy(own_ref, own_ref, loc).wait()

    shape = _S((N_CHIPS * RSH, C), F32)
    return _pcall(body, name=name, in_specs=[ANY] * nw, out_specs=[ANY, ANY], out_shape=[shape, shape],
                  scratch_shapes=[pltpu.SemaphoreType.DMA(()), pltpu.SemaphoreType.DMA(()), pltpu.SemaphoreType.DMA(())],
                  compiler_params=pltpu.CompilerParams(has_side_effects=True))(*grads)


def _rs_chip_exchange(sum_f32, sum_mxu, name):
    R4, C = sum_f32.shape
    RSH = R4 // N_CHIPS

    def body(f_ref, h_ref, own_ref, got_ref, s_sem, r_sem, loc):
        x, y, c, chips = _place()
        q_me = 2 * x + y
        mine = pltpu.make_async_copy(f_ref.at[pl.ds(q_me * RSH, RSH)], own_ref, loc)
        mine.start()
        cps = [_remote(h_ref.at[pl.ds((2 * px + py) * RSH, RSH)], got_ref.at[pl.ds(j * RSH, RSH)],
                       s_sem.at[j], r_sem.at[j], (px, py, c)) for j, (px, py) in enumerate(chips)]
        for cp in cps:
            cp.start()
        for cp in cps:
            cp.wait()
        mine.wait()

    return _pcall(body, name=name, in_specs=[ANY, ANY], out_specs=[ANY, ANY],
                  out_shape=[_S((RSH, C), F32), _S((3 * RSH, C), sum_mxu.dtype)],
                  scratch_shapes=[pltpu.SemaphoreType.DMA((3,)), pltpu.SemaphoreType.DMA((3,)), pltpu.SemaphoreType.DMA(())],
                  compiler_params=pltpu.CompilerParams(has_side_effects=True))(sum_f32, sum_mxu)


def _rs_share_halves(red_half, name):
    RSH, C = red_half.shape

    def body(h_ref, o_ref, s_sem, r_sem, loc):
        x, y, c, _ = _place()
        rows = o_ref.at[pl.ds(c * RSH, RSH)]
        mine = pltpu.make_async_copy(h_ref, rows, loc)
        mine.start()
        cp = _remote(h_ref, rows, s_sem, r_sem, (x, y, 1 - c))
        cp.start()
        cp.wait()
        mine.wait()

    return _pcall(body, name=name, in_specs=[ANY], out_specs=ANY, out_shape=_S((2 * RSH, C), F32),
                  scratch_shapes=[pltpu.SemaphoreType.DMA(()), pltpu.SemaphoreType.DMA(()), pltpu.SemaphoreType.DMA(())],
                  compiler_params=pltpu.CompilerParams(has_side_effects=True))(red_half)


def _allreduce_small(part, name):
    R, C = part.shape

    def body(p_ref, o_ref, buf, s_sem, r_sem):
        x, y, c, _ = _place()
        my_id = 4 * x + 2 * y + c
        buf[my_id] = p_ref[...]
        cps = []
        for k in range(1, N_DEV):
            fx, fy, fc = (k >> 2) & 1, (k >> 1) & 1, k & 1
            peer = (x ^ fx, y ^ fy, c ^ fc)
            cps.append(_remote(p_ref, buf.at[my_id], s_sem.at[k - 1], r_sem.at[k - 1], peer))
        for cp in cps:
            cp.start()
        for cp in cps:
            cp.wait()
        acc = buf[0]
        for d in range(1, N_DEV):
            acc = acc + buf[d]
        o_ref[...] = acc

    vm = pl.BlockSpec(memory_space=pltpu.VMEM)
    return _pcall(body, name=name, in_specs=[vm], out_specs=vm, out_shape=_S((R, C), F32),
                  scratch_shapes=[pltpu.VMEM((N_DEV, R, C), F32), pltpu.SemaphoreType.DMA((N_DEV - 1,)),
                                  pltpu.SemaphoreType.DMA((N_DEV - 1,))],
                  compiler_params=pltpu.CompilerParams(has_side_effects=True, vmem_limit_bytes=VMEM_LIMIT))(part)


def _row_layout(w, transposed):
    return w.T if transposed else w


def _pack_layer(ws, l):
    chunks = [_row_layout(ws[name][l], tr).astype(MXU) for name, tr in BIG]
    halves = [[ch[:ch.shape[0] // 2] for ch in chunks], [ch[ch.shape[0] // 2:] for ch in chunks]]
    return jnp.concatenate(halves[0] + halves[1], axis=0), [ch.shape[0] for ch in chunks]


def _unpack_layer(red, sizes):
    offs, RSH = _half_offsets(sizes)
    out = {}
    for (name, tr), n, o in zip(BIG, sizes, offs):
        g = jnp.concatenate([red[o:o + n // 2], red[RSH + o:RSH + o + n // 2]], axis=0)
        out[name] = g.T if tr else g
    return out


def _small_rows(v):
    flat = v.reshape(-1)
    pad = (-flat.shape[0]) % 1024
    return jnp.pad(flat, (0, pad)).reshape(-1, 1024)


def _pack_small(vals):
    rows = [_small_rows(vals[n]) for n in SMALL]
    cat = jnp.concatenate(rows, axis=0)
    pad = (-cat.shape[0]) % 8
    return jnp.pad(cat, ((0, pad), (0, 0)))


def _unpack_small(packed, like):
    out, r = {}, 0
    for n in SMALL:
        size = like[n].size
        nr = -(-size // 1024)
        out[n] = packed[r:r + nr].reshape(-1)[:size].reshape(like[n].shape)
        r += nr
    return out


def _rope_tables(positions):
    inv_freq = ROPE_THETA ** (-jnp.arange(0, HEAD_DIM, 2, dtype=F32) / HEAD_DIM)
    ang = positions.astype(F32)[:, None] * inv_freq
    cos, sin = jnp.cos(ang), jnp.sin(ang)
    return jnp.concatenate([cos, cos, cos, cos], axis=1), jnp.concatenate([-sin, sin, -sin, sin], axis=1)


def _layer_fwd(l, x, memv, W, P, cos2, sin2):
    t = f"l{l}"
    sv = {"x0": x}
    sv["h1"] = _rms_fwd(x, P["ffn1_norm"], t + "_ffn1_norm")
    sv["a1"], sv["b1"], sv["s1"] = _ffn_up(sv["h1"], W["ffn1_w_gate"], W["ffn1_w_up"], t + "_ffn1_up")
    sv["x1"] = _mm([(sv["s1"], W["ffn1_w_down"])], nt=False, out_dtype=F32, res=x, res_scale=FFN_RES, name=t + "_ffn1_down")

    sv["h2"] = _rms_fwd(sv["x1"], P["mix_norm"], t + "_mix_norm")
    sv["q"], sv["k"], sv["v"], sv["u"] = _in_proj(sv["h2"], W["w_in"], cos2, sin2, t + "_in_proj")
    sv["oa"] = _swa_fwd(sv["q"], sv["k"], sv["v"], P["attn_sinks"], t + "_swa")
    sv["ob"], sv["mg"] = _pool_fwd(sv["u"], sv["oa"], P["pool_w"], P["pool_scale"], P["attn_out_norm"],
                                   P["pool_out_norm"], t + "_pool")
    sv["x2"] = _mm([(sv["mg"], W["w_out"])], nt=False, out_dtype=F32, res=sv["x1"], name=t + "_out_proj")

    sv["h3"] = _rms_fwd(sv["x2"], P["xattn_norm"], t + "_xattn_norm")
    sv["memn"] = _rms_fwd(memv, P["mem_norm"], t + "_mem_norm")
    sv["q3"] = _mm([(sv["h3"], W["xattn_wq"])], nt=False, out_dtype=MXU, name=t + "_xq")
    sv["kv"] = _mm([(sv["memn"], W["xattn_wkv"])], nt=True, out_dtype=MXU, name=t + "_xkv")
    sv["o3"] = _xattn_fwd(sv["q3"], sv["kv"], t + "_xattn")
    sv["x3"] = _mm([(sv["o3"], W["xattn_wo"])], nt=False, out_dtype=F32, res=sv["x2"], name=t + "_xo")

    sv["h4"] = _rms_fwd(sv["x3"], P["ffn2_norm"], t + "_ffn2_norm")
    sv["a2"], sv["b2"], sv["s2"] = _ffn_up(sv["h4"], W["ffn2_w_gate"], W["ffn2_w_up"], t + "_ffn2_up")
    x4 = _mm([(sv["s2"], W["ffn2_w_down"])], nt=False, out_dtype=F32, res=sv["x3"], res_scale=FFN_RES, name=t + "_ffn2_down")
    return x4, sv


def _ffn_bwd(t, dx, x_in, g, h, a, b, s, wgT, wuT, wd):
    d_wd = _mm_tn(s, dx, t + "_dwd", r_scale=FFN_RES)
    da, db = _ffn_mid_bwd(dx, wd, a, b, t + "_mid")
    d_wg = _mm_tn(da, h, t + "_dwg")
    d_wu = _mm_tn(db, h, t + "_dwu")
    dh = _mm([(da, wgT), (db, wuT)], nt=False, out_dtype=F32, name=t + "_dh", tm=TOKEN_TILE // 2)
    dx_in, dg = _rms_bwd(x_in, g, dh, t + "_norm_bwd", dres=dx)
    return dx_in, dg, d_wg, d_wu, d_wd


def _layer_bwd(l, dx, sv, memv, W, P, cos2, sin2):
    t = f"l{l}b"
    GW, GP = {}, {}
    dx, GP["ffn2_norm"], GW["ffn2_w_gate"], GW["ffn2_w_up"], GW["ffn2_w_down"] = _ffn_bwd(
        t + "_ffn2", dx, sv["x3"], P["ffn2_norm"], sv["h4"], sv["a2"], sv["b2"], sv["s2"],
        W["ffn2_w_gate"], W["ffn2_w_up"], W["ffn2_w_down"])

    GW["xattn_wo"] = _mm_tn(sv["o3"], dx, t + "_dwo")
    do3 = _mm([(dx, W["xattn_wo"])], nt=True, out_dtype=MXU, name=t + "_do3")
    dq3, dkv = _xattn_bwd(sv["q3"], sv["kv"], do3, t + "_xattn")
    GW["xattn_wq"] = _mm_tn(sv["h3"], dq3, t + "_dwq")
    dh3 = _mm([(dq3, W["xattn_wq"])], nt=True, out_dtype=F32, name=t + "_dh3")
    GW["xattn_wkv"] = _mm_tn(dkv, sv["memn"], t + "_dwkv")
    dmemn = _mm([(dkv, W["xattn_wkv"])], nt=False, out_dtype=F32, name=t + "_dmemn")
    _, GP["mem_norm"] = _rms_bwd(memv, P["mem_norm"], dmemn, t + "_mem_norm_bwd")
    dx, GP["xattn_norm"] = _rms_bwd(sv["x2"], P["xattn_norm"], dh3, t + "_xattn_norm_bwd", dres=dx)

    GW["w_out"] = _mm_tn(sv["mg"], dx, t + "_dwout")
    dmg = _mm([(dx, W["w_out"])], nt=True, out_dtype=F32, name=t + "_dmg")
    doa, GP["attn_out_norm"] = _rms_bwd(sv["oa"], P["attn_out_norm"], dmg, t + "_oa_norm_bwd", col=0)
    dob, GP["pool_out_norm"] = _rms_bwd(sv["ob"], P["pool_out_norm"], dmg, t + "_ob_norm_bwd", col=1)
    du, GP["pool_w"], GP["pool_scale"] = _pool_bwd(sv["u"], dob, P["pool_w"], P["pool_scale"], t + "_pool")
    dq, dko, dkp, dvo, dvp, dsk = _swa_bwd(sv["q"], sv["k"], sv["v"], doa, P["attn_sinks"], t + "_swa")
    GP["attn_sinks"] = dsk[:, 0]
    dpj = _dproj(dq, dko, dkp, dvo, dvp, du, cos2, sin2, t + "_dproj")
    GW["w_in"] = _mm_tn(dpj, sv["h2"], t + "_dwin")
    dh2 = _mm([(dpj, W["w_in"])], nt=False, out_dtype=F32, name=t + "_dh2")
    dx, GP["mix_norm"] = _rms_bwd(sv["x1"], P["mix_norm"], dh2, t + "_mix_norm_bwd", dres=dx)

    dx, GP["ffn1_norm"], GW["ffn1_w_gate"], GW["ffn1_w_up"], GW["ffn1_w_down"] = _ffn_bwd(
        t + "_ffn1", dx, sv["x0"], P["ffn1_norm"], sv["h1"], sv["a1"], sv["b1"], sv["s1"],
        W["ffn1_w_gate"], W["ffn1_w_up"], W["ffn1_w_down"])
    return dx, GW, GP


def _reduce_layer(l, GW, sizes):
    t = f"l{l}r"
    own, got = _rs_pair_exchange([GW[name] for name, _ in BIG], sizes, t + "_pair")
    sum_f32, sum_mxu = _pair_sum(own, got, t + "_pair_sum")
    mine, theirs = _rs_chip_exchange(sum_f32, sum_mxu, t + "_chips")
    red_half = _final_sum(mine, theirs, t + "_final_sum")
    return _unpack_layer(_rs_share_halves(red_half, t + "_share"), sizes)


def kernel(x, mem, positions, ffn1_norm, ffn1_w_gate, ffn1_w_up, ffn1_w_down, mix_norm, w_in, attn_sinks, pool_w, pool_scale, attn_out_norm, pool_out_norm, w_out, xattn_norm, mem_norm, xattn_wq, xattn_wkv, xattn_wo, ffn2_norm, ffn2_w_gate, ffn2_w_up, ffn2_w_down, final_norm, loss_target, m_ffn1_norm, m_ffn1_w_gate, m_ffn1_w_up, m_ffn1_w_down, m_mix_norm, m_w_in, m_attn_sinks, m_pool_w, m_pool_scale, m_attn_out_norm, m_pool_out_norm, m_w_out, m_xattn_norm, m_mem_norm, m_xattn_wq, m_xattn_wkv, m_xattn_wo, m_ffn2_norm, m_ffn2_w_gate, m_ffn2_w_up, m_ffn2_w_down, m_final_norm, v_ffn1_norm, v_ffn1_w_gate, v_ffn1_w_up, v_ffn1_w_down, v_mix_norm, v_w_in, v_attn_sinks, v_pool_w, v_pool_scale, v_attn_out_norm, v_pool_out_norm, v_w_out, v_xattn_norm, v_mem_norm, v_xattn_wq, v_xattn_wkv, v_xattn_wo, v_ffn2_norm, v_ffn2_w_gate, v_ffn2_w_up, v_ffn2_w_down, v_final_norm):
    ws = dict(ffn1_norm=ffn1_norm, ffn1_w_gate=ffn1_w_gate, ffn1_w_up=ffn1_w_up, ffn1_w_down=ffn1_w_down,
              mix_norm=mix_norm, w_in=w_in, attn_sinks=attn_sinks, pool_w=pool_w, pool_scale=pool_scale,
              attn_out_norm=attn_out_norm, pool_out_norm=pool_out_norm, w_out=w_out, xattn_norm=xattn_norm,
              mem_norm=mem_norm, xattn_wq=xattn_wq, xattn_wkv=xattn_wkv, xattn_wo=xattn_wo, ffn2_norm=ffn2_norm,
              ffn2_w_gate=ffn2_w_gate, ffn2_w_up=ffn2_w_up, ffn2_w_down=ffn2_w_down, final_norm=final_norm)
    ms = dict(ffn1_norm=m_ffn1_norm, ffn1_w_gate=m_ffn1_w_gate, ffn1_w_up=m_ffn1_w_up, ffn1_w_down=m_ffn1_w_down,
              mix_norm=m_mix_norm, w_in=m_w_in, attn_sinks=m_attn_sinks, pool_w=m_pool_w, pool_scale=m_pool_scale,
              attn_out_norm=m_attn_out_norm, pool_out_norm=m_pool_out_norm, w_out=m_w_out, xattn_norm=m_xattn_norm,
              mem_norm=m_mem_norm, xattn_wq=m_xattn_wq, xattn_wkv=m_xattn_wkv, xattn_wo=m_xattn_wo,
              ffn2_norm=m_ffn2_norm, ffn2_w_gate=m_ffn2_w_gate, ffn2_w_up=m_ffn2_w_up, ffn2_w_down=m_ffn2_w_down,
              final_norm=m_final_norm)
    vs = dict(ffn1_norm=v_ffn1_norm, ffn1_w_gate=v_ffn1_w_gate, ffn1_w_up=v_ffn1_w_up, ffn1_w_down=v_ffn1_w_down,
              mix_norm=v_mix_norm, w_in=v_w_in, attn_sinks=v_attn_sinks, pool_w=v_pool_w, pool_scale=v_pool_scale,
              attn_out_norm=v_attn_out_norm, pool_out_norm=v_pool_out_norm, w_out=v_w_out, xattn_norm=v_xattn_norm,
              mem_norm=v_mem_norm, xattn_wq=v_xattn_wq, xattn_wkv=v_xattn_wkv, xattn_wo=v_xattn_wo,
              ffn2_norm=v_ffn2_norm, ffn2_w_gate=v_ffn2_w_gate, ffn2_w_up=v_ffn2_w_up, ffn2_w_down=v_ffn2_w_down,
              final_norm=v_final_norm)
    depth = ffn1_norm.shape[0]
    T, D = x.shape[1], x.shape[2]
    xv = x.reshape(T, D)
    memv = mem.reshape(mem.shape[1], D)
    tgt = loss_target.reshape(T, D)
    cos2, sin2 = _rope_tables(positions.reshape(T))

    Ws, sizes = [], None
    for l in range(depth):
        pack, sizes = _pack_layer(ws, l)
        full = _allgather_weights(pack, sizes, f"l{l}_allgather")
        Ws.append({name: f for (name, _), f in zip(BIG, full)})
    Ps = [{n: (ws[n][l].reshape(1, -1) if n != "pool_w" else ws[n][l]) for n in SMALL if n != "final_norm"}
          for l in range(depth)]

    saved = []
    h = xv
    for l in range(depth):
        h, sv = _layer_fwd(l, h, memv, Ws[l], Ps[l], cos2, sin2)
        saved.append(sv)
    loss_row, dx, d_final = _loss_head(h, tgt, final_norm.reshape(1, D), "loss_head")
    GWs, GPs = [None] * depth, [None] * depth
    for l in reversed(range(depth)):
        dx, GWs[l], GPs[l] = _layer_bwd(l, dx, saved[l], memv, Ws[l], Ps[l], cos2, sin2)

    big_g = [_reduce_layer(l, GWs[l], sizes) for l in range(depth)]
    small_part = {n: jnp.stack([GPs[l][n].reshape(ws[n].shape[1:]) for l in range(depth)]) for n in SMALL if n != "final_norm"}
    small_part["final_norm"] = d_final.reshape(D)
    small_g = _unpack_small(_allreduce_small(_pack_small(small_part), "small_allreduce"), ws)
    loss = lax.psum(loss_row[0, 0], ("x", "y", "c"))

    grads, deltas, new_m, new_v = {}, {}, {}, {}
    for name, _ in BIG:
        g = jnp.stack([big_g[l][name] for l in range(depth)])
        shp = g.shape
        flat = lambda a: a.reshape(shp[0] * shp[1], shp[2])
        d, nm, nv = _adamw(flat(ws[name]), flat(g), flat(ms[name]), flat(vs[name]), "adamw_" + name)
        grads[name], deltas[name], new_m[name], new_v[name] = g, d.reshape(shp), nm.reshape(shp), nv.reshape(shp)
    d, nm, nv = _adamw(_pack_small(ws), _pack_small(small_g), _pack_small(ms), _pack_small(vs), "adamw_small")
    grads.update(small_g)
    deltas.update(_unpack_small(d, ws))
    new_m.update(_unpack_small(nm, ws))
    new_v.update(_unpack_small(nv, ws))

    grad_x = dx.reshape(x.shape)
    return (loss, grad_x, *[grads[n] for n in WEIGHTS], *[deltas[n] for n in WEIGHTS],
            *[new_m[n] for n in WEIGHTS], *[new_v[n] for n in WEIGHTS])
```

```python
import functools

import jax
import jax.numpy as jnp
from jax import lax
from jax.experimental import pallas as pl
from jax.experimental.pallas import tpu as pltpu

F32 = jnp.float32
MXU = jnp.bfloat16

EPS = 1e-6
HEAD_DIM = 64
N_Q_HEADS = 8
N_KV_HEADS = 2
Q_PER_KV = N_Q_HEADS // N_KV_HEADS
ATTN_WIDTH = N_Q_HEADS * HEAD_DIM
KV_WIDTH = N_KV_HEADS * HEAD_DIM
BLOCK = 128
ROPE_THETA = 10000.0
POOL_WINDOWS = (2, 4, 8, 16)
POOL_GROUP = 128
POOL_WIDTH = len(POOL_WINDOWS) * POOL_GROUP
POOL_HALO = 16
X_HEADS = 4
X_HEAD_DIM = 256
FFN_RES = 0.5
NEG = -1e30
ADAM_LR = 0.001
ADAM_B1 = 0.9
ADAM_B2 = 0.999
ADAM_EPS = 1e-08
ADAM_WD = 0.01
ADAM_STEP = 10

N_CHIPS = 4
N_DEV = 8
V7X_VMEM_BYTES = 64 * 1024 * 1024
VMEM_LIMIT = V7X_VMEM_BYTES - 8 * 1024 * 1024
LANES = 128
TOKEN_TILE = 512
MESH = pl.DeviceIdType.MESH

BIG = (("ffn1_w_gate", True), ("ffn1_w_up", True), ("ffn1_w_down", False), ("w_in", True), ("w_out", False),
       ("xattn_wq", False), ("xattn_wkv", True), ("xattn_wo", False),
       ("ffn2_w_gate", True), ("ffn2_w_up", True), ("ffn2_w_down", False))
SMALL = ("ffn1_norm", "mix_norm", "attn_sinks", "pool_w", "pool_scale", "attn_out_norm", "pool_out_norm",
         "xattn_norm", "mem_norm", "ffn2_norm", "final_norm")
WEIGHTS = ("ffn1_norm", "ffn1_w_gate", "ffn1_w_up", "ffn1_w_down", "mix_norm", "w_in", "attn_sinks", "pool_w",
           "pool_scale", "attn_out_norm", "pool_out_norm", "w_out", "xattn_norm", "mem_norm", "xattn_wq",
           "xattn_wkv", "xattn_wo", "ffn2_norm", "ffn2_w_gate", "ffn2_w_up", "ffn2_w_down", "final_norm")


def _S(shape, dtype):
    return jax.ShapeDtypeStruct(tuple(shape), dtype)


def _pcall(body, **kw):
    return pl.pallas_call(body, **kw)


def _cp(*sem):
    return pltpu.CompilerParams(dimension_semantics=tuple(sem), vmem_limit_bytes=VMEM_LIMIT)


def _nt(a, b):
    return lax.dot_general(a, b, (((1,), (1,)), ((), ())), preferred_element_type=F32)


def _nn(a, b):
    return lax.dot_general(a, b, (((1,), (0,)), ((), ())), preferred_element_type=F32)


def _tn(a, b):
    return lax.dot_general(a, b, (((0,), (0,)), ((), ())), preferred_element_type=F32)


def _tile(n, want):
    t = min(n, want)
    assert n % t == 0, (n, want)
    return t


def _resident(shape):
    nd = len(shape)
    return pl.BlockSpec(tuple(shape), lambda *_: (0,) * nd)


def _rms_fwd(x, g, name):
    T, C = x.shape
    tm = _tile(T, TOKEN_TILE)

    def body(x_ref, g_ref, o_ref):
        xv = x_ref[...]
        r = lax.rsqrt(jnp.mean(xv * xv, axis=-1, keepdims=True) + EPS)
        o_ref[...] = (xv * r * g_ref[...]).astype(o_ref.dtype)

    return _pcall(body, name=name, grid=(T // tm,),
                  in_specs=[pl.BlockSpec((tm, C), lambda i: (i, 0)), _resident((1, C))],
                  out_specs=pl.BlockSpec((tm, C), lambda i: (i, 0)),
                  out_shape=_S((T, C), MXU), compiler_params=_cp("parallel"))(x, g)


def _rms_bwd(x, g, dh, name, dres=None, col=0):
    T, C = x.shape
    tm = _tile(T, TOKEN_TILE)

    def body(*refs):
        if dres is None:
            x_ref, g_ref, dh_ref, dx_ref, dg_ref = refs
        else:
            x_ref, g_ref, dh_ref, dres_ref, dx_ref, dg_ref = refs
        xv = x_ref[...]
        r = lax.rsqrt(jnp.mean(xv * xv, axis=-1, keepdims=True) + EPS)
        xh = xv * r
        dhv = dh_ref[...].astype(F32)
        dxn = dhv * g_ref[...]
        dx = r * (dxn - xh * jnp.mean(dxn * xh, axis=-1, keepdims=True))
        if dres is not None:
            dx = dx + dres_ref[...]
        dx_ref[...] = dx

        @pl.when(pl.program_id(0) == 0)
        def _():
            dg_ref[...] = jnp.zeros_like(dg_ref)

        dg_ref[...] += jnp.sum(dhv * xh, axis=0, keepdims=True)

    tok = pl.BlockSpec((tm, C), lambda i: (i, 0))
    in_specs = [tok, _resident((1, C)), pl.BlockSpec((tm, C), lambda i: (i, col))]
    args = [x, g, dh]
    if dres is not None:
        in_specs.append(tok)
        args.append(dres)
    return _pcall(body, name=name, grid=(T // tm,), in_specs=in_specs,
                  out_specs=[tok, _resident((1, C))],
                  out_shape=[_S((T, C), F32), _S((1, C), F32)], compiler_params=_cp("arbitrary"))(*args)


def _mm(pairs, *, nt, out_dtype, name, res=None, res_scale=1.0, a_scale=1.0, tm=TOKEN_TILE):
    M = pairs[0][0].shape[0]
    N = pairs[0][1].shape[0] if nt else pairs[0][1].shape[1]
    tm = _tile(M, tm)
    n = len(pairs)

    def body(*refs):
        a_refs, w_refs = refs[:n], refs[n:2 * n]
        o_ref = refs[-1]
        acc = None
        for a_ref, w_ref in zip(a_refs, w_refs):
            a = a_ref[...]
            if a_scale != 1.0:
                a = a * a_scale
            a = a.astype(MXU)
            p = _nt(a, w_ref[...]) if nt else _nn(a, w_ref[...])
            acc = p if acc is None else acc + p
        if res is not None:
            acc = refs[2 * n][...] + res_scale * acc
        o_ref[...] = acc.astype(o_ref.dtype)

    in_specs = [pl.BlockSpec((tm, a.shape[1]), lambda i: (i, 0)) for a, _ in pairs]
    in_specs += [_resident(w.shape) for _, w in pairs]
    args = [a for a, _ in pairs] + [w for _, w in pairs]
    if res is not None:
        in_specs.append(pl.BlockSpec((tm, N), lambda i: (i, 0)))
        args.append(res)
    return _pcall(body, name=name, grid=(M // tm,), in_specs=in_specs,
                  out_specs=pl.BlockSpec((tm, N), lambda i: (i, 0)),
                  out_shape=_S((M, N), out_dtype), compiler_params=_cp("parallel"))(*args)


def _mm_tn(l, r, name, *, l_scale=1.0, r_scale=1.0, tr=1408, tt=TOKEN_TILE):
    T, R = l.shape
    C = r.shape[1]
    tt = _tile(T, tt)
    tr = tr if R % tr == 0 else (1024 if R % 1024 == 0 and R > 1280 else R)

    def body(l_ref, r_ref, o_ref):
        lv, rv = l_ref[...], r_ref[...]
        if l_scale != 1.0:
            lv = lv * l_scale
        if r_scale != 1.0:
            rv = rv * r_scale
        lv, rv = lv.astype(MXU), rv.astype(MXU)

        @pl.when(pl.program_id(1) == 0)
        def _():
            o_ref[...] = jnp.zeros_like(o_ref)

        o_ref[...] += _tn(lv, rv)

    return _pcall(body, name=name, grid=(R // tr, T // tt),
                  in_specs=[pl.BlockSpec((tt, tr), lambda i, t: (t, i)), pl.BlockSpec((tt, C), lambda i, t: (t, 0))],
                  out_specs=pl.BlockSpec((tr, C), lambda i, t: (i, 0)),
                  out_shape=_S((R, C), F32), compiler_params=_cp("parallel", "arbitrary"))(l, r)


FFN_COL_TILE = 1408


def _ffn_up(h, wgT, wuT, name):
    T, D = h.shape
    Fd = wgT.shape[0]
    tm, tn = _tile(T, TOKEN_TILE), _tile(Fd, FFN_COL_TILE)

    def body(h_ref, wg_ref, wu_ref, a_ref, b_ref, s_ref):
        hv = h_ref[...]
        a = _nt(hv, wg_ref[...])
        b = _nt(hv, wu_ref[...])
        s = a * (1.0 / (1.0 + jnp.exp(-a))) * b
        a_ref[...] = a.astype(a_ref.dtype)
        b_ref[...] = b.astype(b_ref.dtype)
        s_ref[...] = s.astype(s_ref.dtype)

    wspec = pl.BlockSpec((tn, D), lambda j, i: (j, 0))
    ospec = pl.BlockSpec((tm, tn), lambda j, i: (i, j))
    return _pcall(body, name=name, grid=(Fd // tn, T // tm),
                  in_specs=[pl.BlockSpec((tm, D), lambda j, i: (i, 0)), wspec, wspec],
                  out_specs=[ospec, ospec, ospec], out_shape=[_S((T, Fd), MXU)] * 3,
                  compiler_params=_cp("parallel", "parallel"))(h, wgT, wuT)


def _ffn_mid_bwd(dx, wd, a, b, name):
    T, D = dx.shape
    Fd = wd.shape[0]
    tm, tn = _tile(T, TOKEN_TILE), _tile(Fd, FFN_COL_TILE)

    def body(dx_ref, wd_ref, a_ref, b_ref, da_ref, db_ref):
        dy = (dx_ref[...] * FFN_RES).astype(MXU)
        ds = _nt(dy, wd_ref[...])
        av, bv = a_ref[...].astype(F32), b_ref[...].astype(F32)
        sg = 1.0 / (1.0 + jnp.exp(-av))
        da_ref[...] = (ds * bv * (sg * (1.0 + av * (1.0 - sg)))).astype(da_ref.dtype)
        db_ref[...] = (ds * (av * sg)).astype(db_ref.dtype)

    aspec = pl.BlockSpec((tm, tn), lambda j, i: (i, j))
    return _pcall(body, name=name, grid=(Fd // tn, T // tm),
                  in_specs=[pl.BlockSpec((tm, D), lambda j, i: (i, 0)), pl.BlockSpec((tn, D), lambda j, i: (j, 0)),
                            aspec, aspec],
                  out_specs=[aspec, aspec], out_shape=[_S((T, Fd), MXU)] * 2,
                  compiler_params=_cp("parallel", "parallel"))(dx, wd, a, b)


def _swap_halves(t):
    w = t.shape[1]
    lane = lax.broadcasted_iota(jnp.int32, t.shape, 1)
    first = (lane % HEAD_DIM) < (HEAD_DIM // 2)
    return jnp.where(first, pltpu.roll(t, w - HEAD_DIM // 2, 1), pltpu.roll(t, HEAD_DIM // 2, 1))


def _rope(t, cos2, sin2):
    reps = t.shape[1] // LANES
    c = jnp.tile(cos2, (1, reps)) if reps > 1 else cos2
    s = jnp.tile(sin2, (1, reps)) if reps > 1 else sin2
    return t * c + _swap_halves(t) * s


def _rope_bwd(dt, cos2, sin2):
    reps = dt.shape[1] // LANES
    c = jnp.tile(cos2, (1, reps)) if reps > 1 else cos2
    s = jnp.tile(sin2, (1, reps)) if reps > 1 else sin2
    return dt * c + _swap_halves(dt * s)


def _in_proj(h, winT, cos2, sin2, name):
    T, D = h.shape
    tm = _tile(T, TOKEN_TILE)
    qe, ke, ve = ATTN_WIDTH, ATTN_WIDTH + KV_WIDTH, ATTN_WIDTH + 2 * KV_WIDTH

    def body(h_ref, w_ref, c_ref, s_ref, q_ref, k_ref, v_ref, u_ref):
        proj = _nt(h_ref[...], w_ref[...])
        cv, sv = c_ref[...], s_ref[...]
        q_ref[...] = _rope(proj[:, :qe], cv, sv).astype(q_ref.dtype)
        k_ref[...] = _rope(proj[:, qe:ke], cv, sv).astype(k_ref.dtype)
        v_ref[...] = proj[:, ke:ve].astype(v_ref.dtype)
        u_ref[...] = proj[:, ve:]

    def tok(w):
        return pl.BlockSpec((tm, w), lambda i: (i, 0))

    return _pcall(body, name=name, grid=(T // tm,),
                  in_specs=[tok(D), _resident(winT.shape), tok(LANES), tok(LANES)],
                  out_specs=[tok(ATTN_WIDTH), tok(KV_WIDTH), tok(KV_WIDTH), tok(POOL_WIDTH)],
                  out_shape=[_S((T, ATTN_WIDTH), MXU), _S((T, KV_WIDTH), MXU), _S((T, KV_WIDTH), MXU),
                             _S((T, POOL_WIDTH), F32)],
                  compiler_params=_cp("parallel"))(h, winT, cos2, sin2)


def _swa_mask(n):
    rows = Q_PER_KV * BLOCK
    qi = lax.broadcasted_iota(jnp.int32, (rows, 2 * BLOCK), 0) % BLOCK
    kj = lax.broadcasted_iota(jnp.int32, (rows, 2 * BLOCK), 1)
    diff = qi + BLOCK - kj
    return (diff >= 0) & (diff < BLOCK) & ((n - 1) * BLOCK + kj >= 0)


def _swa_probs(qs, kh, sink_col, mask):
    s = _nt(qs, kh) * (HEAD_DIM ** -0.5)
    s = jnp.where(mask, s, NEG)
    m = jnp.maximum(jnp.max(s, axis=1, keepdims=True), sink_col)
    e = jnp.exp(s - m)
    es = jnp.exp(sink_col - m)
    inv = 1.0 / (jnp.sum(e, axis=1, keepdims=True) + es)
    return e * inv, es * inv


def _sink_column(sinks_ref, kv):
    return jnp.concatenate([jnp.full((BLOCK, 1), sinks_ref[0, kv * Q_PER_KV + g], F32) for g in range(Q_PER_KV)], axis=0)


def _stack_heads(t, kv):
    return jnp.concatenate([t[:, (kv * Q_PER_KV + g) * HEAD_DIM:(kv * Q_PER_KV + g + 1) * HEAD_DIM]
                            for g in range(Q_PER_KV)], axis=0)


def _swa_specs(T):
    nb = T // BLOCK
    cur = lambda w: pl.BlockSpec((BLOCK, w), lambda n: (n, 0))
    prev = lambda w: pl.BlockSpec((BLOCK, w), lambda n: (jnp.maximum(n - 1, 0), 0))
    return nb, cur, prev


def _swa_fwd(q, k, v, sinks, name):
    T = q.shape[0]
    nb, cur, prev = _swa_specs(T)

    def body(sinks_ref, q_ref, kc_ref, kp_ref, vc_ref, vp_ref, o_ref):
        mask = _swa_mask(pl.program_id(0))
        qv = q_ref[...]
        kk = jnp.concatenate([kp_ref[...], kc_ref[...]], axis=0)
        vv = jnp.concatenate([vp_ref[...], vc_ref[...]], axis=0)
        for kv in range(N_KV_HEADS):
            hs = slice(kv * HEAD_DIM, (kv + 1) * HEAD_DIM)
            p, _ = _swa_probs(_stack_heads(qv, kv), kk[:, hs], _sink_column(sinks_ref, kv), mask)
            o = _nn(p.astype(MXU), vv[:, hs])
            for g in range(Q_PER_KV):
                c0 = (kv * Q_PER_KV + g) * HEAD_DIM
                o_ref[:, c0:c0 + HEAD_DIM] = o[g * BLOCK:(g + 1) * BLOCK, :]

    return _pcall(body, name=name, grid=(nb,),
                  in_specs=[pl.BlockSpec(memory_space=pltpu.SMEM), cur(ATTN_WIDTH), cur(KV_WIDTH), prev(KV_WIDTH),
                            cur(KV_WIDTH), prev(KV_WIDTH)],
                  out_specs=cur(ATTN_WIDTH), out_shape=_S((T, ATTN_WIDTH), F32),
                  compiler_params=_cp("parallel"))(sinks, q, k, k, v, v)


def _swa_bwd(q, k, v, do, sinks, name):
    T = q.shape[0]
    nb, cur, prev = _swa_specs(T)

    def body(sinks_ref, q_ref, kc_ref, kp_ref, vc_ref, vp_ref, do_ref,
             dq_ref, dko_ref, dkp_ref, dvo_ref, dvp_ref, dsk_ref):
        mask = _swa_mask(pl.program_id(0))
        qv, dov = q_ref[...], do_ref[...].astype(MXU)
        kk = jnp.concatenate([kp_ref[...], kc_ref[...]], axis=0)
        vv = jnp.concatenate([vp_ref[...], vc_ref[...]], axis=0)

        @pl.when(pl.program_id(0) == 0)
        def _():
            dsk_ref[...] = jnp.zeros_like(dsk_ref)

        for kv in range(N_KV_HEADS):
            hs = slice(kv * HEAD_DIM, (kv + 1) * HEAD_DIM)
            qs, dos = _stack_heads(qv, kv), _stack_heads(dov, kv)
            p, ps = _swa_probs(qs, kk[:, hs], _sink_column(sinks_ref, kv), mask)
            dp = _nt(dos, vv[:, hs])
            delta = jnp.sum(p * dp, axis=1, keepdims=True)
            ds = (p * (dp - delta) * (HEAD_DIM ** -0.5)).astype(MXU)
            dq = _nn(ds, kk[:, hs])
            dk = _tn(ds, qs)
            dv = _tn(p.astype(MXU), dos)
            dsink = -ps * delta
            for g in range(Q_PER_KV):
                h = kv * Q_PER_KV + g
                dq_ref[:, h * HEAD_DIM:(h + 1) * HEAD_DIM] = dq[g * BLOCK:(g + 1) * BLOCK, :]
                dsk_ref[h:h + 1, :] += jnp.broadcast_to(
                    jnp.sum(dsink[g * BLOCK:(g + 1) * BLOCK, :], axis=0, keepdims=True), (1, LANES))
            dkp_ref[:, hs] = dk[:BLOCK, :]
            dko_ref[:, hs] = dk[BLOCK:, :]
            dvp_ref[:, hs] = dv[:BLOCK, :]
            dvo_ref[:, hs] = dv[BLOCK:, :]

    kvs = _S((T, KV_WIDTH), F32)
    return _pcall(body, name=name, grid=(nb,),
                  in_specs=[pl.BlockSpec(memory_space=pltpu.SMEM), cur(ATTN_WIDTH), cur(KV_WIDTH), prev(KV_WIDTH),
                            cur(KV_WIDTH), prev(KV_WIDTH), cur(ATTN_WIDTH)],
                  out_specs=[cur(ATTN_WIDTH), cur(KV_WIDTH), cur(KV_WIDTH), cur(KV_WIDTH), cur(KV_WIDTH),
                             _resident((N_Q_HEADS, LANES))],
                  out_shape=[_S((T, ATTN_WIDTH), F32), kvs, kvs, kvs, kvs, _S((N_Q_HEADS, LANES), F32)],
                  compiler_params=_cp("arbitrary"))(sinks, q, k, k, v, v, do)


def _dproj(dq, dko, dkp, dvo, dvp, du, cos2, sin2, name):
    T = dq.shape[0]
    nb = T // BLOCK
    cur = lambda w: pl.BlockSpec((BLOCK, w), lambda n: (n, 0))
    nxt = lambda w: pl.BlockSpec((BLOCK, w), lambda n: (jnp.minimum(n + 1, nb - 1), 0))

    def body(dq_ref, dko_ref, dkp_ref, dvo_ref, dvp_ref, du_ref, c_ref, s_ref, o_ref):
        more = (pl.program_id(0) < nb - 1).astype(F32)
        cv, sv = c_ref[...], s_ref[...]
        dk = dko_ref[...] + more * dkp_ref[...]
        dv = dvo_ref[...] + more * dvp_ref[...]
        o_ref[...] = jnp.concatenate(
            [_rope_bwd(dq_ref[...], cv, sv), _rope_bwd(dk, cv, sv), dv, du_ref[...]], axis=1).astype(o_ref.dtype)

    width = ATTN_WIDTH + 2 * KV_WIDTH + POOL_WIDTH
    return _pcall(body, name=name, grid=(nb,),
                  in_specs=[cur(ATTN_WIDTH), cur(KV_WIDTH), nxt(KV_WIDTH), cur(KV_WIDTH), nxt(KV_WIDTH),
                            cur(POOL_WIDTH), cur(LANES), cur(LANES)],
                  out_specs=cur(width), out_shape=_S((T, width), MXU),
                  compiler_params=_cp("parallel"))(dq, dko, dkp, dvo, dvp, du, cos2, sin2)


def _pool_specs(T):
    tm = _tile(T, TOKEN_TILE)
    hb = tm // POOL_HALO
    nh = T // POOL_HALO
    tok = lambda w: pl.BlockSpec((tm, w), lambda i: (i, 0))
    before = pl.BlockSpec((POOL_HALO, POOL_WIDTH), lambda i: (jnp.maximum(i * hb - 1, 0), 0))
    after = pl.BlockSpec((POOL_HALO, POOL_WIDTH), lambda i: (jnp.minimum((i + 1) * hb, nh - 1), 0))
    return tm, tok, before, after


def _window_counts(i, tm, rows, w):
    t = i * tm + lax.broadcasted_iota(jnp.int32, (rows, 1), 0)
    return jnp.minimum(t + 1, w).astype(F32)


def _pooled(u_ext, i, tm):
    out = []
    for g, w in enumerate(POOL_WINDOWS):
        acc = u_ext[:, g * POOL_GROUP:(g + 1) * POOL_GROUP]
        tok = acc[POOL_HALO:, :]
        sh = 1
        while sh < w:
            acc = acc + pltpu.roll(acc, sh, 0)
            sh *= 2
        out.append(acc[POOL_HALO:, :] / _window_counts(i, tm, tm, w) - tok)
    return out


def _pool_fwd(u, out_a, pool_w, pool_scale, ga, gb, name):
    T = u.shape[0]
    tm, tok, before, _ = _pool_specs(T)

    def body(u_ref, halo_ref, oa_ref, pw_ref, sc_ref, ga_ref, gb_ref, ob_ref, mg_ref):
        i = pl.program_id(0)
        halo = halo_ref[...] * (i > 0).astype(F32)
        pooled = _pooled(jnp.concatenate([halo, u_ref[...]], axis=0), i, tm)
        mixed = [_nn(pooled[g].astype(MXU), pw_ref[g].astype(MXU)) for g in range(len(POOL_WINDOWS))]
        ob = jnp.concatenate(mixed, axis=1) * sc_ref[...]
        ob_ref[...] = ob
        oa = oa_ref[...]
        ra = lax.rsqrt(jnp.mean(oa * oa, axis=-1, keepdims=True) + EPS)
        rb = lax.rsqrt(jnp.mean(ob * ob, axis=-1, keepdims=True) + EPS)
        mg_ref[...] = jnp.concatenate([oa * ra * ga_ref[...], ob * rb * gb_ref[...]], axis=1).astype(mg_ref.dtype)

    vec = _resident((1, POOL_WIDTH))
    return _pcall(body, name=name, grid=(T // tm,),
                  in_specs=[tok(POOL_WIDTH), before, tok(ATTN_WIDTH), _resident(pool_w.shape), vec, vec, vec],
                  out_specs=[tok(POOL_WIDTH), tok(ATTN_WIDTH + POOL_WIDTH)],
                  out_shape=[_S((T, POOL_WIDTH), F32), _S((T, ATTN_WIDTH + POOL_WIDTH), MXU)],
                  compiler_params=_cp("parallel"))(u, u, out_a, pool_w, pool_scale, ga, gb)


def _pool_bwd(u, dob, pool_w, pool_scale, name):
    T = u.shape[0]
    tm, tok, before, after = _pool_specs(T)
    nt = T // tm
    G = len(POOL_WINDOWS)

    def body(u_ref, halo_ref, dob_ref, dnext_ref, pw_ref, sc_ref, du_ref, dpw_ref, dsc_ref):
        i = pl.program_id(0)
        halo = halo_ref[...] * (i > 0).astype(F32)
        pooled = _pooled(jnp.concatenate([halo, u_ref[...]], axis=0), i, tm)
        dnext = dnext_ref[...] * (i < nt - 1).astype(F32)
        dext = jnp.concatenate([dob_ref[...], dnext], axis=0) * sc_ref[...]

        @pl.when(i == 0)
        def _():
            dpw_ref[...] = jnp.zeros_like(dpw_ref)
            dsc_ref[...] = jnp.zeros_like(dsc_ref)

        dus, dscs = [], []
        for g, w in enumerate(POOL_WINDOWS):
            gs = slice(g * POOL_GROUP, (g + 1) * POOL_GROUP)
            pw = pw_ref[g].astype(MXU)
            pg = pooled[g].astype(MXU)
            dmix = dext[:, gs].astype(MXU)
            dscs.append(jnp.sum(dob_ref[:, gs] * _nn(pg, pw), axis=0, keepdims=True))
            dpw_ref[g] += _tn(pg, dmix[:tm, :])
            dpooled = _nt(dmix, pw)
            acc = dpooled / _window_counts(i, tm, tm + POOL_HALO, w)
            sh = 1
            while sh < w:
                acc = acc + pltpu.roll(acc, tm + POOL_HALO - sh, 0)
                sh *= 2
            dus.append(acc[:tm, :] - dpooled[:tm, :])
        du_ref[...] = jnp.concatenate(dus, axis=1)
        dsc_ref[...] += jnp.concatenate(dscs, axis=1)

    vec = _resident((1, POOL_WIDTH))
    return _pcall(body, name=name, grid=(nt,),
                  in_specs=[tok(POOL_WIDTH), before, tok(POOL_WIDTH), after, _resident(pool_w.shape), vec],
                  out_specs=[tok(POOL_WIDTH), _resident(pool_w.shape), vec],
                  out_shape=[_S((T, POOL_WIDTH), F32), _S(pool_w.shape, F32), _S((1, POOL_WIDTH), F32)],
                  compiler_params=_cp("arbitrary"))(u, u, dob, dob, pool_w, pool_scale)


def _xattn_probs(qh, kh):
    s = _nt(qh, kh) * (X_HEAD_DIM ** -0.5)
    e = jnp.exp(s - jnp.max(s, axis=1, keepdims=True))
    return e / jnp.sum(e, axis=1, keepdims=True)


def _xattn_fwd(q, kvm, name):
    T, XW = q.shape
    tm = _tile(T, TOKEN_TILE)

    def body(q_ref, kv_ref, o_ref):
        for h in range(X_HEADS):
            hs = slice(h * X_HEAD_DIM, (h + 1) * X_HEAD_DIM)
            vs = slice(XW + h * X_HEAD_DIM, XW + (h + 1) * X_HEAD_DIM)
            p = _xattn_probs(q_ref[:, hs], kv_ref[:, hs])
            o_ref[:, hs] = _nn(p.astype(MXU), kv_ref[:, vs]).astype(o_ref.dtype)

    return _pcall(body, name=name, grid=(T // tm,),
                  in_specs=[pl.BlockSpec((tm, XW), lambda i: (i, 0)), _resident(kvm.shape)],
                  out_specs=pl.BlockSpec((tm, XW), lambda i: (i, 0)), out_shape=_S((T, XW), MXU),
                  compiler_params=_cp("parallel"))(q, kvm)


def _xattn_bwd(q, kvm, do, name):
    T, XW = q.shape
    tm = _tile(T, TOKEN_TILE)

    def body(q_ref, kv_ref, do_ref, dq_ref, dkv_ref):
        @pl.when(pl.program_id(0) == 0)
        def _():
            dkv_ref[...] = jnp.zeros_like(dkv_ref)

        for h in range(X_HEADS):
            hs = slice(h * X_HEAD_DIM, (h + 1) * X_HEAD_DIM)
            vs = slice(XW + h * X_HEAD_DIM, XW + (h + 1) * X_HEAD_DIM)
            qh, doh = q_ref[:, hs], do_ref[:, hs]
            p = _xattn_probs(qh, kv_ref[:, hs])
            dp = _nt(doh, kv_ref[:, vs])
            ds = (p * (dp - jnp.sum(p * dp, axis=1, keepdims=True)) * (X_HEAD_DIM ** -0.5)).astype(MXU)
            dq_ref[:, hs] = _nn(ds, kv_ref[:, hs]).astype(dq_ref.dtype)
            dkv_ref[:, hs] += _tn(ds, qh)
            dkv_ref[:, vs] += _tn(p.astype(MXU), doh)

    tok = pl.BlockSpec((tm, XW), lambda i: (i, 0))
    return _pcall(body, name=name, grid=(T // tm,),
                  in_specs=[tok, _resident(kvm.shape), tok],
                  out_specs=[tok, _resident(kvm.shape)],
                  out_shape=[_S((T, XW), MXU), _S(kvm.shape, F32)],
                  compiler_params=_cp("arbitrary"))(q, kvm, do)


def _loss_head(x, tgt, g, name):
    T, D = x.shape
    tm = _tile(T, TOKEN_TILE)

    def body(x_ref, t_ref, g_ref, loss_ref, dx_ref, dg_ref):
        xv, gv = x_ref[...], g_ref[...]
        r = lax.rsqrt(jnp.mean(xv * xv, axis=-1, keepdims=True) + EPS)
        xh = xv * r
        e = xh * gv - t_ref[...]
        dy = e * (1.0 / D)
        dxn = dy * gv
        dx_ref[...] = r * (dxn - xh * jnp.mean(dxn * xh, axis=-1, keepdims=True))

        @pl.when(pl.program_id(0) == 0)
        def _():
            loss_ref[...] = jnp.zeros_like(loss_ref)
            dg_ref[...] = jnp.zeros_like(dg_ref)

        part = jnp.sum(jnp.sum(e * e, axis=1, keepdims=True), axis=0, keepdims=True) * (0.5 / D)
        loss_ref[...] += jnp.broadcast_to(part, (1, LANES))
        dg_ref[...] += jnp.sum(dy * xh, axis=0, keepdims=True)

    tok = pl.BlockSpec((tm, D), lambda i: (i, 0))
    return _pcall(body, name=name, grid=(T // tm,),
                  in_specs=[tok, tok, _resident((1, D))],
                  out_specs=[_resident((1, LANES)), tok, _resident((1, D))],
                  out_shape=[_S((1, LANES), F32), _S((T, D), F32), _S((1, D), F32)],
                  compiler_params=_cp("arbitrary"))(x, tgt, g)


def _rows_tile(rows):
    for t in (512, 416, 352, 256, 128, 64, 32, 16, 8):
        if rows % t == 0:
            return t
    return rows


def _pair_sum(grads, gots, sizes, place, name):
    nw = len(sizes)
    C = grads[0].shape[1]

    def body(p_ref, *refs):
        g, got, out = refs[:nw], refs[nw:2 * nw], refs[2 * nw:]
        for w in range(nw):
            out[w][...] = (g[w][...] + got[w][...]).astype(out[w].dtype)

    def blk(w):
        return (sizes[w] // 4, C)

    in_specs = [pl.BlockSpec(blk(w), lambda q, s, p: (4 * q + 2 * p[0] + s, 0)) for w in range(nw)]
    in_specs += [pl.BlockSpec(blk(w), lambda q, s, p: (2 * q + s, 0)) for w in range(nw)]
    out_specs = [pl.BlockSpec(blk(w), lambda q, s, p: (2 * q + s, 0)) for w in range(nw)]
    gs = pltpu.PrefetchScalarGridSpec(num_scalar_prefetch=1, grid=(N_CHIPS, 2), in_specs=in_specs, out_specs=out_specs)
    return _pcall(body, name=name, grid_spec=gs, out_shape=[_S((2 * n, C), MXU) for n in sizes],
                  compiler_params=_cp("parallel", "parallel"))(place, *grads, *gots)


def _final_sum(grads, gots, recvs, sizes, place, name):
    nw = len(sizes)
    C = grads[0].shape[1]

    def body(p_ref, *refs):
        g, got, rv, out = refs[:nw], refs[nw:2 * nw], refs[2 * nw:5 * nw], refs[5 * nw:]
        for w in range(nw):
            acc = g[w][...] + got[w][...]
            for j in range(3):
                acc = acc + rv[3 * w + j][...].astype(F32)
            out[w][...] = acc

    def blk(w):
        return (sizes[w] // 4, C)

    in_specs = [pl.BlockSpec(blk(w), lambda s, p: (4 * p[1] + 2 * p[0] + s, 0)) for w in range(nw)]
    in_specs += [pl.BlockSpec(blk(w), lambda s, p: (2 * p[1] + s, 0)) for w in range(nw)]
    args = list(grads) + list(gots)
    for w in range(nw):
        for j in range(3):
            in_specs.append(pl.BlockSpec(blk(w), lambda s, p, j=j: (2 * j + s, 0)))
            args.append(recvs[w])
    out_specs = [pl.BlockSpec(blk(w), lambda s, p: (2 * p[0] + s, 0)) for w in range(nw)]
    gs = pltpu.PrefetchScalarGridSpec(num_scalar_prefetch=1, grid=(2,), in_specs=in_specs, out_specs=out_specs)
    return _pcall(body, name=name, grid_spec=gs, out_shape=[_S((n, C), F32) for n in sizes],
                  compiler_params=_cp("parallel"))(place, *args)


def _adamw(w, g, m, v, name):
    R, C = w.shape
    tr = _rows_tile(R)
    c1 = 1.0 / (1.0 - ADAM_B1 ** ADAM_STEP)
    c2 = 1.0 / (1.0 - ADAM_B2 ** ADAM_STEP)

    def body(w_ref, g_ref, m_ref, v_ref, d_ref, nm_ref, nv_ref):
        gv = g_ref[...]
        nm = ADAM_B1 * m_ref[...] + (1.0 - ADAM_B1) * gv
        nv = ADAM_B2 * v_ref[...] + (1.0 - ADAM_B2) * (gv * gv)
        d_ref[...] = -ADAM_LR * ((nm * c1) / (jnp.sqrt(nv * c2) + ADAM_EPS) + ADAM_WD * w_ref[...])
        nm_ref[...] = nm
        nv_ref[...] = nv

    spec = pl.BlockSpec((tr, C), lambda i: (i, 0))
    return _pcall(body, name=name, grid=(R // tr,), in_specs=[spec] * 4, out_specs=[spec] * 3,
                  out_shape=[_S((R, C), F32)] * 3, compiler_params=_cp("parallel"))(w, g, m, v)


ANY = pl.BlockSpec(memory_space=pl.ANY)


def _place():
    x, y, c = lax.axis_index("x"), lax.axis_index("y"), lax.axis_index("c")
    chips = [(1 - x, y), (x, 1 - y), (1 - x, 1 - y)]
    return x, y, c, chips


def _remote(src, dst, send_sem, recv_sem, dev):
    return pltpu.make_async_remote_copy(src_ref=src, dst_ref=dst, send_sem=send_sem, recv_sem=recv_sem,
                                        device_id=dev, device_id_type=MESH)


def _drain(like, send_sem, recv_sem, me, *, send=False, recv=False):
    d = _remote(like, like, send_sem, recv_sem, me)
    if send:
        d.wait_send()
    if recv:
        d.wait_recv()


def _dma_sems(n):
    return [pltpu.SemaphoreType.DMA((n,)), pltpu.SemaphoreType.DMA((n,))]


def _comm_params():
    return pltpu.CompilerParams(has_side_effects=True)


def _allgather_weights(bufs, sizes, name):
    nw = len(sizes)

    def body(*refs):
        out = refs[nw:2 * nw]
        s_ici, r_ici, s_fwd, r_fwd = refs[2 * nw:]
        x, y, c, chips = _place()
        me, sib = (x, y, c), (x, y, 1 - c)
        q_me = 2 * x + y

        def rows(w, q):
            hw = sizes[w] // 2
            return out[w].at[pl.ds(q * sizes[w] + c * hw, hw)]

        def three(w):
            return out[w].at[pl.ds(0, 3 * (sizes[w] // 2))]

        for w in range(nw):
            for px, py in chips:
                _remote(rows(w, q_me), rows(w, q_me), s_ici.at[w], r_ici.at[w], (px, py, c)).start()
        for w in range(nw):
            _drain(three(w), s_ici.at[w], r_ici.at[w], me, recv=True)
            for px, py in chips:
                got = rows(w, 2 * px + py)
                _remote(got, got, s_fwd.at[w], r_fwd.at[w], sib).start()
        for w in range(nw):
            _drain(three(w), s_fwd.at[w], r_fwd.at[w], me, recv=True)
        for w in range(nw):
            _drain(three(w), s_ici.at[w], r_ici.at[w], me, send=True)
            _drain(three(w), s_fwd.at[w], r_fwd.at[w], me, send=True)

    return _pcall(body, name=name, in_specs=[ANY] * nw, out_specs=[ANY] * nw,
                  out_shape=[_S(b.shape, b.dtype) for b in bufs],
                  input_output_aliases={w: w for w in range(nw)},
                  scratch_shapes=_dma_sems(nw) + _dma_sems(nw), compiler_params=_comm_params())(*bufs)


def _rs_pair_exchange(grads, sizes, name):
    C = grads[0].shape[1]
    nw = len(sizes)

    def body(*refs):
        g, got = refs[:nw], refs[nw:2 * nw]
        s_sem, r_sem = refs[2 * nw:]
        x, y, c, _ = _place()
        me, sib = (x, y, c), (x, y, 1 - c)
        for w in range(nw):
            hw = sizes[w] // 2
            for q in range(N_CHIPS):
                _remote(g[w].at[pl.ds(q * sizes[w] + (1 - c) * hw, hw)], got[w].at[pl.ds(q * hw, hw)],
                        s_sem.at[w], r_sem.at[w], sib).start()
        for w in range(nw):
            _drain(got[w], s_sem.at[w], r_sem.at[w], me, send=True, recv=True)

    return _pcall(body, name=name, in_specs=[ANY] * nw, out_specs=[ANY] * nw,
                  out_shape=[_S((2 * n, C), F32) for n in sizes],
                  scratch_shapes=_dma_sems(nw), compiler_params=_comm_params())(*grads)


def _rs_chip_exchange(sums, sizes, name):
    C = sums[0].shape[1]
    nw = len(sizes)

    def body(*refs):
        sm, got = refs[:nw], refs[nw:2 * nw]
        s_sem, r_sem = refs[2 * nw:]
        x, y, c, chips = _place()
        for w in range(nw):
            hw = sizes[w] // 2
            for j, (px, py) in enumerate(chips):
                _remote(sm[w].at[pl.ds((2 * px + py) * hw, hw)], got[w].at[pl.ds(j * hw, hw)],
                        s_sem.at[w], r_sem.at[w], (px, py, c)).start()
        for w in range(nw):
            _drain(got[w], s_sem.at[w], r_sem.at[w], (x, y, c), send=True, recv=True)

    return _pcall(body, name=name, in_specs=[ANY] * nw, out_specs=[ANY] * nw,
                  out_shape=[_S((3 * (n // 2), C), sums[0].dtype) for n in sizes],
                  scratch_shapes=_dma_sems(nw), compiler_params=_comm_params())(*sums)


def _rs_share_halves(reds, sizes, name):
    nw = len(sizes)

    def body(*refs):
        out = refs[nw:2 * nw]
        s_sem, r_sem = refs[2 * nw:]
        x, y, c, _ = _place()
        for w in range(nw):
            hw = sizes[w] // 2
            rows = out[w].at[pl.ds(c * hw, hw)]
            _remote(rows, rows, s_sem.at[w], r_sem.at[w], (x, y, 1 - c)).start()
        for w in range(nw):
            _drain(out[w].at[pl.ds(0, sizes[w] // 2)], s_sem.at[w], r_sem.at[w], (x, y, c), send=True, recv=True)

    return _pcall(body, name=name, in_specs=[ANY] * nw, out_specs=[ANY] * nw,
                  out_shape=[_S(r.shape, r.dtype) for r in reds],
                  input_output_aliases={w: w for w in range(nw)},
                  scratch_shapes=_dma_sems(nw), compiler_params=_comm_params())(*reds)


def _allreduce_small(part, name):
    R, C = part.shape

    def body(p_ref, o_ref, buf, s_sem, r_sem):
        x, y, c, _ = _place()
        my_id = 4 * x + 2 * y + c
        buf[my_id] = p_ref[...]
        cps = []
        for k in range(1, N_DEV):
            fx, fy, fc = (k >> 2) & 1, (k >> 1) & 1, k & 1
            peer = (x ^ fx, y ^ fy, c ^ fc)
            cps.append(_remote(p_ref, buf.at[my_id], s_sem.at[k - 1], r_sem.at[k - 1], peer))
        for cp in cps:
            cp.start()
        for cp in cps:
            cp.wait()
        acc = buf[0]
        for d in range(1, N_DEV):
            acc = acc + buf[d]
        o_ref[...] = acc

    vm = pl.BlockSpec(memory_space=pltpu.VMEM)
    return _pcall(body, name=name, in_specs=[vm], out_specs=vm, out_shape=_S((R, C), F32),
                  scratch_shapes=[pltpu.VMEM((N_DEV, R, C), F32)] + _dma_sems(N_DEV - 1),
                  compiler_params=pltpu.CompilerParams(has_side_effects=True, vmem_limit_bytes=VMEM_LIMIT))(part)


def _own_shard_buffers(ws, l, q_me):
    bufs, sizes = [], []
    for name, tr in BIG:
        w = ws[name][l]
        ch = (w.T if tr else w).astype(MXU)
        n = ch.shape[0]
        bufs.append(lax.dynamic_update_slice(jnp.zeros((N_CHIPS * n, ch.shape[1]), MXU), ch, (q_me * n, 0)))
        sizes.append(n)
    return bufs, sizes


def _small_rows(v):
    flat = v.reshape(-1)
    pad = (-flat.shape[0]) % 1024
    return jnp.pad(flat, (0, pad)).reshape(-1, 1024)


def _pack_small(vals):
    rows = [_small_rows(vals[n]) for n in SMALL]
    cat = jnp.concatenate(rows, axis=0)
    pad = (-cat.shape[0]) % 8
    return jnp.pad(cat, ((0, pad), (0, 0)))


def _unpack_small(packed, like):
    out, r = {}, 0
    for n in SMALL:
        size = like[n].size
        nr = -(-size // 1024)
        out[n] = packed[r:r + nr].reshape(-1)[:size].reshape(like[n].shape)
        r += nr
    return out


def _rope_tables(positions):
    inv_freq = ROPE_THETA ** (-jnp.arange(0, HEAD_DIM, 2, dtype=F32) / HEAD_DIM)
    ang = positions.astype(F32)[:, None] * inv_freq
    cos, sin = jnp.cos(ang), jnp.sin(ang)
    return jnp.concatenate([cos, cos, cos, cos], axis=1), jnp.concatenate([-sin, sin, -sin, sin], axis=1)


def _layer_fwd(l, x, memv, W, P, cos2, sin2):
    t = f"l{l}"
    sv = {"x0": x}
    sv["h1"] = _rms_fwd(x, P["ffn1_norm"], t + "_ffn1_norm")
    sv["a1"], sv["b1"], sv["s1"] = _ffn_up(sv["h1"], W["ffn1_w_gate"], W["ffn1_w_up"], t + "_ffn1_up")
    sv["x1"] = _mm([(sv["s1"], W["ffn1_w_down"])], nt=False, out_dtype=F32, res=x, res_scale=FFN_RES, name=t + "_ffn1_down")

    sv["h2"] = _rms_fwd(sv["x1"], P["mix_norm"], t + "_mix_norm")
    sv["q"], sv["k"], sv["v"], sv["u"] = _in_proj(sv["h2"], W["w_in"], cos2, sin2, t + "_in_proj")
    sv["oa"] = _swa_fwd(sv["q"], sv["k"], sv["v"], P["attn_sinks"], t + "_swa")
    sv["ob"], sv["mg"] = _pool_fwd(sv["u"], sv["oa"], P["pool_w"], P["pool_scale"], P["attn_out_norm"],
                                   P["pool_out_norm"], t + "_pool")
    sv["x2"] = _mm([(sv["mg"], W["w_out"])], nt=False, out_dtype=F32, res=sv["x1"], name=t + "_out_proj")

    sv["h3"] = _rms_fwd(sv["x2"], P["xattn_norm"], t + "_xattn_norm")
    sv["memn"] = _rms_fwd(memv, P["mem_norm"], t + "_mem_norm")
    sv["q3"] = _mm([(sv["h3"], W["xattn_wq"])], nt=False, out_dtype=MXU, name=t + "_xq")
    sv["kv"] = _mm([(sv["memn"], W["xattn_wkv"])], nt=True, out_dtype=MXU, name=t + "_xkv")
    sv["o3"] = _xattn_fwd(sv["q3"], sv["kv"], t + "_xattn")
    sv["x3"] = _mm([(sv["o3"], W["xattn_wo"])], nt=False, out_dtype=F32, res=sv["x2"], name=t + "_xo")

    sv["h4"] = _rms_fwd(sv["x3"], P["ffn2_norm"], t + "_ffn2_norm")
    sv["a2"], sv["b2"], sv["s2"] = _ffn_up(sv["h4"], W["ffn2_w_gate"], W["ffn2_w_up"], t + "_ffn2_up")
    x4 = _mm([(sv["s2"], W["ffn2_w_down"])], nt=False, out_dtype=F32, res=sv["x3"], res_scale=FFN_RES, name=t + "_ffn2_down")
    return x4, sv


def _ffn_bwd(t, dx, x_in, g, h, a, b, s, wgT, wuT, wd):
    d_wd = _mm_tn(s, dx, t + "_dwd", r_scale=FFN_RES)
    da, db = _ffn_mid_bwd(dx, wd, a, b, t + "_mid")
    d_wg = _mm_tn(da, h, t + "_dwg")
    d_wu = _mm_tn(db, h, t + "_dwu")
    dh = _mm([(da, wgT), (db, wuT)], nt=False, out_dtype=F32, name=t + "_dh", tm=TOKEN_TILE // 2)
    dx_in, dg = _rms_bwd(x_in, g, dh, t + "_norm_bwd", dres=dx)
    return dx_in, dg, d_wg, d_wu, d_wd


def _layer_bwd(l, dx, sv, memv, W, P, cos2, sin2):
    t = f"l{l}b"
    GW, GP = {}, {}
    dx, GP["ffn2_norm"], GW["ffn2_w_gate"], GW["ffn2_w_up"], GW["ffn2_w_down"] = _ffn_bwd(
        t + "_ffn2", dx, sv["x3"], P["ffn2_norm"], sv["h4"], sv["a2"], sv["b2"], sv["s2"],
        W["ffn2_w_gate"], W["ffn2_w_up"], W["ffn2_w_down"])

    GW["xattn_wo"] = _mm_tn(sv["o3"], dx, t + "_dwo")
    do3 = _mm([(dx, W["xattn_wo"])], nt=True, out_dtype=MXU, name=t + "_do3")
    dq3, dkv = _xattn_bwd(sv["q3"], sv["kv"], do3, t + "_xattn")
    GW["xattn_wq"] = _mm_tn(sv["h3"], dq3, t + "_dwq")
    dh3 = _mm([(dq3, W["xattn_wq"])], nt=True, out_dtype=F32, name=t + "_dh3")
    GW["xattn_wkv"] = _mm_tn(dkv, sv["memn"], t + "_dwkv")
    dmemn = _mm([(dkv, W["xattn_wkv"])], nt=False, out_dtype=F32, name=t + "_dmemn")
    _, GP["mem_norm"] = _rms_bwd(memv, P["mem_norm"], dmemn, t + "_mem_norm_bwd")
    dx, GP["xattn_norm"] = _rms_bwd(sv["x2"], P["xattn_norm"], dh3, t + "_xattn_norm_bwd", dres=dx)

    GW["w_out"] = _mm_tn(sv["mg"], dx, t + "_dwout")
    dmg = _mm([(dx, W["w_out"])], nt=True, out_dtype=F32, name=t + "_dmg")
    doa, GP["attn_out_norm"] = _rms_bwd(sv["oa"], P["attn_out_norm"], dmg, t + "_oa_norm_bwd", col=0)
    dob, GP["pool_out_norm"] = _rms_bwd(sv["ob"], P["pool_out_norm"], dmg, t + "_ob_norm_bwd", col=1)
    du, GP["pool_w"], GP["pool_scale"] = _pool_bwd(sv["u"], dob, P["pool_w"], P["pool_scale"], t + "_pool")
    dq, dko, dkp, dvo, dvp, dsk = _swa_bwd(sv["q"], sv["k"], sv["v"], doa, P["attn_sinks"], t + "_swa")
    GP["attn_sinks"] = dsk[:, 0]
    dpj = _dproj(dq, dko, dkp, dvo, dvp, du, cos2, sin2, t + "_dproj")
    GW["w_in"] = _mm_tn(dpj, sv["h2"], t + "_dwin")
    dh2 = _mm([(dpj, W["w_in"])], nt=False, out_dtype=F32, name=t + "_dh2")
    dx, GP["mix_norm"] = _rms_bwd(sv["x1"], P["mix_norm"], dh2, t + "_mix_norm_bwd", dres=dx)

    dx, GP["ffn1_norm"], GW["ffn1_w_gate"], GW["ffn1_w_up"], GW["ffn1_w_down"] = _ffn_bwd(
        t + "_ffn1", dx, sv["x0"], P["ffn1_norm"], sv["h1"], sv["a1"], sv["b1"], sv["s1"],
        W["ffn1_w_gate"], W["ffn1_w_up"], W["ffn1_w_down"])
    return dx, GW, GP


def _reduce_layer(l, GW, sizes, place):
    t = f"l{l}r"
    grads = [GW[name] for name, _ in BIG]
    gots = _rs_pair_exchange(grads, sizes, t + "_pair")
    sums = _pair_sum(grads, gots, sizes, place, t + "_pair_sum")
    recvs = _rs_chip_exchange(sums, sizes, t + "_chips")
    ffn = [w for w, (name, _) in enumerate(BIG) if "ffn" in name]
    rest = [w for w in range(len(BIG)) if w not in ffn]
    reds = [None] * len(BIG)
    for tag, group in (("_final_sum_ffn", ffn), ("_final_sum_mix", rest)):
        pick = lambda xs: [xs[w] for w in group]
        for w, r in zip(group, _final_sum(pick(grads), pick(gots), pick(recvs), pick(sizes), place, t + tag)):
            reds[w] = r
    reds = _rs_share_halves(reds, sizes, t + "_share")
    return {name: (r.T if tr else r) for (name, tr), r in zip(BIG, reds)}


def kernel(x, mem, positions, ffn1_norm, ffn1_w_gate, ffn1_w_up, ffn1_w_down, mix_norm, w_in, attn_sinks, pool_w, pool_scale, attn_out_norm, pool_out_norm, w_out, xattn_norm, mem_norm, xattn_wq, xattn_wkv, xattn_wo, ffn2_norm, ffn2_w_gate, ffn2_w_up, ffn2_w_down, final_norm, loss_target, m_ffn1_norm, m_ffn1_w_gate, m_ffn1_w_up, m_ffn1_w_down, m_mix_norm, m_w_in, m_attn_sinks, m_pool_w, m_pool_scale, m_attn_out_norm, m_pool_out_norm, m_w_out, m_xattn_norm, m_mem_norm, m_xattn_wq, m_xattn_wkv, m_xattn_wo, m_ffn2_norm, m_ffn2_w_gate, m_ffn2_w_up, m_ffn2_w_down, m_final_norm, v_ffn1_norm, v_ffn1_w_gate, v_ffn1_w_up, v_ffn1_w_down, v_mix_norm, v_w_in, v_attn_sinks, v_pool_w, v_pool_scale, v_attn_out_norm, v_pool_out_norm, v_w_out, v_xattn_norm, v_mem_norm, v_xattn_wq, v_xattn_wkv, v_xattn_wo, v_ffn2_norm, v_ffn2_w_gate, v_ffn2_w_up, v_ffn2_w_down, v_final_norm):
    ws = dict(ffn1_norm=ffn1_norm, ffn1_w_gate=ffn1_w_gate, ffn1_w_up=ffn1_w_up, ffn1_w_down=ffn1_w_down,
              mix_norm=mix_norm, w_in=w_in, attn_sinks=attn_sinks, pool_w=pool_w, pool_scale=pool_scale,
              attn_out_norm=attn_out_norm, pool_out_norm=pool_out_norm, w_out=w_out, xattn_norm=xattn_norm,
              mem_norm=mem_norm, xattn_wq=xattn_wq, xattn_wkv=xattn_wkv, xattn_wo=xattn_wo, ffn2_norm=ffn2_norm,
              ffn2_w_gate=ffn2_w_gate, ffn2_w_up=ffn2_w_up, ffn2_w_down=ffn2_w_down, final_norm=final_norm)
    ms = dict(ffn1_norm=m_ffn1_norm, ffn1_w_gate=m_ffn1_w_gate, ffn1_w_up=m_ffn1_w_up, ffn1_w_down=m_ffn1_w_down,
              mix_norm=m_mix_norm, w_in=m_w_in, attn_sinks=m_attn_sinks, pool_w=m_pool_w, pool_scale=m_pool_scale,
              attn_out_norm=m_attn_out_norm, pool_out_norm=m_pool_out_norm, w_out=m_w_out, xattn_norm=m_xattn_norm,
              mem_norm=m_mem_norm, xattn_wq=m_xattn_wq, xattn_wkv=m_xattn_wkv, xattn_wo=m_xattn_wo,
              ffn2_norm=m_ffn2_norm, ffn2_w_gate=m_ffn2_w_gate, ffn2_w_up=m_ffn2_w_up, ffn2_w_down=m_ffn2_w_down,
              final_norm=m_final_norm)
    vs = dict(ffn1_norm=v_ffn1_norm, ffn1_w_gate=v_ffn1_w_gate, ffn1_w_up=v_ffn1_w_up, ffn1_w_down=v_ffn1_w_down,
              mix_norm=v_mix_norm, w_in=v_w_in, attn_sinks=v_attn_sinks, pool_w=v_pool_w, pool_scale=v_pool_scale,
              attn_out_norm=v_attn_out_norm, pool_out_norm=v_pool_out_norm, w_out=v_w_out, xattn_norm=v_xattn_norm,
              mem_norm=v_mem_norm, xattn_wq=v_xattn_wq, xattn_wkv=v_xattn_wkv, xattn_wo=v_xattn_wo,
              ffn2_norm=v_ffn2_norm, ffn2_w_gate=v_ffn2_w_gate, ffn2_w_up=v_ffn2_w_up, ffn2_w_down=v_ffn2_w_down,
              final_norm=v_final_norm)
    depth = ffn1_norm.shape[0]
    T, D = x.shape[1], x.shape[2]
    xv = x.reshape(T, D)
    memv = mem.reshape(mem.shape[1], D)
    tgt = loss_target.reshape(T, D)
    cos2, sin2 = _rope_tables(positions.reshape(T))

    q_me = 2 * lax.axis_index("x") + lax.axis_index("y")
    place = jnp.stack([lax.axis_index("c"), q_me]).astype(jnp.int32)
    Ws, sizes = [], None
    for l in range(depth):
        bufs, sizes = _own_shard_buffers(ws, l, q_me)
        full = _allgather_weights(bufs, sizes, f"l{l}_allgather")
        Ws.append({name: f for (name, _), f in zip(BIG, full)})
    Ps = [{n: (ws[n][l].reshape(1, -1) if n != "pool_w" else ws[n][l]) for n in SMALL if n != "final_norm"}
          for l in range(depth)]

    saved = []
    h = xv
    for l in range(depth):
        h, sv = _layer_fwd(l, h, memv, Ws[l], Ps[l], cos2, sin2)
        saved.append(sv)
    loss_row, dx, d_final = _loss_head(h, tgt, final_norm.reshape(1, D), "loss_head")
    GWs, GPs = [None] * depth, [None] * depth
    for l in reversed(range(depth)):
        dx, GWs[l], GPs[l] = _layer_bwd(l, dx, saved[l], memv, Ws[l], Ps[l], cos2, sin2)

    big_g = [_reduce_layer(l, GWs[l], sizes, place) for l in range(depth)]
    small_part = {n: jnp.stack([GPs[l][n].reshape(ws[n].shape[1:]) for l in range(depth)]) for n in SMALL if n != "final_norm"}
    small_part["final_norm"] = d_final.reshape(D)
    small_g = _unpack_small(_allreduce_small(_pack_small(small_part), "small_allreduce"), ws)
    loss = lax.psum(loss_row[0, 0], ("x", "y", "c"))

    grads, deltas, new_m, new_v = {}, {}, {}, {}
    for name, _ in BIG:
        g = jnp.stack([big_g[l][name] for l in range(depth)])
        shp = g.shape
        flat = lambda a: a.reshape(shp[0] * shp[1], shp[2])
        d, nm, nv = _adamw(flat(ws[name]), flat(g), flat(ms[name]), flat(vs[name]), "adamw_" + name)
        grads[name], deltas[name], new_m[name], new_v[name] = g, d.reshape(shp), nm.reshape(shp), nv.reshape(shp)
    d, nm, nv = _adamw(_pack_small(ws), _pack_small(small_g), _pack_small(ms), _pack_small(vs), "adamw_small")
    grads.update(small_g)
    deltas.update(_unpack_small(d, ws))
    new_m.update(_unpack_small(nm, ws))
    new_v.update(_unpack_small(nv, ws))

    grad_x = dx.reshape(x.shape)
    return (loss, grad_x, *[grads[n] for n in WEIGHTS], *[deltas[n] for n in WEIGHTS],
            *[new_m[n] for n in WEIGHTS], *[new_v[n] for n in WEIGHTS])
```

```python
import functools

import jax
import jax.numpy as jnp
from jax import lax
from jax.experimental import pallas as pl
from jax.experimental.pallas import tpu as pltpu
from jax.experimental.pallas import tpu_sc as plsc

F32 = jnp.float32
MXU = jnp.bfloat16

EPS = 1e-6
HEAD_DIM = 64
N_Q_HEADS = 8
N_KV_HEADS = 2
Q_PER_KV = N_Q_HEADS // N_KV_HEADS
ATTN_WIDTH = N_Q_HEADS * HEAD_DIM
KV_WIDTH = N_KV_HEADS * HEAD_DIM
BLOCK = 128
ROPE_THETA = 10000.0
POOL_WINDOWS = (2, 4, 8, 16)
POOL_GROUP = 128
POOL_WIDTH = len(POOL_WINDOWS) * POOL_GROUP
POOL_HALO = 16
X_HEADS = 4
X_HEAD_DIM = 256
FFN_RES = 0.5
NEG = -1e30
ADAM_LR = 0.001
ADAM_B1 = 0.9
ADAM_B2 = 0.999
ADAM_EPS = 1e-08
ADAM_WD = 0.01
ADAM_STEP = 10

N_CHIPS = 4
SEQ_GATHER_ID = 1
SEQ_REDUCE_ID = 2
N_DEV = 8
V7X_VMEM_BYTES = 64 * 1024 * 1024
VMEM_LIMIT = V7X_VMEM_BYTES - 8 * 1024 * 1024
LANES = 128
TOKEN_TILE = 512
MESH = pl.DeviceIdType.MESH

BIG = (("ffn1_w_gate", True), ("ffn1_w_up", True), ("ffn1_w_down", False), ("w_in", True), ("w_out", False),
       ("xattn_wq", False), ("xattn_wkv", True), ("xattn_wo", False),
       ("ffn2_w_gate", True), ("ffn2_w_up", True), ("ffn2_w_down", False))
SMALL = ("ffn1_norm", "mix_norm", "attn_sinks", "pool_w", "pool_scale", "attn_out_norm", "pool_out_norm",
         "xattn_norm", "mem_norm", "ffn2_norm", "final_norm")
WEIGHTS = ("ffn1_norm", "ffn1_w_gate", "ffn1_w_up", "ffn1_w_down", "mix_norm", "w_in", "attn_sinks", "pool_w",
           "pool_scale", "attn_out_norm", "pool_out_norm", "w_out", "xattn_norm", "mem_norm", "xattn_wq",
           "xattn_wkv", "xattn_wo", "ffn2_norm", "ffn2_w_gate", "ffn2_w_up", "ffn2_w_down", "final_norm")


def _S(shape, dtype):
    return jax.ShapeDtypeStruct(tuple(shape), dtype)


def _pcall(body, **kw):
    return pl.pallas_call(body, **kw)


def _cp(*sem):
    return pltpu.CompilerParams(dimension_semantics=tuple(sem), vmem_limit_bytes=VMEM_LIMIT)


def _nt(a, b):
    return lax.dot_general(a, b, (((1,), (1,)), ((), ())), preferred_element_type=F32)


def _nn(a, b):
    return lax.dot_general(a, b, (((1,), (0,)), ((), ())), preferred_element_type=F32)


def _tn(a, b):
    return lax.dot_general(a, b, (((0,), (0,)), ((), ())), preferred_element_type=F32)


def _tile(n, want):
    t = min(n, want)
    assert n % t == 0, (n, want)
    return t


def _resident(shape):
    nd = len(shape)
    return pl.BlockSpec(tuple(shape), lambda *_: (0,) * nd)


def _rms_fwd(x, g, name):
    T, C = x.shape
    tm = _tile(T, TOKEN_TILE)

    def body(x_ref, g_ref, o_ref):
        xv = x_ref[...]
        r = lax.rsqrt(jnp.mean(xv * xv, axis=-1, keepdims=True) + EPS)
        o_ref[...] = (xv * r * g_ref[...]).astype(o_ref.dtype)

    return _pcall(body, name=name, grid=(T // tm,),
                  in_specs=[pl.BlockSpec((tm, C), lambda i: (i, 0)), _resident((1, C))],
                  out_specs=pl.BlockSpec((tm, C), lambda i: (i, 0)),
                  out_shape=_S((T, C), MXU), compiler_params=_cp("parallel"))(x, g)


def _rms_bwd(x, g, dh, name, dres=None, col=0):
    T, C = x.shape
    tm = _tile(T, TOKEN_TILE)

    def body(*refs):
        if dres is None:
            x_ref, g_ref, dh_ref, dx_ref, dg_ref = refs
        else:
            x_ref, g_ref, dh_ref, dres_ref, dx_ref, dg_ref = refs
        xv = x_ref[...]
        r = lax.rsqrt(jnp.mean(xv * xv, axis=-1, keepdims=True) + EPS)
        xh = xv * r
        dhv = dh_ref[...].astype(F32)
        dxn = dhv * g_ref[...]
        dx = r * (dxn - xh * jnp.mean(dxn * xh, axis=-1, keepdims=True))
        if dres is not None:
            dx = dx + dres_ref[...]
        dx_ref[...] = dx

        @pl.when(pl.program_id(0) == 0)
        def _():
            dg_ref[...] = jnp.zeros_like(dg_ref)

        dg_ref[...] += jnp.sum(dhv * xh, axis=0, keepdims=True)

    tok = pl.BlockSpec((tm, C), lambda i: (i, 0))
    in_specs = [tok, _resident((1, C)), pl.BlockSpec((tm, C), lambda i: (i, col))]
    args = [x, g, dh]
    if dres is not None:
        in_specs.append(tok)
        args.append(dres)
    return _pcall(body, name=name, grid=(T // tm,), in_specs=in_specs,
                  out_specs=[tok, _resident((1, C))],
                  out_shape=[_S((T, C), F32), _S((1, C), F32)], compiler_params=_cp("arbitrary"))(*args)


def _mm(pairs, *, nt, out_dtype, name, res=None, res_scale=1.0, a_scale=1.0, tm=TOKEN_TILE):
    M = pairs[0][0].shape[0]
    N = pairs[0][1].shape[0] if nt else pairs[0][1].shape[1]
    tm = _tile(M, tm)
    n = len(pairs)

    def body(*refs):
        a_refs, w_refs = refs[:n], refs[n:2 * n]
        o_ref = refs[-1]
        acc = None
        for a_ref, w_ref in zip(a_refs, w_refs):
            a = a_ref[...]
            if a_scale != 1.0:
                a = a * a_scale
            a = a.astype(MXU)
            p = _nt(a, w_ref[...]) if nt else _nn(a, w_ref[...])
            acc = p if acc is None else acc + p
        if res is not None:
            acc = refs[2 * n][...] + res_scale * acc
        o_ref[...] = acc.astype(o_ref.dtype)

    in_specs = [pl.BlockSpec((tm, a.shape[1]), lambda i: (i, 0)) for a, _ in pairs]
    in_specs += [_resident(w.shape) for _, w in pairs]
    args = [a for a, _ in pairs] + [w for _, w in pairs]
    if res is not None:
        in_specs.append(pl.BlockSpec((tm, N), lambda i: (i, 0)))
        args.append(res)
    return _pcall(body, name=name, grid=(M // tm,), in_specs=in_specs,
                  out_specs=pl.BlockSpec((tm, N), lambda i: (i, 0)),
                  out_shape=_S((M, N), out_dtype), compiler_params=_cp("parallel"))(*args)


def _mm_tn(l, r, name, *, l_scale=1.0, r_scale=1.0, tr=1408, tt=TOKEN_TILE):
    T, R = l.shape
    C = r.shape[1]
    tt = _tile(T, tt)
    tr = tr if R % tr == 0 else (1024 if R % 1024 == 0 and R > 1280 else R)

    def body(l_ref, r_ref, o_ref):
        lv, rv = l_ref[...], r_ref[...]
        if l_scale != 1.0:
            lv = lv * l_scale
        if r_scale != 1.0:
            rv = rv * r_scale
        lv, rv = lv.astype(MXU), rv.astype(MXU)

        @pl.when(pl.program_id(1) == 0)
        def _():
            o_ref[...] = jnp.zeros_like(o_ref)

        o_ref[...] += _tn(lv, rv)

    return _pcall(body, name=name, grid=(R // tr, T // tt),
                  in_specs=[pl.BlockSpec((tt, tr), lambda i, t: (t, i)), pl.BlockSpec((tt, C), lambda i, t: (t, 0))],
                  out_specs=pl.BlockSpec((tr, C), lambda i, t: (i, 0)),
                  out_shape=_S((R, C), F32), compiler_params=_cp("parallel", "arbitrary"))(l, r)


FFN_COL_TILE = 1408


def _ffn_up(h, wgT, wuT, name):
    T, D = h.shape
    Fd = wgT.shape[0]
    tm, tn = _tile(T, TOKEN_TILE), _tile(Fd, FFN_COL_TILE)

    def body(h_ref, wg_ref, wu_ref, a_ref, b_ref, s_ref):
        hv = h_ref[...]
        a = _nt(hv, wg_ref[...])
        b = _nt(hv, wu_ref[...])
        s = a * (1.0 / (1.0 + jnp.exp(-a))) * b
        a_ref[...] = a.astype(a_ref.dtype)
        b_ref[...] = b.astype(b_ref.dtype)
        s_ref[...] = s.astype(s_ref.dtype)

    wspec = pl.BlockSpec((tn, D), lambda j, i: (j, 0))
    ospec = pl.BlockSpec((tm, tn), lambda j, i: (i, j))
    return _pcall(body, name=name, grid=(Fd // tn, T // tm),
                  in_specs=[pl.BlockSpec((tm, D), lambda j, i: (i, 0)), wspec, wspec],
                  out_specs=[ospec, ospec, ospec], out_shape=[_S((T, Fd), MXU)] * 3,
                  compiler_params=_cp("parallel", "parallel"))(h, wgT, wuT)


def _ffn_mid_bwd(dx, wd, a, b, name):
    T, D = dx.shape
    Fd = wd.shape[0]
    tm, tn = _tile(T, TOKEN_TILE), _tile(Fd, FFN_COL_TILE)

    def body(dx_ref, wd_ref, a_ref, b_ref, da_ref, db_ref):
        dy = (dx_ref[...] * FFN_RES).astype(MXU)
        ds = _nt(dy, wd_ref[...])
        av, bv = a_ref[...].astype(F32), b_ref[...].astype(F32)
        sg = 1.0 / (1.0 + jnp.exp(-av))
        da_ref[...] = (ds * bv * (sg * (1.0 + av * (1.0 - sg)))).astype(da_ref.dtype)
        db_ref[...] = (ds * (av * sg)).astype(db_ref.dtype)

    aspec = pl.BlockSpec((tm, tn), lambda j, i: (i, j))
    return _pcall(body, name=name, grid=(Fd // tn, T // tm),
                  in_specs=[pl.BlockSpec((tm, D), lambda j, i: (i, 0)), pl.BlockSpec((tn, D), lambda j, i: (j, 0)),
                            aspec, aspec],
                  out_specs=[aspec, aspec], out_shape=[_S((T, Fd), MXU)] * 2,
                  compiler_params=_cp("parallel", "parallel"))(dx, wd, a, b)


def _swap_halves(t):
    w = t.shape[1]
    lane = lax.broadcasted_iota(jnp.int32, t.shape, 1)
    first = (lane % HEAD_DIM) < (HEAD_DIM // 2)
    return jnp.where(first, pltpu.roll(t, w - HEAD_DIM // 2, 1), pltpu.roll(t, HEAD_DIM // 2, 1))


def _rope(t, cos2, sin2):
    reps = t.shape[1] // LANES
    c = jnp.tile(cos2, (1, reps)) if reps > 1 else cos2
    s = jnp.tile(sin2, (1, reps)) if reps > 1 else sin2
    return t * c + _swap_halves(t) * s


def _rope_bwd(dt, cos2, sin2):
    reps = dt.shape[1] // LANES
    c = jnp.tile(cos2, (1, reps)) if reps > 1 else cos2
    s = jnp.tile(sin2, (1, reps)) if reps > 1 else sin2
    return dt * c + _swap_halves(dt * s)


def _in_proj(h, winT, cos2, sin2, name):
    T, D = h.shape
    tm = _tile(T, TOKEN_TILE)
    qe, ke, ve = ATTN_WIDTH, ATTN_WIDTH + KV_WIDTH, ATTN_WIDTH + 2 * KV_WIDTH

    def body(h_ref, w_ref, c_ref, s_ref, q_ref, k_ref, v_ref, u_ref):
        proj = _nt(h_ref[...], w_ref[...])
        cv, sv = c_ref[...], s_ref[...]
        q_ref[...] = _rope(proj[:, :qe], cv, sv).astype(q_ref.dtype)
        k_ref[...] = _rope(proj[:, qe:ke], cv, sv).astype(k_ref.dtype)
        v_ref[...] = proj[:, ke:ve].astype(v_ref.dtype)
        u_ref[...] = proj[:, ve:]

    def tok(w):
        return pl.BlockSpec((tm, w), lambda i: (i, 0))

    return _pcall(body, name=name, grid=(T // tm,),
                  in_specs=[tok(D), _resident(winT.shape), tok(LANES), tok(LANES)],
                  out_specs=[tok(ATTN_WIDTH), tok(KV_WIDTH), tok(KV_WIDTH), tok(POOL_WIDTH)],
                  out_shape=[_S((T, ATTN_WIDTH), MXU), _S((T, KV_WIDTH), MXU), _S((T, KV_WIDTH), MXU),
                             _S((T, POOL_WIDTH), F32)],
                  compiler_params=_cp("parallel"))(h, winT, cos2, sin2)


SWA_TILE_BLOCKS = 4
SM_SCALE = HEAD_DIM ** -0.5


def _swa_bias(first_tile):
    cols = Q_PER_KV * BLOCK
    kj = lax.broadcasted_iota(jnp.int32, (2 * BLOCK, cols), 0)
    qi = lax.broadcasted_iota(jnp.int32, (2 * BLOCK, cols), 1) % BLOCK
    diff = qi + BLOCK - kj
    bias = jnp.where((diff >= 0) & (diff < BLOCK), 0.0, NEG)
    return bias, jnp.where(kj < jnp.where(first_tile, BLOCK, 0), NEG, bias)


def _swa_probs(kh, qs, sink_row, bias):
    s = _nt(kh, qs) + bias
    m = jnp.maximum(jnp.max(s, axis=0, keepdims=True), sink_row)
    e = jnp.exp(s - m)
    es = jnp.exp(sink_row - m)
    inv = 1.0 / (jnp.sum(e, axis=0, keepdims=True) + es)
    return e * inv, es * inv


def _sink_row(sinks_ref, kv):
    return jnp.concatenate([jnp.full((1, BLOCK), sinks_ref[0, kv * Q_PER_KV + g], F32) for g in range(Q_PER_KV)], axis=1)


def _stack_heads(t, kv):
    return jnp.concatenate([t[:, (kv * Q_PER_KV + g) * HEAD_DIM:(kv * Q_PER_KV + g + 1) * HEAD_DIM]
                            for g in range(Q_PER_KV)], axis=0)


def _swa_specs(T):
    tq = _tile(T, SWA_TILE_BLOCKS * BLOCK)
    nbt = tq // BLOCK
    cur = lambda w: pl.BlockSpec((tq, w), lambda i: (i, 0))
    prev = lambda w: pl.BlockSpec((BLOCK, w), lambda i: (jnp.maximum(i * nbt - 1, 0), 0))
    return tq, nbt, cur, prev


def _rows(b):
    return slice(b * BLOCK, (b + 1) * BLOCK)


def _swa_fwd(q, k, v, sinks, name):
    T = q.shape[0]
    tq, nbt, cur, prev = _swa_specs(T)

    def body(sinks_ref, q_ref, k_ref, kp_ref, v_ref, vp_ref, o_ref):
        bias, bias0 = _swa_bias(pl.program_id(0) == 0)
        sink = [_sink_row(sinks_ref, kv) for kv in range(N_KV_HEADS)]
        kx = jnp.concatenate([kp_ref[...], k_ref[...]], axis=0)
        vx = jnp.concatenate([vp_ref[...], v_ref[...]], axis=0)
        for b in range(nbt):
            qv = q_ref[_rows(b), :] * SM_SCALE
            kk, vv = kx[b * BLOCK:(b + 2) * BLOCK], vx[b * BLOCK:(b + 2) * BLOCK]
            for kv in range(N_KV_HEADS):
                hs = slice(kv * HEAD_DIM, (kv + 1) * HEAD_DIM)
                p, _ = _swa_probs(kk[:, hs], _stack_heads(qv, kv), sink[kv], bias0 if b == 0 else bias)
                o_t = _tn(vv[:, hs], p.astype(MXU))
                for g in range(Q_PER_KV):
                    c0 = (kv * Q_PER_KV + g) * HEAD_DIM
                    o_ref[_rows(b), c0:c0 + HEAD_DIM] = o_t[:, _rows(g)].T

    return _pcall(body, name=name, grid=(T // tq,),
                  in_specs=[pl.BlockSpec(memory_space=pltpu.SMEM), cur(ATTN_WIDTH), cur(KV_WIDTH), prev(KV_WIDTH),
                            cur(KV_WIDTH), prev(KV_WIDTH)],
                  out_specs=cur(ATTN_WIDTH), out_shape=_S((T, ATTN_WIDTH), F32),
                  compiler_params=_cp("parallel"))(sinks, q, k, k, v, v)


def _swa_bwd(q, k, v, do, sinks, name):
    T = q.shape[0]
    tq, nbt, cur, prev = _swa_specs(T)
    per_tile = lambda w: pl.BlockSpec((BLOCK, w), lambda i: (i, 0))

    def add(acc, t):
        return t if acc is None else acc + t

    def body(sinks_ref, q_ref, k_ref, kp_ref, v_ref, vp_ref, do_ref,
             dq_ref, dk_ref, dkp_ref, dv_ref, dvp_ref, dsk_ref):
        bias, bias0 = _swa_bias(pl.program_id(0) == 0)
        sink = [_sink_row(sinks_ref, kv) for kv in range(N_KV_HEADS)]
        kx = jnp.concatenate([kp_ref[...], k_ref[...]], axis=0)
        vx = jnp.concatenate([vp_ref[...], v_ref[...]], axis=0)

        @pl.when(pl.program_id(0) == 0)
        def _():
            dsk_ref[...] = jnp.zeros_like(dsk_ref)

        dk_acc, dv_acc = [None] * (nbt + 1), [None] * (nbt + 1)
        dsk_acc = [None] * N_Q_HEADS
        for b in range(nbt):
            qv, dov = q_ref[_rows(b), :] * SM_SCALE, do_ref[_rows(b), :].astype(MXU)
            kk, vv = kx[b * BLOCK:(b + 2) * BLOCK], vx[b * BLOCK:(b + 2) * BLOCK]
            dks, dvs = [], []
            for kv in range(N_KV_HEADS):
                hs = slice(kv * HEAD_DIM, (kv + 1) * HEAD_DIM)
                qs, dos = _stack_heads(qv, kv), _stack_heads(dov, kv)
                p, ps = _swa_probs(kk[:, hs], qs, sink[kv], bias0 if b == 0 else bias)
                dp = _nt(vv[:, hs], dos)
                delta = jnp.sum(p * dp, axis=0, keepdims=True)
                ds = (p * (dp - delta)).astype(MXU)
                dq_t = _tn(kk[:, hs], ds) * SM_SCALE
                dks.append(_nn(ds, qs))
                dvs.append(_nn(p.astype(MXU), dos))
                dsink = -ps * delta
                for g in range(Q_PER_KV):
                    h = kv * Q_PER_KV + g
                    dq_ref[_rows(b), h * HEAD_DIM:(h + 1) * HEAD_DIM] = dq_t[:, _rows(g)].T
                    dsk_acc[h] = add(dsk_acc[h], jnp.sum(dsink[:, _rows(g)], axis=1, keepdims=True))
            dk, dv = jnp.concatenate(dks, axis=1), jnp.concatenate(dvs, axis=1)
            dk_acc[b], dk_acc[b + 1] = add(dk_acc[b], dk[:BLOCK]), add(dk_acc[b + 1], dk[BLOCK:])
            dv_acc[b], dv_acc[b + 1] = add(dv_acc[b], dv[:BLOCK]), add(dv_acc[b + 1], dv[BLOCK:])
        dkp_ref[...], dvp_ref[...] = dk_acc[0], dv_acc[0]
        dk_ref[...] = jnp.concatenate(dk_acc[1:], axis=0)
        dv_ref[...] = jnp.concatenate(dv_acc[1:], axis=0)
        for h in range(N_Q_HEADS):
            dsk_ref[h:h + 1, :] += jnp.broadcast_to(dsk_acc[h], (1, LANES))

    kvs, kvp = _S((T, KV_WIDTH), F32), _S((T // tq * BLOCK, KV_WIDTH), F32)
    return _pcall(body, name=name, grid=(T // tq,),
                  in_specs=[pl.BlockSpec(memory_space=pltpu.SMEM), cur(ATTN_WIDTH), cur(KV_WIDTH), prev(KV_WIDTH),
                            cur(KV_WIDTH), prev(KV_WIDTH), cur(ATTN_WIDTH)],
                  out_specs=[cur(ATTN_WIDTH), cur(KV_WIDTH), per_tile(KV_WIDTH), cur(KV_WIDTH), per_tile(KV_WIDTH),
                             _resident((N_Q_HEADS, LANES))],
                  out_shape=[_S((T, ATTN_WIDTH), F32), kvs, kvp, kvs, kvp, _S((N_Q_HEADS, LANES), F32)],
                  compiler_params=_cp("arbitrary"))(sinks, q, k, k, v, v, do)


def _dproj(dq, dk, dkp, dv, dvp, du, cos2, sin2, name):
    T = dq.shape[0]
    tq, nbt, cur, _ = _swa_specs(T)
    nt = T // tq
    nxt = lambda w: pl.BlockSpec((BLOCK, w), lambda i: (jnp.minimum(i + 1, nt - 1), 0))

    def body(dq_ref, dk_ref, dkp_ref, dv_ref, dvp_ref, du_ref, c_ref, s_ref, o_ref):
        more = (pl.program_id(0) < nt - 1).astype(F32)
        cv, sv = c_ref[...], s_ref[...]

        def whole(t_ref, p_ref):
            t, last = t_ref[...], t_ref[tq - BLOCK:, :] + more * p_ref[...]
            return last if nbt == 1 else jnp.concatenate([t[:tq - BLOCK], last], axis=0)

        o_ref[...] = jnp.concatenate(
            [_rope_bwd(dq_ref[...], cv, sv), _rope_bwd(whole(dk_ref, dkp_ref), cv, sv), whole(dv_ref, dvp_ref),
             du_ref[...]], axis=1).astype(o_ref.dtype)

    width = ATTN_WIDTH + 2 * KV_WIDTH + POOL_WIDTH
    return _pcall(body, name=name, grid=(nt,),
                  in_specs=[cur(ATTN_WIDTH), cur(KV_WIDTH), nxt(KV_WIDTH), cur(KV_WIDTH), nxt(KV_WIDTH),
                            cur(POOL_WIDTH), cur(LANES), cur(LANES)],
                  out_specs=cur(width), out_shape=_S((T, width), MXU),
                  compiler_params=_cp("parallel"))(dq, dk, dkp, dv, dvp, du, cos2, sin2)


def _pool_specs(T):
    tm = _tile(T, TOKEN_TILE)
    hb = tm // POOL_HALO
    nh = T // POOL_HALO
    tok = lambda w: pl.BlockSpec((tm, w), lambda i: (i, 0))
    before = pl.BlockSpec((POOL_HALO, POOL_WIDTH), lambda i: (jnp.maximum(i * hb - 1, 0), 0))
    after = pl.BlockSpec((POOL_HALO, POOL_WIDTH), lambda i: (jnp.minimum((i + 1) * hb, nh - 1), 0))
    return tm, tok, before, after


def _window_counts(i, tm, rows, w):
    t = i * tm + lax.broadcasted_iota(jnp.int32, (rows, 1), 0)
    return jnp.minimum(t + 1, w).astype(F32)


def _pooled(u_ext, i, tm):
    out = []
    for g, w in enumerate(POOL_WINDOWS):
        acc = u_ext[:, g * POOL_GROUP:(g + 1) * POOL_GROUP]
        tok = acc[POOL_HALO:, :]
        sh = 1
        while sh < w:
            acc = acc + pltpu.roll(acc, sh, 0)
            sh *= 2
        out.append(acc[POOL_HALO:, :] / _window_counts(i, tm, tm, w) - tok)
    return out


def _pool_fwd(u, out_a, pool_w, pool_scale, ga, gb, name):
    T = u.shape[0]
    tm, tok, before, _ = _pool_specs(T)

    def body(u_ref, halo_ref, oa_ref, pw_ref, sc_ref, ga_ref, gb_ref, ob_ref, mg_ref):
        i = pl.program_id(0)
        halo = halo_ref[...] * (i > 0).astype(F32)
        pooled = _pooled(jnp.concatenate([halo, u_ref[...]], axis=0), i, tm)
        mixed = [_nn(pooled[g].astype(MXU), pw_ref[g].astype(MXU)) for g in range(len(POOL_WINDOWS))]
        ob = jnp.concatenate(mixed, axis=1) * sc_ref[...]
        ob_ref[...] = ob
        oa = oa_ref[...]
        ra = lax.rsqrt(jnp.mean(oa * oa, axis=-1, keepdims=True) + EPS)
        rb = lax.rsqrt(jnp.mean(ob * ob, axis=-1, keepdims=True) + EPS)
        mg_ref[...] = jnp.concatenate([oa * ra * ga_ref[...], ob * rb * gb_ref[...]], axis=1).astype(mg_ref.dtype)

    vec = _resident((1, POOL_WIDTH))
    return _pcall(body, name=name, grid=(T // tm,),
                  in_specs=[tok(POOL_WIDTH), before, tok(ATTN_WIDTH), _resident(pool_w.shape), vec, vec, vec],
                  out_specs=[tok(POOL_WIDTH), tok(ATTN_WIDTH + POOL_WIDTH)],
                  out_shape=[_S((T, POOL_WIDTH), F32), _S((T, ATTN_WIDTH + POOL_WIDTH), MXU)],
                  compiler_params=_cp("parallel"))(u, u, out_a, pool_w, pool_scale, ga, gb)


def _pool_bwd(u, dob, pool_w, pool_scale, name):
    T = u.shape[0]
    tm, tok, before, after = _pool_specs(T)
    nt = T // tm
    G = len(POOL_WINDOWS)

    def body(u_ref, halo_ref, dob_ref, dnext_ref, pw_ref, sc_ref, du_ref, dpw_ref, dsc_ref):
        i = pl.program_id(0)
        halo = halo_ref[...] * (i > 0).astype(F32)
        pooled = _pooled(jnp.concatenate([halo, u_ref[...]], axis=0), i, tm)
        dnext = dnext_ref[...] * (i < nt - 1).astype(F32)
        dext = jnp.concatenate([dob_ref[...], dnext], axis=0) * sc_ref[...]

        @pl.when(i == 0)
        def _():
            dpw_ref[...] = jnp.zeros_like(dpw_ref)
            dsc_ref[...] = jnp.zeros_like(dsc_ref)

        dus, dscs = [], []
        for g, w in enumerate(POOL_WINDOWS):
            gs = slice(g * POOL_GROUP, (g + 1) * POOL_GROUP)
            pw = pw_ref[g].astype(MXU)
            pg = pooled[g].astype(MXU)
            dmix = dext[:, gs].astype(MXU)
            dscs.append(jnp.sum(dob_ref[:, gs] * _nn(pg, pw), axis=0, keepdims=True))
            dpw_ref[g] += _tn(pg, dmix[:tm, :])
            dpooled = _nt(dmix, pw)
            acc = dpooled / _window_counts(i, tm, tm + POOL_HALO, w)
            sh = 1
            while sh < w:
                acc = acc + pltpu.roll(acc, tm + POOL_HALO - sh, 0)
                sh *= 2
            dus.append(acc[:tm, :] - dpooled[:tm, :])
        du_ref[...] = jnp.concatenate(dus, axis=1)
        dsc_ref[...] += jnp.concatenate(dscs, axis=1)

    vec = _resident((1, POOL_WIDTH))
    return _pcall(body, name=name, grid=(nt,),
                  in_specs=[tok(POOL_WIDTH), before, tok(POOL_WIDTH), after, _resident(pool_w.shape), vec],
                  out_specs=[tok(POOL_WIDTH), _resident(pool_w.shape), vec],
                  out_shape=[_S((T, POOL_WIDTH), F32), _S(pool_w.shape, F32), _S((1, POOL_WIDTH), F32)],
                  compiler_params=_cp("arbitrary"))(u, u, dob, dob, pool_w, pool_scale)


def _xattn_probs(qh, kh):
    s = _nt(qh, kh) * (X_HEAD_DIM ** -0.5)
    e = jnp.exp(s - jnp.max(s, axis=1, keepdims=True))
    return e / jnp.sum(e, axis=1, keepdims=True)


def _xattn_fwd(q, kvm, name):
    T, XW = q.shape
    tm = _tile(T, TOKEN_TILE)

    def body(q_ref, kv_ref, o_ref):
        for h in range(X_HEADS):
            hs = slice(h * X_HEAD_DIM, (h + 1) * X_HEAD_DIM)
            vs = slice(XW + h * X_HEAD_DIM, XW + (h + 1) * X_HEAD_DIM)
            p = _xattn_probs(q_ref[:, hs], kv_ref[:, hs])
            o_ref[:, hs] = _nn(p.astype(MXU), kv_ref[:, vs]).astype(o_ref.dtype)

    return _pcall(body, name=name, grid=(T // tm,),
                  in_specs=[pl.BlockSpec((tm, XW), lambda i: (i, 0)), _resident(kvm.shape)],
                  out_specs=pl.BlockSpec((tm, XW), lambda i: (i, 0)), out_shape=_S((T, XW), MXU),
                  compiler_params=_cp("parallel"))(q, kvm)


def _xattn_bwd(q, kvm, do, name):
    T, XW = q.shape
    tm = _tile(T, TOKEN_TILE)

    def body(q_ref, kv_ref, do_ref, dq_ref, dkv_ref):
        @pl.when(pl.program_id(0) == 0)
        def _():
            dkv_ref[...] = jnp.zeros_like(dkv_ref)

        for h in range(X_HEADS):
            hs = slice(h * X_HEAD_DIM, (h + 1) * X_HEAD_DIM)
            vs = slice(XW + h * X_HEAD_DIM, XW + (h + 1) * X_HEAD_DIM)
            qh, doh = q_ref[:, hs], do_ref[:, hs]
            p = _xattn_probs(qh, kv_ref[:, hs])
            dp = _nt(doh, kv_ref[:, vs])
            ds = (p * (dp - jnp.sum(p * dp, axis=1, keepdims=True)) * (X_HEAD_DIM ** -0.5)).astype(MXU)
            dq_ref[:, hs] = _nn(ds, kv_ref[:, hs]).astype(dq_ref.dtype)
            dkv_ref[:, hs] += _tn(ds, qh)
            dkv_ref[:, vs] += _tn(p.astype(MXU), doh)

    tok = pl.BlockSpec((tm, XW), lambda i: (i, 0))
    return _pcall(body, name=name, grid=(T // tm,),
                  in_specs=[tok, _resident(kvm.shape), tok],
                  out_specs=[tok, _resident(kvm.shape)],
                  out_shape=[_S((T, XW), MXU), _S(kvm.shape, F32)],
                  compiler_params=_cp("arbitrary"))(q, kvm, do)


def _loss_head(x, tgt, g, name):
    T, D = x.shape
    tm = _tile(T, TOKEN_TILE)

    def body(x_ref, t_ref, g_ref, loss_ref, dx_ref, dg_ref):
        xv, gv = x_ref[...], g_ref[...]
        r = lax.rsqrt(jnp.mean(xv * xv, axis=-1, keepdims=True) + EPS)
        xh = xv * r
        e = xh * gv - t_ref[...]
        dy = e * (1.0 / D)
        dxn = dy * gv
        dx_ref[...] = r * (dxn - xh * jnp.mean(dxn * xh, axis=-1, keepdims=True))

        @pl.when(pl.program_id(0) == 0)
        def _():
            loss_ref[...] = jnp.zeros_like(loss_ref)
            dg_ref[...] = jnp.zeros_like(dg_ref)

        part = jnp.sum(jnp.sum(e * e, axis=1, keepdims=True), axis=0, keepdims=True) * (0.5 / D)
        loss_ref[...] += jnp.broadcast_to(part, (1, LANES))
        dg_ref[...] += jnp.sum(dy * xh, axis=0, keepdims=True)

    tok = pl.BlockSpec((tm, D), lambda i: (i, 0))
    return _pcall(body, name=name, grid=(T // tm,),
                  in_specs=[tok, tok, _resident((1, D))],
                  out_specs=[_resident((1, LANES)), tok, _resident((1, D))],
                  out_shape=[_S((1, LANES), F32), _S((T, D), F32), _S((1, D), F32)],
                  compiler_params=_cp("arbitrary"))(x, tgt, g)


def _rows_tile(rows):
    for t in (512, 416, 352, 256, 128, 64, 32, 16, 8):
        if rows % t == 0:
            return t
    return rows


def _pair_sum(grads, gots, sizes, place, name):
    nw = len(sizes)
    C = grads[0].shape[1]

    def body(p_ref, *refs):
        g, got, out = refs[:nw], refs[nw:2 * nw], refs[2 * nw:]
        for w in range(nw):
            out[w][...] = (g[w][...] + got[w][...]).astype(out[w].dtype)

    def blk(w):
        return (sizes[w] // 4, C)

    in_specs = [pl.BlockSpec(blk(w), lambda q, s, p: (4 * q + 2 * p[0] + s, 0)) for w in range(nw)]
    in_specs += [pl.BlockSpec(blk(w), lambda q, s, p: (2 * q + s, 0)) for w in range(nw)]
    out_specs = [pl.BlockSpec(blk(w), lambda q, s, p: (2 * q + s, 0)) for w in range(nw)]
    gs = pltpu.PrefetchScalarGridSpec(num_scalar_prefetch=1, grid=(N_CHIPS, 2), in_specs=in_specs, out_specs=out_specs)
    return _pcall(body, name=name, grid_spec=gs, out_shape=[_S((2 * n, C), MXU) for n in sizes],
                  compiler_params=_cp("parallel", "parallel"))(place, *grads, *gots)


def _final_sum(grads, gots, recvs, sizes, place, name):
    nw = len(sizes)
    C = grads[0].shape[1]

    def body(p_ref, *refs):
        g, got, rv, out = refs[:nw], refs[nw:2 * nw], refs[2 * nw:5 * nw], refs[5 * nw:]
        for w in range(nw):
            acc = g[w][...] + got[w][...]
            for j in range(3):
                acc = acc + rv[3 * w + j][...].astype(F32)
            out[w][...] = acc

    def blk(w):
        return (sizes[w] // 4, C)

    in_specs = [pl.BlockSpec(blk(w), lambda s, p: (4 * p[1] + 2 * p[0] + s, 0)) for w in range(nw)]
    in_specs += [pl.BlockSpec(blk(w), lambda s, p: (2 * p[1] + s, 0)) for w in range(nw)]
    args = list(grads) + list(gots)
    for w in range(nw):
        for j in range(3):
            in_specs.append(pl.BlockSpec(blk(w), lambda s, p, j=j: (2 * j + s, 0)))
            args.append(recvs[w])
    out_specs = [pl.BlockSpec(blk(w), lambda s, p: (2 * p[0] + s, 0)) for w in range(nw)]
    gs = pltpu.PrefetchScalarGridSpec(num_scalar_prefetch=1, grid=(2,), in_specs=in_specs, out_specs=out_specs)
    return _pcall(body, name=name, grid_spec=gs, out_shape=[_S((n, C), F32) for n in sizes],
                  compiler_params=_cp("parallel"))(place, *args)


def _adamw(w, g, m, v, name):
    R, C = w.shape
    tr = _rows_tile(R)
    c1 = 1.0 / (1.0 - ADAM_B1 ** ADAM_STEP)
    c2 = 1.0 / (1.0 - ADAM_B2 ** ADAM_STEP)

    def body(w_ref, g_ref, m_ref, v_ref, d_ref, nm_ref, nv_ref):
        gv = g_ref[...]
        nm = ADAM_B1 * m_ref[...] + (1.0 - ADAM_B1) * gv
        nv = ADAM_B2 * v_ref[...] + (1.0 - ADAM_B2) * (gv * gv)
        d_ref[...] = -ADAM_LR * ((nm * c1) / (jnp.sqrt(nv * c2) + ADAM_EPS) + ADAM_WD * w_ref[...])
        nm_ref[...] = nm
        nv_ref[...] = nv

    spec = pl.BlockSpec((tr, C), lambda i: (i, 0))
    return _pcall(body, name=name, grid=(R // tr,), in_specs=[spec] * 4, out_specs=[spec] * 3,
                  out_shape=[_S((R, C), F32)] * 3, compiler_params=_cp("parallel"))(w, g, m, v)


ANY = pl.BlockSpec(memory_space=pl.ANY)


def _place():
    x, y, c = lax.axis_index("x"), lax.axis_index("y"), lax.axis_index("c")
    chips = [(1 - x, y), (x, 1 - y), (1 - x, 1 - y)]
    return x, y, c, chips


def _remote(src, dst, send_sem, recv_sem, dev):
    return pltpu.make_async_remote_copy(src_ref=src, dst_ref=dst, send_sem=send_sem, recv_sem=recv_sem,
                                        device_id=dev, device_id_type=MESH)


def _drain(like, send_sem, recv_sem, me, *, send=False, recv=False):
    d = _remote(like, like, send_sem, recv_sem, me)
    if send:
        d.wait_send()
    if recv:
        d.wait_recv()


def _dma_sems(n):
    return [pltpu.SemaphoreType.DMA((n,)), pltpu.SemaphoreType.DMA((n,))]


def _comm_params():
    return pltpu.CompilerParams(has_side_effects=True)


def _on_sequencer(exchange, refs, sem_types, peers_of, name, seq_id):
    def launch(*sems):
        x, y, c, chips = _place()
        barrier = pltpu.get_barrier_semaphore()
        peers = peers_of(x, y, c, chips)
        for peer in peers:
            pl.semaphore_signal(barrier, inc=1, device_id=peer, device_id_type=MESH)
        pl.semaphore_wait(barrier, len(peers))
        exchange(refs, *sems)

    pl.kernel(launch, mesh=plsc.ScalarSubcoreMesh(axis_name="seq", num_cores=1), name=name,
              scratch_types=tuple(sem_types), compiler_params=pltpu.CompilerParams(collective_id=seq_id))()


def _hbm_ref(a):
    return jax.new_ref(a, memory_space=pltpu.MemorySpace.HBM)


def _allgather_weights(bufs, sizes, name, seq_id=None):
    nw = len(sizes)

    def exchange(out, s_ici, r_ici, s_fwd, r_fwd):
        x, y, c, chips = _place()
        me, sib = (x, y, c), (x, y, 1 - c)
        q_me = 2 * x + y

        def rows(w, q):
            hw = sizes[w] // 2
            return out[w].at[pl.ds(q * sizes[w] + c * hw, hw)]

        def three(w):
            return out[w].at[pl.ds(0, 3 * (sizes[w] // 2))]

        for w in range(nw):
            for px, py in chips:
                _remote(rows(w, q_me), rows(w, q_me), s_ici.at[w], r_ici.at[w], (px, py, c)).start()
        for w in range(nw):
            _drain(three(w), s_ici.at[w], r_ici.at[w], me, recv=True)
            for px, py in chips:
                got = rows(w, 2 * px + py)
                _remote(got, got, s_fwd.at[w], r_fwd.at[w], sib).start()
        for w in range(nw):
            _drain(three(w), s_fwd.at[w], r_fwd.at[w], me, recv=True)
        for w in range(nw):
            _drain(three(w), s_ici.at[w], r_ici.at[w], me, send=True)
            _drain(three(w), s_fwd.at[w], r_fwd.at[w], me, send=True)

    if seq_id is not None:
        refs = [_hbm_ref(b) for b in bufs]
        _on_sequencer(exchange, refs, _dma_sems(nw) + _dma_sems(nw),
                      lambda x, y, c, chips: [(x, y, 1 - c)] + [(px, py, c) for px, py in chips], name, seq_id)
        return [r[...] for r in refs]

    def body(*refs):
        exchange(refs[nw:2 * nw], *refs[2 * nw:])

    return _pcall(body, name=name, in_specs=[ANY] * nw, out_specs=[ANY] * nw,
                  out_shape=[_S(b.shape, b.dtype) for b in bufs],
                  input_output_aliases={w: w for w in range(nw)},
                  scratch_shapes=_dma_sems(nw) + _dma_sems(nw), compiler_params=_comm_params())(*bufs)


def _rs_pair_exchange(grads, sizes, name):
    C = grads[0].shape[1]
    nw = len(sizes)

    def body(*refs):
        g, got = refs[:nw], refs[nw:2 * nw]
        s_sem, r_sem = refs[2 * nw:]
        x, y, c, _ = _place()
        me, sib = (x, y, c), (x, y, 1 - c)
        for w in range(nw):
            hw = sizes[w] // 2
            for q in range(N_CHIPS):
                _remote(g[w].at[pl.ds(q * sizes[w] + (1 - c) * hw, hw)], got[w].at[pl.ds(q * hw, hw)],
                        s_sem.at[w], r_sem.at[w], sib).start()
        for w in range(nw):
            _drain(got[w], s_sem.at[w], r_sem.at[w], me, send=True, recv=True)

    return _pcall(body, name=name, in_specs=[ANY] * nw, out_specs=[ANY] * nw,
                  out_shape=[_S((2 * n, C), F32) for n in sizes],
                  scratch_shapes=_dma_sems(nw), compiler_params=_comm_params())(*grads)


def _rs_chip_exchange(sums, sizes, name, seq_id=None):
    C = sums[0].shape[1]
    nw = len(sizes)
    out_shape = [_S((3 * (n // 2), C), sums[0].dtype) for n in sizes]

    def exchange(refs, s_sem, r_sem):
        sm, got = refs[:nw], refs[nw:2 * nw]
        x, y, c, chips = _place()
        for w in range(nw):
            hw = sizes[w] // 2
            for j, (px, py) in enumerate(chips):
                _remote(sm[w].at[pl.ds((2 * px + py) * hw, hw)], got[w].at[pl.ds(j * hw, hw)],
                        s_sem.at[w], r_sem.at[w], (px, py, c)).start()
        for w in range(nw):
            _drain(got[w], s_sem.at[w], r_sem.at[w], (x, y, c), send=True, recv=True)

    if seq_id is not None:
        gots = [jax.empty_ref(s, memory_space=pltpu.MemorySpace.HBM) for s in out_shape]
        _on_sequencer(exchange, [_hbm_ref(s) for s in sums] + gots, _dma_sems(nw),
                      lambda x, y, c, chips: [(px, py, c) for px, py in chips], name, seq_id)
        return [r[...] for r in gots]

    def body(*refs):
        exchange(refs[:2 * nw], *refs[2 * nw:])

    return _pcall(body, name=name, in_specs=[ANY] * nw, out_specs=[ANY] * nw, out_shape=out_shape,
                  scratch_shapes=_dma_sems(nw), compiler_params=_comm_params())(*sums)


def _rs_share_halves(reds, sizes, name):
    nw = len(sizes)

    def body(*refs):
        out = refs[nw:2 * nw]
        s_sem, r_sem = refs[2 * nw:]
        x, y, c, _ = _place()
        for w in range(nw):
            hw = sizes[w] // 2
            rows = out[w].at[pl.ds(c * hw, hw)]
            _remote(rows, rows, s_sem.at[w], r_sem.at[w], (x, y, 1 - c)).start()
        for w in range(nw):
            _drain(out[w].at[pl.ds(0, sizes[w] // 2)], s_sem.at[w], r_sem.at[w], (x, y, c), send=True, recv=True)

    return _pcall(body, name=name, in_specs=[ANY] * nw, out_specs=[ANY] * nw,
                  out_shape=[_S(r.shape, r.dtype) for r in reds],
                  input_output_aliases={w: w for w in range(nw)},
                  scratch_shapes=_dma_sems(nw), compiler_params=_comm_params())(*reds)


def _allreduce_small(part, name):
    R, C = part.shape

    def body(p_ref, o_ref, buf, s_sem, r_sem):
        x, y, c, _ = _place()
        my_id = 4 * x + 2 * y + c
        buf[my_id] = p_ref[...]
        cps = []
        for k in range(1, N_DEV):
            fx, fy, fc = (k >> 2) & 1, (k >> 1) & 1, k & 1
            peer = (x ^ fx, y ^ fy, c ^ fc)
            cps.append(_remote(p_ref, buf.at[my_id], s_sem.at[k - 1], r_sem.at[k - 1], peer))
        for cp in cps:
            cp.start()
        for cp in cps:
            cp.wait()
        acc = buf[0]
        for d in range(1, N_DEV):
            acc = acc + buf[d]
        o_ref[...] = acc

    vm = pl.BlockSpec(memory_space=pltpu.VMEM)
    return _pcall(body, name=name, in_specs=[vm], out_specs=vm, out_shape=_S((R, C), F32),
                  scratch_shapes=[pltpu.VMEM((N_DEV, R, C), F32)] + _dma_sems(N_DEV - 1),
                  compiler_params=pltpu.CompilerParams(has_side_effects=True, vmem_limit_bytes=VMEM_LIMIT))(part)


def _own_shard_buffers(ws, l, q_me):
    bufs, sizes = [], []
    for name, tr in BIG:
        w = ws[name][l]
        ch = (w.T if tr else w).astype(MXU)
        n = ch.shape[0]
        bufs.append(lax.dynamic_update_slice(jnp.zeros((N_CHIPS * n, ch.shape[1]), MXU), ch, (q_me * n, 0)))
        sizes.append(n)
    return bufs, sizes


def _small_rows(v):
    flat = v.reshape(-1)
    pad = (-flat.shape[0]) % 1024
    return jnp.pad(flat, (0, pad)).reshape(-1, 1024)


def _pack_small(vals):
    rows = [_small_rows(vals[n]) for n in SMALL]
    cat = jnp.concatenate(rows, axis=0)
    pad = (-cat.shape[0]) % 8
    return jnp.pad(cat, ((0, pad), (0, 0)))


def _unpack_small(packed, like):
    out, r = {}, 0
    for n in SMALL:
        size = like[n].size
        nr = -(-size // 1024)
        out[n] = packed[r:r + nr].reshape(-1)[:size].reshape(like[n].shape)
        r += nr
    return out


def _rope_tables(positions):
    inv_freq = ROPE_THETA ** (-jnp.arange(0, HEAD_DIM, 2, dtype=F32) / HEAD_DIM)
    ang = positions.astype(F32)[:, None] * inv_freq
    cos, sin = jnp.cos(ang), jnp.sin(ang)
    return jnp.concatenate([cos, cos, cos, cos], axis=1), jnp.concatenate([-sin, sin, -sin, sin], axis=1)


def _layer_fwd(l, x, memv, W, P, cos2, sin2):
    t = f"l{l}"
    sv = {"x0": x}
    sv["h1"] = _rms_fwd(x, P["ffn1_norm"], t + "_ffn1_norm")
    sv["a1"], sv["b1"], sv["s1"] = _ffn_up(sv["h1"], W["ffn1_w_gate"], W["ffn1_w_up"], t + "_ffn1_up")
    sv["x1"] = _mm([(sv["s1"], W["ffn1_w_down"])], nt=False, out_dtype=F32, res=x, res_scale=FFN_RES, name=t + "_ffn1_down")

    sv["h2"] = _rms_fwd(sv["x1"], P["mix_norm"], t + "_mix_norm")
    sv["q"], sv["k"], sv["v"], sv["u"] = _in_proj(sv["h2"], W["w_in"], cos2, sin2, t + "_in_proj")
    sv["oa"] = _swa_fwd(sv["q"], sv["k"], sv["v"], P["attn_sinks"], t + "_swa")
    sv["ob"], sv["mg"] = _pool_fwd(sv["u"], sv["oa"], P["pool_w"], P["pool_scale"], P["attn_out_norm"],
                                   P["pool_out_norm"], t + "_pool")
    sv["x2"] = _mm([(sv["mg"], W["w_out"])], nt=False, out_dtype=F32, res=sv["x1"], name=t + "_out_proj")

    sv["h3"] = _rms_fwd(sv["x2"], P["xattn_norm"], t + "_xattn_norm")
    sv["memn"] = _rms_fwd(memv, P["mem_norm"], t + "_mem_norm")
    sv["q3"] = _mm([(sv["h3"], W["xattn_wq"])], nt=False, out_dtype=MXU, name=t + "_xq")
    sv["kv"] = _mm([(sv["memn"], W["xattn_wkv"])], nt=True, out_dtype=MXU, name=t + "_xkv")
    sv["o3"] = _xattn_fwd(sv["q3"], sv["kv"], t + "_xattn")
    sv["x3"] = _mm([(sv["o3"], W["xattn_wo"])], nt=False, out_dtype=F32, res=sv["x2"], name=t + "_xo")

    sv["h4"] = _rms_fwd(sv["x3"], P["ffn2_norm"], t + "_ffn2_norm")
    sv["a2"], sv["b2"], sv["s2"] = _ffn_up(sv["h4"], W["ffn2_w_gate"], W["ffn2_w_up"], t + "_ffn2_up")
    x4 = _mm([(sv["s2"], W["ffn2_w_down"])], nt=False, out_dtype=F32, res=sv["x3"], res_scale=FFN_RES, name=t + "_ffn2_down")
    return x4, sv


def _ffn_bwd(t, dx, x_in, g, h, a, b, s, wgT, wuT, wd):
    d_wd = _mm_tn(s, dx, t + "_dwd", r_scale=FFN_RES)
    da, db = _ffn_mid_bwd(dx, wd, a, b, t + "_mid")
    d_wg = _mm_tn(da, h, t + "_dwg")
    d_wu = _mm_tn(db, h, t + "_dwu")
    dh = _mm([(da, wgT), (db, wuT)], nt=False, out_dtype=F32, name=t + "_dh", tm=TOKEN_TILE // 2)
    dx_in, dg = _rms_bwd(x_in, g, dh, t + "_norm_bwd", dres=dx)
    return dx_in, dg, d_wg, d_wu, d_wd


def _layer_bwd(l, dx, sv, memv, W, P, cos2, sin2):
    t = f"l{l}b"
    GW, GP = {}, {}
    dx, GP["ffn2_norm"], GW["ffn2_w_gate"], GW["ffn2_w_up"], GW["ffn2_w_down"] = _ffn_bwd(
        t + "_ffn2", dx, sv["x3"], P["ffn2_norm"], sv["h4"], sv["a2"], sv["b2"], sv["s2"],
        W["ffn2_w_gate"], W["ffn2_w_up"], W["ffn2_w_down"])

    GW["xattn_wo"] = _mm_tn(sv["o3"], dx, t + "_dwo")
    do3 = _mm([(dx, W["xattn_wo"])], nt=True, out_dtype=MXU, name=t + "_do3")
    dq3, dkv = _xattn_bwd(sv["q3"], sv["kv"], do3, t + "_xattn")
    GW["xattn_wq"] = _mm_tn(sv["h3"], dq3, t + "_dwq")
    dh3 = _mm([(dq3, W["xattn_wq"])], nt=True, out_dtype=F32, name=t + "_dh3")
    GW["xattn_wkv"] = _mm_tn(dkv, sv["memn"], t + "_dwkv")
    dmemn = _mm([(dkv, W["xattn_wkv"])], nt=False, out_dtype=F32, name=t + "_dmemn")
    _, GP["mem_norm"] = _rms_bwd(memv, P["mem_norm"], dmemn, t + "_mem_norm_bwd")
    dx, GP["xattn_norm"] = _rms_bwd(sv["x2"], P["xattn_norm"], dh3, t + "_xattn_norm_bwd", dres=dx)

    GW["w_out"] = _mm_tn(sv["mg"], dx, t + "_dwout")
    dmg = _mm([(dx, W["w_out"])], nt=True, out_dtype=F32, name=t + "_dmg")
    doa, GP["attn_out_norm"] = _rms_bwd(sv["oa"], P["attn_out_norm"], dmg, t + "_oa_norm_bwd", col=0)
    dob, GP["pool_out_norm"] = _rms_bwd(sv["ob"], P["pool_out_norm"], dmg, t + "_ob_norm_bwd", col=1)
    du, GP["pool_w"], GP["pool_scale"] = _pool_bwd(sv["u"], dob, P["pool_w"], P["pool_scale"], t + "_pool")
    dq, dko, dkp, dvo, dvp, dsk = _swa_bwd(sv["q"], sv["k"], sv["v"], doa, P["attn_sinks"], t + "_swa")
    GP["attn_sinks"] = dsk[:, 0]
    dpj = _dproj(dq, dko, dkp, dvo, dvp, du, cos2, sin2, t + "_dproj")
    GW["w_in"] = _mm_tn(dpj, sv["h2"], t + "_dwin")
    dh2 = _mm([(dpj, W["w_in"])], nt=False, out_dtype=F32, name=t + "_dh2")
    dx, GP["mix_norm"] = _rms_bwd(sv["x1"], P["mix_norm"], dh2, t + "_mix_norm_bwd", dres=dx)

    dx, GP["ffn1_norm"], GW["ffn1_w_gate"], GW["ffn1_w_up"], GW["ffn1_w_down"] = _ffn_bwd(
        t + "_ffn1", dx, sv["x0"], P["ffn1_norm"], sv["h1"], sv["a1"], sv["b1"], sv["s1"],
        W["ffn1_w_gate"], W["ffn1_w_up"], W["ffn1_w_down"])
    return dx, GW, GP


def _reduce_layer(l, GW, sizes, place):
    t = f"l{l}r"
    grads = [GW[name] for name, _ in BIG]
    gots = _rs_pair_exchange(grads, sizes, t + "_pair")
    sums = _pair_sum(grads, gots, sizes, place, t + "_pair_sum")
    recvs = _rs_chip_exchange(sums, sizes, t + "_chips", seq_id=SEQ_REDUCE_ID if l > 0 else None)
    ffn = [w for w, (name, _) in enumerate(BIG) if "ffn" in name]
    rest = [w for w in range(len(BIG)) if w not in ffn]
    reds = [None] * len(BIG)
    for tag, group in (("_final_sum_ffn", ffn), ("_final_sum_mix", rest)):
        pick = lambda xs: [xs[w] for w in group]
        for w, r in zip(group, _final_sum(pick(grads), pick(gots), pick(recvs), pick(sizes), place, t + tag)):
            reds[w] = r
    reds = _rs_share_halves(reds, sizes, t + "_share")
    return {name: (r.T if tr else r) for (name, tr), r in zip(BIG, reds)}


def kernel(x, mem, positions, ffn1_norm, ffn1_w_gate, ffn1_w_up, ffn1_w_down, mix_norm, w_in, attn_sinks, pool_w, pool_scale, attn_out_norm, pool_out_norm, w_out, xattn_norm, mem_norm, xattn_wq, xattn_wkv, xattn_wo, ffn2_norm, ffn2_w_gate, ffn2_w_up, ffn2_w_down, final_norm, loss_target, m_ffn1_norm, m_ffn1_w_gate, m_ffn1_w_up, m_ffn1_w_down, m_mix_norm, m_w_in, m_attn_sinks, m_pool_w, m_pool_scale, m_attn_out_norm, m_pool_out_norm, m_w_out, m_xattn_norm, m_mem_norm, m_xattn_wq, m_xattn_wkv, m_xattn_wo, m_ffn2_norm, m_ffn2_w_gate, m_ffn2_w_up, m_ffn2_w_down, m_final_norm, v_ffn1_norm, v_ffn1_w_gate, v_ffn1_w_up, v_ffn1_w_down, v_mix_norm, v_w_in, v_attn_sinks, v_pool_w, v_pool_scale, v_attn_out_norm, v_pool_out_norm, v_w_out, v_xattn_norm, v_mem_norm, v_xattn_wq, v_xattn_wkv, v_xattn_wo, v_ffn2_norm, v_ffn2_w_gate, v_ffn2_w_up, v_ffn2_w_down, v_final_norm):
    ws = dict(ffn1_norm=ffn1_norm, ffn1_w_gate=ffn1_w_gate, ffn1_w_up=ffn1_w_up, ffn1_w_down=ffn1_w_down,
              mix_norm=mix_norm, w_in=w_in, attn_sinks=attn_sinks, pool_w=pool_w, pool_scale=pool_scale,
              attn_out_norm=attn_out_norm, pool_out_norm=pool_out_norm, w_out=w_out, xattn_norm=xattn_norm,
              mem_norm=mem_norm, xattn_wq=xattn_wq, xattn_wkv=xattn_wkv, xattn_wo=xattn_wo, ffn2_norm=ffn2_norm,
              ffn2_w_gate=ffn2_w_gate, ffn2_w_up=ffn2_w_up, ffn2_w_down=ffn2_w_down, final_norm=final_norm)
    ms = dict(ffn1_norm=m_ffn1_norm, ffn1_w_gate=m_ffn1_w_gate, ffn1_w_up=m_ffn1_w_up, ffn1_w_down=m_ffn1_w_down,
              mix_norm=m_mix_norm, w_in=m_w_in, attn_sinks=m_attn_sinks, pool_w=m_pool_w, pool_scale=m_pool_scale,
              attn_out_norm=m_attn_out_norm, pool_out_norm=m_pool_out_norm, w_out=m_w_out, xattn_norm=m_xattn_norm,
              mem_norm=m_mem_norm, xattn_wq=m_xattn_wq, xattn_wkv=m_xattn_wkv, xattn_wo=m_xattn_wo,
              ffn2_norm=m_ffn2_norm, ffn2_w_gate=m_ffn2_w_gate, ffn2_w_up=m_ffn2_w_up, ffn2_w_down=m_ffn2_w_down,
              final_norm=m_final_norm)
    vs = dict(ffn1_norm=v_ffn1_norm, ffn1_w_gate=v_ffn1_w_gate, ffn1_w_up=v_ffn1_w_up, ffn1_w_down=v_ffn1_w_down,
              mix_norm=v_mix_norm, w_in=v_w_in, attn_sinks=v_attn_sinks, pool_w=v_pool_w, pool_scale=v_pool_scale,
              attn_out_norm=v_attn_out_norm, pool_out_norm=v_pool_out_norm, w_out=v_w_out, xattn_norm=v_xattn_norm,
              mem_norm=v_mem_norm, xattn_wq=v_xattn_wq, xattn_wkv=v_xattn_wkv, xattn_wo=v_xattn_wo,
              ffn2_norm=v_ffn2_norm, ffn2_w_gate=v_ffn2_w_gate, ffn2_w_up=v_ffn2_w_up, ffn2_w_down=v_ffn2_w_down,
              final_norm=v_final_norm)
    depth = ffn1_norm.shape[0]
    T, D = x.shape[1], x.shape[2]
    xv = x.reshape(T, D)
    memv = mem.reshape(mem.shape[1], D)
    tgt = loss_target.reshape(T, D)
    cos2, sin2 = _rope_tables(positions.reshape(T))

    q_me = 2 * lax.axis_index("x") + lax.axis_index("y")
    place = jnp.stack([lax.axis_index("c"), q_me]).astype(jnp.int32)
    Ws, sizes = [], None
    for l in range(depth):
        bufs, sizes = _own_shard_buffers(ws, l, q_me)
        full = _allgather_weights(bufs, sizes, f"l{l}_allgather", seq_id=SEQ_GATHER_ID if l > 0 else None)
        Ws.append({name: f for (name, _), f in zip(BIG, full)})
    Ps = [{n: (ws[n][l].reshape(1, -1) if n != "pool_w" else ws[n][l]) for n in SMALL if n != "final_norm"}
          for l in range(depth)]

    saved = []
    h = xv
    for l in range(depth):
        h, sv = _layer_fwd(l, h, memv, Ws[l], Ps[l], cos2, sin2)
        saved.append(sv)
    loss_row, dx, d_final = _loss_head(h, tgt, final_norm.reshape(1, D), "loss_head")
    GWs, GPs = [None] * depth, [None] * depth
    for l in reversed(range(depth)):
        dx, GWs[l], GPs[l] = _layer_bwd(l, dx, saved[l], memv, Ws[l], Ps[l], cos2, sin2)

    big_g = [_reduce_layer(l, GWs[l], sizes, place) for l in range(depth)]
    small_part = {n: jnp.stack([GPs[l][n].reshape(ws[n].shape[1:]) for l in range(depth)]) for n in SMALL if n != "final_norm"}
    small_part["final_norm"] = d_final.reshape(D)
    small_g = _unpack_small(_allreduce_small(_pack_small(small_part), "small_allreduce"), ws)
    loss = lax.psum(loss_row[0, 0], ("x", "y", "c"))

    grads, deltas, new_m, new_v = {}, {}, {}, {}
    for name, _ in BIG:
        g = jnp.stack([big_g[l][name] for l in range(depth)])
        shp = g.shape
        flat = lambda a: a.reshape(shp[0] * shp[1], shp[2])
        d, nm, nv = _adamw(flat(ws[name]), flat(g), flat(ms[name]), flat(vs[name]), "adamw_" + name)
        grads[name], deltas[name], new_m[name], new_v[name] = g, d.reshape(shp), nm.reshape(shp), nv.reshape(shp)
    d, nm, nv = _adamw(_pack_small(ws), _pack_small(small_g), _pack_small(ms), _pack_small(vs), "adamw_small")
    grads.update(small_g)
    deltas.update(_unpack_small(d, ws))
    new_m.update(_unpack_small(nm, ws))
    new_v.update(_unpack_small(nv, ws))

    grad_x = dx.reshape(x.shape)
    return (loss, grad_x, *[grads[n] for n in WEIGHTS], *[deltas[n] for n in WEIGHTS],
            *[new_m[n] for n in WEIGHTS], *[new_v[n] for n in WEIGHTS])
```

```python
import functools

import jax
import jax.numpy as jnp
from jax import lax
from jax.experimental import pallas as pl
from jax.experimental.pallas import tpu as pltpu
from jax.experimental.pallas import tpu_sc as plsc

F32 = jnp.float32
MXU = jnp.bfloat16

EPS = 1e-6
HEAD_DIM = 64
N_Q_HEADS = 8
N_KV_HEADS = 2
Q_PER_KV = N_Q_HEADS // N_KV_HEADS
ATTN_WIDTH = N_Q_HEADS * HEAD_DIM
KV_WIDTH = N_KV_HEADS * HEAD_DIM
BLOCK = 128
ROPE_THETA = 10000.0
POOL_WINDOWS = (2, 4, 8, 16)
POOL_GROUP = 128
POOL_WIDTH = len(POOL_WINDOWS) * POOL_GROUP
POOL_HALO = 16
X_HEADS = 4
X_HEAD_DIM = 256
FFN_RES = 0.5
NEG = -1e30
ADAM_LR = 0.001
ADAM_B1 = 0.9
ADAM_B2 = 0.999
ADAM_EPS = 1e-08
ADAM_WD = 0.01
ADAM_STEP = 10

N_CHIPS = 4
N_DEV = 8
V7X_VMEM_BYTES = 64 * 1024 * 1024
VMEM_LIMIT = V7X_VMEM_BYTES - 8 * 1024 * 1024
LANES = 128
TOKEN_TILE = 512
MESH = pl.DeviceIdType.MESH

BIG = (("ffn1_w_gate", True), ("ffn1_w_up", True), ("ffn1_w_down", False), ("w_in", True), ("w_out", False),
       ("xattn_wq", False), ("xattn_wkv", True), ("xattn_wo", False),
       ("ffn2_w_gate", True), ("ffn2_w_up", True), ("ffn2_w_down", False))
GROUPS = (BIG[:3], BIG[3:])
SMALL = ("ffn1_norm", "mix_norm", "attn_sinks", "pool_w", "pool_scale", "attn_out_norm", "pool_out_norm",
         "xattn_norm", "mem_norm", "ffn2_norm", "final_norm")
WEIGHTS = ("ffn1_norm", "ffn1_w_gate", "ffn1_w_up", "ffn1_w_down", "mix_norm", "w_in", "attn_sinks", "pool_w",
           "pool_scale", "attn_out_norm", "pool_out_norm", "w_out", "xattn_norm", "mem_norm", "xattn_wq",
           "xattn_wkv", "xattn_wo", "ffn2_norm", "ffn2_w_gate", "ffn2_w_up", "ffn2_w_down", "final_norm")


def _S(shape, dtype):
    return jax.ShapeDtypeStruct(tuple(shape), dtype)


def _pcall(body, **kw):
    return pl.pallas_call(body, **kw)


def _cp(*sem):
    return pltpu.CompilerParams(dimension_semantics=tuple(sem), vmem_limit_bytes=VMEM_LIMIT)


def _nt(a, b):
    return lax.dot_general(a, b, (((1,), (1,)), ((), ())), preferred_element_type=F32)


def _nn(a, b):
    return lax.dot_general(a, b, (((1,), (0,)), ((), ())), preferred_element_type=F32)


def _tn(a, b):
    return lax.dot_general(a, b, (((0,), (0,)), ((), ())), preferred_element_type=F32)


def _tile(n, want):
    t = min(n, want)
    assert n % t == 0, (n, want)
    return t


def _resident(shape):
    nd = len(shape)
    return pl.BlockSpec(tuple(shape), lambda *_: (0,) * nd)


def _rms_fwd(x, g, name):
    T, C = x.shape
    tm = _tile(T, TOKEN_TILE)

    def body(x_ref, g_ref, o_ref):
        xv = x_ref[...]
        r = lax.rsqrt(jnp.mean(xv * xv, axis=-1, keepdims=True) + EPS)
        o_ref[...] = (xv * r * g_ref[...]).astype(o_ref.dtype)

    return _pcall(body, name=name, grid=(T // tm,),
                  in_specs=[pl.BlockSpec((tm, C), lambda i: (i, 0)), _resident((1, C))],
                  out_specs=pl.BlockSpec((tm, C), lambda i: (i, 0)),
                  out_shape=_S((T, C), MXU), compiler_params=_cp("parallel"))(x, g)


def _rms_bwd(x, g, dh, name, dres=None, col=0):
    T, C = x.shape
    tm = _tile(T, TOKEN_TILE)

    def body(*refs):
        if dres is None:
            x_ref, g_ref, dh_ref, dx_ref, dg_ref = refs
        else:
            x_ref, g_ref, dh_ref, dres_ref, dx_ref, dg_ref = refs
        xv = x_ref[...]
        r = lax.rsqrt(jnp.mean(xv * xv, axis=-1, keepdims=True) + EPS)
        xh = xv * r
        dhv = dh_ref[...].astype(F32)
        dxn = dhv * g_ref[...]
        dx = r * (dxn - xh * jnp.mean(dxn * xh, axis=-1, keepdims=True))
        if dres is not None:
            dx = dx + dres_ref[...]
        dx_ref[...] = dx

        @pl.when(pl.program_id(0) == 0)
        def _():
            dg_ref[...] = jnp.zeros_like(dg_ref)

        dg_ref[...] += jnp.sum(dhv * xh, axis=0, keepdims=True)

    tok = pl.BlockSpec((tm, C), lambda i: (i, 0))
    in_specs = [tok, _resident((1, C)), pl.BlockSpec((tm, C), lambda i: (i, col))]
    args = [x, g, dh]
    if dres is not None:
        in_specs.append(tok)
        args.append(dres)
    return _pcall(body, name=name, grid=(T // tm,), in_specs=in_specs,
                  out_specs=[tok, _resident((1, C))],
                  out_shape=[_S((T, C), F32), _S((1, C), F32)], compiler_params=_cp("arbitrary"))(*args)


def _mm(pairs, *, nt, out_dtype, name, res=None, res_scale=1.0, a_scale=1.0, tm=TOKEN_TILE):
    M = pairs[0][0].shape[0]
    N = pairs[0][1].shape[0] if nt else pairs[0][1].shape[1]
    tm = _tile(M, tm)
    n = len(pairs)

    def body(*refs):
        a_refs, w_refs = refs[:n], refs[n:2 * n]
        o_ref = refs[-1]
        acc = None
        for a_ref, w_ref in zip(a_refs, w_refs):
            a = a_ref[...]
            if a_scale != 1.0:
                a = a * a_scale
            a = a.astype(MXU)
            p = _nt(a, w_ref[...]) if nt else _nn(a, w_ref[...])
            acc = p if acc is None else acc + p
        if res is not None:
            acc = refs[2 * n][...] + res_scale * acc
        o_ref[...] = acc.astype(o_ref.dtype)

    in_specs = [pl.BlockSpec((tm, a.shape[1]), lambda i: (i, 0)) for a, _ in pairs]
    in_specs += [_resident(w.shape) for _, w in pairs]
    args = [a for a, _ in pairs] + [w for _, w in pairs]
    if res is not None:
        in_specs.append(pl.BlockSpec((tm, N), lambda i: (i, 0)))
        args.append(res)
    return _pcall(body, name=name, grid=(M // tm,), in_specs=in_specs,
                  out_specs=pl.BlockSpec((tm, N), lambda i: (i, 0)),
                  out_shape=_S((M, N), out_dtype), compiler_params=_cp("parallel"))(*args)


def _mm_tn(l, r, name, *, l_scale=1.0, r_scale=1.0, tr=1408, tt=TOKEN_TILE):
    T, R = l.shape
    C = r.shape[1]
    tt = _tile(T, tt)
    tr = tr if R % tr == 0 else (1024 if R % 1024 == 0 and R > 1280 else R)

    def body(l_ref, r_ref, o_ref):
        lv, rv = l_ref[...], r_ref[...]
        if l_scale != 1.0:
            lv = lv * l_scale
        if r_scale != 1.0:
            rv = rv * r_scale
        lv, rv = lv.astype(MXU), rv.astype(MXU)

        @pl.when(pl.program_id(1) == 0)
        def _():
            o_ref[...] = jnp.zeros_like(o_ref)

        o_ref[...] += _tn(lv, rv)

    return _pcall(body, name=name, grid=(R // tr, T // tt),
                  in_specs=[pl.BlockSpec((tt, tr), lambda i, t: (t, i)), pl.BlockSpec((tt, C), lambda i, t: (t, 0))],
                  out_specs=pl.BlockSpec((tr, C), lambda i, t: (i, 0)),
                  out_shape=_S((R, C), F32), compiler_params=_cp("parallel", "arbitrary"))(l, r)


FFN_COL_TILE = 1408


def _ffn_up(h, wgT, wuT, name):
    T, D = h.shape
    Fd = wgT.shape[0]
    tm, tn = _tile(T, TOKEN_TILE), _tile(Fd, FFN_COL_TILE)

    def body(h_ref, wg_ref, wu_ref, a_ref, b_ref, s_ref):
        hv = h_ref[...]
        a = _nt(hv, wg_ref[...])
        b = _nt(hv, wu_ref[...])
        s = a * (1.0 / (1.0 + jnp.exp(-a))) * b
        a_ref[...] = a.astype(a_ref.dtype)
        b_ref[...] = b.astype(b_ref.dtype)
        s_ref[...] = s.astype(s_ref.dtype)

    wspec = pl.BlockSpec((tn, D), lambda j, i: (j, 0))
    ospec = pl.BlockSpec((tm, tn), lambda j, i: (i, j))
    return _pcall(body, name=name, grid=(Fd // tn, T // tm),
                  in_specs=[pl.BlockSpec((tm, D), lambda j, i: (i, 0)), wspec, wspec],
                  out_specs=[ospec, ospec, ospec], out_shape=[_S((T, Fd), MXU)] * 3,
                  compiler_params=_cp("parallel", "parallel"))(h, wgT, wuT)


def _ffn_mid_bwd(dx, wd, a, b, name):
    T, D = dx.shape
    Fd = wd.shape[0]
    tm, tn = _tile(T, TOKEN_TILE), _tile(Fd, FFN_COL_TILE)

    def body(dx_ref, wd_ref, a_ref, b_ref, da_ref, db_ref):
        dy = (dx_ref[...] * FFN_RES).astype(MXU)
        ds = _nt(dy, wd_ref[...])
        av, bv = a_ref[...].astype(F32), b_ref[...].astype(F32)
        sg = 1.0 / (1.0 + jnp.exp(-av))
        da_ref[...] = (ds * bv * (sg * (1.0 + av * (1.0 - sg)))).astype(da_ref.dtype)
        db_ref[...] = (ds * (av * sg)).astype(db_ref.dtype)

    aspec = pl.BlockSpec((tm, tn), lambda j, i: (i, j))
    return _pcall(body, name=name, grid=(Fd // tn, T // tm),
                  in_specs=[pl.BlockSpec((tm, D), lambda j, i: (i, 0)), pl.BlockSpec((tn, D), lambda j, i: (j, 0)),
                            aspec, aspec],
                  out_specs=[aspec, aspec], out_shape=[_S((T, Fd), MXU)] * 2,
                  compiler_params=_cp("parallel", "parallel"))(dx, wd, a, b)


def _swap_halves(t):
    w = t.shape[1]
    lane = lax.broadcasted_iota(jnp.int32, t.shape, 1)
    first = (lane % HEAD_DIM) < (HEAD_DIM // 2)
    return jnp.where(first, pltpu.roll(t, w - HEAD_DIM // 2, 1), pltpu.roll(t, HEAD_DIM // 2, 1))


def _rope(t, cos2, sin2):
    reps = t.shape[1] // LANES
    c = jnp.tile(cos2, (1, reps)) if reps > 1 else cos2
    s = jnp.tile(sin2, (1, reps)) if reps > 1 else sin2
    return t * c + _swap_halves(t) * s


def _rope_bwd(dt, cos2, sin2):
    reps = dt.shape[1] // LANES
    c = jnp.tile(cos2, (1, reps)) if reps > 1 else cos2
    s = jnp.tile(sin2, (1, reps)) if reps > 1 else sin2
    return dt * c + _swap_halves(dt * s)


def _in_proj(h, winT, cos2, sin2, name):
    T, D = h.shape
    tm = _tile(T, TOKEN_TILE)
    qe, ke, ve = ATTN_WIDTH, ATTN_WIDTH + KV_WIDTH, ATTN_WIDTH + 2 * KV_WIDTH

    def body(h_ref, w_ref, c_ref, s_ref, q_ref, k_ref, v_ref, u_ref):
        proj = _nt(h_ref[...], w_ref[...])
        cv, sv = c_ref[...], s_ref[...]
        q_ref[...] = _rope(proj[:, :qe], cv, sv).astype(q_ref.dtype)
        k_ref[...] = _rope(proj[:, qe:ke], cv, sv).astype(k_ref.dtype)
        v_ref[...] = proj[:, ke:ve].astype(v_ref.dtype)
        u_ref[...] = proj[:, ve:]

    def tok(w):
        return pl.BlockSpec((tm, w), lambda i: (i, 0))

    return _pcall(body, name=name, grid=(T // tm,),
                  in_specs=[tok(D), _resident(winT.shape), tok(LANES), tok(LANES)],
                  out_specs=[tok(ATTN_WIDTH), tok(KV_WIDTH), tok(KV_WIDTH), tok(POOL_WIDTH)],
                  out_shape=[_S((T, ATTN_WIDTH), MXU), _S((T, KV_WIDTH), MXU), _S((T, KV_WIDTH), MXU),
                             _S((T, POOL_WIDTH), F32)],
                  compiler_params=_cp("parallel"))(h, winT, cos2, sin2)


SWA_TILE_BLOCKS = 4
SM_SCALE = HEAD_DIM ** -0.5


def _swa_bias(first_tile):
    cols = Q_PER_KV * BLOCK
    kj = lax.broadcasted_iota(jnp.int32, (2 * BLOCK, cols), 0)
    qi = lax.broadcasted_iota(jnp.int32, (2 * BLOCK, cols), 1) % BLOCK
    diff = qi + BLOCK - kj
    bias = jnp.where((diff >= 0) & (diff < BLOCK), 0.0, NEG)
    return bias, jnp.where(kj < jnp.where(first_tile, BLOCK, 0), NEG, bias)


def _swa_probs(kh, qs, sink_row, bias):
    s = _nt(kh, qs) + bias
    m = jnp.maximum(jnp.max(s, axis=0, keepdims=True), sink_row)
    e = jnp.exp(s - m)
    es = jnp.exp(sink_row - m)
    inv = 1.0 / (jnp.sum(e, axis=0, keepdims=True) + es)
    return e * inv, es * inv


def _sink_row(sinks_ref, kv):
    return jnp.concatenate([jnp.full((1, BLOCK), sinks_ref[0, kv * Q_PER_KV + g], F32) for g in range(Q_PER_KV)], axis=1)


def _stack_heads(t, kv):
    return jnp.concatenate([t[:, (kv * Q_PER_KV + g) * HEAD_DIM:(kv * Q_PER_KV + g + 1) * HEAD_DIM]
                            for g in range(Q_PER_KV)], axis=0)


def _swa_specs(T):
    tq = _tile(T, SWA_TILE_BLOCKS * BLOCK)
    nbt = tq // BLOCK
    cur = lambda w: pl.BlockSpec((tq, w), lambda i: (i, 0))
    prev = lambda w: pl.BlockSpec((BLOCK, w), lambda i: (jnp.maximum(i * nbt - 1, 0), 0))
    return tq, nbt, cur, prev


def _rows(b):
    return slice(b * BLOCK, (b + 1) * BLOCK)


def _swa_fwd(q, k, v, sinks, name):
    T = q.shape[0]
    tq, nbt, cur, prev = _swa_specs(T)

    def body(sinks_ref, q_ref, k_ref, kp_ref, v_ref, vp_ref, o_ref):
        bias, bias0 = _swa_bias(pl.program_id(0) == 0)
        sink = [_sink_row(sinks_ref, kv) for kv in range(N_KV_HEADS)]
        kx = jnp.concatenate([kp_ref[...], k_ref[...]], axis=0)
        vx = jnp.concatenate([vp_ref[...], v_ref[...]], axis=0)
        for b in range(nbt):
            qv = q_ref[_rows(b), :] * SM_SCALE
            kk, vv = kx[b * BLOCK:(b + 2) * BLOCK], vx[b * BLOCK:(b + 2) * BLOCK]
            for kv in range(N_KV_HEADS):
                hs = slice(kv * HEAD_DIM, (kv + 1) * HEAD_DIM)
                p, _ = _swa_probs(kk[:, hs], _stack_heads(qv, kv), sink[kv], bias0 if b == 0 else bias)
                o_t = _tn(vv[:, hs], p.astype(MXU))
                for g in range(Q_PER_KV):
                    c0 = (kv * Q_PER_KV + g) * HEAD_DIM
                    o_ref[_rows(b), c0:c0 + HEAD_DIM] = o_t[:, _rows(g)].T

    return _pcall(body, name=name, grid=(T // tq,),
                  in_specs=[pl.BlockSpec(memory_space=pltpu.SMEM), cur(ATTN_WIDTH), cur(KV_WIDTH), prev(KV_WIDTH),
                            cur(KV_WIDTH), prev(KV_WIDTH)],
                  out_specs=cur(ATTN_WIDTH), out_shape=_S((T, ATTN_WIDTH), F32),
                  compiler_params=_cp("parallel"))(sinks, q, k, k, v, v)


def _swa_bwd(q, k, v, do, sinks, name):
    T = q.shape[0]
    tq, nbt, cur, prev = _swa_specs(T)
    per_tile = lambda w: pl.BlockSpec((BLOCK, w), lambda i: (i, 0))

    def add(acc, t):
        return t if acc is None else acc + t

    def body(sinks_ref, q_ref, k_ref, kp_ref, v_ref, vp_ref, do_ref,
             dq_ref, dk_ref, dkp_ref, dv_ref, dvp_ref, dsk_ref):
        bias, bias0 = _swa_bias(pl.program_id(0) == 0)
        sink = [_sink_row(sinks_ref, kv) for kv in range(N_KV_HEADS)]
        kx = jnp.concatenate([kp_ref[...], k_ref[...]], axis=0)
        vx = jnp.concatenate([vp_ref[...], v_ref[...]], axis=0)

        @pl.when(pl.program_id(0) == 0)
        def _():
            dsk_ref[...] = jnp.zeros_like(dsk_ref)

        dk_acc, dv_acc = [None] * (nbt + 1), [None] * (nbt + 1)
        dsk_acc = [None] * N_Q_HEADS
        for b in range(nbt):
            qv, dov = q_ref[_rows(b), :] * SM_SCALE, do_ref[_rows(b), :].astype(MXU)
            kk, vv = kx[b * BLOCK:(b + 2) * BLOCK], vx[b * BLOCK:(b + 2) * BLOCK]
            dks, dvs = [], []
            for kv in range(N_KV_HEADS):
                hs = slice(kv * HEAD_DIM, (kv + 1) * HEAD_DIM)
                qs, dos = _stack_heads(qv, kv), _stack_heads(dov, kv)
                p, ps = _swa_probs(kk[:, hs], qs, sink[kv], bias0 if b == 0 else bias)
                dp = _nt(vv[:, hs], dos)
                delta = jnp.sum(p * dp, axis=0, keepdims=True)
                ds = (p * (dp - delta)).astype(MXU)
                dq_t = _tn(kk[:, hs], ds) * SM_SCALE
                dks.append(_nn(ds, qs))
                dvs.append(_nn(p.astype(MXU), dos))
                dsink = -ps * delta
                for g in range(Q_PER_KV):
                    h = kv * Q_PER_KV + g
                    dq_ref[_rows(b), h * HEAD_DIM:(h + 1) * HEAD_DIM] = dq_t[:, _rows(g)].T
                    dsk_acc[h] = add(dsk_acc[h], jnp.sum(dsink[:, _rows(g)], axis=1, keepdims=True))
            dk, dv = jnp.concatenate(dks, axis=1), jnp.concatenate(dvs, axis=1)
            dk_acc[b], dk_acc[b + 1] = add(dk_acc[b], dk[:BLOCK]), add(dk_acc[b + 1], dk[BLOCK:])
            dv_acc[b], dv_acc[b + 1] = add(dv_acc[b], dv[:BLOCK]), add(dv_acc[b + 1], dv[BLOCK:])
        dkp_ref[...], dvp_ref[...] = dk_acc[0], dv_acc[0]
        dk_ref[...] = jnp.concatenate(dk_acc[1:], axis=0)
        dv_ref[...] = jnp.concatenate(dv_acc[1:], axis=0)
        for h in range(N_Q_HEADS):
            dsk_ref[h:h + 1, :] += jnp.broadcast_to(dsk_acc[h], (1, LANES))

    kvs, kvp = _S((T, KV_WIDTH), F32), _S((T // tq * BLOCK, KV_WIDTH), F32)
    return _pcall(body, name=name, grid=(T // tq,),
                  in_specs=[pl.BlockSpec(memory_space=pltpu.SMEM), cur(ATTN_WIDTH), cur(KV_WIDTH), prev(KV_WIDTH),
                            cur(KV_WIDTH), prev(KV_WIDTH), cur(ATTN_WIDTH)],
                  out_specs=[cur(ATTN_WIDTH), cur(KV_WIDTH), per_tile(KV_WIDTH), cur(KV_WIDTH), per_tile(KV_WIDTH),
                             _resident((N_Q_HEADS, LANES))],
                  out_shape=[_S((T, ATTN_WIDTH), F32), kvs, kvp, kvs, kvp, _S((N_Q_HEADS, LANES), F32)],
                  compiler_params=_cp("arbitrary"))(sinks, q, k, k, v, v, do)


def _dproj(dq, dk, dkp, dv, dvp, du, cos2, sin2, name):
    T = dq.shape[0]
    tq, nbt, cur, _ = _swa_specs(T)
    nt = T // tq
    nxt = lambda w: pl.BlockSpec((BLOCK, w), lambda i: (jnp.minimum(i + 1, nt - 1), 0))

    def body(dq_ref, dk_ref, dkp_ref, dv_ref, dvp_ref, du_ref, c_ref, s_ref, o_ref):
        more = (pl.program_id(0) < nt - 1).astype(F32)
        cv, sv = c_ref[...], s_ref[...]

        def whole(t_ref, p_ref):
            t, last = t_ref[...], t_ref[tq - BLOCK:, :] + more * p_ref[...]
            return last if nbt == 1 else jnp.concatenate([t[:tq - BLOCK], last], axis=0)

        o_ref[...] = jnp.concatenate(
            [_rope_bwd(dq_ref[...], cv, sv), _rope_bwd(whole(dk_ref, dkp_ref), cv, sv), whole(dv_ref, dvp_ref),
             du_ref[...]], axis=1).astype(o_ref.dtype)

    width = ATTN_WIDTH + 2 * KV_WIDTH + POOL_WIDTH
    return _pcall(body, name=name, grid=(nt,),
                  in_specs=[cur(ATTN_WIDTH), cur(KV_WIDTH), nxt(KV_WIDTH), cur(KV_WIDTH), nxt(KV_WIDTH),
                            cur(POOL_WIDTH), cur(LANES), cur(LANES)],
                  out_specs=cur(width), out_shape=_S((T, width), MXU),
                  compiler_params=_cp("parallel"))(dq, dk, dkp, dv, dvp, du, cos2, sin2)


def _pool_specs(T):
    tm = _tile(T, TOKEN_TILE)
    hb = tm // POOL_HALO
    nh = T // POOL_HALO
    tok = lambda w: pl.BlockSpec((tm, w), lambda i: (i, 0))
    before = pl.BlockSpec((POOL_HALO, POOL_WIDTH), lambda i: (jnp.maximum(i * hb - 1, 0), 0))
    after = pl.BlockSpec((POOL_HALO, POOL_WIDTH), lambda i: (jnp.minimum((i + 1) * hb, nh - 1), 0))
    return tm, tok, before, after


def _window_counts(i, tm, rows, w):
    t = i * tm + lax.broadcasted_iota(jnp.int32, (rows, 1), 0)
    return jnp.minimum(t + 1, w).astype(F32)


def _pooled(u_ext, i, tm):
    out = []
    for g, w in enumerate(POOL_WINDOWS):
        acc = u_ext[:, g * POOL_GROUP:(g + 1) * POOL_GROUP]
        tok = acc[POOL_HALO:, :]
        sh = 1
        while sh < w:
            acc = acc + pltpu.roll(acc, sh, 0)
            sh *= 2
        out.append(acc[POOL_HALO:, :] / _window_counts(i, tm, tm, w) - tok)
    return out


def _pool_fwd(u, out_a, pool_w, pool_scale, ga, gb, name):
    T = u.shape[0]
    tm, tok, before, _ = _pool_specs(T)

    def body(u_ref, halo_ref, oa_ref, pw_ref, sc_ref, ga_ref, gb_ref, ob_ref, mg_ref):
        i = pl.program_id(0)
        halo = halo_ref[...] * (i > 0).astype(F32)
        pooled = _pooled(jnp.concatenate([halo, u_ref[...]], axis=0), i, tm)
        mixed = [_nn(pooled[g].astype(MXU), pw_ref[g].astype(MXU)) for g in range(len(POOL_WINDOWS))]
        ob = jnp.concatenate(mixed, axis=1) * sc_ref[...]
        ob_ref[...] = ob
        oa = oa_ref[...]
        ra = lax.rsqrt(jnp.mean(oa * oa, axis=-1, keepdims=True) + EPS)
        rb = lax.rsqrt(jnp.mean(ob * ob, axis=-1, keepdims=True) + EPS)
        mg_ref[...] = jnp.concatenate([oa * ra * ga_ref[...], ob * rb * gb_ref[...]], axis=1).astype(mg_ref.dtype)

    vec = _resident((1, POOL_WIDTH))
    return _pcall(body, name=name, grid=(T // tm,),
                  in_specs=[tok(POOL_WIDTH), before, tok(ATTN_WIDTH), _resident(pool_w.shape), vec, vec, vec],
                  out_specs=[tok(POOL_WIDTH), tok(ATTN_WIDTH + POOL_WIDTH)],
                  out_shape=[_S((T, POOL_WIDTH), F32), _S((T, ATTN_WIDTH + POOL_WIDTH), MXU)],
                  compiler_params=_cp("parallel"))(u, u, out_a, pool_w, pool_scale, ga, gb)


def _pool_bwd(u, dob, pool_w, pool_scale, name):
    T = u.shape[0]
    tm, tok, before, after = _pool_specs(T)
    nt = T // tm
    G = len(POOL_WINDOWS)

    def body(u_ref, halo_ref, dob_ref, dnext_ref, pw_ref, sc_ref, du_ref, dpw_ref, dsc_ref):
        i = pl.program_id(0)
        halo = halo_ref[...] * (i > 0).astype(F32)
        pooled = _pooled(jnp.concatenate([halo, u_ref[...]], axis=0), i, tm)
        dnext = dnext_ref[...] * (i < nt - 1).astype(F32)
        dext = jnp.concatenate([dob_ref[...], dnext], axis=0) * sc_ref[...]

        @pl.when(i == 0)
        def _():
            dpw_ref[...] = jnp.zeros_like(dpw_ref)
            dsc_ref[...] = jnp.zeros_like(dsc_ref)

        dus, dscs = [], []
        for g, w in enumerate(POOL_WINDOWS):
            gs = slice(g * POOL_GROUP, (g + 1) * POOL_GROUP)
            pw = pw_ref[g].astype(MXU)
            pg = pooled[g].astype(MXU)
            dmix = dext[:, gs].astype(MXU)
            dscs.append(jnp.sum(dob_ref[:, gs] * _nn(pg, pw), axis=0, keepdims=True))
            dpw_ref[g] += _tn(pg, dmix[:tm, :])
            dpooled = _nt(dmix, pw)
            acc = dpooled / _window_counts(i, tm, tm + POOL_HALO, w)
            sh = 1
            while sh < w:
                acc = acc + pltpu.roll(acc, tm + POOL_HALO - sh, 0)
                sh *= 2
            dus.append(acc[:tm, :] - dpooled[:tm, :])
        du_ref[...] = jnp.concatenate(dus, axis=1)
        dsc_ref[...] += jnp.concatenate(dscs, axis=1)

    vec = _resident((1, POOL_WIDTH))
    return _pcall(body, name=name, grid=(nt,),
                  in_specs=[tok(POOL_WIDTH), before, tok(POOL_WIDTH), after, _resident(pool_w.shape), vec],
                  out_specs=[tok(POOL_WIDTH), _resident(pool_w.shape), vec],
                  out_shape=[_S((T, POOL_WIDTH), F32), _S(pool_w.shape, F32), _S((1, POOL_WIDTH), F32)],
                  compiler_params=_cp("arbitrary"))(u, u, dob, dob, pool_w, pool_scale)


def _xattn_probs(qh, kh):
    s = _nt(qh, kh) * (X_HEAD_DIM ** -0.5)
    e = jnp.exp(s - jnp.max(s, axis=1, keepdims=True))
    return e / jnp.sum(e, axis=1, keepdims=True)


def _xattn_fwd(q, kvm, name):
    T, XW = q.shape
    tm = _tile(T, TOKEN_TILE)

    def body(q_ref, kv_ref, o_ref):
        for h in range(X_HEADS):
            hs = slice(h * X_HEAD_DIM, (h + 1) * X_HEAD_DIM)
            vs = slice(XW + h * X_HEAD_DIM, XW + (h + 1) * X_HEAD_DIM)
            p = _xattn_probs(q_ref[:, hs], kv_ref[:, hs])
            o_ref[:, hs] = _nn(p.astype(MXU), kv_ref[:, vs]).astype(o_ref.dtype)

    return _pcall(body, name=name, grid=(T // tm,),
                  in_specs=[pl.BlockSpec((tm, XW), lambda i: (i, 0)), _resident(kvm.shape)],
                  out_specs=pl.BlockSpec((tm, XW), lambda i: (i, 0)), out_shape=_S((T, XW), MXU),
                  compiler_params=_cp("parallel"))(q, kvm)


def _xattn_bwd(q, kvm, do, name):
    T, XW = q.shape
    tm = _tile(T, TOKEN_TILE)

    def body(q_ref, kv_ref, do_ref, dq_ref, dkv_ref):
        @pl.when(pl.program_id(0) == 0)
        def _():
            dkv_ref[...] = jnp.zeros_like(dkv_ref)

        for h in range(X_HEADS):
            hs = slice(h * X_HEAD_DIM, (h + 1) * X_HEAD_DIM)
            vs = slice(XW + h * X_HEAD_DIM, XW + (h + 1) * X_HEAD_DIM)
            qh, doh = q_ref[:, hs], do_ref[:, hs]
            p = _xattn_probs(qh, kv_ref[:, hs])
            dp = _nt(doh, kv_ref[:, vs])
            ds = (p * (dp - jnp.sum(p * dp, axis=1, keepdims=True)) * (X_HEAD_DIM ** -0.5)).astype(MXU)
            dq_ref[:, hs] = _nn(ds, kv_ref[:, hs]).astype(dq_ref.dtype)
            dkv_ref[:, hs] += _tn(ds, qh)
            dkv_ref[:, vs] += _tn(p.astype(MXU), doh)

    tok = pl.BlockSpec((tm, XW), lambda i: (i, 0))
    return _pcall(body, name=name, grid=(T // tm,),
                  in_specs=[tok, _resident(kvm.shape), tok],
                  out_specs=[tok, _resident(kvm.shape)],
                  out_shape=[_S((T, XW), MXU), _S(kvm.shape, F32)],
                  compiler_params=_cp("arbitrary"))(q, kvm, do)


def _loss_head(x, tgt, g, name):
    T, D = x.shape
    tm = _tile(T, TOKEN_TILE)

    def body(x_ref, t_ref, g_ref, loss_ref, dx_ref, dg_ref):
        xv, gv = x_ref[...], g_ref[...]
        r = lax.rsqrt(jnp.mean(xv * xv, axis=-1, keepdims=True) + EPS)
        xh = xv * r
        e = xh * gv - t_ref[...]
        dy = e * (1.0 / D)
        dxn = dy * gv
        dx_ref[...] = r * (dxn - xh * jnp.mean(dxn * xh, axis=-1, keepdims=True))

        @pl.when(pl.program_id(0) == 0)
        def _():
            loss_ref[...] = jnp.zeros_like(loss_ref)
            dg_ref[...] = jnp.zeros_like(dg_ref)

        part = jnp.sum(jnp.sum(e * e, axis=1, keepdims=True), axis=0, keepdims=True) * (0.5 / D)
        loss_ref[...] += jnp.broadcast_to(part, (1, LANES))
        dg_ref[...] += jnp.sum(dy * xh, axis=0, keepdims=True)

    tok = pl.BlockSpec((tm, D), lambda i: (i, 0))
    return _pcall(body, name=name, grid=(T // tm,),
                  in_specs=[tok, tok, _resident((1, D))],
                  out_specs=[_resident((1, LANES)), tok, _resident((1, D))],
                  out_shape=[_S((1, LANES), F32), _S((T, D), F32), _S((1, D), F32)],
                  compiler_params=_cp("arbitrary"))(x, tgt, g)


def _rows_tile(rows):
    for t in (512, 416, 352, 256, 128, 64, 32, 16, 8):
        if rows % t == 0:
            return t
    return rows


def _pair_sum(grads, gots, sizes, place, name):
    nw = len(sizes)
    C = grads[0].shape[1]

    def body(p_ref, *refs):
        g, got, out = refs[:nw], refs[nw:2 * nw], refs[2 * nw:]
        for w in range(nw):
            out[w][...] = (g[w][...] + got[w][...]).astype(out[w].dtype)

    def blk(w):
        return (sizes[w] // 4, C)

    in_specs = [pl.BlockSpec(blk(w), lambda q, s, p: (4 * q + 2 * p[0] + s, 0)) for w in range(nw)]
    in_specs += [pl.BlockSpec(blk(w), lambda q, s, p: (2 * q + s, 0)) for w in range(nw)]
    out_specs = [pl.BlockSpec(blk(w), lambda q, s, p: (2 * q + s, 0)) for w in range(nw)]
    gs = pltpu.PrefetchScalarGridSpec(num_scalar_prefetch=1, grid=(N_CHIPS, 2), in_specs=in_specs, out_specs=out_specs)
    return _pcall(body, name=name, grid_spec=gs, out_shape=[_S((2 * n, C), MXU) for n in sizes],
                  compiler_params=_cp("parallel", "parallel"))(place, *grads, *gots)


def _final_sum(grads, gots, recvs, sizes, place, name):
    nw = len(sizes)
    C = grads[0].shape[1]

    def body(p_ref, *refs):
        g, got, rv, out = refs[:nw], refs[nw:2 * nw], refs[2 * nw:5 * nw], refs[5 * nw:]
        for w in range(nw):
            acc = g[w][...] + got[w][...]
            for j in range(3):
                acc = acc + rv[3 * w + j][...].astype(F32)
            out[w][...] = acc

    def blk(w):
        return (sizes[w] // 4, C)

    in_specs = [pl.BlockSpec(blk(w), lambda s, p: (4 * p[1] + 2 * p[0] + s, 0)) for w in range(nw)]
    in_specs += [pl.BlockSpec(blk(w), lambda s, p: (2 * p[1] + s, 0)) for w in range(nw)]
    args = list(grads) + list(gots)
    for w in range(nw):
        for j in range(3):
            in_specs.append(pl.BlockSpec(blk(w), lambda s, p, j=j: (2 * j + s, 0)))
            args.append(recvs[w])
    out_specs = [pl.BlockSpec(blk(w), lambda s, p: (2 * p[0] + s, 0)) for w in range(nw)]
    gs = pltpu.PrefetchScalarGridSpec(num_scalar_prefetch=1, grid=(2,), in_specs=in_specs, out_specs=out_specs)
    return _pcall(body, name=name, grid_spec=gs, out_shape=[_S((n, C), F32) for n in sizes],
                  compiler_params=_cp("parallel"))(place, *args)


def _adamw(w, g, m, v, name):
    R, C = w.shape
    tr = _rows_tile(R)
    c1 = 1.0 / (1.0 - ADAM_B1 ** ADAM_STEP)
    c2 = 1.0 / (1.0 - ADAM_B2 ** ADAM_STEP)

    def body(w_ref, g_ref, m_ref, v_ref, d_ref, nm_ref, nv_ref):
        gv = g_ref[...]
        nm = ADAM_B1 * m_ref[...] + (1.0 - ADAM_B1) * gv
        nv = ADAM_B2 * v_ref[...] + (1.0 - ADAM_B2) * (gv * gv)
        d_ref[...] = -ADAM_LR * ((nm * c1) / (jnp.sqrt(nv * c2) + ADAM_EPS) + ADAM_WD * w_ref[...])
        nm_ref[...] = nm
        nv_ref[...] = nv

    spec = pl.BlockSpec((tr, C), lambda i: (i, 0))
    return _pcall(body, name=name, grid=(R // tr,), in_specs=[spec] * 4, out_specs=[spec] * 3,
                  out_shape=[_S((R, C), F32)] * 3, compiler_params=_cp("parallel"))(w, g, m, v)


ANY = pl.BlockSpec(memory_space=pl.ANY)


def _place():
    x, y, c = lax.axis_index("x"), lax.axis_index("y"), lax.axis_index("c")
    chips = [(1 - x, y), (x, 1 - y), (1 - x, 1 - y)]
    return x, y, c, chips


def _remote(src, dst, send_sem, recv_sem, dev):
    return pltpu.make_async_remote_copy(src_ref=src, dst_ref=dst, send_sem=send_sem, recv_sem=recv_sem,
                                        device_id=dev, device_id_type=MESH)


def _drain(like, send_sem, recv_sem, me, *, send=False, recv=False):
    d = _remote(like, like, send_sem, recv_sem, me)
    if send:
        d.wait_send()
    if recv:
        d.wait_recv()


def _dma_sems(n):
    return [pltpu.SemaphoreType.DMA((n,)), pltpu.SemaphoreType.DMA((n,))]


def _comm_params():
    return pltpu.CompilerParams(has_side_effects=True)


def _on_sequencer(exchange, refs, sem_types, peers_of, name, seq_id):
    def launch(*sems):
        x, y, c, chips = _place()
        barrier = pltpu.get_barrier_semaphore()
        peers = peers_of(x, y, c, chips)
        for peer in peers:
            pl.semaphore_signal(barrier, inc=1, device_id=peer, device_id_type=MESH)
        pl.semaphore_wait(barrier, len(peers))
        exchange(refs, *sems)

    pl.kernel(launch, mesh=plsc.ScalarSubcoreMesh(axis_name="seq", num_cores=1), name=name,
              scratch_types=tuple(sem_types), compiler_params=pltpu.CompilerParams(collective_id=seq_id))()


def _hbm_ref(a):
    return jax.new_ref(a, memory_space=pltpu.MemorySpace.HBM)


def _allgather_weights(bufs, sizes, name, seq_id=None):
    nw = len(sizes)

    def exchange(out, s_ici, r_ici, s_fwd, r_fwd):
        x, y, c, chips = _place()
        me, sib = (x, y, c), (x, y, 1 - c)
        q_me = 2 * x + y

        def rows(w, q):
            hw = sizes[w] // 2
            return out[w].at[pl.ds(q * sizes[w] + c * hw, hw)]

        def three(w):
            return out[w].at[pl.ds(0, 3 * (sizes[w] // 2))]

        for w in range(nw):
            for px, py in chips:
                _remote(rows(w, q_me), rows(w, q_me), s_ici.at[w], r_ici.at[w], (px, py, c)).start()
        for w in range(nw):
            _drain(three(w), s_ici.at[w], r_ici.at[w], me, recv=True)
            for px, py in chips:
                got = rows(w, 2 * px + py)
                _remote(got, got, s_fwd.at[w], r_fwd.at[w], sib).start()
        for w in range(nw):
            _drain(three(w), s_fwd.at[w], r_fwd.at[w], me, recv=True)
        for w in range(nw):
            _drain(three(w), s_ici.at[w], r_ici.at[w], me, send=True)
            _drain(three(w), s_fwd.at[w], r_fwd.at[w], me, send=True)

    if seq_id is not None:
        refs = [_hbm_ref(b) for b in bufs]
        _on_sequencer(exchange, refs, _dma_sems(nw) + _dma_sems(nw),
                      lambda x, y, c, chips: [(x, y, 1 - c)] + [(px, py, c) for px, py in chips], name, seq_id)
        return [r[...] for r in refs]

    def body(*refs):
        exchange(refs[nw:2 * nw], *refs[2 * nw:])

    return _pcall(body, name=name, in_specs=[ANY] * nw, out_specs=[ANY] * nw,
                  out_shape=[_S(b.shape, b.dtype) for b in bufs],
                  input_output_aliases={w: w for w in range(nw)},
                  scratch_shapes=_dma_sems(nw) + _dma_sems(nw), compiler_params=_comm_params())(*bufs)


def _rs_pair_exchange(grads, sizes, name):
    C = grads[0].shape[1]
    nw = len(sizes)

    def body(*refs):
        g, got = refs[:nw], refs[nw:2 * nw]
        s_sem, r_sem = refs[2 * nw:]
        x, y, c, _ = _place()
        me, sib = (x, y, c), (x, y, 1 - c)
        for w in range(nw):
            hw = sizes[w] // 2
            for q in range(N_CHIPS):
                _remote(g[w].at[pl.ds(q * sizes[w] + (1 - c) * hw, hw)], got[w].at[pl.ds(q * hw, hw)],
                        s_sem.at[w], r_sem.at[w], sib).start()
        for w in range(nw):
            _drain(got[w], s_sem.at[w], r_sem.at[w], me, send=True, recv=True)

    return _pcall(body, name=name, in_specs=[ANY] * nw, out_specs=[ANY] * nw,
                  out_shape=[_S((2 * n, C), F32) for n in sizes],
                  scratch_shapes=_dma_sems(nw), compiler_params=_comm_params())(*grads)


def _rs_chip_exchange(sums, sizes, name, seq_id=None):
    C = sums[0].shape[1]
    nw = len(sizes)
    out_shape = [_S((3 * (n // 2), C), sums[0].dtype) for n in sizes]

    def exchange(refs, s_sem, r_sem):
        sm, got = refs[:nw], refs[nw:2 * nw]
        x, y, c, chips = _place()
        for w in range(nw):
            hw = sizes[w] // 2
            for j, (px, py) in enumerate(chips):
                _remote(sm[w].at[pl.ds((2 * px + py) * hw, hw)], got[w].at[pl.ds(j * hw, hw)],
                        s_sem.at[w], r_sem.at[w], (px, py, c)).start()
        for w in range(nw):
            _drain(got[w], s_sem.at[w], r_sem.at[w], (x, y, c), send=True, recv=True)

    if seq_id is not None:
        gots = [jax.empty_ref(s, memory_space=pltpu.MemorySpace.HBM) for s in out_shape]
        _on_sequencer(exchange, [_hbm_ref(s) for s in sums] + gots, _dma_sems(nw),
                      lambda x, y, c, chips: [(px, py, c) for px, py in chips], name, seq_id)
        return [r[...] for r in gots]

    def body(*refs):
        exchange(refs[:2 * nw], *refs[2 * nw:])

    return _pcall(body, name=name, in_specs=[ANY] * nw, out_specs=[ANY] * nw, out_shape=out_shape,
                  scratch_shapes=_dma_sems(nw), compiler_params=_comm_params())(*sums)


def _rs_share_halves(reds, sizes, name):
    nw = len(sizes)

    def body(*refs):
        out = refs[nw:2 * nw]
        s_sem, r_sem = refs[2 * nw:]
        x, y, c, _ = _place()
        for w in range(nw):
            hw = sizes[w] // 2
            rows = out[w].at[pl.ds(c * hw, hw)]
            _remote(rows, rows, s_sem.at[w], r_sem.at[w], (x, y, 1 - c)).start()
        for w in range(nw):
            _drain(out[w].at[pl.ds(0, sizes[w] // 2)], s_sem.at[w], r_sem.at[w], (x, y, c), send=True, recv=True)

    return _pcall(body, name=name, in_specs=[ANY] * nw, out_specs=[ANY] * nw,
                  out_shape=[_S(r.shape, r.dtype) for r in reds],
                  input_output_aliases={w: w for w in range(nw)},
                  scratch_shapes=_dma_sems(nw), compiler_params=_comm_params())(*reds)


def _allreduce_small(part, name):
    R, C = part.shape

    def body(p_ref, o_ref, buf, s_sem, r_sem):
        x, y, c, _ = _place()
        my_id = 4 * x + 2 * y + c
        buf[my_id] = p_ref[...]
        cps = []
        for k in range(1, N_DEV):
            fx, fy, fc = (k >> 2) & 1, (k >> 1) & 1, k & 1
            peer = (x ^ fx, y ^ fy, c ^ fc)
            cps.append(_remote(p_ref, buf.at[my_id], s_sem.at[k - 1], r_sem.at[k - 1], peer))
        for cp in cps:
            cp.start()
        for cp in cps:
            cp.wait()
        acc = buf[0]
        for d in range(1, N_DEV):
            acc = acc + buf[d]
        o_ref[...] = acc

    vm = pl.BlockSpec(memory_space=pltpu.VMEM)
    return _pcall(body, name=name, in_specs=[vm], out_specs=vm, out_shape=_S((R, C), F32),
                  scratch_shapes=[pltpu.VMEM((N_DEV, R, C), F32)] + _dma_sems(N_DEV - 1),
                  compiler_params=pltpu.CompilerParams(has_side_effects=True, vmem_limit_bytes=VMEM_LIMIT))(part)


SHARD_STEPS = 4


def _own_shard_buffers(ws, l, group, place, name):
    nw = len(group)

    def body(p_ref, *refs):
        for (_, tr), i_ref, o_ref in zip(group, refs[:nw], refs[nw:]):
            v = i_ref[...]
            o_ref[...] = (v.T if tr else v).astype(o_ref.dtype)

    in_specs, out_specs, out_shape, sizes = [], [], [], []
    for wname, tr in group:
        _, K, n = ws[wname].shape
        in_specs.append(pl.BlockSpec((None, K // SHARD_STEPS, n), lambda i, p: (l, i, 0)))
        if tr:
            out_specs.append(pl.BlockSpec((n, K // SHARD_STEPS), lambda i, p: (p[1], i)))
            out_shape.append(_S((N_CHIPS * n, K), MXU))
            sizes.append(n)
        else:
            out_specs.append(pl.BlockSpec((K // SHARD_STEPS, n), lambda i, p: (p[1] * SHARD_STEPS + i, 0)))
            out_shape.append(_S((N_CHIPS * K, n), MXU))
            sizes.append(K)
    gs = pltpu.PrefetchScalarGridSpec(num_scalar_prefetch=1, grid=(SHARD_STEPS,), in_specs=in_specs, out_specs=out_specs)
    bufs = _pcall(body, name=name, grid_spec=gs, out_shape=out_shape,
                  compiler_params=_cp("parallel"))(place, *[ws[wname] for wname, _ in group])
    return list(bufs), sizes


def _after(xs, ys):
    return lax.optimization_barrier((xs, ys))[0]


def _small_rows(v):
    flat = v.reshape(-1)
    pad = (-flat.shape[0]) % 1024
    return jnp.pad(flat, (0, pad)).reshape(-1, 1024)


def _pack_small(vals):
    rows = [_small_rows(vals[n]) for n in SMALL]
    cat = jnp.concatenate(rows, axis=0)
    pad = (-cat.shape[0]) % 8
    return jnp.pad(cat, ((0, pad), (0, 0)))


def _unpack_small(packed, like):
    out, r = {}, 0
    for n in SMALL:
        size = like[n].size
        nr = -(-size // 1024)
        out[n] = packed[r:r + nr].reshape(-1)[:size].reshape(like[n].shape)
        r += nr
    return out


def _rope_tables(positions):
    inv_freq = ROPE_THETA ** (-jnp.arange(0, HEAD_DIM, 2, dtype=F32) / HEAD_DIM)
    ang = positions.astype(F32)[:, None] * inv_freq
    cos, sin = jnp.cos(ang), jnp.sin(ang)
    return jnp.concatenate([cos, cos, cos, cos], axis=1), jnp.concatenate([-sin, sin, -sin, sin], axis=1)


def _layer_fwd(l, x, memv, W, P, cos2, sin2):
    t = f"l{l}"
    sv = {"x0": x}
    sv["h1"] = _rms_fwd(x, P["ffn1_norm"], t + "_ffn1_norm")
    sv["a1"], sv["b1"], sv["s1"] = _ffn_up(sv["h1"], W["ffn1_w_gate"], W["ffn1_w_up"], t + "_ffn1_up")
    sv["x1"] = _mm([(sv["s1"], W["ffn1_w_down"])], nt=False, out_dtype=F32, res=x, res_scale=FFN_RES, name=t + "_ffn1_down")

    sv["h2"] = _rms_fwd(sv["x1"], P["mix_norm"], t + "_mix_norm")
    sv["q"], sv["k"], sv["v"], sv["u"] = _in_proj(sv["h2"], W["w_in"], cos2, sin2, t + "_in_proj")
    sv["oa"] = _swa_fwd(sv["q"], sv["k"], sv["v"], P["attn_sinks"], t + "_swa")
    sv["ob"], sv["mg"] = _pool_fwd(sv["u"], sv["oa"], P["pool_w"], P["pool_scale"], P["attn_out_norm"],
                                   P["pool_out_norm"], t + "_pool")
    sv["x2"] = _mm([(sv["mg"], W["w_out"])], nt=False, out_dtype=F32, res=sv["x1"], name=t + "_out_proj")

    sv["h3"] = _rms_fwd(sv["x2"], P["xattn_norm"], t + "_xattn_norm")
    sv["memn"] = _rms_fwd(memv, P["mem_norm"], t + "_mem_norm")
    sv["q3"] = _mm([(sv["h3"], W["xattn_wq"])], nt=False, out_dtype=MXU, name=t + "_xq")
    sv["kv"] = _mm([(sv["memn"], W["xattn_wkv"])], nt=True, out_dtype=MXU, name=t + "_xkv")
    sv["o3"] = _xattn_fwd(sv["q3"], sv["kv"], t + "_xattn")
    sv["x3"] = _mm([(sv["o3"], W["xattn_wo"])], nt=False, out_dtype=F32, res=sv["x2"], name=t + "_xo")

    sv["h4"] = _rms_fwd(sv["x3"], P["ffn2_norm"], t + "_ffn2_norm")
    sv["a2"], sv["b2"], sv["s2"] = _ffn_up(sv["h4"], W["ffn2_w_gate"], W["ffn2_w_up"], t + "_ffn2_up")
    x4 = _mm([(sv["s2"], W["ffn2_w_down"])], nt=False, out_dtype=F32, res=sv["x3"], res_scale=FFN_RES, name=t + "_ffn2_down")
    return x4, sv


def _ffn_bwd(t, dx, x_in, g, h, a, b, s, wgT, wuT, wd):
    d_wd = _mm_tn(s, dx, t + "_dwd", r_scale=FFN_RES)
    da, db = _ffn_mid_bwd(dx, wd, a, b, t + "_mid")
    d_wg = _mm_tn(da, h, t + "_dwg")
    d_wu = _mm_tn(db, h, t + "_dwu")
    dh = _mm([(da, wgT), (db, wuT)], nt=False, out_dtype=F32, name=t + "_dh", tm=TOKEN_TILE // 2)
    dx_in, dg = _rms_bwd(x_in, g, dh, t + "_norm_bwd", dres=dx)
    return dx_in, dg, d_wg, d_wu, d_wd


def _layer_bwd(l, dx, sv, memv, W, P, cos2, sin2):
    t = f"l{l}b"
    GW, GP = {}, {}
    dx, GP["ffn2_norm"], GW["ffn2_w_gate"], GW["ffn2_w_up"], GW["ffn2_w_down"] = _ffn_bwd(
        t + "_ffn2", dx, sv["x3"], P["ffn2_norm"], sv["h4"], sv["a2"], sv["b2"], sv["s2"],
        W["ffn2_w_gate"], W["ffn2_w_up"], W["ffn2_w_down"])

    GW["xattn_wo"] = _mm_tn(sv["o3"], dx, t + "_dwo")
    do3 = _mm([(dx, W["xattn_wo"])], nt=True, out_dtype=MXU, name=t + "_do3")
    dq3, dkv = _xattn_bwd(sv["q3"], sv["kv"], do3, t + "_xattn")
    GW["xattn_wq"] = _mm_tn(sv["h3"], dq3, t + "_dwq")
    dh3 = _mm([(dq3, W["xattn_wq"])], nt=True, out_dtype=F32, name=t + "_dh3")
    GW["xattn_wkv"] = _mm_tn(dkv, sv["memn"], t + "_dwkv")
    dmemn = _mm([(dkv, W["xattn_wkv"])], nt=False, out_dtype=F32, name=t + "_dmemn")
    _, GP["mem_norm"] = _rms_bwd(memv, P["mem_norm"], dmemn, t + "_mem_norm_bwd")
    dx, GP["xattn_norm"] = _rms_bwd(sv["x2"], P["xattn_norm"], dh3, t + "_xattn_norm_bwd", dres=dx)

    GW["w_out"] = _mm_tn(sv["mg"], dx, t + "_dwout")
    dmg = _mm([(dx, W["w_out"])], nt=True, out_dtype=F32, name=t + "_dmg")
    doa, GP["attn_out_norm"] = _rms_bwd(sv["oa"], P["attn_out_norm"], dmg, t + "_oa_norm_bwd", col=0)
    dob, GP["pool_out_norm"] = _rms_bwd(sv["ob"], P["pool_out_norm"], dmg, t + "_ob_norm_bwd", col=1)
    du, GP["pool_w"], GP["pool_scale"] = _pool_bwd(sv["u"], dob, P["pool_w"], P["pool_scale"], t + "_pool")
    dq, dko, dkp, dvo, dvp, dsk = _swa_bwd(sv["q"], sv["k"], sv["v"], doa, P["attn_sinks"], t + "_swa")
    GP["attn_sinks"] = dsk[:, 0]
    dpj = _dproj(dq, dko, dkp, dvo, dvp, du, cos2, sin2, t + "_dproj")
    GW["w_in"] = _mm_tn(dpj, sv["h2"], t + "_dwin")
    dh2 = _mm([(dpj, W["w_in"])], nt=False, out_dtype=F32, name=t + "_dh2")
    dx, GP["mix_norm"] = _rms_bwd(sv["x1"], P["mix_norm"], dh2, t + "_mix_norm_bwd", dres=dx)

    dx, GP["ffn1_norm"], GW["ffn1_w_gate"], GW["ffn1_w_up"], GW["ffn1_w_down"] = _ffn_bwd(
        t + "_ffn1", dx, sv["x0"], P["ffn1_norm"], sv["h1"], sv["a1"], sv["b1"], sv["s1"],
        W["ffn1_w_gate"], W["ffn1_w_up"], W["ffn1_w_down"])
    return dx, GW, GP


def _reduce_group(t, group, GW, sizes, place, seq_id):
    grads = [GW[name] for name, _ in group]
    gots = _rs_pair_exchange(grads, sizes, t + "_pair")
    sums = _pair_sum(grads, gots, sizes, place, t + "_pair_sum")
    recvs = _rs_chip_exchange(sums, sizes, t + "_chips", seq_id=seq_id)
    reds = _final_sum(grads, gots, recvs, sizes, place, t + "_final_sum")
    return _rs_share_halves(reds, sizes, t + "_share")


def _adamw_layer(l, w3, g, m3, v3, transposed, prev, name):
    _, K, n = w3.shape
    if transposed:
        tr = K // SHARD_STEPS
        g_spec = pl.BlockSpec((n, tr), lambda i: (0, i))
    else:
        tr = _rows_tile(K)
        g_spec = pl.BlockSpec((tr, n), lambda i: (i, 0))
    c1 = 1.0 / (1.0 - ADAM_B1 ** ADAM_STEP)
    c2 = 1.0 / (1.0 - ADAM_B2 ** ADAM_STEP)

    def body(w_ref, g_ref, m_ref, v_ref, *rest):
        go_ref, d_ref, nm_ref, nv_ref = rest[-4:]
        gv = g_ref[...].T if transposed else g_ref[...]
        nm = ADAM_B1 * m_ref[...] + (1.0 - ADAM_B1) * gv
        nv = ADAM_B2 * v_ref[...] + (1.0 - ADAM_B2) * (gv * gv)
        go_ref[...] = gv
        d_ref[...] = -ADAM_LR * ((nm * c1) / (jnp.sqrt(nv * c2) + ADAM_EPS) + ADAM_WD * w_ref[...])
        nm_ref[...] = nm
        nv_ref[...] = nv

    slab = pl.BlockSpec((None, tr, n), lambda i: (l, i, 0))
    in_specs, args, aliases = [slab, g_spec, slab, slab], [w3, g, m3, v3], {}
    if prev is not None:
        in_specs += [ANY] * 4
        args += list(prev)
        aliases = {4 + j: j for j in range(4)}
    return _pcall(body, name=name, grid=(K // tr,), in_specs=in_specs, out_specs=[slab] * 4,
                  out_shape=[_S(w3.shape, F32)] * 4, input_output_aliases=aliases,
                  compiler_params=_cp("parallel"))(*args)


def kernel(x, mem, positions, ffn1_norm, ffn1_w_gate, ffn1_w_up, ffn1_w_down, mix_norm, w_in, attn_sinks, pool_w, pool_scale, attn_out_norm, pool_out_norm, w_out, xattn_norm, mem_norm, xattn_wq, xattn_wkv, xattn_wo, ffn2_norm, ffn2_w_gate, ffn2_w_up, ffn2_w_down, final_norm, loss_target, m_ffn1_norm, m_ffn1_w_gate, m_ffn1_w_up, m_ffn1_w_down, m_mix_norm, m_w_in, m_attn_sinks, m_pool_w, m_pool_scale, m_attn_out_norm, m_pool_out_norm, m_w_out, m_xattn_norm, m_mem_norm, m_xattn_wq, m_xattn_wkv, m_xattn_wo, m_ffn2_norm, m_ffn2_w_gate, m_ffn2_w_up, m_ffn2_w_down, m_final_norm, v_ffn1_norm, v_ffn1_w_gate, v_ffn1_w_up, v_ffn1_w_down, v_mix_norm, v_w_in, v_attn_sinks, v_pool_w, v_pool_scale, v_attn_out_norm, v_pool_out_norm, v_w_out, v_xattn_norm, v_mem_norm, v_xattn_wq, v_xattn_wkv, v_xattn_wo, v_ffn2_norm, v_ffn2_w_gate, v_ffn2_w_up, v_ffn2_w_down, v_final_norm):
    ws = dict(ffn1_norm=ffn1_norm, ffn1_w_gate=ffn1_w_gate, ffn1_w_up=ffn1_w_up, ffn1_w_down=ffn1_w_down,
              mix_norm=mix_norm, w_in=w_in, attn_sinks=attn_sinks, pool_w=pool_w, pool_scale=pool_scale,
              attn_out_norm=attn_out_norm, pool_out_norm=pool_out_norm, w_out=w_out, xattn_norm=xattn_norm,
              mem_norm=mem_norm, xattn_wq=xattn_wq, xattn_wkv=xattn_wkv, xattn_wo=xattn_wo, ffn2_norm=ffn2_norm,
              ffn2_w_gate=ffn2_w_gate, ffn2_w_up=ffn2_w_up, ffn2_w_down=ffn2_w_down, final_norm=final_norm)
    ms = dict(ffn1_norm=m_ffn1_norm, ffn1_w_gate=m_ffn1_w_gate, ffn1_w_up=m_ffn1_w_up, ffn1_w_down=m_ffn1_w_down,
              mix_norm=m_mix_norm, w_in=m_w_in, attn_sinks=m_attn_sinks, pool_w=m_pool_w, pool_scale=m_pool_scale,
              attn_out_norm=m_attn_out_norm, pool_out_norm=m_pool_out_norm, w_out=m_w_out, xattn_norm=m_xattn_norm,
              mem_norm=m_mem_norm, xattn_wq=m_xattn_wq, xattn_wkv=m_xattn_wkv, xattn_wo=m_xattn_wo,
              ffn2_norm=m_ffn2_norm, ffn2_w_gate=m_ffn2_w_gate, ffn2_w_up=m_ffn2_w_up, ffn2_w_down=m_ffn2_w_down,
              final_norm=m_final_norm)
    vs = dict(ffn1_norm=v_ffn1_norm, ffn1_w_gate=v_ffn1_w_gate, ffn1_w_up=v_ffn1_w_up, ffn1_w_down=v_ffn1_w_down,
              mix_norm=v_mix_norm, w_in=v_w_in, attn_sinks=v_attn_sinks, pool_w=v_pool_w, pool_scale=v_pool_scale,
              attn_out_norm=v_attn_out_norm, pool_out_norm=v_pool_out_norm, w_out=v_w_out, xattn_norm=v_xattn_norm,
              mem_norm=v_mem_norm, xattn_wq=v_xattn_wq, xattn_wkv=v_xattn_wkv, xattn_wo=v_xattn_wo,
              ffn2_norm=v_ffn2_norm, ffn2_w_gate=v_ffn2_w_gate, ffn2_w_up=v_ffn2_w_up, ffn2_w_down=v_ffn2_w_down,
              final_norm=v_final_norm)
    depth = ffn1_norm.shape[0]
    T, D = x.shape[1], x.shape[2]
    xv = x.reshape(T, D)
    memv = mem.reshape(mem.shape[1], D)
    tgt = loss_target.reshape(T, D)
    cos2, sin2 = _rope_tables(positions.reshape(T))

    q_me = 2 * lax.axis_index("x") + lax.axis_index("y")
    place = jnp.stack([lax.axis_index("c"), q_me]).astype(jnp.int32)
    seq_ids = iter(range(1, 1 + 4 * depth))
    Ws, sizes, first = [dict() for _ in range(depth)], {}, None
    for l in range(depth):
        for gi, group in enumerate(GROUPS):
            t = f"l{l}g{gi}"
            bufs, sizes[gi] = _own_shard_buffers(ws, l, group, place, t + "_shard")
            if first is None:
                full = first = _allgather_weights(bufs, sizes[gi], t + "_allgather")
            else:
                full = _allgather_weights(_after(bufs, first), sizes[gi], t + "_allgather", seq_id=next(seq_ids))
            Ws[l].update({name: f for (name, _), f in zip(group, full)})
    Ps = [{n: (ws[n][l].reshape(1, -1) if n != "pool_w" else ws[n][l]) for n in SMALL if n != "final_norm"}
          for l in range(depth)]

    saved = []
    h = xv
    for l in range(depth):
        h, sv = _layer_fwd(l, h, memv, Ws[l], Ps[l], cos2, sin2)
        saved.append(sv)
    loss_row, dx, d_final = _loss_head(h, tgt, final_norm.reshape(1, D), "loss_head")
    GWs, GPs = [None] * depth, [None] * depth
    for l in reversed(range(depth)):
        dx, GWs[l], GPs[l] = _layer_bwd(l, dx, saved[l], memv, Ws[l], Ps[l], cos2, sin2)

    stacked = {}
    for l in reversed(range(depth)):
        for gi, group in reversed(list(enumerate(GROUPS))):
            reds = _reduce_group(f"l{l}g{gi}r", group, GWs[l], sizes[gi], place, next(seq_ids))
            for (name, tr), red in zip(group, reds):
                stacked[name] = _adamw_layer(l, ws[name], red, ms[name], vs[name], tr, stacked.get(name),
                                             f"l{l}_adamw_{name}")
    small_part = {n: jnp.stack([GPs[l][n].reshape(ws[n].shape[1:]) for l in range(depth)]) for n in SMALL if n != "final_norm"}
    small_part["final_norm"] = d_final.reshape(D)
    small_g = _unpack_small(_allreduce_small(_pack_small(small_part), "small_allreduce"), ws)
    loss = lax.psum(loss_row[0, 0], ("x", "y", "c"))

    grads, deltas, new_m, new_v = {}, {}, {}, {}
    for name, _ in BIG:
        grads[name], deltas[name], new_m[name], new_v[name] = stacked[name]
    d, nm, nv = _adamw(_pack_small(ws), _pack_small(small_g), _pack_small(ms), _pack_small(vs), "adamw_small")
    grads.update(small_g)
    deltas.update(_unpack_small(d, ws))
    new_m.update(_unpack_small(nm, ws))
    new_v.update(_unpack_small(nv, ws))

    grad_x = dx.reshape(x.shape)
    return (loss, grad_x, *[grads[n] for n in WEIGHTS], *[deltas[n] for n in WEIGHTS],
            *[new_m[n] for n in WEIGHTS], *[new_v[n] for n in WEIGHTS])
```

```python
import functools

import jax
import jax.numpy as jnp
from jax import lax
from jax.experimental import pallas as pl
from jax.experimental.pallas import tpu as pltpu
from jax.experimental.pallas import tpu_sc as plsc

F32 = jnp.float32
MXU = jnp.bfloat16

EPS = 1e-6
HEAD_DIM = 64
N_Q_HEADS = 8
N_KV_HEADS = 2
Q_PER_KV = N_Q_HEADS // N_KV_HEADS
ATTN_WIDTH = N_Q_HEADS * HEAD_DIM
KV_WIDTH = N_KV_HEADS * HEAD_DIM
BLOCK = 128
ROPE_THETA = 10000.0
POOL_WINDOWS = (2, 4, 8, 16)
POOL_GROUP = 128
POOL_WIDTH = len(POOL_WINDOWS) * POOL_GROUP
POOL_HALO = 16
X_HEADS = 4
X_HEAD_DIM = 256
FFN_RES = 0.5
NEG = -1e30
ADAM_LR = 0.001
ADAM_B1 = 0.9
ADAM_B2 = 0.999
ADAM_EPS = 1e-08
ADAM_WD = 0.01
ADAM_STEP = 10

N_CHIPS = 4
N_DEV = 8
V7X_VMEM_BYTES = 64 * 1024 * 1024
VMEM_LIMIT = V7X_VMEM_BYTES - 8 * 1024 * 1024
LANES = 128
TOKEN_TILE = 512
MESH = pl.DeviceIdType.MESH

BIG = (("ffn1_w_gate", True), ("ffn1_w_up", True), ("ffn1_w_down", False), ("w_in", True), ("w_out", False),
       ("xattn_wq", False), ("xattn_wkv", True), ("xattn_wo", False),
       ("ffn2_w_gate", True), ("ffn2_w_up", True), ("ffn2_w_down", False))
GROUPS = (BIG[:3], BIG[3:])
SMALL = ("ffn1_norm", "mix_norm", "attn_sinks", "pool_w", "pool_scale", "attn_out_norm", "pool_out_norm",
         "xattn_norm", "mem_norm", "ffn2_norm", "final_norm")
WEIGHTS = ("ffn1_norm", "ffn1_w_gate", "ffn1_w_up", "ffn1_w_down", "mix_norm", "w_in", "attn_sinks", "pool_w",
           "pool_scale", "attn_out_norm", "pool_out_norm", "w_out", "xattn_norm", "mem_norm", "xattn_wq",
           "xattn_wkv", "xattn_wo", "ffn2_norm", "ffn2_w_gate", "ffn2_w_up", "ffn2_w_down", "final_norm")


def _S(shape, dtype):
    return jax.ShapeDtypeStruct(tuple(shape), dtype)


def _pcall(body, **kw):
    return pl.pallas_call(body, **kw)


def _cp(*sem):
    return pltpu.CompilerParams(dimension_semantics=tuple(sem), vmem_limit_bytes=VMEM_LIMIT)


def _nt(a, b):
    return lax.dot_general(a, b, (((1,), (1,)), ((), ())), preferred_element_type=F32)


def _nn(a, b):
    return lax.dot_general(a, b, (((1,), (0,)), ((), ())), preferred_element_type=F32)


def _tn(a, b):
    return lax.dot_general(a, b, (((0,), (0,)), ((), ())), preferred_element_type=F32)


def _tile(n, want):
    t = min(n, want)
    assert n % t == 0, (n, want)
    return t


def _resident(shape):
    nd = len(shape)
    return pl.BlockSpec(tuple(shape), lambda *_: (0,) * nd)


def _rms_fwd(x, g, name):
    T, C = x.shape
    tm = _tile(T, TOKEN_TILE)

    def body(x_ref, g_ref, o_ref):
        xv = x_ref[...]
        r = lax.rsqrt(jnp.mean(xv * xv, axis=-1, keepdims=True) + EPS)
        o_ref[...] = (xv * r * g_ref[...]).astype(o_ref.dtype)

    return _pcall(body, name=name, grid=(T // tm,),
                  in_specs=[pl.BlockSpec((tm, C), lambda i: (i, 0)), _resident((1, C))],
                  out_specs=pl.BlockSpec((tm, C), lambda i: (i, 0)),
                  out_shape=_S((T, C), MXU), compiler_params=_cp("parallel"))(x, g)


def _rms_bwd(x, g, dh, name, dres=None, col=0):
    T, C = x.shape
    tm = _tile(T, TOKEN_TILE)

    def body(*refs):
        if dres is None:
            x_ref, g_ref, dh_ref, dx_ref, dg_ref = refs
        else:
            x_ref, g_ref, dh_ref, dres_ref, dx_ref, dg_ref = refs
        xv = x_ref[...]
        r = lax.rsqrt(jnp.mean(xv * xv, axis=-1, keepdims=True) + EPS)
        xh = xv * r
        dhv = dh_ref[...].astype(F32)
        dxn = dhv * g_ref[...]
        dx = r * (dxn - xh * jnp.mean(dxn * xh, axis=-1, keepdims=True))
        if dres is not None:
            dx = dx + dres_ref[...]
        dx_ref[...] = dx

        @pl.when(pl.program_id(0) == 0)
        def _():
            dg_ref[...] = jnp.zeros_like(dg_ref)

        dg_ref[...] += jnp.sum(dhv * xh, axis=0, keepdims=True)

    tok = pl.BlockSpec((tm, C), lambda i: (i, 0))
    in_specs = [tok, _resident((1, C)), pl.BlockSpec((tm, C), lambda i: (i, col))]
    args = [x, g, dh]
    if dres is not None:
        in_specs.append(tok)
        args.append(dres)
    return _pcall(body, name=name, grid=(T // tm,), in_specs=in_specs,
                  out_specs=[tok, _resident((1, C))],
                  out_shape=[_S((T, C), F32), _S((1, C), F32)], compiler_params=_cp("arbitrary"))(*args)


def _mm(pairs, *, nt, out_dtype, name, res=None, res_scale=1.0, a_scale=1.0, tm=TOKEN_TILE):
    M = pairs[0][0].shape[0]
    N = pairs[0][1].shape[0] if nt else pairs[0][1].shape[1]
    tm = _tile(M, tm)
    n = len(pairs)

    def body(*refs):
        a_refs, w_refs = refs[:n], refs[n:2 * n]
        o_ref = refs[-1]
        acc = None
        for a_ref, w_ref in zip(a_refs, w_refs):
            a = a_ref[...]
            if a_scale != 1.0:
                a = a * a_scale
            a = a.astype(MXU)
            p = _nt(a, w_ref[...]) if nt else _nn(a, w_ref[...])
            acc = p if acc is None else acc + p
        if res is not None:
            acc = refs[2 * n][...] + res_scale * acc
        o_ref[...] = acc.astype(o_ref.dtype)

    in_specs = [pl.BlockSpec((tm, a.shape[1]), lambda i: (i, 0)) for a, _ in pairs]
    in_specs += [_resident(w.shape) for _, w in pairs]
    args = [a for a, _ in pairs] + [w for _, w in pairs]
    if res is not None:
        in_specs.append(pl.BlockSpec((tm, N), lambda i: (i, 0)))
        args.append(res)
    return _pcall(body, name=name, grid=(M // tm,), in_specs=in_specs,
                  out_specs=pl.BlockSpec((tm, N), lambda i: (i, 0)),
                  out_shape=_S((M, N), out_dtype), compiler_params=_cp("parallel"))(*args)


def _mm_tn(l, r, name, *, l_scale=1.0, r_scale=1.0, tr=1408, tt=TOKEN_TILE):
    T, R = l.shape
    C = r.shape[1]
    tt = _tile(T, tt)
    tr = tr if R % tr == 0 else (1024 if R % 1024 == 0 and R > 1280 else R)

    def body(l_ref, r_ref, o_ref):
        lv, rv = l_ref[...], r_ref[...]
        if l_scale != 1.0:
            lv = lv * l_scale
        if r_scale != 1.0:
            rv = rv * r_scale
        lv, rv = lv.astype(MXU), rv.astype(MXU)

        @pl.when(pl.program_id(1) == 0)
        def _():
            o_ref[...] = jnp.zeros_like(o_ref)

        o_ref[...] += _tn(lv, rv)

    return _pcall(body, name=name, grid=(R // tr, T // tt),
                  in_specs=[pl.BlockSpec((tt, tr), lambda i, t: (t, i)), pl.BlockSpec((tt, C), lambda i, t: (t, 0))],
                  out_specs=pl.BlockSpec((tr, C), lambda i, t: (i, 0)),
                  out_shape=_S((R, C), F32), compiler_params=_cp("parallel", "arbitrary"))(l, r)


FFN_COL_TILE = 1408


def _sigmoid(a):
    return 0.5 * (jnp.tanh(0.5 * a) + 1.0)


def _ffn_up(h, wgT, wuT, name):
    T, D = h.shape
    Fd = wgT.shape[0]
    tm, tn = _tile(T, TOKEN_TILE), _tile(Fd, FFN_COL_TILE)

    def body(h_ref, wg_ref, wu_ref, a_ref, b_ref, s_ref):
        hv = h_ref[...]
        a = _nt(hv, wg_ref[...])
        b = _nt(hv, wu_ref[...])
        s = a * _sigmoid(a) * b
        a_ref[...] = a.astype(a_ref.dtype)
        b_ref[...] = b.astype(b_ref.dtype)
        s_ref[...] = s.astype(s_ref.dtype)

    wspec = pl.BlockSpec((tn, D), lambda j, i: (j, 0))
    ospec = pl.BlockSpec((tm, tn), lambda j, i: (i, j))
    return _pcall(body, name=name, grid=(Fd // tn, T // tm),
                  in_specs=[pl.BlockSpec((tm, D), lambda j, i: (i, 0)), wspec, wspec],
                  out_specs=[ospec, ospec, ospec], out_shape=[_S((T, Fd), MXU)] * 3,
                  compiler_params=_cp("parallel", "parallel"))(h, wgT, wuT)


def _ffn_mid_bwd(dx, wd, a, b, name):
    T, D = dx.shape
    Fd = wd.shape[0]
    tm, tn = _tile(T, TOKEN_TILE), _tile(Fd, FFN_COL_TILE)

    def body(dx_ref, wd_ref, a_ref, b_ref, da_ref, db_ref):
        dy = (dx_ref[...] * FFN_RES).astype(MXU)
        ds = _nt(dy, wd_ref[...])
        av, bv = a_ref[...].astype(F32), b_ref[...].astype(F32)
        sg = _sigmoid(av)
        da_ref[...] = (ds * bv * (sg * (1.0 + av * (1.0 - sg)))).astype(da_ref.dtype)
        db_ref[...] = (ds * (av * sg)).astype(db_ref.dtype)

    aspec = pl.BlockSpec((tm, tn), lambda j, i: (i, j))
    return _pcall(body, name=name, grid=(Fd // tn, T // tm),
                  in_specs=[pl.BlockSpec((tm, D), lambda j, i: (i, 0)), pl.BlockSpec((tn, D), lambda j, i: (j, 0)),
                            aspec, aspec],
                  out_specs=[aspec, aspec], out_shape=[_S((T, Fd), MXU)] * 2,
                  compiler_params=_cp("parallel", "parallel"))(dx, wd, a, b)


def _swap_halves(t):
    w = t.shape[1]
    lane = lax.broadcasted_iota(jnp.int32, t.shape, 1)
    first = (lane % HEAD_DIM) < (HEAD_DIM // 2)
    return jnp.where(first, pltpu.roll(t, w - HEAD_DIM // 2, 1), pltpu.roll(t, HEAD_DIM // 2, 1))


def _rope(t, cos2, sin2):
    reps = t.shape[1] // LANES
    c = jnp.tile(cos2, (1, reps)) if reps > 1 else cos2
    s = jnp.tile(sin2, (1, reps)) if reps > 1 else sin2
    return t * c + _swap_halves(t) * s


def _rope_bwd(dt, cos2, sin2):
    reps = dt.shape[1] // LANES
    c = jnp.tile(cos2, (1, reps)) if reps > 1 else cos2
    s = jnp.tile(sin2, (1, reps)) if reps > 1 else sin2
    return dt * c + _swap_halves(dt * s)


def _in_proj(h, winT, cos2, sin2, name):
    T, D = h.shape
    tm = _tile(T, TOKEN_TILE)
    qe, ke, ve = ATTN_WIDTH, ATTN_WIDTH + KV_WIDTH, ATTN_WIDTH + 2 * KV_WIDTH

    def body(h_ref, w_ref, c_ref, s_ref, q_ref, k_ref, v_ref, u_ref):
        proj = _nt(h_ref[...], w_ref[...])
        cv, sv = c_ref[...], s_ref[...]
        q_ref[...] = _rope(proj[:, :qe], cv, sv).astype(q_ref.dtype)
        k_ref[...] = _rope(proj[:, qe:ke], cv, sv).astype(k_ref.dtype)
        v_ref[...] = proj[:, ke:ve].astype(v_ref.dtype)
        u_ref[...] = proj[:, ve:]

    def tok(w):
        return pl.BlockSpec((tm, w), lambda i: (i, 0))

    return _pcall(body, name=name, grid=(T // tm,),
                  in_specs=[tok(D), _resident(winT.shape), tok(LANES), tok(LANES)],
                  out_specs=[tok(ATTN_WIDTH), tok(KV_WIDTH), tok(KV_WIDTH), tok(POOL_WIDTH)],
                  out_shape=[_S((T, ATTN_WIDTH), MXU), _S((T, KV_WIDTH), MXU), _S((T, KV_WIDTH), MXU),
                             _S((T, POOL_WIDTH), F32)],
                  compiler_params=_cp("parallel"))(h, winT, cos2, sin2)


SWA_TILE_BLOCKS = 4
SM_SCALE = HEAD_DIM ** -0.5


def _swa_bias(first_tile):
    cols = Q_PER_KV * BLOCK
    kj = lax.broadcasted_iota(jnp.int32, (2 * BLOCK, cols), 0)
    qi = lax.broadcasted_iota(jnp.int32, (2 * BLOCK, cols), 1) % BLOCK
    diff = qi + BLOCK - kj
    bias = jnp.where((diff >= 0) & (diff < BLOCK), 0.0, NEG)
    return bias, jnp.where(kj < jnp.where(first_tile, BLOCK, 0), NEG, bias)


def _swa_probs(kh, qs, sink_row, bias):
    s = _nt(kh, qs) + bias
    m = jnp.maximum(jnp.max(s, axis=0, keepdims=True), sink_row)
    e = jnp.exp(s - m)
    es = jnp.exp(sink_row - m)
    inv = 1.0 / (jnp.sum(e, axis=0, keepdims=True) + es)
    return e * inv, es * inv


def _sink_row(sinks_ref, kv):
    return jnp.concatenate([jnp.full((1, BLOCK), sinks_ref[0, kv * Q_PER_KV + g], F32) for g in range(Q_PER_KV)], axis=1)


def _stack_heads(t, kv):
    return jnp.concatenate([t[:, (kv * Q_PER_KV + g) * HEAD_DIM:(kv * Q_PER_KV + g + 1) * HEAD_DIM]
                            for g in range(Q_PER_KV)], axis=0)


def _swa_specs(T):
    tq = _tile(T, SWA_TILE_BLOCKS * BLOCK)
    nbt = tq // BLOCK
    cur = lambda w: pl.BlockSpec((tq, w), lambda i: (i, 0))
    prev = lambda w: pl.BlockSpec((BLOCK, w), lambda i: (jnp.maximum(i * nbt - 1, 0), 0))
    return tq, nbt, cur, prev


def _rows(b):
    return slice(b * BLOCK, (b + 1) * BLOCK)


def _swa_fwd(q, k, v, sinks, name):
    T = q.shape[0]
    tq, nbt, cur, prev = _swa_specs(T)

    def body(sinks_ref, q_ref, k_ref, kp_ref, v_ref, vp_ref, o_ref):
        bias, bias0 = _swa_bias(pl.program_id(0) == 0)
        sink = [_sink_row(sinks_ref, kv) for kv in range(N_KV_HEADS)]
        kx = jnp.concatenate([kp_ref[...], k_ref[...]], axis=0)
        vx = jnp.concatenate([vp_ref[...], v_ref[...]], axis=0)
        for b in range(nbt):
            qv = q_ref[_rows(b), :] * SM_SCALE
            kk, vv = kx[b * BLOCK:(b + 2) * BLOCK], vx[b * BLOCK:(b + 2) * BLOCK]
            for kv in range(N_KV_HEADS):
                hs = slice(kv * HEAD_DIM, (kv + 1) * HEAD_DIM)
                p, _ = _swa_probs(kk[:, hs], _stack_heads(qv, kv), sink[kv], bias0 if b == 0 else bias)
                o_t = _tn(vv[:, hs], p.astype(MXU))
                for g in range(Q_PER_KV):
                    c0 = (kv * Q_PER_KV + g) * HEAD_DIM
                    o_ref[_rows(b), c0:c0 + HEAD_DIM] = o_t[:, _rows(g)].T

    return _pcall(body, name=name, grid=(T // tq,),
                  in_specs=[pl.BlockSpec(memory_space=pltpu.SMEM), cur(ATTN_WIDTH), cur(KV_WIDTH), prev(KV_WIDTH),
                            cur(KV_WIDTH), prev(KV_WIDTH)],
                  out_specs=cur(ATTN_WIDTH), out_shape=_S((T, ATTN_WIDTH), F32),
                  compiler_params=_cp("parallel"))(sinks, q, k, k, v, v)


def _swa_bwd(q, k, v, do, sinks, name):
    T = q.shape[0]
    tq, nbt, cur, prev = _swa_specs(T)
    per_tile = lambda w: pl.BlockSpec((BLOCK, w), lambda i: (i, 0))

    def add(acc, t):
        return t if acc is None else acc + t

    def body(sinks_ref, q_ref, k_ref, kp_ref, v_ref, vp_ref, do_ref,
             dq_ref, dk_ref, dkp_ref, dv_ref, dvp_ref, dsk_ref):
        bias, bias0 = _swa_bias(pl.program_id(0) == 0)
        sink = [_sink_row(sinks_ref, kv) for kv in range(N_KV_HEADS)]
        kx = jnp.concatenate([kp_ref[...], k_ref[...]], axis=0)
        vx = jnp.concatenate([vp_ref[...], v_ref[...]], axis=0)

        @pl.when(pl.program_id(0) == 0)
        def _():
            dsk_ref[...] = jnp.zeros_like(dsk_ref)

        dk_acc, dv_acc = [None] * (nbt + 1), [None] * (nbt + 1)
        dsk_acc = [None] * N_Q_HEADS
        for b in range(nbt):
            qv, dov = q_ref[_rows(b), :] * SM_SCALE, do_ref[_rows(b), :].astype(MXU)
            kk, vv = kx[b * BLOCK:(b + 2) * BLOCK], vx[b * BLOCK:(b + 2) * BLOCK]
            dks, dvs = [], []
            for kv in range(N_KV_HEADS):
                hs = slice(kv * HEAD_DIM, (kv + 1) * HEAD_DIM)
                qs, dos = _stack_heads(qv, kv), _stack_heads(dov, kv)
                p, ps = _swa_probs(kk[:, hs], qs, sink[kv], bias0 if b == 0 else bias)
                dp = _nt(vv[:, hs], dos)
                delta = jnp.sum(p * dp, axis=0, keepdims=True)
                ds = (p * (dp - delta)).astype(MXU)
                dq_t = _tn(kk[:, hs], ds) * SM_SCALE
                dks.append(_nn(ds, qs))
                dvs.append(_nn(p.astype(MXU), dos))
                dsink = -ps * delta
                for g in range(Q_PER_KV):
                    h = kv * Q_PER_KV + g
                    dq_ref[_rows(b), h * HEAD_DIM:(h + 1) * HEAD_DIM] = dq_t[:, _rows(g)].T
                    dsk_acc[h] = add(dsk_acc[h], jnp.sum(dsink[:, _rows(g)], axis=1, keepdims=True))
            dk, dv = jnp.concatenate(dks, axis=1), jnp.concatenate(dvs, axis=1)
            dk_acc[b], dk_acc[b + 1] = add(dk_acc[b], dk[:BLOCK]), add(dk_acc[b + 1], dk[BLOCK:])
            dv_acc[b], dv_acc[b + 1] = add(dv_acc[b], dv[:BLOCK]), add(dv_acc[b + 1], dv[BLOCK:])
        dkp_ref[...], dvp_ref[...] = dk_acc[0], dv_acc[0]
        dk_ref[...] = jnp.concatenate(dk_acc[1:], axis=0)
        dv_ref[...] = jnp.concatenate(dv_acc[1:], axis=0)
        for h in range(N_Q_HEADS):
            dsk_ref[h:h + 1, :] += jnp.broadcast_to(dsk_acc[h], (1, LANES))

    kvs, kvp = _S((T, KV_WIDTH), F32), _S((T // tq * BLOCK, KV_WIDTH), F32)
    return _pcall(body, name=name, grid=(T // tq,),
                  in_specs=[pl.BlockSpec(memory_space=pltpu.SMEM), cur(ATTN_WIDTH), cur(KV_WIDTH), prev(KV_WIDTH),
                            cur(KV_WIDTH), prev(KV_WIDTH), cur(ATTN_WIDTH)],
                  out_specs=[cur(ATTN_WIDTH), cur(KV_WIDTH), per_tile(KV_WIDTH), cur(KV_WIDTH), per_tile(KV_WIDTH),
                             _resident((N_Q_HEADS, LANES))],
                  out_shape=[_S((T, ATTN_WIDTH), F32), kvs, kvp, kvs, kvp, _S((N_Q_HEADS, LANES), F32)],
                  compiler_params=_cp("arbitrary"))(sinks, q, k, k, v, v, do)


def _dproj(dq, dk, dkp, dv, dvp, du, cos2, sin2, name):
    T = dq.shape[0]
    tq, nbt, cur, _ = _swa_specs(T)
    nt = T // tq
    nxt = lambda w: pl.BlockSpec((BLOCK, w), lambda i: (jnp.minimum(i + 1, nt - 1), 0))

    def body(dq_ref, dk_ref, dkp_ref, dv_ref, dvp_ref, du_ref, c_ref, s_ref, o_ref):
        more = (pl.program_id(0) < nt - 1).astype(F32)
        cv, sv = c_ref[...], s_ref[...]

        def whole(t_ref, p_ref):
            t, last = t_ref[...], t_ref[tq - BLOCK:, :] + more * p_ref[...]
            return last if nbt == 1 else jnp.concatenate([t[:tq - BLOCK], last], axis=0)

        o_ref[...] = jnp.concatenate(
            [_rope_bwd(dq_ref[...], cv, sv), _rope_bwd(whole(dk_ref, dkp_ref), cv, sv), whole(dv_ref, dvp_ref),
             du_ref[...]], axis=1).astype(o_ref.dtype)

    width = ATTN_WIDTH + 2 * KV_WIDTH + POOL_WIDTH
    return _pcall(body, name=name, grid=(nt,),
                  in_specs=[cur(ATTN_WIDTH), cur(KV_WIDTH), nxt(KV_WIDTH), cur(KV_WIDTH), nxt(KV_WIDTH),
                            cur(POOL_WIDTH), cur(LANES), cur(LANES)],
                  out_specs=cur(width), out_shape=_S((T, width), MXU),
                  compiler_params=_cp("parallel"))(dq, dk, dkp, dv, dvp, du, cos2, sin2)


def _pool_specs(T):
    tm = _tile(T, TOKEN_TILE)
    hb = tm // POOL_HALO
    nh = T // POOL_HALO
    tok = lambda w: pl.BlockSpec((tm, w), lambda i: (i, 0))
    before = pl.BlockSpec((POOL_HALO, POOL_WIDTH), lambda i: (jnp.maximum(i * hb - 1, 0), 0))
    after = pl.BlockSpec((POOL_HALO, POOL_WIDTH), lambda i: (jnp.minimum((i + 1) * hb, nh - 1), 0))
    return tm, tok, before, after


def _window_counts(i, tm, rows, w):
    t = i * tm + lax.broadcasted_iota(jnp.int32, (rows, 1), 0)
    return jnp.minimum(t + 1, w).astype(F32)


def _pooled(u_ext, i, tm):
    out = []
    for g, w in enumerate(POOL_WINDOWS):
        acc = u_ext[:, g * POOL_GROUP:(g + 1) * POOL_GROUP]
        tok = acc[POOL_HALO:, :]
        sh = 1
        while sh < w:
            acc = acc + pltpu.roll(acc, sh, 0)
            sh *= 2
        out.append(acc[POOL_HALO:, :] / _window_counts(i, tm, tm, w) - tok)
    return out


def _pool_fwd(u, out_a, pool_w, pool_scale, ga, gb, name):
    T = u.shape[0]
    tm, tok, before, _ = _pool_specs(T)

    def body(u_ref, halo_ref, oa_ref, pw_ref, sc_ref, ga_ref, gb_ref, ob_ref, mg_ref):
        i = pl.program_id(0)
        halo = halo_ref[...] * (i > 0).astype(F32)
        pooled = _pooled(jnp.concatenate([halo, u_ref[...]], axis=0), i, tm)
        mixed = [_nn(pooled[g].astype(MXU), pw_ref[g].astype(MXU)) for g in range(len(POOL_WINDOWS))]
        ob = jnp.concatenate(mixed, axis=1) * sc_ref[...]
        ob_ref[...] = ob
        oa = oa_ref[...]
        ra = lax.rsqrt(jnp.mean(oa * oa, axis=-1, keepdims=True) + EPS)
        rb = lax.rsqrt(jnp.mean(ob * ob, axis=-1, keepdims=True) + EPS)
        mg_ref[...] = jnp.concatenate([oa * ra * ga_ref[...], ob * rb * gb_ref[...]], axis=1).astype(mg_ref.dtype)

    vec = _resident((1, POOL_WIDTH))
    return _pcall(body, name=name, grid=(T // tm,),
                  in_specs=[tok(POOL_WIDTH), before, tok(ATTN_WIDTH), _resident(pool_w.shape), vec, vec, vec],
                  out_specs=[tok(POOL_WIDTH), tok(ATTN_WIDTH + POOL_WIDTH)],
                  out_shape=[_S((T, POOL_WIDTH), F32), _S((T, ATTN_WIDTH + POOL_WIDTH), MXU)],
                  compiler_params=_cp("parallel"))(u, u, out_a, pool_w, pool_scale, ga, gb)


def _pool_bwd(u, dob, pool_w, pool_scale, name):
    T = u.shape[0]
    tm, tok, before, after = _pool_specs(T)
    nt = T // tm
    G = len(POOL_WINDOWS)

    def body(u_ref, halo_ref, dob_ref, dnext_ref, pw_ref, sc_ref, du_ref, dpw_ref, dsc_ref):
        i = pl.program_id(0)
        halo = halo_ref[...] * (i > 0).astype(F32)
        pooled = _pooled(jnp.concatenate([halo, u_ref[...]], axis=0), i, tm)
        dnext = dnext_ref[...] * (i < nt - 1).astype(F32)
        dext = jnp.concatenate([dob_ref[...], dnext], axis=0) * sc_ref[...]

        @pl.when(i == 0)
        def _():
            dpw_ref[...] = jnp.zeros_like(dpw_ref)
            dsc_ref[...] = jnp.zeros_like(dsc_ref)

        dus, dscs = [], []
        for g, w in enumerate(POOL_WINDOWS):
            gs = slice(g * POOL_GROUP, (g + 1) * POOL_GROUP)
            pw = pw_ref[g].astype(MXU)
            pg = pooled[g].astype(MXU)
            dmix = dext[:, gs].astype(MXU)
            dscs.append(jnp.sum(dob_ref[:, gs] * _nn(pg, pw), axis=0, keepdims=True))
            dpw_ref[g] += _tn(pg, dmix[:tm, :])
            dpooled = _nt(dmix, pw)
            acc = dpooled / _window_counts(i, tm, tm + POOL_HALO, w)
            sh = 1
            while sh < w:
                acc = acc + pltpu.roll(acc, tm + POOL_HALO - sh, 0)
                sh *= 2
            dus.append(acc[:tm, :] - dpooled[:tm, :])
        du_ref[...] = jnp.concatenate(dus, axis=1)
        dsc_ref[...] += jnp.concatenate(dscs, axis=1)

    vec = _resident((1, POOL_WIDTH))
    return _pcall(body, name=name, grid=(nt,),
                  in_specs=[tok(POOL_WIDTH), before, tok(POOL_WIDTH), after, _resident(pool_w.shape), vec],
                  out_specs=[tok(POOL_WIDTH), _resident(pool_w.shape), vec],
                  out_shape=[_S((T, POOL_WIDTH), F32), _S(pool_w.shape, F32), _S((1, POOL_WIDTH), F32)],
                  compiler_params=_cp("arbitrary"))(u, u, dob, dob, pool_w, pool_scale)


def _xattn_probs(qh, kh):
    s = _nt(qh, kh) * (X_HEAD_DIM ** -0.5)
    e = jnp.exp(s - jnp.max(s, axis=1, keepdims=True))
    return e / jnp.sum(e, axis=1, keepdims=True)


def _xattn_fwd(q, kvm, name):
    T, XW = q.shape
    tm = _tile(T, TOKEN_TILE)

    def body(q_ref, kv_ref, o_ref):
        for h in range(X_HEADS):
            hs = slice(h * X_HEAD_DIM, (h + 1) * X_HEAD_DIM)
            vs = slice(XW + h * X_HEAD_DIM, XW + (h + 1) * X_HEAD_DIM)
            p = _xattn_probs(q_ref[:, hs], kv_ref[:, hs])
            o_ref[:, hs] = _nn(p.astype(MXU), kv_ref[:, vs]).astype(o_ref.dtype)

    return _pcall(body, name=name, grid=(T // tm,),
                  in_specs=[pl.BlockSpec((tm, XW), lambda i: (i, 0)), _resident(kvm.shape)],
                  out_specs=pl.BlockSpec((tm, XW), lambda i: (i, 0)), out_shape=_S((T, XW), MXU),
                  compiler_params=_cp("parallel"))(q, kvm)


def _xattn_bwd(q, kvm, do, name):
    T, XW = q.shape
    tm = _tile(T, TOKEN_TILE)

    def body(q_ref, kv_ref, do_ref, dq_ref, dkv_ref):
        @pl.when(pl.program_id(0) == 0)
        def _():
            dkv_ref[...] = jnp.zeros_like(dkv_ref)

        for h in range(X_HEADS):
            hs = slice(h * X_HEAD_DIM, (h + 1) * X_HEAD_DIM)
            vs = slice(XW + h * X_HEAD_DIM, XW + (h + 1) * X_HEAD_DIM)
            qh, doh = q_ref[:, hs], do_ref[:, hs]
            p = _xattn_probs(qh, kv_ref[:, hs])
            dp = _nt(doh, kv_ref[:, vs])
            ds = (p * (dp - jnp.sum(p * dp, axis=1, keepdims=True)) * (X_HEAD_DIM ** -0.5)).astype(MXU)
            dq_ref[:, hs] = _nn(ds, kv_ref[:, hs]).astype(dq_ref.dtype)
            dkv_ref[:, hs] += _tn(ds, qh)
            dkv_ref[:, vs] += _tn(p.astype(MXU), doh)

    tok = pl.BlockSpec((tm, XW), lambda i: (i, 0))
    return _pcall(body, name=name, grid=(T // tm,),
                  in_specs=[tok, _resident(kvm.shape), tok],
                  out_specs=[tok, _resident(kvm.shape)],
                  out_shape=[_S((T, XW), MXU), _S(kvm.shape, F32)],
                  compiler_params=_cp("arbitrary"))(q, kvm, do)


def _loss_head(x, tgt, g, name):
    T, D = x.shape
    tm = _tile(T, TOKEN_TILE)

    def body(x_ref, t_ref, g_ref, loss_ref, dx_ref, dg_ref):
        xv, gv = x_ref[...], g_ref[...]
        r = lax.rsqrt(jnp.mean(xv * xv, axis=-1, keepdims=True) + EPS)
        xh = xv * r
        e = xh * gv - t_ref[...]
        dy = e * (1.0 / D)
        dxn = dy * gv
        dx_ref[...] = r * (dxn - xh * jnp.mean(dxn * xh, axis=-1, keepdims=True))

        @pl.when(pl.program_id(0) == 0)
        def _():
            loss_ref[...] = jnp.zeros_like(loss_ref)
            dg_ref[...] = jnp.zeros_like(dg_ref)

        part = jnp.sum(jnp.sum(e * e, axis=1, keepdims=True), axis=0, keepdims=True) * (0.5 / D)
        loss_ref[...] += jnp.broadcast_to(part, (1, LANES))
        dg_ref[...] += jnp.sum(dy * xh, axis=0, keepdims=True)

    tok = pl.BlockSpec((tm, D), lambda i: (i, 0))
    return _pcall(body, name=name, grid=(T // tm,),
                  in_specs=[tok, tok, _resident((1, D))],
                  out_specs=[_resident((1, LANES)), tok, _resident((1, D))],
                  out_shape=[_S((1, LANES), F32), _S((T, D), F32), _S((1, D), F32)],
                  compiler_params=_cp("arbitrary"))(x, tgt, g)


def _rows_tile(rows):
    for t in (512, 416, 352, 256, 128, 64, 32, 16, 8):
        if rows % t == 0:
            return t
    return rows


def _pair_sum(grads, gots, sizes, place, name):
    nw = len(sizes)
    C = grads[0].shape[1]

    def body(p_ref, *refs):
        g, got, out = refs[:nw], refs[nw:2 * nw], refs[2 * nw:]
        for w in range(nw):
            out[w][...] = (g[w][...] + got[w][...]).astype(out[w].dtype)

    def blk(w):
        return (sizes[w] // 4, C)

    in_specs = [pl.BlockSpec(blk(w), lambda q, s, p: (4 * q + 2 * p[0] + s, 0)) for w in range(nw)]
    in_specs += [pl.BlockSpec(blk(w), lambda q, s, p: (2 * q + s, 0)) for w in range(nw)]
    out_specs = [pl.BlockSpec(blk(w), lambda q, s, p: (2 * q + s, 0)) for w in range(nw)]
    gs = pltpu.PrefetchScalarGridSpec(num_scalar_prefetch=1, grid=(N_CHIPS, 2), in_specs=in_specs, out_specs=out_specs)
    return _pcall(body, name=name, grid_spec=gs, out_shape=[_S((2 * n, C), MXU) for n in sizes],
                  compiler_params=_cp("parallel", "parallel"))(place, *grads, *gots)


def _final_sum(grads, gots, recvs, sizes, place, name):
    nw = len(sizes)
    C = grads[0].shape[1]

    def body(p_ref, *refs):
        g, got, rv, out = refs[:nw], refs[nw:2 * nw], refs[2 * nw:5 * nw], refs[5 * nw:]
        for w in range(nw):
            acc = g[w][...] + got[w][...]
            for j in range(3):
                acc = acc + rv[3 * w + j][...].astype(F32)
            out[w][...] = acc

    def blk(w):
        return (sizes[w] // 4, C)

    in_specs = [pl.BlockSpec(blk(w), lambda s, p: (4 * p[1] + 2 * p[0] + s, 0)) for w in range(nw)]
    in_specs += [pl.BlockSpec(blk(w), lambda s, p: (2 * p[1] + s, 0)) for w in range(nw)]
    args = list(grads) + list(gots)
    for w in range(nw):
        for j in range(3):
            in_specs.append(pl.BlockSpec(blk(w), lambda s, p, j=j: (2 * j + s, 0)))
            args.append(recvs[w])
    out_specs = [pl.BlockSpec(blk(w), lambda s, p: (2 * p[0] + s, 0)) for w in range(nw)]
    gs = pltpu.PrefetchScalarGridSpec(num_scalar_prefetch=1, grid=(2,), in_specs=in_specs, out_specs=out_specs)
    return _pcall(body, name=name, grid_spec=gs, out_shape=[_S((n, C), F32) for n in sizes],
                  compiler_params=_cp("parallel"))(place, *args)


def _adamw(w, g, m, v, name):
    R, C = w.shape
    tr = _rows_tile(R)
    c1 = 1.0 / (1.0 - ADAM_B1 ** ADAM_STEP)
    c2 = 1.0 / (1.0 - ADAM_B2 ** ADAM_STEP)

    def body(w_ref, g_ref, m_ref, v_ref, d_ref, nm_ref, nv_ref):
        gv = g_ref[...]
        nm = ADAM_B1 * m_ref[...] + (1.0 - ADAM_B1) * gv
        nv = ADAM_B2 * v_ref[...] + (1.0 - ADAM_B2) * (gv * gv)
        d_ref[...] = -ADAM_LR * ((nm * c1) / (jnp.sqrt(nv * c2) + ADAM_EPS) + ADAM_WD * w_ref[...])
        nm_ref[...] = nm
        nv_ref[...] = nv

    spec = pl.BlockSpec((tr, C), lambda i: (i, 0))
    return _pcall(body, name=name, grid=(R // tr,), in_specs=[spec] * 4, out_specs=[spec] * 3,
                  out_shape=[_S((R, C), F32)] * 3, compiler_params=_cp("parallel"))(w, g, m, v)


ANY = pl.BlockSpec(memory_space=pl.ANY)


def _place():
    x, y, c = lax.axis_index("x"), lax.axis_index("y"), lax.axis_index("c")
    chips = [(1 - x, y), (x, 1 - y), (1 - x, 1 - y)]
    return x, y, c, chips


def _remote(src, dst, send_sem, recv_sem, dev):
    return pltpu.make_async_remote_copy(src_ref=src, dst_ref=dst, send_sem=send_sem, recv_sem=recv_sem,
                                        device_id=dev, device_id_type=MESH)


def _drain(like, send_sem, recv_sem, me, *, send=False, recv=False):
    d = _remote(like, like, send_sem, recv_sem, me)
    if send:
        d.wait_send()
    if recv:
        d.wait_recv()


def _dma_sems(n):
    return [pltpu.SemaphoreType.DMA((n,)), pltpu.SemaphoreType.DMA((n,))]


def _comm_params():
    return pltpu.CompilerParams(has_side_effects=True)


def _on_sequencer(exchange, refs, sem_types, peers_of, name, seq_id):
    def launch(*sems):
        x, y, c, chips = _place()
        barrier = pltpu.get_barrier_semaphore()
        peers = peers_of(x, y, c, chips)
        for peer in peers:
            pl.semaphore_signal(barrier, inc=1, device_id=peer, device_id_type=MESH)
        pl.semaphore_wait(barrier, len(peers))
        exchange(refs, *sems)

    pl.kernel(launch, mesh=plsc.ScalarSubcoreMesh(axis_name="seq", num_cores=1), name=name,
              scratch_types=tuple(sem_types), compiler_params=pltpu.CompilerParams(collective_id=seq_id))()


def _hbm_ref(a):
    return jax.new_ref(a, memory_space=pltpu.MemorySpace.HBM)


def _allgather_weights(bufs, sizes, name, seq_id=None):
    nw = len(sizes)

    def exchange(out, s_ici, r_ici, s_fwd, r_fwd):
        x, y, c, chips = _place()
        me, sib = (x, y, c), (x, y, 1 - c)
        q_me = 2 * x + y

        def rows(w, q):
            hw = sizes[w] // 2
            return out[w].at[pl.ds(q * sizes[w] + c * hw, hw)]

        def three(w):
            return out[w].at[pl.ds(0, 3 * (sizes[w] // 2))]

        for w in range(nw):
            for px, py in chips:
                _remote(rows(w, q_me), rows(w, q_me), s_ici.at[w], r_ici.at[w], (px, py, c)).start()
        for w in range(nw):
            _drain(three(w), s_ici.at[w], r_ici.at[w], me, recv=True)
            for px, py in chips:
                got = rows(w, 2 * px + py)
                _remote(got, got, s_fwd.at[w], r_fwd.at[w], sib).start()
        for w in range(nw):
            _drain(three(w), s_fwd.at[w], r_fwd.at[w], me, recv=True)
        for w in range(nw):
            _drain(three(w), s_ici.at[w], r_ici.at[w], me, send=True)
            _drain(three(w), s_fwd.at[w], r_fwd.at[w], me, send=True)

    if seq_id is not None:
        refs = [_hbm_ref(b) for b in bufs]
        _on_sequencer(exchange, refs, _dma_sems(nw) + _dma_sems(nw),
                      lambda x, y, c, chips: [(x, y, 1 - c)] + [(px, py, c) for px, py in chips], name, seq_id)
        return [r[...] for r in refs]

    def body(*refs):
        exchange(refs[nw:2 * nw], *refs[2 * nw:])

    return _pcall(body, name=name, in_specs=[ANY] * nw, out_specs=[ANY] * nw,
                  out_shape=[_S(b.shape, b.dtype) for b in bufs],
                  input_output_aliases={w: w for w in range(nw)},
                  scratch_shapes=_dma_sems(nw) + _dma_sems(nw), compiler_params=_comm_params())(*bufs)


def _sibling_only(x, y, c, chips):
    return [(x, y, 1 - c)]


def _rs_pair_exchange(grads, sizes, name, seq_id=None):
    C = grads[0].shape[1]
    nw = len(sizes)
    out_shape = [_S((2 * n, C), F32) for n in sizes]

    def exchange(refs, s_sem, r_sem):
        g, got = refs[:nw], refs[nw:2 * nw]
        x, y, c, _ = _place()
        me, sib = (x, y, c), (x, y, 1 - c)
        for w in range(nw):
            hw = sizes[w] // 2
            for q in range(N_CHIPS):
                _remote(g[w].at[pl.ds(q * sizes[w] + (1 - c) * hw, hw)], got[w].at[pl.ds(q * hw, hw)],
                        s_sem.at[w], r_sem.at[w], sib).start()
        for w in range(nw):
            _drain(got[w], s_sem.at[w], r_sem.at[w], me, send=True, recv=True)

    if seq_id is not None:
        gots = [jax.empty_ref(s, memory_space=pltpu.MemorySpace.HBM) for s in out_shape]
        _on_sequencer(exchange, [_hbm_ref(g) for g in grads] + gots, _dma_sems(nw), _sibling_only, name, seq_id)
        return [r[...] for r in gots]

    def body(*refs):
        exchange(refs[:2 * nw], *refs[2 * nw:])

    return _pcall(body, name=name, in_specs=[ANY] * nw, out_specs=[ANY] * nw, out_shape=out_shape,
                  scratch_shapes=_dma_sems(nw), compiler_params=_comm_params())(*grads)


def _rs_chip_exchange(sums, sizes, name, seq_id=None):
    C = sums[0].shape[1]
    nw = len(sizes)
    out_shape = [_S((3 * (n // 2), C), sums[0].dtype) for n in sizes]

    def exchange(refs, s_sem, r_sem):
        sm, got = refs[:nw], refs[nw:2 * nw]
        x, y, c, chips = _place()
        for w in range(nw):
            hw = sizes[w] // 2
            for j, (px, py) in enumerate(chips):
                _remote(sm[w].at[pl.ds((2 * px + py) * hw, hw)], got[w].at[pl.ds(j * hw, hw)],
                        s_sem.at[w], r_sem.at[w], (px, py, c)).start()
        for w in range(nw):
            _drain(got[w], s_sem.at[w], r_sem.at[w], (x, y, c), send=True, recv=True)

    if seq_id is not None:
        gots = [jax.empty_ref(s, memory_space=pltpu.MemorySpace.HBM) for s in out_shape]
        _on_sequencer(exchange, [_hbm_ref(s) for s in sums] + gots, _dma_sems(nw),
                      lambda x, y, c, chips: [(px, py, c) for px, py in chips], name, seq_id)
        return [r[...] for r in gots]

    def body(*refs):
        exchange(refs[:2 * nw], *refs[2 * nw:])

    return _pcall(body, name=name, in_specs=[ANY] * nw, out_specs=[ANY] * nw, out_shape=out_shape,
                  scratch_shapes=_dma_sems(nw), compiler_params=_comm_params())(*sums)


def _rs_share_halves(reds, sizes, name, seq_id=None):
    nw = len(sizes)

    def exchange(out, s_sem, r_sem):
        x, y, c, _ = _place()
        for w in range(nw):
            hw = sizes[w] // 2
            rows = out[w].at[pl.ds(c * hw, hw)]
            _remote(rows, rows, s_sem.at[w], r_sem.at[w], (x, y, 1 - c)).start()
        for w in range(nw):
            _drain(out[w].at[pl.ds(0, sizes[w] // 2)], s_sem.at[w], r_sem.at[w], (x, y, c), send=True, recv=True)

    if seq_id is not None:
        refs = [_hbm_ref(r) for r in reds]
        _on_sequencer(exchange, refs, _dma_sems(nw), _sibling_only, name, seq_id)
        return [r[...] for r in refs]

    def body(*refs):
        exchange(refs[nw:2 * nw], *refs[2 * nw:])

    return _pcall(body, name=name, in_specs=[ANY] * nw, out_specs=[ANY] * nw,
                  out_shape=[_S(r.shape, r.dtype) for r in reds],
                  input_output_aliases={w: w for w in range(nw)},
                  scratch_shapes=_dma_sems(nw), compiler_params=_comm_params())(*reds)


def _allreduce_small(part, name):
    R, C = part.shape

    def body(p_ref, o_ref, buf, s_sem, r_sem):
        x, y, c, _ = _place()
        my_id = 4 * x + 2 * y + c
        buf[my_id] = p_ref[...]
        cps = []
        for k in range(1, N_DEV):
            fx, fy, fc = (k >> 2) & 1, (k >> 1) & 1, k & 1
            peer = (x ^ fx, y ^ fy, c ^ fc)
            cps.append(_remote(p_ref, buf.at[my_id], s_sem.at[k - 1], r_sem.at[k - 1], peer))
        for cp in cps:
            cp.start()
        for cp in cps:
            cp.wait()
        acc = buf[0]
        for d in range(1, N_DEV):
            acc = acc + buf[d]
        o_ref[...] = acc

    vm = pl.BlockSpec(memory_space=pltpu.VMEM)
    return _pcall(body, name=name, in_specs=[vm], out_specs=vm, out_shape=_S((R, C), F32),
                  scratch_shapes=[pltpu.VMEM((N_DEV, R, C), F32)] + _dma_sems(N_DEV - 1),
                  compiler_params=pltpu.CompilerParams(has_side_effects=True, vmem_limit_bytes=VMEM_LIMIT))(part)


SHARD_STEPS = 4


def _own_shard_buffers(ws, l, group, place, name):
    nw = len(group)

    def body(p_ref, *refs):
        for (_, tr), i_ref, o_ref in zip(group, refs[:nw], refs[nw:]):
            v = i_ref[...]
            o_ref[...] = (v.T if tr else v).astype(o_ref.dtype)

    in_specs, out_specs, out_shape, sizes = [], [], [], []
    for wname, tr in group:
        _, K, n = ws[wname].shape
        in_specs.append(pl.BlockSpec((None, K // SHARD_STEPS, n), lambda i, p: (l, i, 0)))
        if tr:
            out_specs.append(pl.BlockSpec((n, K // SHARD_STEPS), lambda i, p: (p[1], i)))
            out_shape.append(_S((N_CHIPS * n, K), MXU))
            sizes.append(n)
        else:
            out_specs.append(pl.BlockSpec((K // SHARD_STEPS, n), lambda i, p: (p[1] * SHARD_STEPS + i, 0)))
            out_shape.append(_S((N_CHIPS * K, n), MXU))
            sizes.append(K)
    gs = pltpu.PrefetchScalarGridSpec(num_scalar_prefetch=1, grid=(SHARD_STEPS,), in_specs=in_specs, out_specs=out_specs)
    bufs = _pcall(body, name=name, grid_spec=gs, out_shape=out_shape,
                  compiler_params=_cp("parallel"))(place, *[ws[wname] for wname, _ in group])
    return list(bufs), sizes


def _after(xs, ys):
    return lax.optimization_barrier((xs, ys))[0]


def _small_rows(v):
    flat = v.reshape(-1)
    pad = (-flat.shape[0]) % 1024
    return jnp.pad(flat, (0, pad)).reshape(-1, 1024)


def _pack_small(vals):
    rows = [_small_rows(vals[n]) for n in SMALL]
    cat = jnp.concatenate(rows, axis=0)
    pad = (-cat.shape[0]) % 8
    return jnp.pad(cat, ((0, pad), (0, 0)))


def _unpack_small(packed, like):
    out, r = {}, 0
    for n in SMALL:
        size = like[n].size
        nr = -(-size // 1024)
        out[n] = packed[r:r + nr].reshape(-1)[:size].reshape(like[n].shape)
        r += nr
    return out


def _rope_tables(positions):
    inv_freq = ROPE_THETA ** (-jnp.arange(0, HEAD_DIM, 2, dtype=F32) / HEAD_DIM)
    ang = positions.astype(F32)[:, None] * inv_freq
    cos, sin = jnp.cos(ang), jnp.sin(ang)
    return jnp.concatenate([cos, cos, cos, cos], axis=1), jnp.concatenate([-sin, sin, -sin, sin], axis=1)


def _layer_fwd(l, x, memv, W, P, cos2, sin2):
    t = f"l{l}"
    sv = {"x0": x}
    sv["h1"] = _rms_fwd(x, P["ffn1_norm"], t + "_ffn1_norm")
    sv["a1"], sv["b1"], sv["s1"] = _ffn_up(sv["h1"], W["ffn1_w_gate"], W["ffn1_w_up"], t + "_ffn1_up")
    sv["x1"] = _mm([(sv["s1"], W["ffn1_w_down"])], nt=False, out_dtype=F32, res=x, res_scale=FFN_RES, name=t + "_ffn1_down")

    sv["h2"] = _rms_fwd(sv["x1"], P["mix_norm"], t + "_mix_norm")
    sv["q"], sv["k"], sv["v"], sv["u"] = _in_proj(sv["h2"], W["w_in"], cos2, sin2, t + "_in_proj")
    sv["oa"] = _swa_fwd(sv["q"], sv["k"], sv["v"], P["attn_sinks"], t + "_swa")
    sv["ob"], sv["mg"] = _pool_fwd(sv["u"], sv["oa"], P["pool_w"], P["pool_scale"], P["attn_out_norm"],
                                   P["pool_out_norm"], t + "_pool")
    sv["x2"] = _mm([(sv["mg"], W["w_out"])], nt=False, out_dtype=F32, res=sv["x1"], name=t + "_out_proj")

    sv["h3"] = _rms_fwd(sv["x2"], P["xattn_norm"], t + "_xattn_norm")
    sv["memn"] = _rms_fwd(memv, P["mem_norm"], t + "_mem_norm")
    sv["q3"] = _mm([(sv["h3"], W["xattn_wq"])], nt=False, out_dtype=MXU, name=t + "_xq")
    sv["kv"] = _mm([(sv["memn"], W["xattn_wkv"])], nt=True, out_dtype=MXU, name=t + "_xkv")
    sv["o3"] = _xattn_fwd(sv["q3"], sv["kv"], t + "_xattn")
    sv["x3"] = _mm([(sv["o3"], W["xattn_wo"])], nt=False, out_dtype=F32, res=sv["x2"], name=t + "_xo")

    sv["h4"] = _rms_fwd(sv["x3"], P["ffn2_norm"], t + "_ffn2_norm")
    sv["a2"], sv["b2"], sv["s2"] = _ffn_up(sv["h4"], W["ffn2_w_gate"], W["ffn2_w_up"], t + "_ffn2_up")
    x4 = _mm([(sv["s2"], W["ffn2_w_down"])], nt=False, out_dtype=F32, res=sv["x3"], res_scale=FFN_RES, name=t + "_ffn2_down")
    return x4, sv


def _ffn_bwd(t, dx, x_in, g, h, a, b, s, wgT, wuT, wd):
    d_wd = _mm_tn(s, dx, t + "_dwd", r_scale=FFN_RES)
    da, db = _ffn_mid_bwd(dx, wd, a, b, t + "_mid")
    d_wg = _mm_tn(da, h, t + "_dwg")
    d_wu = _mm_tn(db, h, t + "_dwu")
    dh = _mm([(da, wgT), (db, wuT)], nt=False, out_dtype=F32, name=t + "_dh", tm=TOKEN_TILE // 2)
    dx_in, dg = _rms_bwd(x_in, g, dh, t + "_norm_bwd", dres=dx)
    return dx_in, dg, d_wg, d_wu, d_wd


def _layer_bwd(l, dx, sv, memv, W, P, cos2, sin2):
    t = f"l{l}b"
    GW, GP = {}, {}
    dx, GP["ffn2_norm"], GW["ffn2_w_gate"], GW["ffn2_w_up"], GW["ffn2_w_down"] = _ffn_bwd(
        t + "_ffn2", dx, sv["x3"], P["ffn2_norm"], sv["h4"], sv["a2"], sv["b2"], sv["s2"],
        W["ffn2_w_gate"], W["ffn2_w_up"], W["ffn2_w_down"])

    GW["xattn_wo"] = _mm_tn(sv["o3"], dx, t + "_dwo")
    do3 = _mm([(dx, W["xattn_wo"])], nt=True, out_dtype=MXU, name=t + "_do3")
    dq3, dkv = _xattn_bwd(sv["q3"], sv["kv"], do3, t + "_xattn")
    GW["xattn_wq"] = _mm_tn(sv["h3"], dq3, t + "_dwq")
    dh3 = _mm([(dq3, W["xattn_wq"])], nt=True, out_dtype=F32, name=t + "_dh3")
    GW["xattn_wkv"] = _mm_tn(dkv, sv["memn"], t + "_dwkv")
    dmemn = _mm([(dkv, W["xattn_wkv"])], nt=False, out_dtype=F32, name=t + "_dmemn")
    _, GP["mem_norm"] = _rms_bwd(memv, P["mem_norm"], dmemn, t + "_mem_norm_bwd")
    dx, GP["xattn_norm"] = _rms_bwd(sv["x2"], P["xattn_norm"], dh3, t + "_xattn_norm_bwd", dres=dx)

    GW["w_out"] = _mm_tn(sv["mg"], dx, t + "_dwout")
    dmg = _mm([(dx, W["w_out"])], nt=True, out_dtype=F32, name=t + "_dmg")
    doa, GP["attn_out_norm"] = _rms_bwd(sv["oa"], P["attn_out_norm"], dmg, t + "_oa_norm_bwd", col=0)
    dob, GP["pool_out_norm"] = _rms_bwd(sv["ob"], P["pool_out_norm"], dmg, t + "_ob_norm_bwd", col=1)
    du, GP["pool_w"], GP["pool_scale"] = _pool_bwd(sv["u"], dob, P["pool_w"], P["pool_scale"], t + "_pool")
    dq, dko, dkp, dvo, dvp, dsk = _swa_bwd(sv["q"], sv["k"], sv["v"], doa, P["attn_sinks"], t + "_swa")
    GP["attn_sinks"] = dsk[:, 0]
    dpj = _dproj(dq, dko, dkp, dvo, dvp, du, cos2, sin2, t + "_dproj")
    GW["w_in"] = _mm_tn(dpj, sv["h2"], t + "_dwin")
    dh2 = _mm([(dpj, W["w_in"])], nt=False, out_dtype=F32, name=t + "_dh2")
    dx, GP["mix_norm"] = _rms_bwd(sv["x1"], P["mix_norm"], dh2, t + "_mix_norm_bwd", dres=dx)

    dx, GP["ffn1_norm"], GW["ffn1_w_gate"], GW["ffn1_w_up"], GW["ffn1_w_down"] = _ffn_bwd(
        t + "_ffn1", dx, sv["x0"], P["ffn1_norm"], sv["h1"], sv["a1"], sv["b1"], sv["s1"],
        W["ffn1_w_gate"], W["ffn1_w_up"], W["ffn1_w_down"])
    return dx, GW, GP


def _reduce_group(t, group, GW, sizes, place, seq_ids):
    grads = [GW[name] for name, _ in group]
    gots = _rs_pair_exchange(grads, sizes, t + "_pair", seq_id=next(seq_ids))
    sums = _pair_sum(grads, gots, sizes, place, t + "_pair_sum")
    recvs = _rs_chip_exchange(sums, sizes, t + "_chips", seq_id=next(seq_ids))
    reds = _final_sum(grads, gots, recvs, sizes, place, t + "_final_sum")
    return _rs_share_halves(reds, sizes, t + "_share", seq_id=next(seq_ids))


def _adamw_layer(l, w3, g, m3, v3, transposed, prev, name):
    _, K, n = w3.shape
    if transposed:
        tr = K // SHARD_STEPS
        g_spec = pl.BlockSpec((n, tr), lambda i: (0, i))
    else:
        tr = _rows_tile(K)
        g_spec = pl.BlockSpec((tr, n), lambda i: (i, 0))
    c1 = 1.0 / (1.0 - ADAM_B1 ** ADAM_STEP)
    c2 = 1.0 / (1.0 - ADAM_B2 ** ADAM_STEP)

    def body(w_ref, g_ref, m_ref, v_ref, *rest):
        go_ref, d_ref, nm_ref, nv_ref = rest[-4:]
        gv = g_ref[...].T if transposed else g_ref[...]
        nm = ADAM_B1 * m_ref[...] + (1.0 - ADAM_B1) * gv
        nv = ADAM_B2 * v_ref[...] + (1.0 - ADAM_B2) * (gv * gv)
        go_ref[...] = gv
        d_ref[...] = -ADAM_LR * ((nm * c1) / (jnp.sqrt(nv * c2) + ADAM_EPS) + ADAM_WD * w_ref[...])
        nm_ref[...] = nm
        nv_ref[...] = nv

    slab = pl.BlockSpec((None, tr, n), lambda i: (l, i, 0))
    in_specs, args, aliases = [slab, g_spec, slab, slab], [w3, g, m3, v3], {}
    if prev is not None:
        in_specs += [ANY] * 4
        args += list(prev)
        aliases = {4 + j: j for j in range(4)}
    return _pcall(body, name=name, grid=(K // tr,), in_specs=in_specs, out_specs=[slab] * 4,
                  out_shape=[_S(w3.shape, F32)] * 4, input_output_aliases=aliases,
                  compiler_params=_cp("parallel"))(*args)


def kernel(x, mem, positions, ffn1_norm, ffn1_w_gate, ffn1_w_up, ffn1_w_down, mix_norm, w_in, attn_sinks, pool_w, pool_scale, attn_out_norm, pool_out_norm, w_out, xattn_norm, mem_norm, xattn_wq, xattn_wkv, xattn_wo, ffn2_norm, ffn2_w_gate, ffn2_w_up, ffn2_w_down, final_norm, loss_target, m_ffn1_norm, m_ffn1_w_gate, m_ffn1_w_up, m_ffn1_w_down, m_mix_norm, m_w_in, m_attn_sinks, m_pool_w, m_pool_scale, m_attn_out_norm, m_pool_out_norm, m_w_out, m_xattn_norm, m_mem_norm, m_xattn_wq, m_xattn_wkv, m_xattn_wo, m_ffn2_norm, m_ffn2_w_gate, m_ffn2_w_up, m_ffn2_w_down, m_final_norm, v_ffn1_norm, v_ffn1_w_gate, v_ffn1_w_up, v_ffn1_w_down, v_mix_norm, v_w_in, v_attn_sinks, v_pool_w, v_pool_scale, v_attn_out_norm, v_pool_out_norm, v_w_out, v_xattn_norm, v_mem_norm, v_xattn_wq, v_xattn_wkv, v_xattn_wo, v_ffn2_norm, v_ffn2_w_gate, v_ffn2_w_up, v_ffn2_w_down, v_final_norm):
    ws = dict(ffn1_norm=ffn1_norm, ffn1_w_gate=ffn1_w_gate, ffn1_w_up=ffn1_w_up, ffn1_w_down=ffn1_w_down,
              mix_norm=mix_norm, w_in=w_in, attn_sinks=attn_sinks, pool_w=pool_w, pool_scale=pool_scale,
              attn_out_norm=attn_out_norm, pool_out_norm=pool_out_norm, w_out=w_out, xattn_norm=xattn_norm,
              mem_norm=mem_norm, xattn_wq=xattn_wq, xattn_wkv=xattn_wkv, xattn_wo=xattn_wo, ffn2_norm=ffn2_norm,
              ffn2_w_gate=ffn2_w_gate, ffn2_w_up=ffn2_w_up, ffn2_w_down=ffn2_w_down, final_norm=final_norm)
    ms = dict(ffn1_norm=m_ffn1_norm, ffn1_w_gate=m_ffn1_w_gate, ffn1_w_up=m_ffn1_w_up, ffn1_w_down=m_ffn1_w_down,
              mix_norm=m_mix_norm, w_in=m_w_in, attn_sinks=m_attn_sinks, pool_w=m_pool_w, pool_scale=m_pool_scale,
              attn_out_norm=m_attn_out_norm, pool_out_norm=m_pool_out_norm, w_out=m_w_out, xattn_norm=m_xattn_norm,
              mem_norm=m_mem_norm, xattn_wq=m_xattn_wq, xattn_wkv=m_xattn_wkv, xattn_wo=m_xattn_wo,
              ffn2_norm=m_ffn2_norm, ffn2_w_gate=m_ffn2_w_gate, ffn2_w_up=m_ffn2_w_up, ffn2_w_down=m_ffn2_w_down,
              final_norm=m_final_norm)
    vs = dict(ffn1_norm=v_ffn1_norm, ffn1_w_gate=v_ffn1_w_gate, ffn1_w_up=v_ffn1_w_up, ffn1_w_down=v_ffn1_w_down,
              mix_norm=v_mix_norm, w_in=v_w_in, attn_sinks=v_attn_sinks, pool_w=v_pool_w, pool_scale=v_pool_scale,
              attn_out_norm=v_attn_out_norm, pool_out_norm=v_pool_out_norm, w_out=v_w_out, xattn_norm=v_xattn_norm,
              mem_norm=v_mem_norm, xattn_wq=v_xattn_wq, xattn_wkv=v_xattn_wkv, xattn_wo=v_xattn_wo,
              ffn2_norm=v_ffn2_norm, ffn2_w_gate=v_ffn2_w_gate, ffn2_w_up=v_ffn2_w_up, ffn2_w_down=v_ffn2_w_down,
              final_norm=v_final_norm)
    depth = ffn1_norm.shape[0]
    T, D = x.shape[1], x.shape[2]
    xv = x.reshape(T, D)
    memv = mem.reshape(mem.shape[1], D)
    tgt = loss_target.reshape(T, D)
    cos2, sin2 = _rope_tables(positions.reshape(T))
    in_kernel = {name: tr and ws[name].shape[2] % LANES == 0 for name, tr in BIG}
    swapped = [name for name, tr in BIG if tr and not in_kernel[name]]
    rows = lambda d: {name: (jnp.swapaxes(d[name], 1, 2) if name in swapped else d[name]) for name, _ in BIG}
    wr, mr, vr = rows(ws), rows(ms), rows(vs)
    groups = [[(name, in_kernel[name]) for name, _ in g] for g in GROUPS]

    q_me = 2 * lax.axis_index("x") + lax.axis_index("y")
    place = jnp.stack([lax.axis_index("c"), q_me]).astype(jnp.int32)
    seq_ids = iter(range(1, 1 + 8 * depth))
    Ws, sizes, first = [dict() for _ in range(depth)], {}, None
    for l in range(depth):
        for gi, group in enumerate(groups):
            t = f"l{l}g{gi}"
            bufs, sizes[gi] = _own_shard_buffers(wr, l, group, place, t + "_shard")
            if first is None:
                full = first = _allgather_weights(bufs, sizes[gi], t + "_allgather")
            else:
                full = _allgather_weights(_after(bufs, first), sizes[gi], t + "_allgather", seq_id=next(seq_ids))
            Ws[l].update({name: f for (name, _), f in zip(group, full)})
    Ps = [{n: (ws[n][l].reshape(1, -1) if n != "pool_w" else ws[n][l]) for n in SMALL if n != "final_norm"}
          for l in range(depth)]

    saved = []
    h = xv
    for l in range(depth):
        h, sv = _layer_fwd(l, h, memv, Ws[l], Ps[l], cos2, sin2)
        saved.append(sv)
    loss_row, dx, d_final = _loss_head(h, tgt, final_norm.reshape(1, D), "loss_head")
    GWs, GPs = [None] * depth, [None] * depth
    for l in reversed(range(depth)):
        dx, GWs[l], GPs[l] = _layer_bwd(l, dx, saved[l], memv, Ws[l], Ps[l], cos2, sin2)

    stacked = {}
    for l in reversed(range(depth)):
        for gi, group in reversed(list(enumerate(groups))):
            reds = _reduce_group(f"l{l}g{gi}r", group, GWs[l], sizes[gi], place, seq_ids)
            for (name, tr), red in zip(group, reds):
                stacked[name] = _adamw_layer(l, wr[name], red, mr[name], vr[name], tr, stacked.get(name),
                                             f"l{l}_adamw_{name}")
    for name in swapped:
        stacked[name] = [jnp.swapaxes(a, 1, 2) for a in stacked[name]]
    small_part = {n: jnp.stack([GPs[l][n].reshape(ws[n].shape[1:]) for l in range(depth)]) for n in SMALL if n != "final_norm"}
    small_part["final_norm"] = d_final.reshape(D)
    small_g = _unpack_small(_allreduce_small(_pack_small(small_part), "small_allreduce"), ws)
    loss = lax.psum(loss_row[0, 0], ("x", "y", "c"))

    grads, deltas, new_m, new_v = {}, {}, {}, {}
    for name, _ in BIG:
        grads[name], deltas[name], new_m[name], new_v[name] = stacked[name]
    d, nm, nv = _adamw(_pack_small(ws), _pack_small(small_g), _pack_small(ms), _pack_small(vs), "adamw_small")
    grads.update(small_g)
    deltas.update(_unpack_small(d, ws))
    new_m.update(_unpack_small(nm, ws))
    new_v.update(_unpack_small(nv, ws))

    grad_x = dx.reshape(x.shape)
    return (loss, grad_x, *[grads[n] for n in WEIGHTS], *[deltas[n] for n in WEIGHTS],
            *[new_m[n] for n in WEIGHTS], *[new_v[n] for n in WEIGHTS])
```

```python
import functools

import jax
import jax.numpy as jnp
from jax import lax
from jax.experimental import pallas as pl
from jax.experimental.pallas import tpu as pltpu
from jax.experimental.pallas import tpu_sc as plsc

F32 = jnp.float32
MXU = jnp.bfloat16

EPS = 1e-6
HEAD_DIM = 64
N_Q_HEADS = 8
N_KV_HEADS = 2
Q_PER_KV = N_Q_HEADS // N_KV_HEADS
ATTN_WIDTH = N_Q_HEADS * HEAD_DIM
KV_WIDTH = N_KV_HEADS * HEAD_DIM
BLOCK = 128
ROPE_THETA = 10000.0
POOL_WINDOWS = (2, 4, 8, 16)
POOL_GROUP = 128
POOL_WIDTH = len(POOL_WINDOWS) * POOL_GROUP
POOL_HALO = 16
X_HEADS = 4
X_HEAD_DIM = 256
FFN_RES = 0.5
NEG = -1e30
ADAM_LR = 0.001
ADAM_B1 = 0.9
ADAM_B2 = 0.999
ADAM_EPS = 1e-08
ADAM_WD = 0.01
ADAM_STEP = 10

N_CHIPS = 4
N_DEV = 8
V7X_VMEM_BYTES = 64 * 1024 * 1024
VMEM_LIMIT = V7X_VMEM_BYTES - 8 * 1024 * 1024
LANES = 128
TOKEN_TILE = 512
MESH = pl.DeviceIdType.MESH

BIG = (("ffn1_w_gate", True), ("ffn1_w_up", True), ("ffn1_w_down", False), ("w_in", True), ("w_out", False),
       ("xattn_wq", False), ("xattn_wkv", True), ("xattn_wo", False),
       ("ffn2_w_gate", True), ("ffn2_w_up", True), ("ffn2_w_down", False))
GROUPS = (BIG[:3], BIG[3:])
SMALL = ("ffn1_norm", "mix_norm", "attn_sinks", "pool_w", "pool_scale", "attn_out_norm", "pool_out_norm",
         "xattn_norm", "mem_norm", "ffn2_norm", "final_norm")
WEIGHTS = ("ffn1_norm", "ffn1_w_gate", "ffn1_w_up", "ffn1_w_down", "mix_norm", "w_in", "attn_sinks", "pool_w",
           "pool_scale", "attn_out_norm", "pool_out_norm", "w_out", "xattn_norm", "mem_norm", "xattn_wq",
           "xattn_wkv", "xattn_wo", "ffn2_norm", "ffn2_w_gate", "ffn2_w_up", "ffn2_w_down", "final_norm")


def _S(shape, dtype):
    return jax.ShapeDtypeStruct(tuple(shape), dtype)


def _pcall(body, **kw):
    return pl.pallas_call(body, **kw)


def _cp(*sem):
    return pltpu.CompilerParams(dimension_semantics=tuple(sem), vmem_limit_bytes=VMEM_LIMIT)


def _nt(a, b):
    return lax.dot_general(a, b, (((1,), (1,)), ((), ())), preferred_element_type=F32)


def _nn(a, b):
    return lax.dot_general(a, b, (((1,), (0,)), ((), ())), preferred_element_type=F32)


def _tn(a, b):
    return lax.dot_general(a, b, (((0,), (0,)), ((), ())), preferred_element_type=F32)


def _tile(n, want):
    t = min(n, want)
    assert n % t == 0, (n, want)
    return t


def _resident(shape):
    nd = len(shape)
    return pl.BlockSpec(tuple(shape), lambda *_: (0,) * nd)


def _rms_fwd(x, g, name):
    T, C = x.shape
    tm = _tile(T, TOKEN_TILE)

    def body(x_ref, g_ref, o_ref):
        xv = x_ref[...]
        r = lax.rsqrt(jnp.mean(xv * xv, axis=-1, keepdims=True) + EPS)
        o_ref[...] = (xv * r * g_ref[...]).astype(o_ref.dtype)

    return _pcall(body, name=name, grid=(T // tm,),
                  in_specs=[pl.BlockSpec((tm, C), lambda i: (i, 0)), _resident((1, C))],
                  out_specs=pl.BlockSpec((tm, C), lambda i: (i, 0)),
                  out_shape=_S((T, C), MXU), compiler_params=_cp("parallel"))(x, g)


def _rms_bwd(x, g, dh, name, dres=None, col=0):
    T, C = x.shape
    tm = _tile(T, TOKEN_TILE)

    def body(*refs):
        if dres is None:
            x_ref, g_ref, dh_ref, dx_ref, dg_ref = refs
        else:
            x_ref, g_ref, dh_ref, dres_ref, dx_ref, dg_ref = refs
        xv = x_ref[...]
        r = lax.rsqrt(jnp.mean(xv * xv, axis=-1, keepdims=True) + EPS)
        xh = xv * r
        dhv = dh_ref[...].astype(F32)
        dxn = dhv * g_ref[...]
        dx = r * (dxn - xh * jnp.mean(dxn * xh, axis=-1, keepdims=True))
        if dres is not None:
            dx = dx + dres_ref[...]
        dx_ref[...] = dx

        @pl.when(pl.program_id(0) == 0)
        def _():
            dg_ref[...] = jnp.zeros_like(dg_ref)

        dg_ref[...] += jnp.sum(dhv * xh, axis=0, keepdims=True)

    tok = pl.BlockSpec((tm, C), lambda i: (i, 0))
    in_specs = [tok, _resident((1, C)), pl.BlockSpec((tm, C), lambda i: (i, col))]
    args = [x, g, dh]
    if dres is not None:
        in_specs.append(tok)
        args.append(dres)
    return _pcall(body, name=name, grid=(T // tm,), in_specs=in_specs,
                  out_specs=[tok, _resident((1, C))],
                  out_shape=[_S((T, C), F32), _S((1, C), F32)], compiler_params=_cp("arbitrary"))(*args)


def _mm(pairs, *, nt, out_dtype, name, res=None, res_scale=1.0, a_scale=1.0, tm=TOKEN_TILE):
    M = pairs[0][0].shape[0]
    N = pairs[0][1].shape[0] if nt else pairs[0][1].shape[1]
    tm = _tile(M, tm)
    n = len(pairs)

    def body(*refs):
        a_refs, w_refs = refs[:n], refs[n:2 * n]
        o_ref = refs[-1]
        acc = None
        for a_ref, w_ref in zip(a_refs, w_refs):
            a = a_ref[...]
            if a_scale != 1.0:
                a = a * a_scale
            a = a.astype(MXU)
            p = _nt(a, w_ref[...]) if nt else _nn(a, w_ref[...])
            acc = p if acc is None else acc + p
        if res is not None:
            acc = refs[2 * n][...] + res_scale * acc
        o_ref[...] = acc.astype(o_ref.dtype)

    in_specs = [pl.BlockSpec((tm, a.shape[1]), lambda i: (i, 0)) for a, _ in pairs]
    in_specs += [_resident(w.shape) for _, w in pairs]
    args = [a for a, _ in pairs] + [w for _, w in pairs]
    if res is not None:
        in_specs.append(pl.BlockSpec((tm, N), lambda i: (i, 0)))
        args.append(res)
    return _pcall(body, name=name, grid=(M // tm,), in_specs=in_specs,
                  out_specs=pl.BlockSpec((tm, N), lambda i: (i, 0)),
                  out_shape=_S((M, N), out_dtype), compiler_params=_cp("parallel"))(*args)


def _mm_tn(l, r, name, *, l_scale=1.0, r_scale=1.0, tr=1408, tt=TOKEN_TILE):
    T, R = l.shape
    C = r.shape[1]
    tt = _tile(T, tt)
    tr = tr if R % tr == 0 else (1024 if R % 1024 == 0 and R > 1280 else R)

    def body(l_ref, r_ref, o_ref):
        lv, rv = l_ref[...], r_ref[...]
        if l_scale != 1.0:
            lv = lv * l_scale
        if r_scale != 1.0:
            rv = rv * r_scale
        lv, rv = lv.astype(MXU), rv.astype(MXU)

        @pl.when(pl.program_id(1) == 0)
        def _():
            o_ref[...] = jnp.zeros_like(o_ref)

        o_ref[...] += _tn(lv, rv)

    return _pcall(body, name=name, grid=(R // tr, T // tt),
                  in_specs=[pl.BlockSpec((tt, tr), lambda i, t: (t, i)), pl.BlockSpec((tt, C), lambda i, t: (t, 0))],
                  out_specs=pl.BlockSpec((tr, C), lambda i, t: (i, 0)),
                  out_shape=_S((R, C), F32), compiler_params=_cp("parallel", "arbitrary"))(l, r)


FFN_COL_TILE = 1408


def _sigmoid(a):
    return 0.5 * (jnp.tanh(0.5 * a) + 1.0)


def _ffn_up(h, wgT, wuT, name):
    T, D = h.shape
    Fd = wgT.shape[0]
    tm, tn = _tile(T, TOKEN_TILE), _tile(Fd, FFN_COL_TILE)

    def body(h_ref, wg_ref, wu_ref, a_ref, b_ref, s_ref):
        hv = h_ref[...]
        a = _nt(hv, wg_ref[...])
        b = _nt(hv, wu_ref[...])
        s = a * _sigmoid(a) * b
        a_ref[...] = a.astype(a_ref.dtype)
        b_ref[...] = b.astype(b_ref.dtype)
        s_ref[...] = s.astype(s_ref.dtype)

    wspec = pl.BlockSpec((tn, D), lambda j, i: (j, 0))
    ospec = pl.BlockSpec((tm, tn), lambda j, i: (i, j))
    return _pcall(body, name=name, grid=(Fd // tn, T // tm),
                  in_specs=[pl.BlockSpec((tm, D), lambda j, i: (i, 0)), wspec, wspec],
                  out_specs=[ospec, ospec, ospec], out_shape=[_S((T, Fd), MXU)] * 3,
                  compiler_params=_cp("parallel", "parallel"))(h, wgT, wuT)


def _ffn_mid_bwd(dx, wd, a, b, name):
    T, D = dx.shape
    Fd = wd.shape[0]
    tm, tn = _tile(T, TOKEN_TILE), _tile(Fd, FFN_COL_TILE)

    def body(dx_ref, wd_ref, a_ref, b_ref, da_ref, db_ref):
        dy = (dx_ref[...] * FFN_RES).astype(MXU)
        ds = _nt(dy, wd_ref[...])
        av, bv = a_ref[...].astype(F32), b_ref[...].astype(F32)
        sg = _sigmoid(av)
        da_ref[...] = (ds * bv * (sg * (1.0 + av * (1.0 - sg)))).astype(da_ref.dtype)
        db_ref[...] = (ds * (av * sg)).astype(db_ref.dtype)

    aspec = pl.BlockSpec((tm, tn), lambda j, i: (i, j))
    return _pcall(body, name=name, grid=(Fd // tn, T // tm),
                  in_specs=[pl.BlockSpec((tm, D), lambda j, i: (i, 0)), pl.BlockSpec((tn, D), lambda j, i: (j, 0)),
                            aspec, aspec],
                  out_specs=[aspec, aspec], out_shape=[_S((T, Fd), MXU)] * 2,
                  compiler_params=_cp("parallel", "parallel"))(dx, wd, a, b)


def _swap_halves(t):
    w = t.shape[1]
    lane = lax.broadcasted_iota(jnp.int32, t.shape, 1)
    first = (lane % HEAD_DIM) < (HEAD_DIM // 2)
    return jnp.where(first, pltpu.roll(t, w - HEAD_DIM // 2, 1), pltpu.roll(t, HEAD_DIM // 2, 1))


def _rope(t, cos2, sin2):
    reps = t.shape[1] // LANES
    c = jnp.tile(cos2, (1, reps)) if reps > 1 else cos2
    s = jnp.tile(sin2, (1, reps)) if reps > 1 else sin2
    return t * c + _swap_halves(t) * s


def _rope_bwd(dt, cos2, sin2):
    reps = dt.shape[1] // LANES
    c = jnp.tile(cos2, (1, reps)) if reps > 1 else cos2
    s = jnp.tile(sin2, (1, reps)) if reps > 1 else sin2
    return dt * c + _swap_halves(dt * s)


def _in_proj(h, winT, cos2, sin2, name):
    T, D = h.shape
    tm = _tile(T, TOKEN_TILE)
    qe, ke, ve = ATTN_WIDTH, ATTN_WIDTH + KV_WIDTH, ATTN_WIDTH + 2 * KV_WIDTH

    def body(h_ref, w_ref, c_ref, s_ref, q_ref, k_ref, v_ref, u_ref):
        proj = _nt(h_ref[...], w_ref[...])
        cv, sv = c_ref[...], s_ref[...]
        q_ref[...] = _rope(proj[:, :qe], cv, sv).astype(q_ref.dtype)
        k_ref[...] = _rope(proj[:, qe:ke], cv, sv).astype(k_ref.dtype)
        v_ref[...] = proj[:, ke:ve].astype(v_ref.dtype)
        u_ref[...] = proj[:, ve:]

    def tok(w):
        return pl.BlockSpec((tm, w), lambda i: (i, 0))

    return _pcall(body, name=name, grid=(T // tm,),
                  in_specs=[tok(D), _resident(winT.shape), tok(LANES), tok(LANES)],
                  out_specs=[tok(ATTN_WIDTH), tok(KV_WIDTH), tok(KV_WIDTH), tok(POOL_WIDTH)],
                  out_shape=[_S((T, ATTN_WIDTH), MXU), _S((T, KV_WIDTH), MXU), _S((T, KV_WIDTH), MXU),
                             _S((T, POOL_WIDTH), F32)],
                  compiler_params=_cp("parallel"))(h, winT, cos2, sin2)


SWA_TILE_BLOCKS = 4
SM_SCALE = HEAD_DIM ** -0.5


def _swa_bias(first_tile):
    cols = Q_PER_KV * BLOCK
    kj = lax.broadcasted_iota(jnp.int32, (2 * BLOCK, cols), 0)
    qi = lax.broadcasted_iota(jnp.int32, (2 * BLOCK, cols), 1) % BLOCK
    diff = qi + BLOCK - kj
    bias = jnp.where((diff >= 0) & (diff < BLOCK), 0.0, NEG)
    return bias, jnp.where(kj < jnp.where(first_tile, BLOCK, 0), NEG, bias)


def _swa_probs(kh, qs, sink_row, bias):
    s = _nt(kh, qs) + bias
    m = jnp.maximum(jnp.max(s, axis=0, keepdims=True), sink_row)
    e = jnp.exp(s - m)
    es = jnp.exp(sink_row - m)
    inv = 1.0 / (jnp.sum(e, axis=0, keepdims=True) + es)
    return e * inv, es * inv


def _sink_row(sinks_ref, kv):
    return jnp.concatenate([jnp.full((1, BLOCK), sinks_ref[0, kv * Q_PER_KV + g], F32) for g in range(Q_PER_KV)], axis=1)


def _stack_heads(t, kv):
    return jnp.concatenate([t[:, (kv * Q_PER_KV + g) * HEAD_DIM:(kv * Q_PER_KV + g + 1) * HEAD_DIM]
                            for g in range(Q_PER_KV)], axis=0)


def _swa_specs(T):
    tq = _tile(T, SWA_TILE_BLOCKS * BLOCK)
    nbt = tq // BLOCK
    cur = lambda w: pl.BlockSpec((tq, w), lambda i: (i, 0))
    prev = lambda w: pl.BlockSpec((BLOCK, w), lambda i: (jnp.maximum(i * nbt - 1, 0), 0))
    return tq, nbt, cur, prev


def _rows(b):
    return slice(b * BLOCK, (b + 1) * BLOCK)


def _swa_fwd(q, k, v, sinks, name):
    T = q.shape[0]
    tq, nbt, cur, prev = _swa_specs(T)

    def body(sinks_ref, q_ref, k_ref, kp_ref, v_ref, vp_ref, o_ref):
        bias, bias0 = _swa_bias(pl.program_id(0) == 0)
        sink = [_sink_row(sinks_ref, kv) for kv in range(N_KV_HEADS)]
        kx = jnp.concatenate([kp_ref[...], k_ref[...]], axis=0)
        vx = jnp.concatenate([vp_ref[...], v_ref[...]], axis=0)
        for b in range(nbt):
            qv = q_ref[_rows(b), :] * SM_SCALE
            kk, vv = kx[b * BLOCK:(b + 2) * BLOCK], vx[b * BLOCK:(b + 2) * BLOCK]
            for kv in range(N_KV_HEADS):
                hs = slice(kv * HEAD_DIM, (kv + 1) * HEAD_DIM)
                p, _ = _swa_probs(kk[:, hs], _stack_heads(qv, kv), sink[kv], bias0 if b == 0 else bias)
                o_t = _tn(vv[:, hs], p.astype(MXU))
                for g in range(Q_PER_KV):
                    c0 = (kv * Q_PER_KV + g) * HEAD_DIM
                    o_ref[_rows(b), c0:c0 + HEAD_DIM] = o_t[:, _rows(g)].T

    return _pcall(body, name=name, grid=(T // tq,),
                  in_specs=[pl.BlockSpec(memory_space=pltpu.SMEM), cur(ATTN_WIDTH), cur(KV_WIDTH), prev(KV_WIDTH),
                            cur(KV_WIDTH), prev(KV_WIDTH)],
                  out_specs=cur(ATTN_WIDTH), out_shape=_S((T, ATTN_WIDTH), F32),
                  compiler_params=_cp("parallel"))(sinks, q, k, k, v, v)


def _swa_bwd(q, k, v, do, sinks, name):
    T = q.shape[0]
    tq, nbt, cur, prev = _swa_specs(T)
    per_tile = lambda w: pl.BlockSpec((BLOCK, w), lambda i: (i, 0))

    def add(acc, t):
        return t if acc is None else acc + t

    def body(sinks_ref, q_ref, k_ref, kp_ref, v_ref, vp_ref, do_ref,
             dq_ref, dk_ref, dkp_ref, dv_ref, dvp_ref, dsk_ref):
        bias, bias0 = _swa_bias(pl.program_id(0) == 0)
        sink = [_sink_row(sinks_ref, kv) for kv in range(N_KV_HEADS)]
        kx = jnp.concatenate([kp_ref[...], k_ref[...]], axis=0)
        vx = jnp.concatenate([vp_ref[...], v_ref[...]], axis=0)

        @pl.when(pl.program_id(0) == 0)
        def _():
            dsk_ref[...] = jnp.zeros_like(dsk_ref)

        dk_acc, dv_acc = [None] * (nbt + 1), [None] * (nbt + 1)
        dsk_acc = [None] * N_Q_HEADS
        for b in range(nbt):
            qv, dov = q_ref[_rows(b), :] * SM_SCALE, do_ref[_rows(b), :].astype(MXU)
            kk, vv = kx[b * BLOCK:(b + 2) * BLOCK], vx[b * BLOCK:(b + 2) * BLOCK]
            dks, dvs = [], []
            for kv in range(N_KV_HEADS):
                hs = slice(kv * HEAD_DIM, (kv + 1) * HEAD_DIM)
                qs, dos = _stack_heads(qv, kv), _stack_heads(dov, kv)
                p, ps = _swa_probs(kk[:, hs], qs, sink[kv], bias0 if b == 0 else bias)
                dp = _nt(vv[:, hs], dos)
                delta = jnp.sum(p * dp, axis=0, keepdims=True)
                ds = (p * (dp - delta)).astype(MXU)
                dq_t = _tn(kk[:, hs], ds) * SM_SCALE
                dks.append(_nn(ds, qs))
                dvs.append(_nn(p.astype(MXU), dos))
                dsink = -ps * delta
                for g in range(Q_PER_KV):
                    h = kv * Q_PER_KV + g
                    dq_ref[_rows(b), h * HEAD_DIM:(h + 1) * HEAD_DIM] = dq_t[:, _rows(g)].T
                    dsk_acc[h] = add(dsk_acc[h], jnp.sum(dsink[:, _rows(g)], axis=1, keepdims=True))
            dk, dv = jnp.concatenate(dks, axis=1), jnp.concatenate(dvs, axis=1)
            dk_acc[b], dk_acc[b + 1] = add(dk_acc[b], dk[:BLOCK]), add(dk_acc[b + 1], dk[BLOCK:])
            dv_acc[b], dv_acc[b + 1] = add(dv_acc[b], dv[:BLOCK]), add(dv_acc[b + 1], dv[BLOCK:])
        dkp_ref[...], dvp_ref[...] = dk_acc[0], dv_acc[0]
        dk_ref[...] = jnp.concatenate(dk_acc[1:], axis=0)
        dv_ref[...] = jnp.concatenate(dv_acc[1:], axis=0)
        for h in range(N_Q_HEADS):
            dsk_ref[h:h + 1, :] += jnp.broadcast_to(dsk_acc[h], (1, LANES))

    kvs, kvp = _S((T, KV_WIDTH), F32), _S((T // tq * BLOCK, KV_WIDTH), F32)
    return _pcall(body, name=name, grid=(T // tq,),
                  in_specs=[pl.BlockSpec(memory_space=pltpu.SMEM), cur(ATTN_WIDTH), cur(KV_WIDTH), prev(KV_WIDTH),
                            cur(KV_WIDTH), prev(KV_WIDTH), cur(ATTN_WIDTH)],
                  out_specs=[cur(ATTN_WIDTH), cur(KV_WIDTH), per_tile(KV_WIDTH), cur(KV_WIDTH), per_tile(KV_WIDTH),
                             _resident((N_Q_HEADS, LANES))],
                  out_shape=[_S((T, ATTN_WIDTH), F32), kvs, kvp, kvs, kvp, _S((N_Q_HEADS, LANES), F32)],
                  compiler_params=_cp("arbitrary"))(sinks, q, k, k, v, v, do)


def _dproj(dq, dk, dkp, dv, dvp, du, cos2, sin2, name):
    T = dq.shape[0]
    tq, nbt, cur, _ = _swa_specs(T)
    nt = T // tq
    nxt = lambda w: pl.BlockSpec((BLOCK, w), lambda i: (jnp.minimum(i + 1, nt - 1), 0))

    def body(dq_ref, dk_ref, dkp_ref, dv_ref, dvp_ref, du_ref, c_ref, s_ref, o_ref):
        more = (pl.program_id(0) < nt - 1).astype(F32)
        cv, sv = c_ref[...], s_ref[...]

        def whole(t_ref, p_ref):
            t, last = t_ref[...], t_ref[tq - BLOCK:, :] + more * p_ref[...]
            return last if nbt == 1 else jnp.concatenate([t[:tq - BLOCK], last], axis=0)

        o_ref[...] = jnp.concatenate(
            [_rope_bwd(dq_ref[...], cv, sv), _rope_bwd(whole(dk_ref, dkp_ref), cv, sv), whole(dv_ref, dvp_ref),
             du_ref[...]], axis=1).astype(o_ref.dtype)

    width = ATTN_WIDTH + 2 * KV_WIDTH + POOL_WIDTH
    return _pcall(body, name=name, grid=(nt,),
                  in_specs=[cur(ATTN_WIDTH), cur(KV_WIDTH), nxt(KV_WIDTH), cur(KV_WIDTH), nxt(KV_WIDTH),
                            cur(POOL_WIDTH), cur(LANES), cur(LANES)],
                  out_specs=cur(width), out_shape=_S((T, width), MXU),
                  compiler_params=_cp("parallel"))(dq, dk, dkp, dv, dvp, du, cos2, sin2)


def _pool_specs(T):
    tm = _tile(T, TOKEN_TILE)
    hb = tm // POOL_HALO
    nh = T // POOL_HALO
    tok = lambda w: pl.BlockSpec((tm, w), lambda i: (i, 0))
    before = pl.BlockSpec((POOL_HALO, POOL_WIDTH), lambda i: (jnp.maximum(i * hb - 1, 0), 0))
    after = pl.BlockSpec((POOL_HALO, POOL_WIDTH), lambda i: (jnp.minimum((i + 1) * hb, nh - 1), 0))
    return tm, tok, before, after


def _window_counts(i, tm, rows, w):
    t = i * tm + lax.broadcasted_iota(jnp.int32, (rows, 1), 0)
    return jnp.minimum(t + 1, w).astype(F32)


def _pooled(u_ext, i, tm):
    out = []
    for g, w in enumerate(POOL_WINDOWS):
        acc = u_ext[:, g * POOL_GROUP:(g + 1) * POOL_GROUP]
        tok = acc[POOL_HALO:, :]
        sh = 1
        while sh < w:
            acc = acc + pltpu.roll(acc, sh, 0)
            sh *= 2
        out.append(acc[POOL_HALO:, :] / _window_counts(i, tm, tm, w) - tok)
    return out


def _pool_fwd(u, out_a, pool_w, pool_scale, ga, gb, name):
    T = u.shape[0]
    tm, tok, before, _ = _pool_specs(T)

    def body(u_ref, halo_ref, oa_ref, pw_ref, sc_ref, ga_ref, gb_ref, ob_ref, mg_ref):
        i = pl.program_id(0)
        halo = halo_ref[...] * (i > 0).astype(F32)
        pooled = _pooled(jnp.concatenate([halo, u_ref[...]], axis=0), i, tm)
        mixed = [_nn(pooled[g].astype(MXU), pw_ref[g].astype(MXU)) for g in range(len(POOL_WINDOWS))]
        ob = jnp.concatenate(mixed, axis=1) * sc_ref[...]
        ob_ref[...] = ob
        oa = oa_ref[...]
        ra = lax.rsqrt(jnp.mean(oa * oa, axis=-1, keepdims=True) + EPS)
        rb = lax.rsqrt(jnp.mean(ob * ob, axis=-1, keepdims=True) + EPS)
        mg_ref[...] = jnp.concatenate([oa * ra * ga_ref[...], ob * rb * gb_ref[...]], axis=1).astype(mg_ref.dtype)

    vec = _resident((1, POOL_WIDTH))
    return _pcall(body, name=name, grid=(T // tm,),
                  in_specs=[tok(POOL_WIDTH), before, tok(ATTN_WIDTH), _resident(pool_w.shape), vec, vec, vec],
                  out_specs=[tok(POOL_WIDTH), tok(ATTN_WIDTH + POOL_WIDTH)],
                  out_shape=[_S((T, POOL_WIDTH), F32), _S((T, ATTN_WIDTH + POOL_WIDTH), MXU)],
                  compiler_params=_cp("parallel"))(u, u, out_a, pool_w, pool_scale, ga, gb)


def _pool_bwd(u, dob, pool_w, pool_scale, name):
    T = u.shape[0]
    tm, tok, before, after = _pool_specs(T)
    nt = T // tm
    G = len(POOL_WINDOWS)

    def body(u_ref, halo_ref, dob_ref, dnext_ref, pw_ref, sc_ref, du_ref, dpw_ref, dsc_ref):
        i = pl.program_id(0)
        halo = halo_ref[...] * (i > 0).astype(F32)
        pooled = _pooled(jnp.concatenate([halo, u_ref[...]], axis=0), i, tm)
        dnext = dnext_ref[...] * (i < nt - 1).astype(F32)
        dext = jnp.concatenate([dob_ref[...], dnext], axis=0) * sc_ref[...]

        @pl.when(i == 0)
        def _():
            dpw_ref[...] = jnp.zeros_like(dpw_ref)
            dsc_ref[...] = jnp.zeros_like(dsc_ref)

        dus, dscs = [], []
        for g, w in enumerate(POOL_WINDOWS):
            gs = slice(g * POOL_GROUP, (g + 1) * POOL_GROUP)
            pw = pw_ref[g].astype(MXU)
            pg = pooled[g].astype(MXU)
            dmix = dext[:, gs].astype(MXU)
            dscs.append(jnp.sum(dob_ref[:, gs] * _nn(pg, pw), axis=0, keepdims=True))
            dpw_ref[g] += _tn(pg, dmix[:tm, :])
            dpooled = _nt(dmix, pw)
            acc = dpooled / _window_counts(i, tm, tm + POOL_HALO, w)
            sh = 1
            while sh < w:
                acc = acc + pltpu.roll(acc, tm + POOL_HALO - sh, 0)
                sh *= 2
            dus.append(acc[:tm, :] - dpooled[:tm, :])
        du_ref[...] = jnp.concatenate(dus, axis=1)
        dsc_ref[...] += jnp.concatenate(dscs, axis=1)

    vec = _resident((1, POOL_WIDTH))
    return _pcall(body, name=name, grid=(nt,),
                  in_specs=[tok(POOL_WIDTH), before, tok(POOL_WIDTH), after, _resident(pool_w.shape), vec],
                  out_specs=[tok(POOL_WIDTH), _resident(pool_w.shape), vec],
                  out_shape=[_S((T, POOL_WIDTH), F32), _S(pool_w.shape, F32), _S((1, POOL_WIDTH), F32)],
                  compiler_params=_cp("arbitrary"))(u, u, dob, dob, pool_w, pool_scale)


def _xattn_probs(qh, kh):
    s = _nt(qh, kh) * (X_HEAD_DIM ** -0.5)
    e = jnp.exp(s - jnp.max(s, axis=1, keepdims=True))
    return e / jnp.sum(e, axis=1, keepdims=True)


def _xattn_fwd(q, kvm, name):
    T, XW = q.shape
    tm = _tile(T, TOKEN_TILE)

    def body(q_ref, kv_ref, o_ref):
        for h in range(X_HEADS):
            hs = slice(h * X_HEAD_DIM, (h + 1) * X_HEAD_DIM)
            vs = slice(XW + h * X_HEAD_DIM, XW + (h + 1) * X_HEAD_DIM)
            p = _xattn_probs(q_ref[:, hs], kv_ref[:, hs])
            o_ref[:, hs] = _nn(p.astype(MXU), kv_ref[:, vs]).astype(o_ref.dtype)

    return _pcall(body, name=name, grid=(T // tm,),
                  in_specs=[pl.BlockSpec((tm, XW), lambda i: (i, 0)), _resident(kvm.shape)],
                  out_specs=pl.BlockSpec((tm, XW), lambda i: (i, 0)), out_shape=_S((T, XW), MXU),
                  compiler_params=_cp("parallel"))(q, kvm)


def _xattn_bwd(q, kvm, do, name):
    T, XW = q.shape
    tm = _tile(T, TOKEN_TILE)

    def body(q_ref, kv_ref, do_ref, dq_ref, dkv_ref):
        @pl.when(pl.program_id(0) == 0)
        def _():
            dkv_ref[...] = jnp.zeros_like(dkv_ref)

        for h in range(X_HEADS):
            hs = slice(h * X_HEAD_DIM, (h + 1) * X_HEAD_DIM)
            vs = slice(XW + h * X_HEAD_DIM, XW + (h + 1) * X_HEAD_DIM)
            qh, doh = q_ref[:, hs], do_ref[:, hs]
            p = _xattn_probs(qh, kv_ref[:, hs])
            dp = _nt(doh, kv_ref[:, vs])
            ds = (p * (dp - jnp.sum(p * dp, axis=1, keepdims=True)) * (X_HEAD_DIM ** -0.5)).astype(MXU)
            dq_ref[:, hs] = _nn(ds, kv_ref[:, hs]).astype(dq_ref.dtype)
            dkv_ref[:, hs] += _tn(ds, qh)
            dkv_ref[:, vs] += _tn(p.astype(MXU), doh)

    tok = pl.BlockSpec((tm, XW), lambda i: (i, 0))
    return _pcall(body, name=name, grid=(T // tm,),
                  in_specs=[tok, _resident(kvm.shape), tok],
                  out_specs=[tok, _resident(kvm.shape)],
                  out_shape=[_S((T, XW), MXU), _S(kvm.shape, F32)],
                  compiler_params=_cp("arbitrary"))(q, kvm, do)


def _loss_head(x, tgt, g, name):
    T, D = x.shape
    tm = _tile(T, TOKEN_TILE)

    def body(x_ref, t_ref, g_ref, loss_ref, dx_ref, dg_ref):
        xv, gv = x_ref[...], g_ref[...]
        r = lax.rsqrt(jnp.mean(xv * xv, axis=-1, keepdims=True) + EPS)
        xh = xv * r
        e = xh * gv - t_ref[...]
        dy = e * (1.0 / D)
        dxn = dy * gv
        dx_ref[...] = r * (dxn - xh * jnp.mean(dxn * xh, axis=-1, keepdims=True))

        @pl.when(pl.program_id(0) == 0)
        def _():
            loss_ref[...] = jnp.zeros_like(loss_ref)
            dg_ref[...] = jnp.zeros_like(dg_ref)

        part = jnp.sum(jnp.sum(e * e, axis=1, keepdims=True), axis=0, keepdims=True) * (0.5 / D)
        loss_ref[...] += jnp.broadcast_to(part, (1, LANES))
        dg_ref[...] += jnp.sum(dy * xh, axis=0, keepdims=True)

    tok = pl.BlockSpec((tm, D), lambda i: (i, 0))
    return _pcall(body, name=name, grid=(T // tm,),
                  in_specs=[tok, tok, _resident((1, D))],
                  out_specs=[_resident((1, LANES)), tok, _resident((1, D))],
                  out_shape=[_S((1, LANES), F32), _S((T, D), F32), _S((1, D), F32)],
                  compiler_params=_cp("arbitrary"))(x, tgt, g)


def _rows_tile(rows):
    for t in (512, 416, 352, 256, 128, 64, 32, 16, 8):
        if rows % t == 0:
            return t
    return rows


def _pair_sum(grads, gots, sizes, place, name):
    nw = len(sizes)
    C = grads[0].shape[1]

    def body(p_ref, *refs):
        g, got, out = refs[:nw], refs[nw:2 * nw], refs[2 * nw:]
        for w in range(nw):
            out[w][...] = (g[w][...] + got[w][...]).astype(out[w].dtype)

    def blk(w):
        return (sizes[w] // 4, C)

    in_specs = [pl.BlockSpec(blk(w), lambda q, s, p: (4 * q + 2 * p[0] + s, 0)) for w in range(nw)]
    in_specs += [pl.BlockSpec(blk(w), lambda q, s, p: (2 * q + s, 0)) for w in range(nw)]
    out_specs = [pl.BlockSpec(blk(w), lambda q, s, p: (2 * q + s, 0)) for w in range(nw)]
    gs = pltpu.PrefetchScalarGridSpec(num_scalar_prefetch=1, grid=(N_CHIPS, 2), in_specs=in_specs, out_specs=out_specs)
    return _pcall(body, name=name, grid_spec=gs, out_shape=[_S((2 * n, C), MXU) for n in sizes],
                  compiler_params=_cp("parallel", "parallel"))(place, *grads, *gots)


def _final_sum(grads, gots, recvs, sizes, place, name):
    nw = len(sizes)
    C = grads[0].shape[1]

    def body(p_ref, *refs):
        g, got, rv, out = refs[:nw], refs[nw:2 * nw], refs[2 * nw:5 * nw], refs[5 * nw:]
        for w in range(nw):
            acc = g[w][...] + got[w][...]
            for j in range(3):
                acc = acc + rv[3 * w + j][...].astype(F32)
            out[w][...] = acc

    def blk(w):
        return (sizes[w] // 4, C)

    in_specs = [pl.BlockSpec(blk(w), lambda s, p: (4 * p[1] + 2 * p[0] + s, 0)) for w in range(nw)]
    in_specs += [pl.BlockSpec(blk(w), lambda s, p: (2 * p[1] + s, 0)) for w in range(nw)]
    args = list(grads) + list(gots)
    for w in range(nw):
        for j in range(3):
            in_specs.append(pl.BlockSpec(blk(w), lambda s, p, j=j: (2 * j + s, 0)))
            args.append(recvs[w])
    out_specs = [pl.BlockSpec(blk(w), lambda s, p: (2 * p[0] + s, 0)) for w in range(nw)]
    gs = pltpu.PrefetchScalarGridSpec(num_scalar_prefetch=1, grid=(2,), in_specs=in_specs, out_specs=out_specs)
    return _pcall(body, name=name, grid_spec=gs, out_shape=[_S((n, C), F32) for n in sizes],
                  compiler_params=_cp("parallel"))(place, *args)


def _adamw(w, g, m, v, name):
    R, C = w.shape
    tr = _rows_tile(R)
    c1 = 1.0 / (1.0 - ADAM_B1 ** ADAM_STEP)
    c2 = 1.0 / (1.0 - ADAM_B2 ** ADAM_STEP)

    def body(w_ref, g_ref, m_ref, v_ref, d_ref, nm_ref, nv_ref):
        gv = g_ref[...]
        nm = ADAM_B1 * m_ref[...] + (1.0 - ADAM_B1) * gv
        nv = ADAM_B2 * v_ref[...] + (1.0 - ADAM_B2) * (gv * gv)
        d_ref[...] = -ADAM_LR * ((nm * c1) / (jnp.sqrt(nv * c2) + ADAM_EPS) + ADAM_WD * w_ref[...])
        nm_ref[...] = nm
        nv_ref[...] = nv

    spec = pl.BlockSpec((tr, C), lambda i: (i, 0))
    return _pcall(body, name=name, grid=(R // tr,), in_specs=[spec] * 4, out_specs=[spec] * 3,
                  out_shape=[_S((R, C), F32)] * 3, compiler_params=_cp("parallel"))(w, g, m, v)


ANY = pl.BlockSpec(memory_space=pl.ANY)


def _place():
    x, y, c = lax.axis_index("x"), lax.axis_index("y"), lax.axis_index("c")
    chips = [(1 - x, y), (x, 1 - y), (1 - x, 1 - y)]
    return x, y, c, chips


def _remote(src, dst, send_sem, recv_sem, dev):
    return pltpu.make_async_remote_copy(src_ref=src, dst_ref=dst, send_sem=send_sem, recv_sem=recv_sem,
                                        device_id=dev, device_id_type=MESH)


def _drain(like, send_sem, recv_sem, me, *, send=False, recv=False):
    d = _remote(like, like, send_sem, recv_sem, me)
    if send:
        d.wait_send()
    if recv:
        d.wait_recv()


def _dma_sems(n):
    return [pltpu.SemaphoreType.DMA((n,)), pltpu.SemaphoreType.DMA((n,))]


def _comm_params():
    return pltpu.CompilerParams(has_side_effects=True)


def _on_sequencer(exchange, refs, sem_types, peers_of, name, seq_id):
    def launch(*sems):
        x, y, c, chips = _place()
        barrier = pltpu.get_barrier_semaphore()
        peers = peers_of(x, y, c, chips)
        for peer in peers:
            pl.semaphore_signal(barrier, inc=1, device_id=peer, device_id_type=MESH)
        pl.semaphore_wait(barrier, len(peers))
        exchange(refs, *sems)

    pl.kernel(launch, mesh=plsc.ScalarSubcoreMesh(axis_name="seq", num_cores=1), name=name,
              scratch_types=tuple(sem_types), compiler_params=pltpu.CompilerParams(collective_id=seq_id))()


def _hbm_ref(a):
    return jax.new_ref(a, memory_space=pltpu.MemorySpace.HBM)


def _allgather_weights(bufs, sizes, name, seq_id=None):
    nw = len(sizes)

    def exchange(out, s_ici, r_ici, s_fwd, r_fwd):
        x, y, c, chips = _place()
        me, sib = (x, y, c), (x, y, 1 - c)
        q_me = 2 * x + y

        def rows(w, q):
            hw = sizes[w] // 2
            return out[w].at[pl.ds(q * sizes[w] + c * hw, hw)]

        def three(w):
            return out[w].at[pl.ds(0, 3 * (sizes[w] // 2))]

        for w in range(nw):
            for px, py in chips:
                _remote(rows(w, q_me), rows(w, q_me), s_ici.at[w], r_ici.at[w], (px, py, c)).start()
        for w in range(nw):
            _drain(three(w), s_ici.at[w], r_ici.at[w], me, recv=True)
            for px, py in chips:
                got = rows(w, 2 * px + py)
                _remote(got, got, s_fwd.at[w], r_fwd.at[w], sib).start()
        for w in range(nw):
            _drain(three(w), s_fwd.at[w], r_fwd.at[w], me, recv=True)
        for w in range(nw):
            _drain(three(w), s_ici.at[w], r_ici.at[w], me, send=True)
            _drain(three(w), s_fwd.at[w], r_fwd.at[w], me, send=True)

    if seq_id is not None:
        refs = [_hbm_ref(b) for b in bufs]
        _on_sequencer(exchange, refs, _dma_sems(nw) + _dma_sems(nw),
                      lambda x, y, c, chips: [(x, y, 1 - c)] + [(px, py, c) for px, py in chips], name, seq_id)
        return [r[...] for r in refs]

    def body(*refs):
        exchange(refs[nw:2 * nw], *refs[2 * nw:])

    return _pcall(body, name=name, in_specs=[ANY] * nw, out_specs=[ANY] * nw,
                  out_shape=[_S(b.shape, b.dtype) for b in bufs],
                  input_output_aliases={w: w for w in range(nw)},
                  scratch_shapes=_dma_sems(nw) + _dma_sems(nw), compiler_params=_comm_params())(*bufs)


def _sibling_only(x, y, c, chips):
    return [(x, y, 1 - c)]


def _rs_pair_exchange(grads, sizes, name, seq_id=None):
    C = grads[0].shape[1]
    nw = len(sizes)
    out_shape = [_S((2 * n, C), F32) for n in sizes]

    def exchange(refs, s_sem, r_sem):
        g, got = refs[:nw], refs[nw:2 * nw]
        x, y, c, _ = _place()
        me, sib = (x, y, c), (x, y, 1 - c)
        for w in range(nw):
            hw = sizes[w] // 2
            for q in range(N_CHIPS):
                _remote(g[w].at[pl.ds(q * sizes[w] + (1 - c) * hw, hw)], got[w].at[pl.ds(q * hw, hw)],
                        s_sem.at[w], r_sem.at[w], sib).start()
        for w in range(nw):
            _drain(got[w], s_sem.at[w], r_sem.at[w], me, send=True, recv=True)

    if seq_id is not None:
        gots = [jax.empty_ref(s, memory_space=pltpu.MemorySpace.HBM) for s in out_shape]
        _on_sequencer(exchange, [_hbm_ref(g) for g in grads] + gots, _dma_sems(nw), _sibling_only, name, seq_id)
        return [r[...] for r in gots]

    def body(*refs):
        exchange(refs[:2 * nw], *refs[2 * nw:])

    return _pcall(body, name=name, in_specs=[ANY] * nw, out_specs=[ANY] * nw, out_shape=out_shape,
                  scratch_shapes=_dma_sems(nw), compiler_params=_comm_params())(*grads)


def _rs_chip_exchange(sums, sizes, name, seq_id=None):
    C = sums[0].shape[1]
    nw = len(sizes)
    out_shape = [_S((3 * (n // 2), C), sums[0].dtype) for n in sizes]

    def exchange(refs, s_sem, r_sem):
        sm, got = refs[:nw], refs[nw:2 * nw]
        x, y, c, chips = _place()
        for w in range(nw):
            hw = sizes[w] // 2
            for j, (px, py) in enumerate(chips):
                _remote(sm[w].at[pl.ds((2 * px + py) * hw, hw)], got[w].at[pl.ds(j * hw, hw)],
                        s_sem.at[w], r_sem.at[w], (px, py, c)).start()
        for w in range(nw):
            _drain(got[w], s_sem.at[w], r_sem.at[w], (x, y, c), send=True, recv=True)

    if seq_id is not None:
        gots = [jax.empty_ref(s, memory_space=pltpu.MemorySpace.HBM) for s in out_shape]
        _on_sequencer(exchange, [_hbm_ref(s) for s in sums] + gots, _dma_sems(nw),
                      lambda x, y, c, chips: [(px, py, c) for px, py in chips], name, seq_id)
        return [r[...] for r in gots]

    def body(*refs):
        exchange(refs[:2 * nw], *refs[2 * nw:])

    return _pcall(body, name=name, in_specs=[ANY] * nw, out_specs=[ANY] * nw, out_shape=out_shape,
                  scratch_shapes=_dma_sems(nw), compiler_params=_comm_params())(*sums)


def _rs_share_halves(reds, sizes, name, seq_id=None):
    nw = len(sizes)

    def exchange(out, s_sem, r_sem):
        x, y, c, _ = _place()
        for w in range(nw):
            hw = sizes[w] // 2
            rows = out[w].at[pl.ds(c * hw, hw)]
            _remote(rows, rows, s_sem.at[w], r_sem.at[w], (x, y, 1 - c)).start()
        for w in range(nw):
            _drain(out[w].at[pl.ds(0, sizes[w] // 2)], s_sem.at[w], r_sem.at[w], (x, y, c), send=True, recv=True)

    if seq_id is not None:
        refs = [_hbm_ref(r) for r in reds]
        _on_sequencer(exchange, refs, _dma_sems(nw), _sibling_only, name, seq_id)
        return [r[...] for r in refs]

    def body(*refs):
        exchange(refs[nw:2 * nw], *refs[2 * nw:])

    return _pcall(body, name=name, in_specs=[ANY] * nw, out_specs=[ANY] * nw,
                  out_shape=[_S(r.shape, r.dtype) for r in reds],
                  input_output_aliases={w: w for w in range(nw)},
                  scratch_shapes=_dma_sems(nw), compiler_params=_comm_params())(*reds)


def _allreduce_small(part, name):
    R, C = part.shape

    def body(p_ref, o_ref, buf, s_sem, r_sem):
        x, y, c, _ = _place()
        my_id = 4 * x + 2 * y + c
        buf[my_id] = p_ref[...]
        cps = []
        for k in range(1, N_DEV):
            fx, fy, fc = (k >> 2) & 1, (k >> 1) & 1, k & 1
            peer = (x ^ fx, y ^ fy, c ^ fc)
            cps.append(_remote(p_ref, buf.at[my_id], s_sem.at[k - 1], r_sem.at[k - 1], peer))
        for cp in cps:
            cp.start()
        for cp in cps:
            cp.wait()
        acc = buf[0]
        for d in range(1, N_DEV):
            acc = acc + buf[d]
        o_ref[...] = acc

    vm = pl.BlockSpec(memory_space=pltpu.VMEM)
    return _pcall(body, name=name, in_specs=[vm], out_specs=vm, out_shape=_S((R, C), F32),
                  scratch_shapes=[pltpu.VMEM((N_DEV, R, C), F32)] + _dma_sems(N_DEV - 1),
                  compiler_params=pltpu.CompilerParams(has_side_effects=True, vmem_limit_bytes=VMEM_LIMIT))(part)


SHARD_STEPS = 4


def _own_shard_buffers(ws, l, group, place, name):
    nw = len(group)

    def body(p_ref, *refs):
        for (_, tr), i_ref, o_ref in zip(group, refs[:nw], refs[nw:]):
            v = i_ref[...]
            o_ref[...] = (v.T if tr else v).astype(o_ref.dtype)

    in_specs, out_specs, out_shape, sizes = [], [], [], []
    for wname, tr in group:
        _, K, n = ws[wname].shape
        in_specs.append(pl.BlockSpec((None, K // SHARD_STEPS, n), lambda i, p: (l, i, 0)))
        if tr:
            out_specs.append(pl.BlockSpec((n, K // SHARD_STEPS), lambda i, p: (p[1], i)))
            out_shape.append(_S((N_CHIPS * n, K), MXU))
            sizes.append(n)
        else:
            out_specs.append(pl.BlockSpec((K // SHARD_STEPS, n), lambda i, p: (p[1] * SHARD_STEPS + i, 0)))
            out_shape.append(_S((N_CHIPS * K, n), MXU))
            sizes.append(K)
    gs = pltpu.PrefetchScalarGridSpec(num_scalar_prefetch=1, grid=(SHARD_STEPS,), in_specs=in_specs, out_specs=out_specs)
    bufs = _pcall(body, name=name, grid_spec=gs, out_shape=out_shape,
                  compiler_params=_cp("parallel"))(place, *[ws[wname] for wname, _ in group])
    return list(bufs), sizes


def _after(xs, ys):
    return lax.optimization_barrier((xs, ys))[0]


def _small_rows(v):
    flat = v.reshape(-1)
    pad = (-flat.shape[0]) % 1024
    return jnp.pad(flat, (0, pad)).reshape(-1, 1024)


def _pack_small(vals):
    rows = [_small_rows(vals[n]) for n in SMALL]
    cat = jnp.concatenate(rows, axis=0)
    pad = (-cat.shape[0]) % 8
    return jnp.pad(cat, ((0, pad), (0, 0)))


def _unpack_small(packed, like):
    out, r = {}, 0
    for n in SMALL:
        size = like[n].size
        nr = -(-size // 1024)
        out[n] = packed[r:r + nr].reshape(-1)[:size].reshape(like[n].shape)
        r += nr
    return out


def _rope_tables(positions):
    inv_freq = ROPE_THETA ** (-jnp.arange(0, HEAD_DIM, 2, dtype=F32) / HEAD_DIM)
    ang = positions.astype(F32)[:, None] * inv_freq
    cos, sin = jnp.cos(ang), jnp.sin(ang)
    return jnp.concatenate([cos, cos, cos, cos], axis=1), jnp.concatenate([-sin, sin, -sin, sin], axis=1)


def _layer_fwd(l, x, memv, W, P, cos2, sin2):
    t = f"l{l}"
    sv = {"x0": x}
    sv["h1"] = _rms_fwd(x, P["ffn1_norm"], t + "_ffn1_norm")
    sv["a1"], sv["b1"], sv["s1"] = _ffn_up(sv["h1"], W["ffn1_w_gate"], W["ffn1_w_up"], t + "_ffn1_up")
    sv["x1"] = _mm([(sv["s1"], W["ffn1_w_down"])], nt=False, out_dtype=F32, res=x, res_scale=FFN_RES, name=t + "_ffn1_down")

    sv["h2"] = _rms_fwd(sv["x1"], P["mix_norm"], t + "_mix_norm")
    sv["q"], sv["k"], sv["v"], sv["u"] = _in_proj(sv["h2"], W["w_in"], cos2, sin2, t + "_in_proj")
    sv["oa"] = _swa_fwd(sv["q"], sv["k"], sv["v"], P["attn_sinks"], t + "_swa")
    sv["ob"], sv["mg"] = _pool_fwd(sv["u"], sv["oa"], P["pool_w"], P["pool_scale"], P["attn_out_norm"],
                                   P["pool_out_norm"], t + "_pool")
    sv["x2"] = _mm([(sv["mg"], W["w_out"])], nt=False, out_dtype=F32, res=sv["x1"], name=t + "_out_proj")

    sv["h3"] = _rms_fwd(sv["x2"], P["xattn_norm"], t + "_xattn_norm")
    sv["memn"] = _rms_fwd(memv, P["mem_norm"], t + "_mem_norm")
    sv["q3"] = _mm([(sv["h3"], W["xattn_wq"])], nt=False, out_dtype=MXU, name=t + "_xq")
    sv["kv"] = _mm([(sv["memn"], W["xattn_wkv"])], nt=True, out_dtype=MXU, name=t + "_xkv")
    sv["o3"] = _xattn_fwd(sv["q3"], sv["kv"], t + "_xattn")
    sv["x3"] = _mm([(sv["o3"], W["xattn_wo"])], nt=False, out_dtype=F32, res=sv["x2"], name=t + "_xo")

    sv["h4"] = _rms_fwd(sv["x3"], P["ffn2_norm"], t + "_ffn2_norm")
    sv["a2"], sv["b2"], sv["s2"] = _ffn_up(sv["h4"], W["ffn2_w_gate"], W["ffn2_w_up"], t + "_ffn2_up")
    x4 = _mm([(sv["s2"], W["ffn2_w_down"])], nt=False, out_dtype=F32, res=sv["x3"], res_scale=FFN_RES, name=t + "_ffn2_down")
    return x4, sv


def _ffn_bwd(t, dx, x_in, g, h, a, b, s, wgT, wuT, wd, behind=None, begin=None):
    d_wd = _mm_tn(s, dx, t + "_dwd", r_scale=FFN_RES)
    da, db = _ffn_mid_bwd(dx, wd, a, b, t + "_mid")
    if behind is not None:
        da = _after(da, behind)
    d_wg = _mm_tn(da, h, t + "_dwg")
    d_wu = _mm_tn(db, h, t + "_dwu")
    dh = _mm([(da, wgT), (db, wuT)], nt=False, out_dtype=F32, name=t + "_dh", tm=TOKEN_TILE // 2)
    if begin is not None:
        dh = _after(dh, begin(d_wg, d_wu, d_wd))
    dx_in, dg = _rms_bwd(x_in, g, dh, t + "_norm_bwd", dres=dx)
    return dx_in, dg, d_wg, d_wu, d_wd


def _layer_bwd(l, dx, sv, memv, W, P, cos2, sin2, begin_rest, begin_ffn1):
    t = f"l{l}b"
    GW, GP, marks = {}, {}, {}
    dx, GP["ffn2_norm"], GW["ffn2_w_gate"], GW["ffn2_w_up"], GW["ffn2_w_down"] = _ffn_bwd(
        t + "_ffn2", dx, sv["x3"], P["ffn2_norm"], sv["h4"], sv["a2"], sv["b2"], sv["s2"],
        W["ffn2_w_gate"], W["ffn2_w_up"], W["ffn2_w_down"])
    marks["ffn2"] = dx

    GW["xattn_wo"] = _mm_tn(sv["o3"], dx, t + "_dwo")
    do3 = _mm([(dx, W["xattn_wo"])], nt=True, out_dtype=MXU, name=t + "_do3")
    dq3, dkv = _xattn_bwd(sv["q3"], sv["kv"], do3, t + "_xattn")
    GW["xattn_wq"] = _mm_tn(sv["h3"], dq3, t + "_dwq")
    dh3 = _mm([(dq3, W["xattn_wq"])], nt=True, out_dtype=F32, name=t + "_dh3")
    GW["xattn_wkv"] = _mm_tn(dkv, sv["memn"], t + "_dwkv")
    dmemn = _mm([(dkv, W["xattn_wkv"])], nt=False, out_dtype=F32, name=t + "_dmemn")
    _, GP["mem_norm"] = _rms_bwd(memv, P["mem_norm"], dmemn, t + "_mem_norm_bwd")
    dx, GP["xattn_norm"] = _rms_bwd(sv["x2"], P["xattn_norm"], dh3, t + "_xattn_norm_bwd", dres=dx)
    marks["xattn"] = dx

    GW["w_out"] = _mm_tn(sv["mg"], dx, t + "_dwout")
    dmg = _mm([(dx, W["w_out"])], nt=True, out_dtype=F32, name=t + "_dmg")
    doa, GP["attn_out_norm"] = _rms_bwd(sv["oa"], P["attn_out_norm"], dmg, t + "_oa_norm_bwd", col=0)
    dob, GP["pool_out_norm"] = _rms_bwd(sv["ob"], P["pool_out_norm"], dmg, t + "_ob_norm_bwd", col=1)
    du, GP["pool_w"], GP["pool_scale"] = _pool_bwd(sv["u"], dob, P["pool_w"], P["pool_scale"], t + "_pool")
    dq, dko, dkp, dvo, dvp, dsk = _swa_bwd(sv["q"], sv["k"], sv["v"], doa, P["attn_sinks"], t + "_swa")
    GP["attn_sinks"] = dsk[:, 0]
    dpj = _dproj(dq, dko, dkp, dvo, dvp, du, cos2, sin2, t + "_dproj")
    GW["w_in"] = _mm_tn(dpj, sv["h2"], t + "_dwin")
    dh2 = _mm([(dpj, W["w_in"])], nt=False, out_dtype=F32, name=t + "_dh2")
    dx, GP["mix_norm"] = _rms_bwd(sv["x1"], P["mix_norm"], dh2, t + "_mix_norm_bwd", dres=dx)

    dx, GP["ffn1_norm"], _, _, _ = _ffn_bwd(
        t + "_ffn1", dx, sv["x0"], P["ffn1_norm"], sv["h1"], sv["a1"], sv["b1"], sv["s1"],
        W["ffn1_w_gate"], W["ffn1_w_up"], W["ffn1_w_down"], behind=begin_rest(GW), begin=begin_ffn1)
    return dx, GP, marks


def _reduce_begin(t, grads, sizes, place, seq_ids):
    gots = _rs_pair_exchange(grads, sizes, t + "_pair")
    sums = _pair_sum(grads, gots, sizes, place, t + "_pair_sum")
    recvs = _rs_chip_exchange(sums, sizes, t + "_chips", seq_id=next(seq_ids))
    return dict(t=t, grads=grads, gots=gots, sums=sums, recvs=recvs, sizes=sizes)


def _reduce_end(st, place, seq_ids, late=None):
    recvs = st["recvs"] if late is None else _after(st["recvs"], late)
    reds = _final_sum(st["grads"], st["gots"], recvs, st["sizes"], place, st["t"] + "_final_sum")
    return _rs_share_halves(reds, st["sizes"], st["t"] + "_share")


def _adamw_layer(l, w3, g, m3, v3, transposed, prev, name):
    _, K, n = w3.shape
    if transposed:
        tr = K // SHARD_STEPS
        g_spec = pl.BlockSpec((n, tr), lambda i: (0, i))
    else:
        tr = _rows_tile(K)
        g_spec = pl.BlockSpec((tr, n), lambda i: (i, 0))
    c1 = 1.0 / (1.0 - ADAM_B1 ** ADAM_STEP)
    c2 = 1.0 / (1.0 - ADAM_B2 ** ADAM_STEP)

    def body(w_ref, g_ref, m_ref, v_ref, *rest):
        go_ref, d_ref, nm_ref, nv_ref = rest[-4:]
        gv = g_ref[...].T if transposed else g_ref[...]
        nm = ADAM_B1 * m_ref[...] + (1.0 - ADAM_B1) * gv
        nv = ADAM_B2 * v_ref[...] + (1.0 - ADAM_B2) * (gv * gv)
        go_ref[...] = gv
        d_ref[...] = -ADAM_LR * ((nm * c1) / (jnp.sqrt(nv * c2) + ADAM_EPS) + ADAM_WD * w_ref[...])
        nm_ref[...] = nm
        nv_ref[...] = nv

    slab = pl.BlockSpec((None, tr, n), lambda i: (l, i, 0))
    in_specs, args, aliases = [slab, g_spec, slab, slab], [w3, g, m3, v3], {}
    if prev is not None:
        in_specs += [ANY] * 4
        args += list(prev)
        aliases = {4 + j: j for j in range(4)}
    return _pcall(body, name=name, grid=(K // tr,), in_specs=in_specs, out_specs=[slab] * 4,
                  out_shape=[_S(w3.shape, F32)] * 4, input_output_aliases=aliases,
                  compiler_params=_cp("parallel"))(*args)


def kernel(x, mem, positions, ffn1_norm, ffn1_w_gate, ffn1_w_up, ffn1_w_down, mix_norm, w_in, attn_sinks, pool_w, pool_scale, attn_out_norm, pool_out_norm, w_out, xattn_norm, mem_norm, xattn_wq, xattn_wkv, xattn_wo, ffn2_norm, ffn2_w_gate, ffn2_w_up, ffn2_w_down, final_norm, loss_target, m_ffn1_norm, m_ffn1_w_gate, m_ffn1_w_up, m_ffn1_w_down, m_mix_norm, m_w_in, m_attn_sinks, m_pool_w, m_pool_scale, m_attn_out_norm, m_pool_out_norm, m_w_out, m_xattn_norm, m_mem_norm, m_xattn_wq, m_xattn_wkv, m_xattn_wo, m_ffn2_norm, m_ffn2_w_gate, m_ffn2_w_up, m_ffn2_w_down, m_final_norm, v_ffn1_norm, v_ffn1_w_gate, v_ffn1_w_up, v_ffn1_w_down, v_mix_norm, v_w_in, v_attn_sinks, v_pool_w, v_pool_scale, v_attn_out_norm, v_pool_out_norm, v_w_out, v_xattn_norm, v_mem_norm, v_xattn_wq, v_xattn_wkv, v_xattn_wo, v_ffn2_norm, v_ffn2_w_gate, v_ffn2_w_up, v_ffn2_w_down, v_final_norm):
    ws = dict(ffn1_norm=ffn1_norm, ffn1_w_gate=ffn1_w_gate, ffn1_w_up=ffn1_w_up, ffn1_w_down=ffn1_w_down,
              mix_norm=mix_norm, w_in=w_in, attn_sinks=attn_sinks, pool_w=pool_w, pool_scale=pool_scale,
              attn_out_norm=attn_out_norm, pool_out_norm=pool_out_norm, w_out=w_out, xattn_norm=xattn_norm,
              mem_norm=mem_norm, xattn_wq=xattn_wq, xattn_wkv=xattn_wkv, xattn_wo=xattn_wo, ffn2_norm=ffn2_norm,
              ffn2_w_gate=ffn2_w_gate, ffn2_w_up=ffn2_w_up, ffn2_w_down=ffn2_w_down, final_norm=final_norm)
    ms = dict(ffn1_norm=m_ffn1_norm, ffn1_w_gate=m_ffn1_w_gate, ffn1_w_up=m_ffn1_w_up, ffn1_w_down=m_ffn1_w_down,
              mix_norm=m_mix_norm, w_in=m_w_in, attn_sinks=m_attn_sinks, pool_w=m_pool_w, pool_scale=m_pool_scale,
              attn_out_norm=m_attn_out_norm, pool_out_norm=m_pool_out_norm, w_out=m_w_out, xattn_norm=m_xattn_norm,
              mem_norm=m_mem_norm, xattn_wq=m_xattn_wq, xattn_wkv=m_xattn_wkv, xattn_wo=m_xattn_wo,
              ffn2_norm=m_ffn2_norm, ffn2_w_gate=m_ffn2_w_gate, ffn2_w_up=m_ffn2_w_up, ffn2_w_down=m_ffn2_w_down,
              final_norm=m_final_norm)
    vs = dict(ffn1_norm=v_ffn1_norm, ffn1_w_gate=v_ffn1_w_gate, ffn1_w_up=v_ffn1_w_up, ffn1_w_down=v_ffn1_w_down,
              mix_norm=v_mix_norm, w_in=v_w_in, attn_sinks=v_attn_sinks, pool_w=v_pool_w, pool_scale=v_pool_scale,
              attn_out_norm=v_attn_out_norm, pool_out_norm=v_pool_out_norm, w_out=v_w_out, xattn_norm=v_xattn_norm,
              mem_norm=v_mem_norm, xattn_wq=v_xattn_wq, xattn_wkv=v_xattn_wkv, xattn_wo=v_xattn_wo,
              ffn2_norm=v_ffn2_norm, ffn2_w_gate=v_ffn2_w_gate, ffn2_w_up=v_ffn2_w_up, ffn2_w_down=v_ffn2_w_down,
              final_norm=v_final_norm)
    depth = ffn1_norm.shape[0]
    T, D = x.shape[1], x.shape[2]
    xv = x.reshape(T, D)
    memv = mem.reshape(mem.shape[1], D)
    tgt = loss_target.reshape(T, D)
    cos2, sin2 = _rope_tables(positions.reshape(T))
    in_kernel = {name: tr and ws[name].shape[2] % LANES == 0 for name, tr in BIG}
    swapped = [name for name, tr in BIG if tr and not in_kernel[name]]
    rows = lambda d: {name: (jnp.swapaxes(d[name], 1, 2) if name in swapped else d[name]) for name, _ in BIG}
    wr, mr, vr = rows(ws), rows(ms), rows(vs)
    groups = [[(name, in_kernel[name]) for name, _ in g] for g in GROUPS]

    q_me = 2 * lax.axis_index("x") + lax.axis_index("y")
    place = jnp.stack([lax.axis_index("c"), q_me]).astype(jnp.int32)
    seq_ids = iter(range(1, 1 + 8 * depth))
    Ws, sizes, first = [dict() for _ in range(depth)], {}, None
    for l in range(depth):
        for gi, group in enumerate(groups):
            t = f"l{l}g{gi}"
            bufs, sizes[gi] = _own_shard_buffers(wr, l, group, place, t + "_shard")
            if first is None:
                full = first = _allgather_weights(bufs, sizes[gi], t + "_allgather")
            else:
                full = _allgather_weights(_after(bufs, first), sizes[gi], t + "_allgather", seq_id=next(seq_ids))
            Ws[l].update({name: f for (name, _), f in zip(group, full)})
    Ps = [{n: (ws[n][l].reshape(1, -1) if n != "pool_w" else ws[n][l]) for n in SMALL if n != "final_norm"}
          for l in range(depth)]

    saved = []
    h = xv
    for l in range(depth):
        h, sv = _layer_fwd(l, h, memv, Ws[l], Ps[l], cos2, sin2)
        saved.append(sv)
    loss_row, dx, d_final = _loss_head(h, tgt, final_norm.reshape(1, D), "loss_head")
    GPs, marks, begun = [None] * depth, [None] * depth, {}
    for l in reversed(range(depth)):
        def begin_rest(GW, l=l):
            begun[l, 1] = _reduce_begin(f"l{l}g1r", [GW[name] for name, _ in groups[1]], sizes[1], place, seq_ids)
            return begun[l, 1]["sums"]

        def begin_ffn1(*gs, l=l):
            begun[l, 0] = _reduce_begin(f"l{l}g0r", list(gs), sizes[0], place, seq_ids)
            return begun[l, 0]["sums"]

        dx, GPs[l], marks[l] = _layer_bwd(l, dx, saved[l], memv, Ws[l], Ps[l], cos2, sin2, begin_rest, begin_ffn1)

    stacked = {}
    for l in reversed(range(depth)):
        for gi, group in reversed(list(enumerate(groups))):
            if l > 0:
                late = marks[l - 1]["ffn2" if gi == 1 else "xattn"]
            else:
                late = dx if gi == 1 else None
            reds = _reduce_end(begun[l, gi], place, seq_ids, late)
            for (name, tr), red in zip(group, reds):
                stacked[name] = _adamw_layer(l, wr[name], red, mr[name], vr[name], tr, stacked.get(name),
                                             f"l{l}_adamw_{name}")
    for name in swapped:
        stacked[name] = [jnp.swapaxes(a, 1, 2) for a in stacked[name]]
    small_part = {n: jnp.stack([GPs[l][n].reshape(ws[n].shape[1:]) for l in range(depth)]) for n in SMALL if n != "final_norm"}
    small_part["final_norm"] = d_final.reshape(D)
    small_g = _unpack_small(_allreduce_small(_pack_small(small_part), "small_allreduce"), ws)
    loss = lax.psum(loss_row[0, 0], ("x", "y", "c"))

    grads, deltas, new_m, new_v = {}, {}, {}, {}
    for name, _ in BIG:
        grads[name], deltas[name], new_m[name], new_v[name] = stacked[name]
    d, nm, nv = _adamw(_pack_small(ws), _pack_small(small_g), _pack_small(ms), _pack_small(vs), "adamw_small")
    grads.update(small_g)
    deltas.update(_unpack_small(d, ws))
    new_m.update(_unpack_small(nm, ws))
    new_v.update(_unpack_small(nv, ws))

    grad_x = dx.reshape(x.shape)
    return (loss, grad_x, *[grads[n] for n in WEIGHTS], *[deltas[n] for n in WEIGHTS],
            *[new_m[n] for n in WEIGHTS], *[new_v[n] for n in WEIGHTS])
```

```python
import functools

import jax
import jax.numpy as jnp
from jax import lax
from jax.experimental import pallas as pl
from jax.experimental.pallas import tpu as pltpu
from jax.experimental.pallas import tpu_sc as plsc

F32 = jnp.float32
MXU = jnp.bfloat16

EPS = 1e-6
HEAD_DIM = 64
N_Q_HEADS = 8
N_KV_HEADS = 2
Q_PER_KV = N_Q_HEADS // N_KV_HEADS
ATTN_WIDTH = N_Q_HEADS * HEAD_DIM
KV_WIDTH = N_KV_HEADS * HEAD_DIM
BLOCK = 128
ROPE_THETA = 10000.0
POOL_WINDOWS = (2, 4, 8, 16)
POOL_GROUP = 128
POOL_WIDTH = len(POOL_WINDOWS) * POOL_GROUP
POOL_HALO = 16
X_HEADS = 4
X_HEAD_DIM = 256
FFN_RES = 0.5
NEG = -1e30
ADAM_LR = 0.001
ADAM_B1 = 0.9
ADAM_B2 = 0.999
ADAM_EPS = 1e-08
ADAM_WD = 0.01
ADAM_STEP = 10

N_CHIPS = 4
N_DEV = 8
V7X_VMEM_BYTES = 64 * 1024 * 1024
VMEM_LIMIT = V7X_VMEM_BYTES - 8 * 1024 * 1024
LANES = 128
TOKEN_TILE = 512
MESH = pl.DeviceIdType.MESH

BIG = (("ffn1_w_gate", True), ("ffn1_w_up", True), ("ffn1_w_down", False), ("w_in", True), ("w_out", False),
       ("xattn_wq", False), ("xattn_wkv", True), ("xattn_wo", False),
       ("ffn2_w_gate", True), ("ffn2_w_up", True), ("ffn2_w_down", False))
GROUPS = (BIG[:3], BIG[3:])
SMALL = ("ffn1_norm", "mix_norm", "attn_sinks", "pool_w", "pool_scale", "attn_out_norm", "pool_out_norm",
         "xattn_norm", "mem_norm", "ffn2_norm", "final_norm")
WEIGHTS = ("ffn1_norm", "ffn1_w_gate", "ffn1_w_up", "ffn1_w_down", "mix_norm", "w_in", "attn_sinks", "pool_w",
           "pool_scale", "attn_out_norm", "pool_out_norm", "w_out", "xattn_norm", "mem_norm", "xattn_wq",
           "xattn_wkv", "xattn_wo", "ffn2_norm", "ffn2_w_gate", "ffn2_w_up", "ffn2_w_down", "final_norm")


def _S(shape, dtype):
    return jax.ShapeDtypeStruct(tuple(shape), dtype)


def _pcall(body, **kw):
    return pl.pallas_call(body, **kw)


def _cp(*sem):
    return pltpu.CompilerParams(dimension_semantics=tuple(sem), vmem_limit_bytes=VMEM_LIMIT)


def _nt(a, b):
    return lax.dot_general(a, b, (((1,), (1,)), ((), ())), preferred_element_type=F32)


def _nn(a, b):
    return lax.dot_general(a, b, (((1,), (0,)), ((), ())), preferred_element_type=F32)


def _tn(a, b):
    return lax.dot_general(a, b, (((0,), (0,)), ((), ())), preferred_element_type=F32)


def _tile(n, want):
    t = min(n, want)
    assert n % t == 0, (n, want)
    return t


def _resident(shape):
    nd = len(shape)
    return pl.BlockSpec(tuple(shape), lambda *_: (0,) * nd)


def _rms_fwd(x, g, name):
    T, C = x.shape
    tm = _tile(T, TOKEN_TILE)

    def body(x_ref, g_ref, o_ref):
        xv = x_ref[...]
        r = lax.rsqrt(jnp.mean(xv * xv, axis=-1, keepdims=True) + EPS)
        o_ref[...] = (xv * r * g_ref[...]).astype(o_ref.dtype)

    return _pcall(body, name=name, grid=(T // tm,),
                  in_specs=[pl.BlockSpec((tm, C), lambda i: (i, 0)), _resident((1, C))],
                  out_specs=pl.BlockSpec((tm, C), lambda i: (i, 0)),
                  out_shape=_S((T, C), MXU), compiler_params=_cp("parallel"))(x, g)


def _rms_bwd(x, g, dh, name, dres=None, col=0):
    T, C = x.shape
    tm = _tile(T, TOKEN_TILE)

    def body(*refs):
        if dres is None:
            x_ref, g_ref, dh_ref, dx_ref, dg_ref = refs
        else:
            x_ref, g_ref, dh_ref, dres_ref, dx_ref, dg_ref = refs
        xv = x_ref[...]
        r = lax.rsqrt(jnp.mean(xv * xv, axis=-1, keepdims=True) + EPS)
        xh = xv * r
        dhv = dh_ref[...].astype(F32)
        dxn = dhv * g_ref[...]
        dx = r * (dxn - xh * jnp.mean(dxn * xh, axis=-1, keepdims=True))
        if dres is not None:
            dx = dx + dres_ref[...]
        dx_ref[...] = dx

        @pl.when(pl.program_id(0) == 0)
        def _():
            dg_ref[...] = jnp.zeros_like(dg_ref)

        dg_ref[...] += jnp.sum(dhv * xh, axis=0, keepdims=True)

    tok = pl.BlockSpec((tm, C), lambda i: (i, 0))
    in_specs = [tok, _resident((1, C)), pl.BlockSpec((tm, C), lambda i: (i, col))]
    args = [x, g, dh]
    if dres is not None:
        in_specs.append(tok)
        args.append(dres)
    return _pcall(body, name=name, grid=(T // tm,), in_specs=in_specs,
                  out_specs=[tok, _resident((1, C))],
                  out_shape=[_S((T, C), F32), _S((1, C), F32)], compiler_params=_cp("arbitrary"))(*args)


def _mm(pairs, *, nt, out_dtype, name, res=None, res_scale=1.0, a_scale=1.0, tm=TOKEN_TILE):
    M = pairs[0][0].shape[0]
    N = pairs[0][1].shape[0] if nt else pairs[0][1].shape[1]
    tm = _tile(M, tm)
    n = len(pairs)

    def body(*refs):
        a_refs, w_refs = refs[:n], refs[n:2 * n]
        o_ref = refs[-1]
        acc = None
        for a_ref, w_ref in zip(a_refs, w_refs):
            a = a_ref[...]
            if a_scale != 1.0:
                a = a * a_scale
            a = a.astype(MXU)
            p = _nt(a, w_ref[...]) if nt else _nn(a, w_ref[...])
            acc = p if acc is None else acc + p
        if res is not None:
            acc = refs[2 * n][...] + res_scale * acc
        o_ref[...] = acc.astype(o_ref.dtype)

    in_specs = [pl.BlockSpec((tm, a.shape[1]), lambda i: (i, 0)) for a, _ in pairs]
    in_specs += [_resident(w.shape) for _, w in pairs]
    args = [a for a, _ in pairs] + [w for _, w in pairs]
    if res is not None:
        in_specs.append(pl.BlockSpec((tm, N), lambda i: (i, 0)))
        args.append(res)
    return _pcall(body, name=name, grid=(M // tm,), in_specs=in_specs,
                  out_specs=pl.BlockSpec((tm, N), lambda i: (i, 0)),
                  out_shape=_S((M, N), out_dtype), compiler_params=_cp("parallel"))(*args)


def _mm_norm_bwd(pairs, norms, *, nt, name, tm=TOKEN_TILE):
    M = pairs[0][0].shape[0]
    tm = _tile(M, tm)
    n, k = len(pairs), len(norms)
    has_res = [d is not None for _, _, d in norms]

    def body(*refs):
        a_refs, w_refs = refs[:n], refs[n:2 * n]
        rest = list(refs[2 * n:])
        acc = None
        for a_ref, w_ref in zip(a_refs, w_refs):
            p = _nt(a_ref[...].astype(MXU), w_ref[...]) if nt else _nn(a_ref[...].astype(MXU), w_ref[...])
            acc = p if acc is None else acc + p
        ins = [(rest.pop(0), rest.pop(0), rest.pop(0) if has_res[j] else None) for j in range(k)]
        dx_refs, dg_refs = rest[:k], rest[k:]
        c0 = 0
        for (x_ref, g_ref, d_ref), dx_ref, dg_ref in zip(ins, dx_refs, dg_refs):
            xv = x_ref[...]
            dhv = acc[:, c0:c0 + xv.shape[1]]
            c0 += xv.shape[1]
            r = lax.rsqrt(jnp.mean(xv * xv, axis=-1, keepdims=True) + EPS)
            xh = xv * r
            dxn = dhv * g_ref[...]
            dx = r * (dxn - xh * jnp.mean(dxn * xh, axis=-1, keepdims=True))
            dx_ref[...] = dx if d_ref is None else dx + d_ref[...]

            @pl.when(pl.program_id(0) == 0)
            def _():
                dg_ref[...] = jnp.zeros_like(dg_ref)

            dg_ref[...] += jnp.sum(dhv * xh, axis=0, keepdims=True)

    tok = lambda c: pl.BlockSpec((tm, c), lambda i: (i, 0))
    in_specs = [tok(a.shape[1]) for a, _ in pairs] + [_resident(w.shape) for _, w in pairs]
    args = [a for a, _ in pairs] + [w for _, w in pairs]
    for x, g, d in norms:
        in_specs += [tok(x.shape[1]), _resident(g.shape)] + ([tok(x.shape[1])] if d is not None else [])
        args += [x, g] + ([d] if d is not None else [])
    out_specs = [tok(x.shape[1]) for x, _, _ in norms] + [_resident((1, x.shape[1])) for x, _, _ in norms]
    out_shape = [_S((M, x.shape[1]), F32) for x, _, _ in norms] + [_S((1, x.shape[1]), F32) for x, _, _ in norms]
    outs = _pcall(body, name=name, grid=(M // tm,), in_specs=in_specs, out_specs=out_specs, out_shape=out_shape,
                  compiler_params=_cp("arbitrary"))(*args)
    return list(outs[:k]), list(outs[k:])


def _mm_tn(l, r, name, *, l_scale=1.0, r_scale=1.0, tr=1408, tt=TOKEN_TILE):
    T, R = l.shape
    C = r.shape[1]
    tt = _tile(T, tt)
    tr = tr if R % tr == 0 else (1024 if R % 1024 == 0 and R > 1280 else R)

    def body(l_ref, r_ref, o_ref):
        lv, rv = l_ref[...], r_ref[...]
        if l_scale != 1.0:
            lv = lv * l_scale
        if r_scale != 1.0:
            rv = rv * r_scale
        lv, rv = lv.astype(MXU), rv.astype(MXU)

        @pl.when(pl.program_id(1) == 0)
        def _():
            o_ref[...] = jnp.zeros_like(o_ref)

        o_ref[...] += _tn(lv, rv)

    return _pcall(body, name=name, grid=(R // tr, T // tt),
                  in_specs=[pl.BlockSpec((tt, tr), lambda i, t: (t, i)), pl.BlockSpec((tt, C), lambda i, t: (t, 0))],
                  out_specs=pl.BlockSpec((tr, C), lambda i, t: (i, 0)),
                  out_shape=_S((R, C), F32), compiler_params=_cp("parallel", "arbitrary"))(l, r)


FFN_COL_TILE = 1408


def _sigmoid(a):
    return 0.5 * (jnp.tanh(0.5 * a) + 1.0)


def _ffn_up(h, wgT, wuT, name):
    T, D = h.shape
    Fd = wgT.shape[0]
    tm, tn = _tile(T, TOKEN_TILE), _tile(Fd, FFN_COL_TILE)

    def body(h_ref, wg_ref, wu_ref, a_ref, b_ref, s_ref):
        hv = h_ref[...]
        a = _nt(hv, wg_ref[...])
        b = _nt(hv, wu_ref[...])
        s = a * _sigmoid(a) * b
        a_ref[...] = a.astype(a_ref.dtype)
        b_ref[...] = b.astype(b_ref.dtype)
        s_ref[...] = s.astype(s_ref.dtype)

    wspec = pl.BlockSpec((tn, D), lambda j, i: (j, 0))
    ospec = pl.BlockSpec((tm, tn), lambda j, i: (i, j))
    return _pcall(body, name=name, grid=(Fd // tn, T // tm),
                  in_specs=[pl.BlockSpec((tm, D), lambda j, i: (i, 0)), wspec, wspec],
                  out_specs=[ospec, ospec, ospec], out_shape=[_S((T, Fd), MXU)] * 3,
                  compiler_params=_cp("parallel", "parallel"))(h, wgT, wuT)


def _ffn_mid_bwd(dx, wd, a, b, name):
    T, D = dx.shape
    Fd = wd.shape[0]
    tm, tn = _tile(T, TOKEN_TILE), _tile(Fd, FFN_COL_TILE)

    def body(dx_ref, wd_ref, a_ref, b_ref, da_ref, db_ref):
        dy = (dx_ref[...] * FFN_RES).astype(MXU)
        ds = _nt(dy, wd_ref[...])
        av, bv = a_ref[...].astype(F32), b_ref[...].astype(F32)
        sg = _sigmoid(av)
        da_ref[...] = (ds * bv * (sg * (1.0 + av * (1.0 - sg)))).astype(da_ref.dtype)
        db_ref[...] = (ds * (av * sg)).astype(db_ref.dtype)

    aspec = pl.BlockSpec((tm, tn), lambda j, i: (i, j))
    return _pcall(body, name=name, grid=(Fd // tn, T // tm),
                  in_specs=[pl.BlockSpec((tm, D), lambda j, i: (i, 0)), pl.BlockSpec((tn, D), lambda j, i: (j, 0)),
                            aspec, aspec],
                  out_specs=[aspec, aspec], out_shape=[_S((T, Fd), MXU)] * 2,
                  compiler_params=_cp("parallel", "parallel"))(dx, wd, a, b)


def _swap_halves(t):
    w = t.shape[1]
    lane = lax.broadcasted_iota(jnp.int32, t.shape, 1)
    first = (lane % HEAD_DIM) < (HEAD_DIM // 2)
    return jnp.where(first, pltpu.roll(t, w - HEAD_DIM // 2, 1), pltpu.roll(t, HEAD_DIM // 2, 1))


def _rope(t, cos2, sin2):
    reps = t.shape[1] // LANES
    c = jnp.tile(cos2, (1, reps)) if reps > 1 else cos2
    s = jnp.tile(sin2, (1, reps)) if reps > 1 else sin2
    return t * c + _swap_halves(t) * s


def _rope_bwd(dt, cos2, sin2):
    reps = dt.shape[1] // LANES
    c = jnp.tile(cos2, (1, reps)) if reps > 1 else cos2
    s = jnp.tile(sin2, (1, reps)) if reps > 1 else sin2
    return dt * c + _swap_halves(dt * s)


def _in_proj(h, winT, cos2, sin2, name):
    T, D = h.shape
    tm = _tile(T, TOKEN_TILE)
    qe, ke, ve = ATTN_WIDTH, ATTN_WIDTH + KV_WIDTH, ATTN_WIDTH + 2 * KV_WIDTH

    def body(h_ref, w_ref, c_ref, s_ref, q_ref, k_ref, v_ref, u_ref):
        proj = _nt(h_ref[...], w_ref[...])
        cv, sv = c_ref[...], s_ref[...]
        q_ref[...] = _rope(proj[:, :qe], cv, sv).astype(q_ref.dtype)
        k_ref[...] = _rope(proj[:, qe:ke], cv, sv).astype(k_ref.dtype)
        v_ref[...] = proj[:, ke:ve].astype(v_ref.dtype)
        u_ref[...] = proj[:, ve:]

    def tok(w):
        return pl.BlockSpec((tm, w), lambda i: (i, 0))

    return _pcall(body, name=name, grid=(T // tm,),
                  in_specs=[tok(D), _resident(winT.shape), tok(LANES), tok(LANES)],
                  out_specs=[tok(ATTN_WIDTH), tok(KV_WIDTH), tok(KV_WIDTH), tok(POOL_WIDTH)],
                  out_shape=[_S((T, ATTN_WIDTH), MXU), _S((T, KV_WIDTH), MXU), _S((T, KV_WIDTH), MXU),
                             _S((T, POOL_WIDTH), F32)],
                  compiler_params=_cp("parallel"))(h, winT, cos2, sin2)


SWA_TILE_BLOCKS = 4
SM_SCALE = HEAD_DIM ** -0.5


def _swa_bias(first_tile):
    cols = Q_PER_KV * BLOCK
    kj = lax.broadcasted_iota(jnp.int32, (2 * BLOCK, cols), 0)
    qi = lax.broadcasted_iota(jnp.int32, (2 * BLOCK, cols), 1) % BLOCK
    diff = qi + BLOCK - kj
    bias = jnp.where((diff >= 0) & (diff < BLOCK), 0.0, NEG)
    return bias, jnp.where(kj < jnp.where(first_tile, BLOCK, 0), NEG, bias)


def _swa_probs(kh, qs, sink_row, bias):
    s = _nt(kh, qs) + bias
    m = jnp.maximum(jnp.max(s, axis=0, keepdims=True), sink_row)
    e = jnp.exp(s - m)
    es = jnp.exp(sink_row - m)
    inv = 1.0 / (jnp.sum(e, axis=0, keepdims=True) + es)
    return e * inv, es * inv


def _sink_row(sinks_ref, kv):
    return jnp.concatenate([jnp.full((1, BLOCK), sinks_ref[0, kv * Q_PER_KV + g], F32) for g in range(Q_PER_KV)], axis=1)


def _stack_heads(t, kv):
    return jnp.concatenate([t[:, (kv * Q_PER_KV + g) * HEAD_DIM:(kv * Q_PER_KV + g + 1) * HEAD_DIM]
                            for g in range(Q_PER_KV)], axis=0)


def _swa_specs(T):
    tq = _tile(T, SWA_TILE_BLOCKS * BLOCK)
    nbt = tq // BLOCK
    cur = lambda w: pl.BlockSpec((tq, w), lambda i: (i, 0))
    prev = lambda w: pl.BlockSpec((BLOCK, w), lambda i: (jnp.maximum(i * nbt - 1, 0), 0))
    return tq, nbt, cur, prev


def _rows(b):
    return slice(b * BLOCK, (b + 1) * BLOCK)


def _swa_fwd(q, k, v, sinks, name):
    T = q.shape[0]
    tq, nbt, cur, prev = _swa_specs(T)

    def body(sinks_ref, q_ref, k_ref, kp_ref, v_ref, vp_ref, o_ref):
        bias, bias0 = _swa_bias(pl.program_id(0) == 0)
        sink = [_sink_row(sinks_ref, kv) for kv in range(N_KV_HEADS)]
        kx = jnp.concatenate([kp_ref[...], k_ref[...]], axis=0)
        vx = jnp.concatenate([vp_ref[...], v_ref[...]], axis=0)
        for b in range(nbt):
            qv = q_ref[_rows(b), :] * SM_SCALE
            kk, vv = kx[b * BLOCK:(b + 2) * BLOCK], vx[b * BLOCK:(b + 2) * BLOCK]
            for kv in range(N_KV_HEADS):
                hs = slice(kv * HEAD_DIM, (kv + 1) * HEAD_DIM)
                p, _ = _swa_probs(kk[:, hs], _stack_heads(qv, kv), sink[kv], bias0 if b == 0 else bias)
                o_t = _tn(vv[:, hs], p.astype(MXU))
                for g in range(Q_PER_KV):
                    c0 = (kv * Q_PER_KV + g) * HEAD_DIM
                    o_ref[_rows(b), c0:c0 + HEAD_DIM] = o_t[:, _rows(g)].T

    return _pcall(body, name=name, grid=(T // tq,),
                  in_specs=[pl.BlockSpec(memory_space=pltpu.SMEM), cur(ATTN_WIDTH), cur(KV_WIDTH), prev(KV_WIDTH),
                            cur(KV_WIDTH), prev(KV_WIDTH)],
                  out_specs=cur(ATTN_WIDTH), out_shape=_S((T, ATTN_WIDTH), F32),
                  compiler_params=_cp("parallel"))(sinks, q, k, k, v, v)


def _swa_bwd(q, k, v, do, sinks, name):
    T = q.shape[0]
    tq, nbt, cur, prev = _swa_specs(T)
    per_tile = lambda w: pl.BlockSpec((BLOCK, w), lambda i: (i, 0))

    def add(acc, t):
        return t if acc is None else acc + t

    def body(sinks_ref, q_ref, k_ref, kp_ref, v_ref, vp_ref, do_ref,
             dq_ref, dk_ref, dkp_ref, dv_ref, dvp_ref, dsk_ref):
        bias, bias0 = _swa_bias(pl.program_id(0) == 0)
        sink = [_sink_row(sinks_ref, kv) for kv in range(N_KV_HEADS)]
        kx = jnp.concatenate([kp_ref[...], k_ref[...]], axis=0)
        vx = jnp.concatenate([vp_ref[...], v_ref[...]], axis=0)

        @pl.when(pl.program_id(0) == 0)
        def _():
            dsk_ref[...] = jnp.zeros_like(dsk_ref)

        dk_acc, dv_acc = [None] * (nbt + 1), [None] * (nbt + 1)
        dsk_acc = [None] * N_Q_HEADS
        for b in range(nbt):
            qv, dov = q_ref[_rows(b), :] * SM_SCALE, do_ref[_rows(b), :].astype(MXU)
            kk, vv = kx[b * BLOCK:(b + 2) * BLOCK], vx[b * BLOCK:(b + 2) * BLOCK]
            dks, dvs = [], []
            for kv in range(N_KV_HEADS):
                hs = slice(kv * HEAD_DIM, (kv + 1) * HEAD_DIM)
                qs, dos = _stack_heads(qv, kv), _stack_heads(dov, kv)
                p, ps = _swa_probs(kk[:, hs], qs, sink[kv], bias0 if b == 0 else bias)
                dp = _nt(vv[:, hs], dos)
                delta = jnp.sum(p * dp, axis=0, keepdims=True)
                ds = (p * (dp - delta)).astype(MXU)
                dq_t = _tn(kk[:, hs], ds) * SM_SCALE
                dks.append(_nn(ds, qs))
                dvs.append(_nn(p.astype(MXU), dos))
                dsink = -ps * delta
                for g in range(Q_PER_KV):
                    h = kv * Q_PER_KV + g
                    dq_ref[_rows(b), h * HEAD_DIM:(h + 1) * HEAD_DIM] = dq_t[:, _rows(g)].T
                    dsk_acc[h] = add(dsk_acc[h], jnp.sum(dsink[:, _rows(g)], axis=1, keepdims=True))
            dk, dv = jnp.concatenate(dks, axis=1), jnp.concatenate(dvs, axis=1)
            dk_acc[b], dk_acc[b + 1] = add(dk_acc[b], dk[:BLOCK]), add(dk_acc[b + 1], dk[BLOCK:])
            dv_acc[b], dv_acc[b + 1] = add(dv_acc[b], dv[:BLOCK]), add(dv_acc[b + 1], dv[BLOCK:])
        dkp_ref[...], dvp_ref[...] = dk_acc[0], dv_acc[0]
        dk_ref[...] = jnp.concatenate(dk_acc[1:], axis=0)
        dv_ref[...] = jnp.concatenate(dv_acc[1:], axis=0)
        for h in range(N_Q_HEADS):
            dsk_ref[h:h + 1, :] += jnp.broadcast_to(dsk_acc[h], (1, LANES))

    kvs, kvp = _S((T, KV_WIDTH), F32), _S((T // tq * BLOCK, KV_WIDTH), F32)
    return _pcall(body, name=name, grid=(T // tq,),
                  in_specs=[pl.BlockSpec(memory_space=pltpu.SMEM), cur(ATTN_WIDTH), cur(KV_WIDTH), prev(KV_WIDTH),
                            cur(KV_WIDTH), prev(KV_WIDTH), cur(ATTN_WIDTH)],
                  out_specs=[cur(ATTN_WIDTH), cur(KV_WIDTH), per_tile(KV_WIDTH), cur(KV_WIDTH), per_tile(KV_WIDTH),
                             _resident((N_Q_HEADS, LANES))],
                  out_shape=[_S((T, ATTN_WIDTH), F32), kvs, kvp, kvs, kvp, _S((N_Q_HEADS, LANES), F32)],
                  compiler_params=_cp("arbitrary"))(sinks, q, k, k, v, v, do)


def _dproj(dq, dk, dkp, dv, dvp, du, cos2, sin2, name):
    T = dq.shape[0]
    tq, nbt, cur, _ = _swa_specs(T)
    nt = T // tq
    nxt = lambda w: pl.BlockSpec((BLOCK, w), lambda i: (jnp.minimum(i + 1, nt - 1), 0))

    def body(dq_ref, dk_ref, dkp_ref, dv_ref, dvp_ref, du_ref, c_ref, s_ref, o_ref):
        more = (pl.program_id(0) < nt - 1).astype(F32)
        cv, sv = c_ref[...], s_ref[...]

        def whole(t_ref, p_ref):
            t, last = t_ref[...], t_ref[tq - BLOCK:, :] + more * p_ref[...]
            return last if nbt == 1 else jnp.concatenate([t[:tq - BLOCK], last], axis=0)

        o_ref[...] = jnp.concatenate(
            [_rope_bwd(dq_ref[...], cv, sv), _rope_bwd(whole(dk_ref, dkp_ref), cv, sv), whole(dv_ref, dvp_ref),
             du_ref[...]], axis=1).astype(o_ref.dtype)

    width = ATTN_WIDTH + 2 * KV_WIDTH + POOL_WIDTH
    return _pcall(body, name=name, grid=(nt,),
                  in_specs=[cur(ATTN_WIDTH), cur(KV_WIDTH), nxt(KV_WIDTH), cur(KV_WIDTH), nxt(KV_WIDTH),
                            cur(POOL_WIDTH), cur(LANES), cur(LANES)],
                  out_specs=cur(width), out_shape=_S((T, width), MXU),
                  compiler_params=_cp("parallel"))(dq, dk, dkp, dv, dvp, du, cos2, sin2)


def _pool_specs(T):
    tm = _tile(T, TOKEN_TILE)
    hb = tm // POOL_HALO
    nh = T // POOL_HALO
    tok = lambda w: pl.BlockSpec((tm, w), lambda i: (i, 0))
    before = pl.BlockSpec((POOL_HALO, POOL_WIDTH), lambda i: (jnp.maximum(i * hb - 1, 0), 0))
    after = pl.BlockSpec((POOL_HALO, POOL_WIDTH), lambda i: (jnp.minimum((i + 1) * hb, nh - 1), 0))
    return tm, tok, before, after


def _window_counts(i, tm, rows, w):
    t = i * tm + lax.broadcasted_iota(jnp.int32, (rows, 1), 0)
    return jnp.minimum(t + 1, w).astype(F32)


def _pooled(u_ext, i, tm):
    out = []
    for g, w in enumerate(POOL_WINDOWS):
        acc = u_ext[:, g * POOL_GROUP:(g + 1) * POOL_GROUP]
        tok = acc[POOL_HALO:, :]
        sh = 1
        while sh < w:
            acc = acc + pltpu.roll(acc, sh, 0)
            sh *= 2
        out.append(acc[POOL_HALO:, :] / _window_counts(i, tm, tm, w) - tok)
    return out


def _pool_fwd(u, out_a, pool_w, pool_scale, ga, gb, name):
    T = u.shape[0]
    tm, tok, before, _ = _pool_specs(T)

    def body(u_ref, halo_ref, oa_ref, pw_ref, sc_ref, ga_ref, gb_ref, ob_ref, mg_ref):
        i = pl.program_id(0)
        halo = halo_ref[...] * (i > 0).astype(F32)
        pooled = _pooled(jnp.concatenate([halo, u_ref[...]], axis=0), i, tm)
        mixed = [_nn(pooled[g].astype(MXU), pw_ref[g].astype(MXU)) for g in range(len(POOL_WINDOWS))]
        ob = jnp.concatenate(mixed, axis=1) * sc_ref[...]
        ob_ref[...] = ob
        oa = oa_ref[...]
        ra = lax.rsqrt(jnp.mean(oa * oa, axis=-1, keepdims=True) + EPS)
        rb = lax.rsqrt(jnp.mean(ob * ob, axis=-1, keepdims=True) + EPS)
        mg_ref[...] = jnp.concatenate([oa * ra * ga_ref[...], ob * rb * gb_ref[...]], axis=1).astype(mg_ref.dtype)

    vec = _resident((1, POOL_WIDTH))
    return _pcall(body, name=name, grid=(T // tm,),
                  in_specs=[tok(POOL_WIDTH), before, tok(ATTN_WIDTH), _resident(pool_w.shape), vec, vec, vec],
                  out_specs=[tok(POOL_WIDTH), tok(ATTN_WIDTH + POOL_WIDTH)],
                  out_shape=[_S((T, POOL_WIDTH), F32), _S((T, ATTN_WIDTH + POOL_WIDTH), MXU)],
                  compiler_params=_cp("parallel"))(u, u, out_a, pool_w, pool_scale, ga, gb)


def _pool_bwd(u, dob, pool_w, pool_scale, name):
    T = u.shape[0]
    tm, tok, before, after = _pool_specs(T)
    nt = T // tm
    G = len(POOL_WINDOWS)

    def body(u_ref, halo_ref, dob_ref, dnext_ref, pw_ref, sc_ref, du_ref, dpw_ref, dsc_ref):
        i = pl.program_id(0)
        halo = halo_ref[...] * (i > 0).astype(F32)
        pooled = _pooled(jnp.concatenate([halo, u_ref[...]], axis=0), i, tm)
        dnext = dnext_ref[...] * (i < nt - 1).astype(F32)
        dext = jnp.concatenate([dob_ref[...], dnext], axis=0) * sc_ref[...]

        @pl.when(i == 0)
        def _():
            dpw_ref[...] = jnp.zeros_like(dpw_ref)
            dsc_ref[...] = jnp.zeros_like(dsc_ref)

        dus, dscs = [], []
        for g, w in enumerate(POOL_WINDOWS):
            gs = slice(g * POOL_GROUP, (g + 1) * POOL_GROUP)
            pw = pw_ref[g].astype(MXU)
            pg = pooled[g].astype(MXU)
            dmix = dext[:, gs].astype(MXU)
            dscs.append(jnp.sum(dob_ref[:, gs] * _nn(pg, pw), axis=0, keepdims=True))
            dpw_ref[g] += _tn(pg, dmix[:tm, :])
            dpooled = _nt(dmix, pw)
            acc = dpooled / _window_counts(i, tm, tm + POOL_HALO, w)
            sh = 1
            while sh < w:
                acc = acc + pltpu.roll(acc, tm + POOL_HALO - sh, 0)
                sh *= 2
            dus.append(acc[:tm, :] - dpooled[:tm, :])
        du_ref[...] = jnp.concatenate(dus, axis=1)
        dsc_ref[...] += jnp.concatenate(dscs, axis=1)

    vec = _resident((1, POOL_WIDTH))
    return _pcall(body, name=name, grid=(nt,),
                  in_specs=[tok(POOL_WIDTH), before, tok(POOL_WIDTH), after, _resident(pool_w.shape), vec],
                  out_specs=[tok(POOL_WIDTH), _resident(pool_w.shape), vec],
                  out_shape=[_S((T, POOL_WIDTH), F32), _S(pool_w.shape, F32), _S((1, POOL_WIDTH), F32)],
                  compiler_params=_cp("arbitrary"))(u, u, dob, dob, pool_w, pool_scale)


def _xattn_probs(qh, kh):
    s = _nt(qh, kh) * (X_HEAD_DIM ** -0.5)
    e = jnp.exp(s - jnp.max(s, axis=1, keepdims=True))
    return e / jnp.sum(e, axis=1, keepdims=True)


def _xattn_fwd(q, kvm, name):
    T, XW = q.shape
    tm = _tile(T, TOKEN_TILE)

    def body(q_ref, kv_ref, o_ref):
        for h in range(X_HEADS):
            hs = slice(h * X_HEAD_DIM, (h + 1) * X_HEAD_DIM)
            vs = slice(XW + h * X_HEAD_DIM, XW + (h + 1) * X_HEAD_DIM)
            p = _xattn_probs(q_ref[:, hs], kv_ref[:, hs])
            o_ref[:, hs] = _nn(p.astype(MXU), kv_ref[:, vs]).astype(o_ref.dtype)

    return _pcall(body, name=name, grid=(T // tm,),
                  in_specs=[pl.BlockSpec((tm, XW), lambda i: (i, 0)), _resident(kvm.shape)],
                  out_specs=pl.BlockSpec((tm, XW), lambda i: (i, 0)), out_shape=_S((T, XW), MXU),
                  compiler_params=_cp("parallel"))(q, kvm)


def _xattn_bwd(q, kvm, do, name):
    T, XW = q.shape
    tm = _tile(T, TOKEN_TILE)

    def body(q_ref, kv_ref, do_ref, dq_ref, dkv_ref):
        @pl.when(pl.program_id(0) == 0)
        def _():
            dkv_ref[...] = jnp.zeros_like(dkv_ref)

        for h in range(X_HEADS):
            hs = slice(h * X_HEAD_DIM, (h + 1) * X_HEAD_DIM)
            vs = slice(XW + h * X_HEAD_DIM, XW + (h + 1) * X_HEAD_DIM)
            qh, doh = q_ref[:, hs], do_ref[:, hs]
            p = _xattn_probs(qh, kv_ref[:, hs])
            dp = _nt(doh, kv_ref[:, vs])
            ds = (p * (dp - jnp.sum(p * dp, axis=1, keepdims=True)) * (X_HEAD_DIM ** -0.5)).astype(MXU)
            dq_ref[:, hs] = _nn(ds, kv_ref[:, hs]).astype(dq_ref.dtype)
            dkv_ref[:, hs] += _tn(ds, qh)
            dkv_ref[:, vs] += _tn(p.astype(MXU), doh)

    tok = pl.BlockSpec((tm, XW), lambda i: (i, 0))
    return _pcall(body, name=name, grid=(T // tm,),
                  in_specs=[tok, _resident(kvm.shape), tok],
                  out_specs=[tok, _resident(kvm.shape)],
                  out_shape=[_S((T, XW), MXU), _S(kvm.shape, F32)],
                  compiler_params=_cp("arbitrary"))(q, kvm, do)


def _loss_head(x, tgt, g, name):
    T, D = x.shape
    tm = _tile(T, TOKEN_TILE)

    def body(x_ref, t_ref, g_ref, loss_ref, dx_ref, dg_ref):
        xv, gv = x_ref[...], g_ref[...]
        r = lax.rsqrt(jnp.mean(xv * xv, axis=-1, keepdims=True) + EPS)
        xh = xv * r
        e = xh * gv - t_ref[...]
        dy = e * (1.0 / D)
        dxn = dy * gv
        dx_ref[...] = r * (dxn - xh * jnp.mean(dxn * xh, axis=-1, keepdims=True))

        @pl.when(pl.program_id(0) == 0)
        def _():
            loss_ref[...] = jnp.zeros_like(loss_ref)
            dg_ref[...] = jnp.zeros_like(dg_ref)

        part = jnp.sum(jnp.sum(e * e, axis=1, keepdims=True), axis=0, keepdims=True) * (0.5 / D)
        loss_ref[...] += jnp.broadcast_to(part, (1, LANES))
        dg_ref[...] += jnp.sum(dy * xh, axis=0, keepdims=True)

    tok = pl.BlockSpec((tm, D), lambda i: (i, 0))
    return _pcall(body, name=name, grid=(T // tm,),
                  in_specs=[tok, tok, _resident((1, D))],
                  out_specs=[_resident((1, LANES)), tok, _resident((1, D))],
                  out_shape=[_S((1, LANES), F32), _S((T, D), F32), _S((1, D), F32)],
                  compiler_params=_cp("arbitrary"))(x, tgt, g)


def _rows_tile(rows):
    for t in (512, 416, 352, 256, 128, 64, 32, 16, 8):
        if rows % t == 0:
            return t
    return rows


def _pair_sum(grads, gots, sizes, place, name):
    nw = len(sizes)
    C = grads[0].shape[1]

    def body(p_ref, *refs):
        g, got, out = refs[:nw], refs[nw:2 * nw], refs[2 * nw:]
        for w in range(nw):
            out[w][...] = (g[w][...] + got[w][...]).astype(out[w].dtype)

    def blk(w):
        return (sizes[w] // 4, C)

    in_specs = [pl.BlockSpec(blk(w), lambda q, s, p: (4 * q + 2 * p[0] + s, 0)) for w in range(nw)]
    in_specs += [pl.BlockSpec(blk(w), lambda q, s, p: (2 * q + s, 0)) for w in range(nw)]
    out_specs = [pl.BlockSpec(blk(w), lambda q, s, p: (2 * q + s, 0)) for w in range(nw)]
    gs = pltpu.PrefetchScalarGridSpec(num_scalar_prefetch=1, grid=(N_CHIPS, 2), in_specs=in_specs, out_specs=out_specs)
    return _pcall(body, name=name, grid_spec=gs, out_shape=[_S((2 * n, C), MXU) for n in sizes],
                  compiler_params=_cp("parallel", "parallel"))(place, *grads, *gots)


def _final_sum(grads, gots, recvs, sizes, place, name):
    nw = len(sizes)
    C = grads[0].shape[1]

    def body(p_ref, *refs):
        g, got, rv, out = refs[:nw], refs[nw:2 * nw], refs[2 * nw:5 * nw], refs[5 * nw:]
        for w in range(nw):
            acc = g[w][...] + got[w][...]
            for j in range(3):
                acc = acc + rv[3 * w + j][...].astype(F32)
            out[w][...] = acc

    def blk(w):
        return (sizes[w] // 4, C)

    in_specs = [pl.BlockSpec(blk(w), lambda s, p: (4 * p[1] + 2 * p[0] + s, 0)) for w in range(nw)]
    in_specs += [pl.BlockSpec(blk(w), lambda s, p: (2 * p[1] + s, 0)) for w in range(nw)]
    args = list(grads) + list(gots)
    for w in range(nw):
        for j in range(3):
            in_specs.append(pl.BlockSpec(blk(w), lambda s, p, j=j: (2 * j + s, 0)))
            args.append(recvs[w])
    out_specs = [pl.BlockSpec(blk(w), lambda s, p: (2 * p[0] + s, 0)) for w in range(nw)]
    gs = pltpu.PrefetchScalarGridSpec(num_scalar_prefetch=1, grid=(2,), in_specs=in_specs, out_specs=out_specs)
    return _pcall(body, name=name, grid_spec=gs, out_shape=[_S((n, C), F32) for n in sizes],
                  compiler_params=_cp("parallel"))(place, *args)


def _adamw(w, g, m, v, name):
    R, C = w.shape
    tr = _rows_tile(R)
    c1 = 1.0 / (1.0 - ADAM_B1 ** ADAM_STEP)
    c2 = 1.0 / (1.0 - ADAM_B2 ** ADAM_STEP)

    def body(w_ref, g_ref, m_ref, v_ref, d_ref, nm_ref, nv_ref):
        gv = g_ref[...]
        nm = ADAM_B1 * m_ref[...] + (1.0 - ADAM_B1) * gv
        nv = ADAM_B2 * v_ref[...] + (1.0 - ADAM_B2) * (gv * gv)
        d_ref[...] = -ADAM_LR * ((nm * c1) / (jnp.sqrt(nv * c2) + ADAM_EPS) + ADAM_WD * w_ref[...])
        nm_ref[...] = nm
        nv_ref[...] = nv

    spec = pl.BlockSpec((tr, C), lambda i: (i, 0))
    return _pcall(body, name=name, grid=(R // tr,), in_specs=[spec] * 4, out_specs=[spec] * 3,
                  out_shape=[_S((R, C), F32)] * 3, compiler_params=_cp("parallel"))(w, g, m, v)


ANY = pl.BlockSpec(memory_space=pl.ANY)


def _place():
    x, y, c = lax.axis_index("x"), lax.axis_index("y"), lax.axis_index("c")
    chips = [(1 - x, y), (x, 1 - y), (1 - x, 1 - y)]
    return x, y, c, chips


def _remote(src, dst, send_sem, recv_sem, dev):
    return pltpu.make_async_remote_copy(src_ref=src, dst_ref=dst, send_sem=send_sem, recv_sem=recv_sem,
                                        device_id=dev, device_id_type=MESH)


def _drain(like, send_sem, recv_sem, me, *, send=False, recv=False):
    d = _remote(like, like, send_sem, recv_sem, me)
    if send:
        d.wait_send()
    if recv:
        d.wait_recv()


def _dma_sems(n):
    return [pltpu.SemaphoreType.DMA((n,)), pltpu.SemaphoreType.DMA((n,))]


def _comm_params():
    return pltpu.CompilerParams(has_side_effects=True)


def _on_sequencer(exchange, refs, sem_types, peers_of, name, seq_id):
    def launch(*sems):
        x, y, c, chips = _place()
        barrier = pltpu.get_barrier_semaphore()
        peers = peers_of(x, y, c, chips)
        for peer in peers:
            pl.semaphore_signal(barrier, inc=1, device_id=peer, device_id_type=MESH)
        pl.semaphore_wait(barrier, len(peers))
        exchange(refs, *sems)

    pl.kernel(launch, mesh=plsc.ScalarSubcoreMesh(axis_name="seq", num_cores=1), name=name,
              scratch_types=tuple(sem_types), compiler_params=pltpu.CompilerParams(collective_id=seq_id))()


def _hbm_ref(a):
    return jax.new_ref(a, memory_space=pltpu.MemorySpace.HBM)


def _allgather_weights(bufs, sizes, name, seq_id=None):
    nw = len(sizes)

    def exchange(out, s_ici, r_ici, s_fwd, r_fwd):
        x, y, c, chips = _place()
        me, sib = (x, y, c), (x, y, 1 - c)
        q_me = 2 * x + y

        def rows(w, q):
            hw = sizes[w] // 2
            return out[w].at[pl.ds(q * sizes[w] + c * hw, hw)]

        def three(w):
            return out[w].at[pl.ds(0, 3 * (sizes[w] // 2))]

        for w in range(nw):
            for px, py in chips:
                _remote(rows(w, q_me), rows(w, q_me), s_ici.at[w], r_ici.at[w], (px, py, c)).start()
        for w in range(nw):
            _drain(three(w), s_ici.at[w], r_ici.at[w], me, recv=True)
            for px, py in chips:
                got = rows(w, 2 * px + py)
                _remote(got, got, s_fwd.at[w], r_fwd.at[w], sib).start()
        for w in range(nw):
            _drain(three(w), s_fwd.at[w], r_fwd.at[w], me, recv=True)
        for w in range(nw):
            _drain(three(w), s_ici.at[w], r_ici.at[w], me, send=True)
            _drain(three(w), s_fwd.at[w], r_fwd.at[w], me, send=True)

    if seq_id is not None:
        refs = [_hbm_ref(b) for b in bufs]
        _on_sequencer(exchange, refs, _dma_sems(nw) + _dma_sems(nw),
                      lambda x, y, c, chips: [(x, y, 1 - c)] + [(px, py, c) for px, py in chips], name, seq_id)
        return [r[...] for r in refs]

    def body(*refs):
        exchange(refs[nw:2 * nw], *refs[2 * nw:])

    return _pcall(body, name=name, in_specs=[ANY] * nw, out_specs=[ANY] * nw,
                  out_shape=[_S(b.shape, b.dtype) for b in bufs],
                  input_output_aliases={w: w for w in range(nw)},
                  scratch_shapes=_dma_sems(nw) + _dma_sems(nw), compiler_params=_comm_params())(*bufs)


def _sibling_only(x, y, c, chips):
    return [(x, y, 1 - c)]


def _rs_pair_exchange(grads, sizes, name, seq_id=None):
    C = grads[0].shape[1]
    nw = len(sizes)
    out_shape = [_S((2 * n, C), F32) for n in sizes]

    def exchange(refs, s_sem, r_sem):
        g, got = refs[:nw], refs[nw:2 * nw]
        x, y, c, _ = _place()
        me, sib = (x, y, c), (x, y, 1 - c)
        for w in range(nw):
            hw = sizes[w] // 2
            for q in range(N_CHIPS):
                _remote(g[w].at[pl.ds(q * sizes[w] + (1 - c) * hw, hw)], got[w].at[pl.ds(q * hw, hw)],
                        s_sem.at[w], r_sem.at[w], sib).start()
        for w in range(nw):
            _drain(got[w], s_sem.at[w], r_sem.at[w], me, send=True, recv=True)

    if seq_id is not None:
        gots = [jax.empty_ref(s, memory_space=pltpu.MemorySpace.HBM) for s in out_shape]
        _on_sequencer(exchange, [_hbm_ref(g) for g in grads] + gots, _dma_sems(nw), _sibling_only, name, seq_id)
        return [r[...] for r in gots]

    def body(*refs):
        exchange(refs[:2 * nw], *refs[2 * nw:])

    return _pcall(body, name=name, in_specs=[ANY] * nw, out_specs=[ANY] * nw, out_shape=out_shape,
                  scratch_shapes=_dma_sems(nw), compiler_params=_comm_params())(*grads)


def _rs_chip_exchange(sums, sizes, name, seq_id=None):
    C = sums[0].shape[1]
    nw = len(sizes)
    out_shape = [_S((3 * (n // 2), C), sums[0].dtype) for n in sizes]

    def exchange(refs, s_sem, r_sem):
        sm, got = refs[:nw], refs[nw:2 * nw]
        x, y, c, chips = _place()
        for w in range(nw):
            hw = sizes[w] // 2
            for j, (px, py) in enumerate(chips):
                _remote(sm[w].at[pl.ds((2 * px + py) * hw, hw)], got[w].at[pl.ds(j * hw, hw)],
                        s_sem.at[w], r_sem.at[w], (px, py, c)).start()
        for w in range(nw):
            _drain(got[w], s_sem.at[w], r_sem.at[w], (x, y, c), send=True, recv=True)

    if seq_id is not None:
        gots = [jax.empty_ref(s, memory_space=pltpu.MemorySpace.HBM) for s in out_shape]
        _on_sequencer(exchange, [_hbm_ref(s) for s in sums] + gots, _dma_sems(nw),
                      lambda x, y, c, chips: [(px, py, c) for px, py in chips], name, seq_id)
        return [r[...] for r in gots]

    def body(*refs):
        exchange(refs[:2 * nw], *refs[2 * nw:])

    return _pcall(body, name=name, in_specs=[ANY] * nw, out_specs=[ANY] * nw, out_shape=out_shape,
                  scratch_shapes=_dma_sems(nw), compiler_params=_comm_params())(*sums)


def _rs_share_halves(reds, sizes, name, seq_id=None):
    nw = len(sizes)

    def exchange(out, s_sem, r_sem):
        x, y, c, _ = _place()
        for w in range(nw):
            hw = sizes[w] // 2
            rows = out[w].at[pl.ds(c * hw, hw)]
            _remote(rows, rows, s_sem.at[w], r_sem.at[w], (x, y, 1 - c)).start()
        for w in range(nw):
            _drain(out[w].at[pl.ds(0, sizes[w] // 2)], s_sem.at[w], r_sem.at[w], (x, y, c), send=True, recv=True)

    if seq_id is not None:
        refs = [_hbm_ref(r) for r in reds]
        _on_sequencer(exchange, refs, _dma_sems(nw), _sibling_only, name, seq_id)
        return [r[...] for r in refs]

    def body(*refs):
        exchange(refs[nw:2 * nw], *refs[2 * nw:])

    return _pcall(body, name=name, in_specs=[ANY] * nw, out_specs=[ANY] * nw,
                  out_shape=[_S(r.shape, r.dtype) for r in reds],
                  input_output_aliases={w: w for w in range(nw)},
                  scratch_shapes=_dma_sems(nw), compiler_params=_comm_params())(*reds)


def _allreduce_small(part, name):
    R, C = part.shape

    def body(p_ref, o_ref, buf, s_sem, r_sem):
        x, y, c, _ = _place()
        my_id = 4 * x + 2 * y + c
        buf[my_id] = p_ref[...]
        cps = []
        for k in range(1, N_DEV):
            fx, fy, fc = (k >> 2) & 1, (k >> 1) & 1, k & 1
            peer = (x ^ fx, y ^ fy, c ^ fc)
            cps.append(_remote(p_ref, buf.at[my_id], s_sem.at[k - 1], r_sem.at[k - 1], peer))
        for cp in cps:
            cp.start()
        for cp in cps:
            cp.wait()
        acc = buf[0]
        for d in range(1, N_DEV):
            acc = acc + buf[d]
        o_ref[...] = acc

    vm = pl.BlockSpec(memory_space=pltpu.VMEM)
    return _pcall(body, name=name, in_specs=[vm], out_specs=vm, out_shape=_S((R, C), F32),
                  scratch_shapes=[pltpu.VMEM((N_DEV, R, C), F32)] + _dma_sems(N_DEV - 1),
                  compiler_params=pltpu.CompilerParams(has_side_effects=True, vmem_limit_bytes=VMEM_LIMIT))(part)


SHARD_STEPS = 4


def _own_shard_buffers(ws, l, group, place, name):
    nw = len(group)

    def body(p_ref, *refs):
        for (_, tr), i_ref, o_ref in zip(group, refs[:nw], refs[nw:]):
            v = i_ref[...]
            o_ref[...] = (v.T if tr else v).astype(o_ref.dtype)

    in_specs, out_specs, out_shape, sizes = [], [], [], []
    for wname, tr in group:
        _, K, n = ws[wname].shape
        in_specs.append(pl.BlockSpec((None, K // SHARD_STEPS, n), lambda i, p: (l, i, 0)))
        if tr:
            out_specs.append(pl.BlockSpec((n, K // SHARD_STEPS), lambda i, p: (p[1], i)))
            out_shape.append(_S((N_CHIPS * n, K), MXU))
            sizes.append(n)
        else:
            out_specs.append(pl.BlockSpec((K // SHARD_STEPS, n), lambda i, p: (p[1] * SHARD_STEPS + i, 0)))
            out_shape.append(_S((N_CHIPS * K, n), MXU))
            sizes.append(K)
    gs = pltpu.PrefetchScalarGridSpec(num_scalar_prefetch=1, grid=(SHARD_STEPS,), in_specs=in_specs, out_specs=out_specs)
    bufs = _pcall(body, name=name, grid_spec=gs, out_shape=out_shape,
                  compiler_params=_cp("parallel"))(place, *[ws[wname] for wname, _ in group])
    return list(bufs), sizes


def _after(xs, ys):
    return lax.optimization_barrier((xs, ys))[0]


def _small_rows(v):
    flat = v.reshape(-1)
    pad = (-flat.shape[0]) % 1024
    return jnp.pad(flat, (0, pad)).reshape(-1, 1024)


def _pack_small(vals):
    rows = [_small_rows(vals[n]) for n in SMALL]
    cat = jnp.concatenate(rows, axis=0)
    pad = (-cat.shape[0]) % 8
    return jnp.pad(cat, ((0, pad), (0, 0)))


def _unpack_small(packed, like):
    out, r = {}, 0
    for n in SMALL:
        size = like[n].size
        nr = -(-size // 1024)
        out[n] = packed[r:r + nr].reshape(-1)[:size].reshape(like[n].shape)
        r += nr
    return out


def _rope_tables(positions):
    inv_freq = ROPE_THETA ** (-jnp.arange(0, HEAD_DIM, 2, dtype=F32) / HEAD_DIM)
    ang = positions.astype(F32)[:, None] * inv_freq
    cos, sin = jnp.cos(ang), jnp.sin(ang)
    return jnp.concatenate([cos, cos, cos, cos], axis=1), jnp.concatenate([-sin, sin, -sin, sin], axis=1)


def _layer_fwd(l, x, memv, W, P, cos2, sin2):
    t = f"l{l}"
    sv = {"x0": x}
    sv["h1"] = _rms_fwd(x, P["ffn1_norm"], t + "_ffn1_norm")
    sv["a1"], sv["b1"], sv["s1"] = _ffn_up(sv["h1"], W["ffn1_w_gate"], W["ffn1_w_up"], t + "_ffn1_up")
    sv["x1"] = _mm([(sv["s1"], W["ffn1_w_down"])], nt=False, out_dtype=F32, res=x, res_scale=FFN_RES, name=t + "_ffn1_down")

    sv["h2"] = _rms_fwd(sv["x1"], P["mix_norm"], t + "_mix_norm")
    sv["q"], sv["k"], sv["v"], sv["u"] = _in_proj(sv["h2"], W["w_in"], cos2, sin2, t + "_in_proj")
    sv["oa"] = _swa_fwd(sv["q"], sv["k"], sv["v"], P["attn_sinks"], t + "_swa")
    sv["ob"], sv["mg"] = _pool_fwd(sv["u"], sv["oa"], P["pool_w"], P["pool_scale"], P["attn_out_norm"],
                                   P["pool_out_norm"], t + "_pool")
    sv["x2"] = _mm([(sv["mg"], W["w_out"])], nt=False, out_dtype=F32, res=sv["x1"], name=t + "_out_proj")

    sv["h3"] = _rms_fwd(sv["x2"], P["xattn_norm"], t + "_xattn_norm")
    sv["memn"] = _rms_fwd(memv, P["mem_norm"], t + "_mem_norm")
    sv["q3"] = _mm([(sv["h3"], W["xattn_wq"])], nt=False, out_dtype=MXU, name=t + "_xq")
    sv["kv"] = _mm([(sv["memn"], W["xattn_wkv"])], nt=True, out_dtype=MXU, name=t + "_xkv")
    sv["o3"] = _xattn_fwd(sv["q3"], sv["kv"], t + "_xattn")
    sv["x3"] = _mm([(sv["o3"], W["xattn_wo"])], nt=False, out_dtype=F32, res=sv["x2"], name=t + "_xo")

    sv["h4"] = _rms_fwd(sv["x3"], P["ffn2_norm"], t + "_ffn2_norm")
    sv["a2"], sv["b2"], sv["s2"] = _ffn_up(sv["h4"], W["ffn2_w_gate"], W["ffn2_w_up"], t + "_ffn2_up")
    x4 = _mm([(sv["s2"], W["ffn2_w_down"])], nt=False, out_dtype=F32, res=sv["x3"], res_scale=FFN_RES, name=t + "_ffn2_down")
    return x4, sv


def _ffn_bwd(t, dx, x_in, g, h, a, b, s, wgT, wuT, wd, behind=None, begin=None):
    d_wd = _mm_tn(s, dx, t + "_dwd", r_scale=FFN_RES)
    da, db = _ffn_mid_bwd(dx, wd, a, b, t + "_mid")
    if behind is not None:
        da = _after(da, behind)
    d_wg = _mm_tn(da, h, t + "_dwg")
    d_wu = _mm_tn(db, h, t + "_dwu")
    if begin is not None:
        da = _after(da, begin(d_wg, d_wu, d_wd))
    (dx_in,), (dg,) = _mm_norm_bwd([(da, wgT), (db, wuT)], [(x_in, g, dx)], nt=False, name=t + "_dh",
                                   tm=TOKEN_TILE // 2)
    return dx_in, dg, d_wg, d_wu, d_wd


def _layer_bwd(l, dx, sv, memv, W, P, cos2, sin2, begin_rest, begin_ffn1):
    t = f"l{l}b"
    GW, GP, marks = {}, {}, {}
    dx, GP["ffn2_norm"], GW["ffn2_w_gate"], GW["ffn2_w_up"], GW["ffn2_w_down"] = _ffn_bwd(
        t + "_ffn2", dx, sv["x3"], P["ffn2_norm"], sv["h4"], sv["a2"], sv["b2"], sv["s2"],
        W["ffn2_w_gate"], W["ffn2_w_up"], W["ffn2_w_down"])
    marks["ffn2"] = dx

    GW["xattn_wo"] = _mm_tn(sv["o3"], dx, t + "_dwo")
    do3 = _mm([(dx, W["xattn_wo"])], nt=True, out_dtype=MXU, name=t + "_do3")
    dq3, dkv = _xattn_bwd(sv["q3"], sv["kv"], do3, t + "_xattn")
    GW["xattn_wq"] = _mm_tn(sv["h3"], dq3, t + "_dwq")
    GW["xattn_wkv"] = _mm_tn(dkv, sv["memn"], t + "_dwkv")
    dmemn = _mm([(dkv, W["xattn_wkv"])], nt=False, out_dtype=F32, name=t + "_dmemn")
    _, GP["mem_norm"] = _rms_bwd(memv, P["mem_norm"], dmemn, t + "_mem_norm_bwd")
    (dx,), (GP["xattn_norm"],) = _mm_norm_bwd([(dq3, W["xattn_wq"])], [(sv["x2"], P["xattn_norm"], dx)], nt=True,
                                             name=t + "_dh3")
    marks["xattn"] = dx

    GW["w_out"] = _mm_tn(sv["mg"], dx, t + "_dwout")
    (doa, dob), (GP["attn_out_norm"], GP["pool_out_norm"]) = _mm_norm_bwd(
        [(dx, W["w_out"])], [(sv["oa"], P["attn_out_norm"], None), (sv["ob"], P["pool_out_norm"], None)], nt=True,
        name=t + "_dmg")
    du, GP["pool_w"], GP["pool_scale"] = _pool_bwd(sv["u"], dob, P["pool_w"], P["pool_scale"], t + "_pool")
    dq, dko, dkp, dvo, dvp, dsk = _swa_bwd(sv["q"], sv["k"], sv["v"], doa, P["attn_sinks"], t + "_swa")
    GP["attn_sinks"] = dsk[:, 0]
    dpj = _dproj(dq, dko, dkp, dvo, dvp, du, cos2, sin2, t + "_dproj")
    GW["w_in"] = _mm_tn(dpj, sv["h2"], t + "_dwin")
    (dx,), (GP["mix_norm"],) = _mm_norm_bwd([(dpj, W["w_in"])], [(sv["x1"], P["mix_norm"], dx)], nt=False,
                                           name=t + "_dh2")

    dx, GP["ffn1_norm"], _, _, _ = _ffn_bwd(
        t + "_ffn1", dx, sv["x0"], P["ffn1_norm"], sv["h1"], sv["a1"], sv["b1"], sv["s1"],
        W["ffn1_w_gate"], W["ffn1_w_up"], W["ffn1_w_down"], behind=begin_rest(GW), begin=begin_ffn1)
    return dx, GP, marks


def _reduce_begin(t, grads, sizes, place, seq_ids):
    gots = _rs_pair_exchange(grads, sizes, t + "_pair")
    sums = _pair_sum(grads, gots, sizes, place, t + "_pair_sum")
    recvs = _rs_chip_exchange(sums, sizes, t + "_chips", seq_id=next(seq_ids))
    return dict(t=t, grads=grads, gots=gots, sums=sums, recvs=recvs, sizes=sizes)


def _reduce_end(st, place, seq_ids, late=None):
    recvs = st["recvs"] if late is None else _after(st["recvs"], late)
    reds = _final_sum(st["grads"], st["gots"], recvs, st["sizes"], place, st["t"] + "_final_sum")
    return _rs_share_halves(reds, st["sizes"], st["t"] + "_share")


def _adamw_layer(l, w3, g, m3, v3, transposed, prev, name):
    _, K, n = w3.shape
    if transposed:
        tr = K // SHARD_STEPS
        g_spec = pl.BlockSpec((n, tr), lambda i: (0, i))
    else:
        tr = _rows_tile(K)
        g_spec = pl.BlockSpec((tr, n), lambda i: (i, 0))
    c1 = 1.0 / (1.0 - ADAM_B1 ** ADAM_STEP)
    c2 = 1.0 / (1.0 - ADAM_B2 ** ADAM_STEP)

    def body(w_ref, g_ref, m_ref, v_ref, *rest):
        go_ref, d_ref, nm_ref, nv_ref = rest[-4:]
        gv = g_ref[...].T if transposed else g_ref[...]
        nm = ADAM_B1 * m_ref[...] + (1.0 - ADAM_B1) * gv
        nv = ADAM_B2 * v_ref[...] + (1.0 - ADAM_B2) * (gv * gv)
        go_ref[...] = gv
        d_ref[...] = -ADAM_LR * ((nm * c1) / (jnp.sqrt(nv * c2) + ADAM_EPS) + ADAM_WD * w_ref[...])
        nm_ref[...] = nm
        nv_ref[...] = nv

    slab = pl.BlockSpec((None, tr, n), lambda i: (l, i, 0))
    in_specs, args, aliases = [slab, g_spec, slab, slab], [w3, g, m3, v3], {}
    if prev is not None:
        in_specs += [ANY] * 4
        args += list(prev)
        aliases = {4 + j: j for j in range(4)}
    return _pcall(body, name=name, grid=(K // tr,), in_specs=in_specs, out_specs=[slab] * 4,
                  out_shape=[_S(w3.shape, F32)] * 4, input_output_aliases=aliases,
                  compiler_params=_cp("parallel"))(*args)


def kernel(x, mem, positions, ffn1_norm, ffn1_w_gate, ffn1_w_up, ffn1_w_down, mix_norm, w_in, attn_sinks, pool_w, pool_scale, attn_out_norm, pool_out_norm, w_out, xattn_norm, mem_norm, xattn_wq, xattn_wkv, xattn_wo, ffn2_norm, ffn2_w_gate, ffn2_w_up, ffn2_w_down, final_norm, loss_target, m_ffn1_norm, m_ffn1_w_gate, m_ffn1_w_up, m_ffn1_w_down, m_mix_norm, m_w_in, m_attn_sinks, m_pool_w, m_pool_scale, m_attn_out_norm, m_pool_out_norm, m_w_out, m_xattn_norm, m_mem_norm, m_xattn_wq, m_xattn_wkv, m_xattn_wo, m_ffn2_norm, m_ffn2_w_gate, m_ffn2_w_up, m_ffn2_w_down, m_final_norm, v_ffn1_norm, v_ffn1_w_gate, v_ffn1_w_up, v_ffn1_w_down, v_mix_norm, v_w_in, v_attn_sinks, v_pool_w, v_pool_scale, v_attn_out_norm, v_pool_out_norm, v_w_out, v_xattn_norm, v_mem_norm, v_xattn_wq, v_xattn_wkv, v_xattn_wo, v_ffn2_norm, v_ffn2_w_gate, v_ffn2_w_up, v_ffn2_w_down, v_final_norm):
    ws = dict(ffn1_norm=ffn1_norm, ffn1_w_gate=ffn1_w_gate, ffn1_w_up=ffn1_w_up, ffn1_w_down=ffn1_w_down,
              mix_norm=mix_norm, w_in=w_in, attn_sinks=attn_sinks, pool_w=pool_w, pool_scale=pool_scale,
              attn_out_norm=attn_out_norm, pool_out_norm=pool_out_norm, w_out=w_out, xattn_norm=xattn_norm,
              mem_norm=mem_norm, xattn_wq=xattn_wq, xattn_wkv=xattn_wkv, xattn_wo=xattn_wo, ffn2_norm=ffn2_norm,
              ffn2_w_gate=ffn2_w_gate, ffn2_w_up=ffn2_w_up, ffn2_w_down=ffn2_w_down, final_norm=final_norm)
    ms = dict(ffn1_norm=m_ffn1_norm, ffn1_w_gate=m_ffn1_w_gate, ffn1_w_up=m_ffn1_w_up, ffn1_w_down=m_ffn1_w_down,
              mix_norm=m_mix_norm, w_in=m_w_in, attn_sinks=m_attn_sinks, pool_w=m_pool_w, pool_scale=m_pool_scale,
              attn_out_norm=m_attn_out_norm, pool_out_norm=m_pool_out_norm, w_out=m_w_out, xattn_norm=m_xattn_norm,
              mem_norm=m_mem_norm, xattn_wq=m_xattn_wq, xattn_wkv=m_xattn_wkv, xattn_wo=m_xattn_wo,
              ffn2_norm=m_ffn2_norm, ffn2_w_gate=m_ffn2_w_gate, ffn2_w_up=m_ffn2_w_up, ffn2_w_down=m_ffn2_w_down,
              final_norm=m_final_norm)
    vs = dict(ffn1_norm=v_ffn1_norm, ffn1_w_gate=v_ffn1_w_gate, ffn1_w_up=v_ffn1_w_up, ffn1_w_down=v_ffn1_w_down,
              mix_norm=v_mix_norm, w_in=v_w_in, attn_sinks=v_attn_sinks, pool_w=v_pool_w, pool_scale=v_pool_scale,
              attn_out_norm=v_attn_out_norm, pool_out_norm=v_pool_out_norm, w_out=v_w_out, xattn_norm=v_xattn_norm,
              mem_norm=v_mem_norm, xattn_wq=v_xattn_wq, xattn_wkv=v_xattn_wkv, xattn_wo=v_xattn_wo,
              ffn2_norm=v_ffn2_norm, ffn2_w_gate=v_ffn2_w_gate, ffn2_w_up=v_ffn2_w_up, ffn2_w_down=v_ffn2_w_down,
              final_norm=v_final_norm)
    depth = ffn1_norm.shape[0]
    T, D = x.shape[1], x.shape[2]
    xv = x.reshape(T, D)
    memv = mem.reshape(mem.shape[1], D)
    tgt = loss_target.reshape(T, D)
    cos2, sin2 = _rope_tables(positions.reshape(T))
    in_kernel = {name: tr and ws[name].shape[2] % LANES == 0 for name, tr in BIG}
    swapped = [name for name, tr in BIG if tr and not in_kernel[name]]
    rows = lambda d: {name: (jnp.swapaxes(d[name], 1, 2) if name in swapped else d[name]) for name, _ in BIG}
    wr, mr, vr = rows(ws), rows(ms), rows(vs)
    groups = [[(name, in_kernel[name]) for name, _ in g] for g in GROUPS]

    q_me = 2 * lax.axis_index("x") + lax.axis_index("y")
    place = jnp.stack([lax.axis_index("c"), q_me]).astype(jnp.int32)
    seq_ids = iter(range(1, 1 + 8 * depth))
    Ws, sizes, first = [dict() for _ in range(depth)], {}, None
    for l in range(depth):
        for gi, group in enumerate(groups):
            t = f"l{l}g{gi}"
            bufs, sizes[gi] = _own_shard_buffers(wr, l, group, place, t + "_shard")
            if first is None:
                full = first = _allgather_weights(bufs, sizes[gi], t + "_allgather")
            else:
                full = _allgather_weights(_after(bufs, first), sizes[gi], t + "_allgather", seq_id=next(seq_ids))
            Ws[l].update({name: f for (name, _), f in zip(group, full)})
    Ps = [{n: (ws[n][l].reshape(1, -1) if n != "pool_w" else ws[n][l]) for n in SMALL if n != "final_norm"}
          for l in range(depth)]

    saved = []
    h = xv
    for l in range(depth):
        h, sv = _layer_fwd(l, h, memv, Ws[l], Ps[l], cos2, sin2)
        saved.append(sv)
    loss_row, dx, d_final = _loss_head(h, tgt, final_norm.reshape(1, D), "loss_head")
    GPs, marks, begun = [None] * depth, [None] * depth, {}
    for l in reversed(range(depth)):
        def begin_rest(GW, l=l):
            begun[l, 1] = _reduce_begin(f"l{l}g1r", [GW[name] for name, _ in groups[1]], sizes[1], place, seq_ids)
            return begun[l, 1]["sums"]

        def begin_ffn1(*gs, l=l):
            begun[l, 0] = _reduce_begin(f"l{l}g0r", list(gs), sizes[0], place, seq_ids)
            return begun[l, 0]["sums"]

        dx, GPs[l], marks[l] = _layer_bwd(l, dx, saved[l], memv, Ws[l], Ps[l], cos2, sin2, begin_rest, begin_ffn1)

    stacked = {}
    for l in reversed(range(depth)):
        for gi, group in reversed(list(enumerate(groups))):
            if l > 0:
                late = marks[l - 1]["ffn2" if gi == 1 else "xattn"]
            else:
                late = dx if gi == 1 else None
            reds = _reduce_end(begun[l, gi], place, seq_ids, late)
            for (name, tr), red in zip(group, reds):
                stacked[name] = _adamw_layer(l, wr[name], red, mr[name], vr[name], tr, stacked.get(name),
                                             f"l{l}_adamw_{name}")
    for name in swapped:
        stacked[name] = [jnp.swapaxes(a, 1, 2) for a in stacked[name]]
    small_part = {n: jnp.stack([GPs[l][n].reshape(ws[n].shape[1:]) for l in range(depth)]) for n in SMALL if n != "final_norm"}
    small_part["final_norm"] = d_final.reshape(D)
    small_g = _unpack_small(_allreduce_small(_pack_small(small_part), "small_allreduce"), ws)
    loss = lax.psum(loss_row[0, 0], ("x", "y", "c"))

    grads, deltas, new_m, new_v = {}, {}, {}, {}
    for name, _ in BIG:
        grads[name], deltas[name], new_m[name], new_v[name] = stacked[name]
    d, nm, nv = _adamw(_pack_small(ws), _pack_small(small_g), _pack_small(ms), _pack_small(vs), "adamw_small")
    grads.update(small_g)
    deltas.update(_unpack_small(d, ws))
    new_m.update(_unpack_small(nm, ws))
    new_v.update(_unpack_small(nv, ws))

    grad_x = dx.reshape(x.shape)
    return (loss, grad_x, *[grads[n] for n in WEIGHTS], *[deltas[n] for n in WEIGHTS],
            *[new_m[n] for n in WEIGHTS], *[new_v[n] for n in WEIGHTS])
```

```python
import functools

import jax
import jax.numpy as jnp
from jax import lax
from jax.experimental import pallas as pl
from jax.experimental.pallas import tpu as pltpu
from jax.experimental.pallas import tpu_sc as plsc

F32 = jnp.float32
MXU = jnp.bfloat16

EPS = 1e-6
HEAD_DIM = 64
N_Q_HEADS = 8
N_KV_HEADS = 2
Q_PER_KV = N_Q_HEADS // N_KV_HEADS
ATTN_WIDTH = N_Q_HEADS * HEAD_DIM
KV_WIDTH = N_KV_HEADS * HEAD_DIM
BLOCK = 128
ROPE_THETA = 10000.0
POOL_WINDOWS = (2, 4, 8, 16)
POOL_GROUP = 128
POOL_WIDTH = len(POOL_WINDOWS) * POOL_GROUP
POOL_HALO = 16
X_HEADS = 4
X_HEAD_DIM = 256
FFN_RES = 0.5
NEG = -1e30
ADAM_LR = 0.001
ADAM_B1 = 0.9
ADAM_B2 = 0.999
ADAM_EPS = 1e-08
ADAM_WD = 0.01
ADAM_STEP = 10

N_CHIPS = 4
N_DEV = 8
V7X_VMEM_BYTES = 64 * 1024 * 1024
VMEM_LIMIT = V7X_VMEM_BYTES - 8 * 1024 * 1024
LANES = 128
TOKEN_TILE = 512
MESH = pl.DeviceIdType.MESH

BIG = (("ffn1_w_gate", True), ("ffn1_w_up", True), ("ffn1_w_down", False), ("w_in", True), ("w_out", False),
       ("xattn_wq", False), ("xattn_wkv", True), ("xattn_wo", False),
       ("ffn2_w_gate", True), ("ffn2_w_up", True), ("ffn2_w_down", False))
GROUPS = (BIG[:3], BIG[3:])
SMALL = ("ffn1_norm", "mix_norm", "attn_sinks", "pool_w", "pool_scale", "attn_out_norm", "pool_out_norm",
         "xattn_norm", "mem_norm", "ffn2_norm", "final_norm")
WEIGHTS = ("ffn1_norm", "ffn1_w_gate", "ffn1_w_up", "ffn1_w_down", "mix_norm", "w_in", "attn_sinks", "pool_w",
           "pool_scale", "attn_out_norm", "pool_out_norm", "w_out", "xattn_norm", "mem_norm", "xattn_wq",
           "xattn_wkv", "xattn_wo", "ffn2_norm", "ffn2_w_gate", "ffn2_w_up", "ffn2_w_down", "final_norm")


def _S(shape, dtype):
    return jax.ShapeDtypeStruct(tuple(shape), dtype)


def _pcall(body, **kw):
    return pl.pallas_call(body, **kw)


def _cp(*sem):
    return pltpu.CompilerParams(dimension_semantics=tuple(sem), vmem_limit_bytes=VMEM_LIMIT)


def _nt(a, b):
    return lax.dot_general(a, b, (((1,), (1,)), ((), ())), preferred_element_type=F32)


def _nn(a, b):
    return lax.dot_general(a, b, (((1,), (0,)), ((), ())), preferred_element_type=F32)


def _tn(a, b):
    return lax.dot_general(a, b, (((0,), (0,)), ((), ())), preferred_element_type=F32)


def _tile(n, want):
    t = min(n, want)
    assert n % t == 0, (n, want)
    return t


def _resident(shape):
    nd = len(shape)
    return pl.BlockSpec(tuple(shape), lambda *_: (0,) * nd)


def _rms_fwd(x, g, name):
    T, C = x.shape
    tm = _tile(T, TOKEN_TILE)

    def body(x_ref, g_ref, o_ref):
        xv = x_ref[...]
        r = lax.rsqrt(jnp.mean(xv * xv, axis=-1, keepdims=True) + EPS)
        o_ref[...] = (xv * r * g_ref[...]).astype(o_ref.dtype)

    return _pcall(body, name=name, grid=(T // tm,),
                  in_specs=[pl.BlockSpec((tm, C), lambda i: (i, 0)), _resident((1, C))],
                  out_specs=pl.BlockSpec((tm, C), lambda i: (i, 0)),
                  out_shape=_S((T, C), MXU), compiler_params=_cp("parallel"))(x, g)


def _rms_bwd(x, g, dh, name, dres=None, col=0):
    T, C = x.shape
    tm = _tile(T, TOKEN_TILE)

    def body(*refs):
        if dres is None:
            x_ref, g_ref, dh_ref, dx_ref, dg_ref = refs
        else:
            x_ref, g_ref, dh_ref, dres_ref, dx_ref, dg_ref = refs
        xv = x_ref[...]
        r = lax.rsqrt(jnp.mean(xv * xv, axis=-1, keepdims=True) + EPS)
        xh = xv * r
        dhv = dh_ref[...].astype(F32)
        dxn = dhv * g_ref[...]
        dx = r * (dxn - xh * jnp.mean(dxn * xh, axis=-1, keepdims=True))
        if dres is not None:
            dx = dx + dres_ref[...]
        dx_ref[...] = dx

        @pl.when(pl.program_id(0) == 0)
        def _():
            dg_ref[...] = jnp.zeros_like(dg_ref)

        dg_ref[...] += jnp.sum(dhv * xh, axis=0, keepdims=True)

    tok = pl.BlockSpec((tm, C), lambda i: (i, 0))
    in_specs = [tok, _resident((1, C)), pl.BlockSpec((tm, C), lambda i: (i, col))]
    args = [x, g, dh]
    if dres is not None:
        in_specs.append(tok)
        args.append(dres)
    return _pcall(body, name=name, grid=(T // tm,), in_specs=in_specs,
                  out_specs=[tok, _resident((1, C))],
                  out_shape=[_S((T, C), F32), _S((1, C), F32)], compiler_params=_cp("arbitrary"))(*args)


def _mm(pairs, *, nt, out_dtype, name, res=None, res_scale=1.0, a_scale=1.0, tm=TOKEN_TILE, norm_g=None):
    M = pairs[0][0].shape[0]
    N = pairs[0][1].shape[0] if nt else pairs[0][1].shape[1]
    tm = _tile(M, tm)
    n = len(pairs)

    def body(*refs):
        a_refs, w_refs = refs[:n], refs[n:2 * n]
        o_ref = refs[-1] if norm_g is None else refs[-2]
        acc = None
        for a_ref, w_ref in zip(a_refs, w_refs):
            a = a_ref[...]
            if a_scale != 1.0:
                a = a * a_scale
            a = a.astype(MXU)
            p = _nt(a, w_ref[...]) if nt else _nn(a, w_ref[...])
            acc = p if acc is None else acc + p
        if res is not None:
            acc = refs[2 * n][...] + res_scale * acc
        o_ref[...] = acc.astype(o_ref.dtype)
        if norm_g is not None:
            r = lax.rsqrt(jnp.mean(acc * acc, axis=-1, keepdims=True) + EPS)
            refs[-1][...] = (acc * r * refs[-3][...]).astype(MXU)

    tok = pl.BlockSpec((tm, N), lambda i: (i, 0))
    in_specs = [pl.BlockSpec((tm, a.shape[1]), lambda i: (i, 0)) for a, _ in pairs]
    in_specs += [_resident(w.shape) for _, w in pairs]
    args = [a for a, _ in pairs] + [w for _, w in pairs]
    if res is not None:
        in_specs.append(tok)
        args.append(res)
    if norm_g is None:
        return _pcall(body, name=name, grid=(M // tm,), in_specs=in_specs, out_specs=tok,
                      out_shape=_S((M, N), out_dtype), compiler_params=_cp("parallel"))(*args)
    return _pcall(body, name=name, grid=(M // tm,), in_specs=in_specs + [_resident(norm_g.shape)],
                  out_specs=[tok, tok], out_shape=[_S((M, N), out_dtype), _S((M, N), MXU)],
                  compiler_params=_cp("parallel"))(*args, norm_g)


def _mm_norm_bwd(pairs, norms, *, nt, name, tm=TOKEN_TILE):
    M = pairs[0][0].shape[0]
    tm = _tile(M, tm)
    n, k = len(pairs), len(norms)
    has_res = [d is not None for _, _, d in norms]

    def body(*refs):
        a_refs, w_refs = refs[:n], refs[n:2 * n]
        rest = list(refs[2 * n:])
        acc = None
        for a_ref, w_ref in zip(a_refs, w_refs):
            p = _nt(a_ref[...].astype(MXU), w_ref[...]) if nt else _nn(a_ref[...].astype(MXU), w_ref[...])
            acc = p if acc is None else acc + p
        ins = [(rest.pop(0), rest.pop(0), rest.pop(0) if has_res[j] else None) for j in range(k)]
        dx_refs, dg_refs = rest[:k], rest[k:]
        c0 = 0
        for (x_ref, g_ref, d_ref), dx_ref, dg_ref in zip(ins, dx_refs, dg_refs):
            xv = x_ref[...]
            dhv = acc[:, c0:c0 + xv.shape[1]]
            c0 += xv.shape[1]
            r = lax.rsqrt(jnp.mean(xv * xv, axis=-1, keepdims=True) + EPS)
            xh = xv * r
            dxn = dhv * g_ref[...]
            dx = r * (dxn - xh * jnp.mean(dxn * xh, axis=-1, keepdims=True))
            dx_ref[...] = dx if d_ref is None else dx + d_ref[...]

            @pl.when(pl.program_id(0) == 0)
            def _():
                dg_ref[...] = jnp.zeros_like(dg_ref)

            dg_ref[...] += jnp.sum(dhv * xh, axis=0, keepdims=True)

    tok = lambda c: pl.BlockSpec((tm, c), lambda i: (i, 0))
    in_specs = [tok(a.shape[1]) for a, _ in pairs] + [_resident(w.shape) for _, w in pairs]
    args = [a for a, _ in pairs] + [w for _, w in pairs]
    for x, g, d in norms:
        in_specs += [tok(x.shape[1]), _resident(g.shape)] + ([tok(x.shape[1])] if d is not None else [])
        args += [x, g] + ([d] if d is not None else [])
    out_specs = [tok(x.shape[1]) for x, _, _ in norms] + [_resident((1, x.shape[1])) for x, _, _ in norms]
    out_shape = [_S((M, x.shape[1]), F32) for x, _, _ in norms] + [_S((1, x.shape[1]), F32) for x, _, _ in norms]
    outs = _pcall(body, name=name, grid=(M // tm,), in_specs=in_specs, out_specs=out_specs, out_shape=out_shape,
                  compiler_params=_cp("arbitrary"))(*args)
    return list(outs[:k]), list(outs[k:])


def _mm_tn(l, r, name, *, l_scale=1.0, r_scale=1.0, tr=1408, tt=TOKEN_TILE):
    T, R = l.shape
    C = r.shape[1]
    tt = _tile(T, tt)
    tr = tr if R % tr == 0 else (1024 if R % 1024 == 0 and R > 1280 else R)

    def body(l_ref, r_ref, o_ref):
        lv, rv = l_ref[...], r_ref[...]
        if l_scale != 1.0:
            lv = lv * l_scale
        if r_scale != 1.0:
            rv = rv * r_scale
        lv, rv = lv.astype(MXU), rv.astype(MXU)

        @pl.when(pl.program_id(1) == 0)
        def _():
            o_ref[...] = jnp.zeros_like(o_ref)

        o_ref[...] += _tn(lv, rv)

    return _pcall(body, name=name, grid=(R // tr, T // tt),
                  in_specs=[pl.BlockSpec((tt, tr), lambda i, t: (t, i)), pl.BlockSpec((tt, C), lambda i, t: (t, 0))],
                  out_specs=pl.BlockSpec((tr, C), lambda i, t: (i, 0)),
                  out_shape=_S((R, C), F32), compiler_params=_cp("parallel", "arbitrary"))(l, r)


FFN_COL_TILE = 1408


def _sigmoid(a):
    return 0.5 * (jnp.tanh(0.5 * a) + 1.0)


def _ffn_up(h, wgT, wuT, name):
    T, D = h.shape
    Fd = wgT.shape[0]
    tm, tn = _tile(T, TOKEN_TILE), _tile(Fd, FFN_COL_TILE)

    def body(h_ref, wg_ref, wu_ref, a_ref, b_ref, s_ref):
        hv = h_ref[...]
        a = _nt(hv, wg_ref[...])
        b = _nt(hv, wu_ref[...])
        s = a * _sigmoid(a) * b
        a_ref[...] = a.astype(a_ref.dtype)
        b_ref[...] = b.astype(b_ref.dtype)
        s_ref[...] = s.astype(s_ref.dtype)

    wspec = pl.BlockSpec((tn, D), lambda j, i: (j, 0))
    ospec = pl.BlockSpec((tm, tn), lambda j, i: (i, j))
    return _pcall(body, name=name, grid=(Fd // tn, T // tm),
                  in_specs=[pl.BlockSpec((tm, D), lambda j, i: (i, 0)), wspec, wspec],
                  out_specs=[ospec, ospec, ospec], out_shape=[_S((T, Fd), MXU)] * 3,
                  compiler_params=_cp("parallel", "parallel"))(h, wgT, wuT)


def _ffn_mid_bwd(dx, wd, a, b, name):
    T, D = dx.shape
    Fd = wd.shape[0]
    tm, tn = _tile(T, TOKEN_TILE), _tile(Fd, FFN_COL_TILE)

    def body(dx_ref, wd_ref, a_ref, b_ref, da_ref, db_ref):
        dy = (dx_ref[...] * FFN_RES).astype(MXU)
        ds = _nt(dy, wd_ref[...])
        av, bv = a_ref[...].astype(F32), b_ref[...].astype(F32)
        sg = _sigmoid(av)
        da_ref[...] = (ds * bv * (sg * (1.0 + av * (1.0 - sg)))).astype(da_ref.dtype)
        db_ref[...] = (ds * (av * sg)).astype(db_ref.dtype)

    aspec = pl.BlockSpec((tm, tn), lambda j, i: (i, j))
    return _pcall(body, name=name, grid=(Fd // tn, T // tm),
                  in_specs=[pl.BlockSpec((tm, D), lambda j, i: (i, 0)), pl.BlockSpec((tn, D), lambda j, i: (j, 0)),
                            aspec, aspec],
                  out_specs=[aspec, aspec], out_shape=[_S((T, Fd), MXU)] * 2,
                  compiler_params=_cp("parallel", "parallel"))(dx, wd, a, b)


def _swap_halves(t):
    w = t.shape[1]
    lane = lax.broadcasted_iota(jnp.int32, t.shape, 1)
    first = (lane % HEAD_DIM) < (HEAD_DIM // 2)
    return jnp.where(first, pltpu.roll(t, w - HEAD_DIM // 2, 1), pltpu.roll(t, HEAD_DIM // 2, 1))


def _rope(t, cos2, sin2):
    reps = t.shape[1] // LANES
    c = jnp.tile(cos2, (1, reps)) if reps > 1 else cos2
    s = jnp.tile(sin2, (1, reps)) if reps > 1 else sin2
    return t * c + _swap_halves(t) * s


def _rope_bwd(dt, cos2, sin2):
    reps = dt.shape[1] // LANES
    c = jnp.tile(cos2, (1, reps)) if reps > 1 else cos2
    s = jnp.tile(sin2, (1, reps)) if reps > 1 else sin2
    return dt * c + _swap_halves(dt * s)


def _in_proj(h, winT, cos2, sin2, name):
    T, D = h.shape
    tm = _tile(T, TOKEN_TILE)
    qe, ke, ve = ATTN_WIDTH, ATTN_WIDTH + KV_WIDTH, ATTN_WIDTH + 2 * KV_WIDTH

    def body(h_ref, w_ref, c_ref, s_ref, q_ref, k_ref, v_ref, u_ref):
        proj = _nt(h_ref[...], w_ref[...])
        cv, sv = c_ref[...], s_ref[...]
        q_ref[...] = _rope(proj[:, :qe], cv, sv).astype(q_ref.dtype)
        k_ref[...] = _rope(proj[:, qe:ke], cv, sv).astype(k_ref.dtype)
        v_ref[...] = proj[:, ke:ve].astype(v_ref.dtype)
        u_ref[...] = proj[:, ve:]

    def tok(w):
        return pl.BlockSpec((tm, w), lambda i: (i, 0))

    return _pcall(body, name=name, grid=(T // tm,),
                  in_specs=[tok(D), _resident(winT.shape), tok(LANES), tok(LANES)],
                  out_specs=[tok(ATTN_WIDTH), tok(KV_WIDTH), tok(KV_WIDTH), tok(POOL_WIDTH)],
                  out_shape=[_S((T, ATTN_WIDTH), MXU), _S((T, KV_WIDTH), MXU), _S((T, KV_WIDTH), MXU),
                             _S((T, POOL_WIDTH), F32)],
                  compiler_params=_cp("parallel"))(h, winT, cos2, sin2)


SWA_TILE_BLOCKS = 4
SM_SCALE = HEAD_DIM ** -0.5


def _swa_bias(first_tile):
    cols = Q_PER_KV * BLOCK
    kj = lax.broadcasted_iota(jnp.int32, (2 * BLOCK, cols), 0)
    qi = lax.broadcasted_iota(jnp.int32, (2 * BLOCK, cols), 1) % BLOCK
    diff = qi + BLOCK - kj
    bias = jnp.where((diff >= 0) & (diff < BLOCK), 0.0, NEG)
    return bias, jnp.where(kj < jnp.where(first_tile, BLOCK, 0), NEG, bias)


def _swa_probs(kh, qs, sink_row, bias):
    s = _nt(kh, qs) + bias
    m = jnp.maximum(jnp.max(s, axis=0, keepdims=True), sink_row)
    e = jnp.exp(s - m)
    es = jnp.exp(sink_row - m)
    inv = 1.0 / (jnp.sum(e, axis=0, keepdims=True) + es)
    return e * inv, es * inv


def _sink_row(sinks_ref, kv):
    return jnp.concatenate([jnp.full((1, BLOCK), sinks_ref[0, kv * Q_PER_KV + g], F32) for g in range(Q_PER_KV)], axis=1)


def _stack_heads(t, kv):
    return jnp.concatenate([t[:, (kv * Q_PER_KV + g) * HEAD_DIM:(kv * Q_PER_KV + g + 1) * HEAD_DIM]
                            for g in range(Q_PER_KV)], axis=0)


def _swa_specs(T):
    tq = _tile(T, SWA_TILE_BLOCKS * BLOCK)
    nbt = tq // BLOCK
    cur = lambda w: pl.BlockSpec((tq, w), lambda i: (i, 0))
    prev = lambda w: pl.BlockSpec((BLOCK, w), lambda i: (jnp.maximum(i * nbt - 1, 0), 0))
    return tq, nbt, cur, prev


def _rows(b):
    return slice(b * BLOCK, (b + 1) * BLOCK)


def _swa_fwd(q, k, v, sinks, name):
    T = q.shape[0]
    tq, nbt, cur, prev = _swa_specs(T)

    def body(sinks_ref, q_ref, k_ref, kp_ref, v_ref, vp_ref, o_ref):
        bias, bias0 = _swa_bias(pl.program_id(0) == 0)
        sink = [_sink_row(sinks_ref, kv) for kv in range(N_KV_HEADS)]
        kx = jnp.concatenate([kp_ref[...], k_ref[...]], axis=0)
        vx = jnp.concatenate([vp_ref[...], v_ref[...]], axis=0)
        for b in range(nbt):
            qv = q_ref[_rows(b), :] * SM_SCALE
            kk, vv = kx[b * BLOCK:(b + 2) * BLOCK], vx[b * BLOCK:(b + 2) * BLOCK]
            for kv in range(N_KV_HEADS):
                hs = slice(kv * HEAD_DIM, (kv + 1) * HEAD_DIM)
                p, _ = _swa_probs(kk[:, hs], _stack_heads(qv, kv), sink[kv], bias0 if b == 0 else bias)
                o_t = _tn(vv[:, hs], p.astype(MXU))
                for g in range(Q_PER_KV):
                    c0 = (kv * Q_PER_KV + g) * HEAD_DIM
                    o_ref[_rows(b), c0:c0 + HEAD_DIM] = o_t[:, _rows(g)].T

    return _pcall(body, name=name, grid=(T // tq,),
                  in_specs=[pl.BlockSpec(memory_space=pltpu.SMEM), cur(ATTN_WIDTH), cur(KV_WIDTH), prev(KV_WIDTH),
                            cur(KV_WIDTH), prev(KV_WIDTH)],
                  out_specs=cur(ATTN_WIDTH), out_shape=_S((T, ATTN_WIDTH), F32),
                  compiler_params=_cp("parallel"))(sinks, q, k, k, v, v)


def _swa_bwd(q, k, v, do, sinks, name):
    T = q.shape[0]
    tq, nbt, cur, prev = _swa_specs(T)
    per_tile = lambda w: pl.BlockSpec((BLOCK, w), lambda i: (i, 0))

    def add(acc, t):
        return t if acc is None else acc + t

    def body(sinks_ref, q_ref, k_ref, kp_ref, v_ref, vp_ref, do_ref,
             dq_ref, dk_ref, dkp_ref, dv_ref, dvp_ref, dsk_ref):
        bias, bias0 = _swa_bias(pl.program_id(0) == 0)
        sink = [_sink_row(sinks_ref, kv) for kv in range(N_KV_HEADS)]
        kx = jnp.concatenate([kp_ref[...], k_ref[...]], axis=0)
        vx = jnp.concatenate([vp_ref[...], v_ref[...]], axis=0)

        @pl.when(pl.program_id(0) == 0)
        def _():
            dsk_ref[...] = jnp.zeros_like(dsk_ref)

        dk_acc, dv_acc = [None] * (nbt + 1), [None] * (nbt + 1)
        dsk_acc = [None] * N_Q_HEADS
        for b in range(nbt):
            qv, dov = q_ref[_rows(b), :] * SM_SCALE, do_ref[_rows(b), :].astype(MXU)
            kk, vv = kx[b * BLOCK:(b + 2) * BLOCK], vx[b * BLOCK:(b + 2) * BLOCK]
            dks, dvs = [], []
            for kv in range(N_KV_HEADS):
                hs = slice(kv * HEAD_DIM, (kv + 1) * HEAD_DIM)
                qs, dos = _stack_heads(qv, kv), _stack_heads(dov, kv)
                p, ps = _swa_probs(kk[:, hs], qs, sink[kv], bias0 if b == 0 else bias)
                dp = _nt(vv[:, hs], dos)
                delta = jnp.sum(p * dp, axis=0, keepdims=True)
                ds = (p * (dp - delta)).astype(MXU)
                dq_t = _tn(kk[:, hs], ds) * SM_SCALE
                dks.append(_nn(ds, qs))
                dvs.append(_nn(p.astype(MXU), dos))
                dsink = -ps * delta
                for g in range(Q_PER_KV):
                    h = kv * Q_PER_KV + g
                    dq_ref[_rows(b), h * HEAD_DIM:(h + 1) * HEAD_DIM] = dq_t[:, _rows(g)].T
                    dsk_acc[h] = add(dsk_acc[h], jnp.sum(dsink[:, _rows(g)], axis=1, keepdims=True))
            dk, dv = jnp.concatenate(dks, axis=1), jnp.concatenate(dvs, axis=1)
            dk_acc[b], dk_acc[b + 1] = add(dk_acc[b], dk[:BLOCK]), add(dk_acc[b + 1], dk[BLOCK:])
            dv_acc[b], dv_acc[b + 1] = add(dv_acc[b], dv[:BLOCK]), add(dv_acc[b + 1], dv[BLOCK:])
        dkp_ref[...], dvp_ref[...] = dk_acc[0], dv_acc[0]
        dk_ref[...] = jnp.concatenate(dk_acc[1:], axis=0)
        dv_ref[...] = jnp.concatenate(dv_acc[1:], axis=0)
        for h in range(N_Q_HEADS):
            dsk_ref[h:h + 1, :] += jnp.broadcast_to(dsk_acc[h], (1, LANES))

    kvs, kvp = _S((T, KV_WIDTH), F32), _S((T // tq * BLOCK, KV_WIDTH), F32)
    return _pcall(body, name=name, grid=(T // tq,),
                  in_specs=[pl.BlockSpec(memory_space=pltpu.SMEM), cur(ATTN_WIDTH), cur(KV_WIDTH), prev(KV_WIDTH),
                            cur(KV_WIDTH), prev(KV_WIDTH), cur(ATTN_WIDTH)],
                  out_specs=[cur(ATTN_WIDTH), cur(KV_WIDTH), per_tile(KV_WIDTH), cur(KV_WIDTH), per_tile(KV_WIDTH),
                             _resident((N_Q_HEADS, LANES))],
                  out_shape=[_S((T, ATTN_WIDTH), F32), kvs, kvp, kvs, kvp, _S((N_Q_HEADS, LANES), F32)],
                  compiler_params=_cp("arbitrary"))(sinks, q, k, k, v, v, do)


def _dproj(dq, dk, dkp, dv, dvp, du, cos2, sin2, name):
    T = dq.shape[0]
    tq, nbt, cur, _ = _swa_specs(T)
    nt = T // tq
    nxt = lambda w: pl.BlockSpec((BLOCK, w), lambda i: (jnp.minimum(i + 1, nt - 1), 0))

    def body(dq_ref, dk_ref, dkp_ref, dv_ref, dvp_ref, du_ref, c_ref, s_ref, o_ref):
        more = (pl.program_id(0) < nt - 1).astype(F32)
        cv, sv = c_ref[...], s_ref[...]

        def whole(t_ref, p_ref):
            t, last = t_ref[...], t_ref[tq - BLOCK:, :] + more * p_ref[...]
            return last if nbt == 1 else jnp.concatenate([t[:tq - BLOCK], last], axis=0)

        o_ref[...] = jnp.concatenate(
            [_rope_bwd(dq_ref[...], cv, sv), _rope_bwd(whole(dk_ref, dkp_ref), cv, sv), whole(dv_ref, dvp_ref),
             du_ref[...]], axis=1).astype(o_ref.dtype)

    width = ATTN_WIDTH + 2 * KV_WIDTH + POOL_WIDTH
    return _pcall(body, name=name, grid=(nt,),
                  in_specs=[cur(ATTN_WIDTH), cur(KV_WIDTH), nxt(KV_WIDTH), cur(KV_WIDTH), nxt(KV_WIDTH),
                            cur(POOL_WIDTH), cur(LANES), cur(LANES)],
                  out_specs=cur(width), out_shape=_S((T, width), MXU),
                  compiler_params=_cp("parallel"))(dq, dk, dkp, dv, dvp, du, cos2, sin2)


def _pool_specs(T):
    tm = _tile(T, TOKEN_TILE)
    hb = tm // POOL_HALO
    nh = T // POOL_HALO
    tok = lambda w: pl.BlockSpec((tm, w), lambda i: (i, 0))
    before = pl.BlockSpec((POOL_HALO, POOL_WIDTH), lambda i: (jnp.maximum(i * hb - 1, 0), 0))
    after = pl.BlockSpec((POOL_HALO, POOL_WIDTH), lambda i: (jnp.minimum((i + 1) * hb, nh - 1), 0))
    return tm, tok, before, after


def _window_counts(i, tm, rows, w):
    t = i * tm + lax.broadcasted_iota(jnp.int32, (rows, 1), 0)
    return jnp.minimum(t + 1, w).astype(F32)


def _pooled(u_ext, i, tm):
    out = []
    for g, w in enumerate(POOL_WINDOWS):
        acc = u_ext[:, g * POOL_GROUP:(g + 1) * POOL_GROUP]
        tok = acc[POOL_HALO:, :]
        sh = 1
        while sh < w:
            acc = acc + pltpu.roll(acc, sh, 0)
            sh *= 2
        out.append(acc[POOL_HALO:, :] / _window_counts(i, tm, tm, w) - tok)
    return out


def _pool_fwd(u, out_a, pool_w, pool_scale, ga, gb, name):
    T = u.shape[0]
    tm, tok, before, _ = _pool_specs(T)

    def body(u_ref, halo_ref, oa_ref, pw_ref, sc_ref, ga_ref, gb_ref, ob_ref, mg_ref):
        i = pl.program_id(0)
        halo = halo_ref[...] * (i > 0).astype(F32)
        pooled = _pooled(jnp.concatenate([halo, u_ref[...]], axis=0), i, tm)
        mixed = [_nn(pooled[g].astype(MXU), pw_ref[g].astype(MXU)) for g in range(len(POOL_WINDOWS))]
        ob = jnp.concatenate(mixed, axis=1) * sc_ref[...]
        ob_ref[...] = ob
        oa = oa_ref[...]
        ra = lax.rsqrt(jnp.mean(oa * oa, axis=-1, keepdims=True) + EPS)
        rb = lax.rsqrt(jnp.mean(ob * ob, axis=-1, keepdims=True) + EPS)
        mg_ref[...] = jnp.concatenate([oa * ra * ga_ref[...], ob * rb * gb_ref[...]], axis=1).astype(mg_ref.dtype)

    vec = _resident((1, POOL_WIDTH))
    return _pcall(body, name=name, grid=(T // tm,),
                  in_specs=[tok(POOL_WIDTH), before, tok(ATTN_WIDTH), _resident(pool_w.shape), vec, vec, vec],
                  out_specs=[tok(POOL_WIDTH), tok(ATTN_WIDTH + POOL_WIDTH)],
                  out_shape=[_S((T, POOL_WIDTH), F32), _S((T, ATTN_WIDTH + POOL_WIDTH), MXU)],
                  compiler_params=_cp("parallel"))(u, u, out_a, pool_w, pool_scale, ga, gb)


def _pool_bwd(u, dob, pool_w, pool_scale, name):
    T = u.shape[0]
    tm, tok, before, after = _pool_specs(T)
    nt = T // tm
    G = len(POOL_WINDOWS)

    def body(u_ref, halo_ref, dob_ref, dnext_ref, pw_ref, sc_ref, du_ref, dpw_ref, dsc_ref):
        i = pl.program_id(0)
        halo = halo_ref[...] * (i > 0).astype(F32)
        pooled = _pooled(jnp.concatenate([halo, u_ref[...]], axis=0), i, tm)
        dnext = dnext_ref[...] * (i < nt - 1).astype(F32)
        dext = jnp.concatenate([dob_ref[...], dnext], axis=0) * sc_ref[...]

        @pl.when(i == 0)
        def _():
            dpw_ref[...] = jnp.zeros_like(dpw_ref)
            dsc_ref[...] = jnp.zeros_like(dsc_ref)

        dus, dscs = [], []
        for g, w in enumerate(POOL_WINDOWS):
            gs = slice(g * POOL_GROUP, (g + 1) * POOL_GROUP)
            pw = pw_ref[g].astype(MXU)
            pg = pooled[g].astype(MXU)
            dmix = dext[:, gs].astype(MXU)
            dscs.append(jnp.sum(dob_ref[:, gs] * _nn(pg, pw), axis=0, keepdims=True))
            dpw_ref[g] += _tn(pg, dmix[:tm, :])
            dpooled = _nt(dmix, pw)
            acc = dpooled / _window_counts(i, tm, tm + POOL_HALO, w)
            sh = 1
            while sh < w:
                acc = acc + pltpu.roll(acc, tm + POOL_HALO - sh, 0)
                sh *= 2
            dus.append(acc[:tm, :] - dpooled[:tm, :])
        du_ref[...] = jnp.concatenate(dus, axis=1)
        dsc_ref[...] += jnp.concatenate(dscs, axis=1)

    vec = _resident((1, POOL_WIDTH))
    return _pcall(body, name=name, grid=(nt,),
                  in_specs=[tok(POOL_WIDTH), before, tok(POOL_WIDTH), after, _resident(pool_w.shape), vec],
                  out_specs=[tok(POOL_WIDTH), _resident(pool_w.shape), vec],
                  out_shape=[_S((T, POOL_WIDTH), F32), _S(pool_w.shape, F32), _S((1, POOL_WIDTH), F32)],
                  compiler_params=_cp("arbitrary"))(u, u, dob, dob, pool_w, pool_scale)


def _xattn_probs(qh, kh):
    s = _nt(qh, kh) * (X_HEAD_DIM ** -0.5)
    e = jnp.exp(s - jnp.max(s, axis=1, keepdims=True))
    return e / jnp.sum(e, axis=1, keepdims=True)


def _xattn_fwd(q, kvm, name):
    T, XW = q.shape
    tm = _tile(T, TOKEN_TILE)

    def body(q_ref, kv_ref, o_ref):
        for h in range(X_HEADS):
            hs = slice(h * X_HEAD_DIM, (h + 1) * X_HEAD_DIM)
            vs = slice(XW + h * X_HEAD_DIM, XW + (h + 1) * X_HEAD_DIM)
            p = _xattn_probs(q_ref[:, hs], kv_ref[:, hs])
            o_ref[:, hs] = _nn(p.astype(MXU), kv_ref[:, vs]).astype(o_ref.dtype)

    return _pcall(body, name=name, grid=(T // tm,),
                  in_specs=[pl.BlockSpec((tm, XW), lambda i: (i, 0)), _resident(kvm.shape)],
                  out_specs=pl.BlockSpec((tm, XW), lambda i: (i, 0)), out_shape=_S((T, XW), MXU),
                  compiler_params=_cp("parallel"))(q, kvm)


def _xattn_bwd(q, kvm, do, name):
    T, XW = q.shape
    tm = _tile(T, TOKEN_TILE)

    def body(q_ref, kv_ref, do_ref, dq_ref, dkv_ref):
        @pl.when(pl.program_id(0) == 0)
        def _():
            dkv_ref[...] = jnp.zeros_like(dkv_ref)

        for h in range(X_HEADS):
            hs = slice(h * X_HEAD_DIM, (h + 1) * X_HEAD_DIM)
            vs = slice(XW + h * X_HEAD_DIM, XW + (h + 1) * X_HEAD_DIM)
            qh, doh = q_ref[:, hs], do_ref[:, hs]
            p = _xattn_probs(qh, kv_ref[:, hs])
            dp = _nt(doh, kv_ref[:, vs])
            ds = (p * (dp - jnp.sum(p * dp, axis=1, keepdims=True)) * (X_HEAD_DIM ** -0.5)).astype(MXU)
            dq_ref[:, hs] = _nn(ds, kv_ref[:, hs]).astype(dq_ref.dtype)
            dkv_ref[:, hs] += _tn(ds, qh)
            dkv_ref[:, vs] += _tn(p.astype(MXU), doh)

    tok = pl.BlockSpec((tm, XW), lambda i: (i, 0))
    return _pcall(body, name=name, grid=(T // tm,),
                  in_specs=[tok, _resident(kvm.shape), tok],
                  out_specs=[tok, _resident(kvm.shape)],
                  out_shape=[_S((T, XW), MXU), _S(kvm.shape, F32)],
                  compiler_params=_cp("arbitrary"))(q, kvm, do)


def _loss_head(x, tgt, g, name):
    T, D = x.shape
    tm = _tile(T, TOKEN_TILE)

    def body(x_ref, t_ref, g_ref, loss_ref, dx_ref, dg_ref):
        xv, gv = x_ref[...], g_ref[...]
        r = lax.rsqrt(jnp.mean(xv * xv, axis=-1, keepdims=True) + EPS)
        xh = xv * r
        e = xh * gv - t_ref[...]
        dy = e * (1.0 / D)
        dxn = dy * gv
        dx_ref[...] = r * (dxn - xh * jnp.mean(dxn * xh, axis=-1, keepdims=True))

        @pl.when(pl.program_id(0) == 0)
        def _():
            loss_ref[...] = jnp.zeros_like(loss_ref)
            dg_ref[...] = jnp.zeros_like(dg_ref)

        part = jnp.sum(jnp.sum(e * e, axis=1, keepdims=True), axis=0, keepdims=True) * (0.5 / D)
        loss_ref[...] += jnp.broadcast_to(part, (1, LANES))
        dg_ref[...] += jnp.sum(dy * xh, axis=0, keepdims=True)

    tok = pl.BlockSpec((tm, D), lambda i: (i, 0))
    return _pcall(body, name=name, grid=(T // tm,),
                  in_specs=[tok, tok, _resident((1, D))],
                  out_specs=[_resident((1, LANES)), tok, _resident((1, D))],
                  out_shape=[_S((1, LANES), F32), _S((T, D), F32), _S((1, D), F32)],
                  compiler_params=_cp("arbitrary"))(x, tgt, g)


def _rows_tile(rows):
    for t in (512, 416, 352, 256, 128, 64, 32, 16, 8):
        if rows % t == 0:
            return t
    return rows


def _pair_sum(grads, gots, sizes, place, name):
    nw = len(sizes)
    C = grads[0].shape[1]

    def body(p_ref, *refs):
        g, got, out = refs[:nw], refs[nw:2 * nw], refs[2 * nw:]
        for w in range(nw):
            out[w][...] = (g[w][...] + got[w][...]).astype(out[w].dtype)

    def blk(w):
        return (sizes[w] // 4, C)

    in_specs = [pl.BlockSpec(blk(w), lambda q, s, p: (4 * q + 2 * p[0] + s, 0)) for w in range(nw)]
    in_specs += [pl.BlockSpec(blk(w), lambda q, s, p: (2 * q + s, 0)) for w in range(nw)]
    out_specs = [pl.BlockSpec(blk(w), lambda q, s, p: (2 * q + s, 0)) for w in range(nw)]
    gs = pltpu.PrefetchScalarGridSpec(num_scalar_prefetch=1, grid=(N_CHIPS, 2), in_specs=in_specs, out_specs=out_specs)
    return _pcall(body, name=name, grid_spec=gs, out_shape=[_S((2 * n, C), MXU) for n in sizes],
                  compiler_params=_cp("parallel", "parallel"))(place, *grads, *gots)


def _final_sum(grads, gots, recvs, sizes, place, name):
    nw = len(sizes)
    C = grads[0].shape[1]

    def body(p_ref, *refs):
        g, got, rv, out = refs[:nw], refs[nw:2 * nw], refs[2 * nw:5 * nw], refs[5 * nw:]
        for w in range(nw):
            acc = g[w][...] + got[w][...]
            for j in range(3):
                acc = acc + rv[3 * w + j][...].astype(F32)
            out[w][...] = acc

    def blk(w):
        return (sizes[w] // 4, C)

    in_specs = [pl.BlockSpec(blk(w), lambda s, p: (4 * p[1] + 2 * p[0] + s, 0)) for w in range(nw)]
    in_specs += [pl.BlockSpec(blk(w), lambda s, p: (2 * p[1] + s, 0)) for w in range(nw)]
    args = list(grads) + list(gots)
    for w in range(nw):
        for j in range(3):
            in_specs.append(pl.BlockSpec(blk(w), lambda s, p, j=j: (2 * j + s, 0)))
            args.append(recvs[w])
    out_specs = [pl.BlockSpec(blk(w), lambda s, p: (2 * p[0] + s, 0)) for w in range(nw)]
    gs = pltpu.PrefetchScalarGridSpec(num_scalar_prefetch=1, grid=(2,), in_specs=in_specs, out_specs=out_specs)
    return _pcall(body, name=name, grid_spec=gs, out_shape=[_S((n, C), F32) for n in sizes],
                  compiler_params=_cp("parallel"))(place, *args)


def _adamw(w, g, m, v, name):
    R, C = w.shape
    tr = _rows_tile(R)
    c1 = 1.0 / (1.0 - ADAM_B1 ** ADAM_STEP)
    c2 = 1.0 / (1.0 - ADAM_B2 ** ADAM_STEP)

    def body(w_ref, g_ref, m_ref, v_ref, d_ref, nm_ref, nv_ref):
        gv = g_ref[...]
        nm = ADAM_B1 * m_ref[...] + (1.0 - ADAM_B1) * gv
        nv = ADAM_B2 * v_ref[...] + (1.0 - ADAM_B2) * (gv * gv)
        d_ref[...] = -ADAM_LR * ((nm * c1) / (jnp.sqrt(nv * c2) + ADAM_EPS) + ADAM_WD * w_ref[...])
        nm_ref[...] = nm
        nv_ref[...] = nv

    spec = pl.BlockSpec((tr, C), lambda i: (i, 0))
    return _pcall(body, name=name, grid=(R // tr,), in_specs=[spec] * 4, out_specs=[spec] * 3,
                  out_shape=[_S((R, C), F32)] * 3, compiler_params=_cp("parallel"))(w, g, m, v)


ANY = pl.BlockSpec(memory_space=pl.ANY)


def _place():
    x, y, c = lax.axis_index("x"), lax.axis_index("y"), lax.axis_index("c")
    chips = [(1 - x, y), (x, 1 - y), (1 - x, 1 - y)]
    return x, y, c, chips


def _remote(src, dst, send_sem, recv_sem, dev):
    return pltpu.make_async_remote_copy(src_ref=src, dst_ref=dst, send_sem=send_sem, recv_sem=recv_sem,
                                        device_id=dev, device_id_type=MESH)


def _drain(like, send_sem, recv_sem, me, *, send=False, recv=False):
    d = _remote(like, like, send_sem, recv_sem, me)
    if send:
        d.wait_send()
    if recv:
        d.wait_recv()


def _dma_sems(n):
    return [pltpu.SemaphoreType.DMA((n,)), pltpu.SemaphoreType.DMA((n,))]


def _comm_params():
    return pltpu.CompilerParams(has_side_effects=True)


def _on_sequencer(exchange, refs, sem_types, peers_of, name, seq_id):
    def launch(*sems):
        x, y, c, chips = _place()
        barrier = pltpu.get_barrier_semaphore()
        peers = peers_of(x, y, c, chips)
        for peer in peers:
            pl.semaphore_signal(barrier, inc=1, device_id=peer, device_id_type=MESH)
        pl.semaphore_wait(barrier, len(peers))
        exchange(refs, *sems)

    pl.kernel(launch, mesh=plsc.ScalarSubcoreMesh(axis_name="seq", num_cores=1), name=name,
              scratch_types=tuple(sem_types), compiler_params=pltpu.CompilerParams(collective_id=seq_id))()


def _hbm_ref(a):
    return jax.new_ref(a, memory_space=pltpu.MemorySpace.HBM)


def _allgather_weights(bufs, sizes, name, seq_id=None):
    nw = len(sizes)

    def exchange(out, s_ici, r_ici, s_fwd, r_fwd):
        x, y, c, chips = _place()
        me, sib = (x, y, c), (x, y, 1 - c)
        q_me = 2 * x + y

        def rows(w, q):
            hw = sizes[w] // 2
            return out[w].at[pl.ds(q * sizes[w] + c * hw, hw)]

        def three(w):
            return out[w].at[pl.ds(0, 3 * (sizes[w] // 2))]

        for w in range(nw):
            for px, py in chips:
                _remote(rows(w, q_me), rows(w, q_me), s_ici.at[w], r_ici.at[w], (px, py, c)).start()
        for w in range(nw):
            _drain(three(w), s_ici.at[w], r_ici.at[w], me, recv=True)
            for px, py in chips:
                got = rows(w, 2 * px + py)
                _remote(got, got, s_fwd.at[w], r_fwd.at[w], sib).start()
        for w in range(nw):
            _drain(three(w), s_fwd.at[w], r_fwd.at[w], me, recv=True)
        for w in range(nw):
            _drain(three(w), s_ici.at[w], r_ici.at[w], me, send=True)
            _drain(three(w), s_fwd.at[w], r_fwd.at[w], me, send=True)

    if seq_id is not None:
        refs = [_hbm_ref(b) for b in bufs]
        _on_sequencer(exchange, refs, _dma_sems(nw) + _dma_sems(nw),
                      lambda x, y, c, chips: [(x, y, 1 - c)] + [(px, py, c) for px, py in chips], name, seq_id)
        return [r[...] for r in refs]

    def body(*refs):
        exchange(refs[nw:2 * nw], *refs[2 * nw:])

    return _pcall(body, name=name, in_specs=[ANY] * nw, out_specs=[ANY] * nw,
                  out_shape=[_S(b.shape, b.dtype) for b in bufs],
                  input_output_aliases={w: w for w in range(nw)},
                  scratch_shapes=_dma_sems(nw) + _dma_sems(nw), compiler_params=_comm_params())(*bufs)


def _sibling_only(x, y, c, chips):
    return [(x, y, 1 - c)]


def _rs_pair_exchange(grads, sizes, name, seq_id=None):
    C = grads[0].shape[1]
    nw = len(sizes)
    out_shape = [_S((2 * n, C), F32) for n in sizes]

    def exchange(refs, s_sem, r_sem):
        g, got = refs[:nw], refs[nw:2 * nw]
        x, y, c, _ = _place()
        me, sib = (x, y, c), (x, y, 1 - c)
        for w in range(nw):
            hw = sizes[w] // 2
            for q in range(N_CHIPS):
                _remote(g[w].at[pl.ds(q * sizes[w] + (1 - c) * hw, hw)], got[w].at[pl.ds(q * hw, hw)],
                        s_sem.at[w], r_sem.at[w], sib).start()
        for w in range(nw):
            _drain(got[w], s_sem.at[w], r_sem.at[w], me, send=True, recv=True)

    if seq_id is not None:
        gots = [jax.empty_ref(s, memory_space=pltpu.MemorySpace.HBM) for s in out_shape]
        _on_sequencer(exchange, [_hbm_ref(g) for g in grads] + gots, _dma_sems(nw), _sibling_only, name, seq_id)
        return [r[...] for r in gots]

    def body(*refs):
        exchange(refs[:2 * nw], *refs[2 * nw:])

    return _pcall(body, name=name, in_specs=[ANY] * nw, out_specs=[ANY] * nw, out_shape=out_shape,
                  scratch_shapes=_dma_sems(nw), compiler_params=_comm_params())(*grads)


def _rs_chip_exchange(sums, sizes, name, seq_id=None):
    C = sums[0].shape[1]
    nw = len(sizes)
    out_shape = [_S((3 * (n // 2), C), sums[0].dtype) for n in sizes]

    def exchange(refs, s_sem, r_sem):
        sm, got = refs[:nw], refs[nw:2 * nw]
        x, y, c, chips = _place()
        for w in range(nw):
            hw = sizes[w] // 2
            for j, (px, py) in enumerate(chips):
                _remote(sm[w].at[pl.ds((2 * px + py) * hw, hw)], got[w].at[pl.ds(j * hw, hw)],
                        s_sem.at[w], r_sem.at[w], (px, py, c)).start()
        for w in range(nw):
            _drain(got[w], s_sem.at[w], r_sem.at[w], (x, y, c), send=True, recv=True)

    if seq_id is not None:
        gots = [jax.empty_ref(s, memory_space=pltpu.MemorySpace.HBM) for s in out_shape]
        _on_sequencer(exchange, [_hbm_ref(s) for s in sums] + gots, _dma_sems(nw),
                      lambda x, y, c, chips: [(px, py, c) for px, py in chips], name, seq_id)
        return [r[...] for r in gots]

    def body(*refs):
        exchange(refs[:2 * nw], *refs[2 * nw:])

    return _pcall(body, name=name, in_specs=[ANY] * nw, out_specs=[ANY] * nw, out_shape=out_shape,
                  scratch_shapes=_dma_sems(nw), compiler_params=_comm_params())(*sums)


def _rs_share_halves(reds, sizes, name, seq_id=None):
    nw = len(sizes)

    def exchange(out, s_sem, r_sem):
        x, y, c, _ = _place()
        for w in range(nw):
            hw = sizes[w] // 2
            rows = out[w].at[pl.ds(c * hw, hw)]
            _remote(rows, rows, s_sem.at[w], r_sem.at[w], (x, y, 1 - c)).start()
        for w in range(nw):
            _drain(out[w].at[pl.ds(0, sizes[w] // 2)], s_sem.at[w], r_sem.at[w], (x, y, c), send=True, recv=True)

    if seq_id is not None:
        refs = [_hbm_ref(r) for r in reds]
        _on_sequencer(exchange, refs, _dma_sems(nw), _sibling_only, name, seq_id)
        return [r[...] for r in refs]

    def body(*refs):
        exchange(refs[nw:2 * nw], *refs[2 * nw:])

    return _pcall(body, name=name, in_specs=[ANY] * nw, out_specs=[ANY] * nw,
                  out_shape=[_S(r.shape, r.dtype) for r in reds],
                  input_output_aliases={w: w for w in range(nw)},
                  scratch_shapes=_dma_sems(nw), compiler_params=_comm_params())(*reds)


def _allreduce_small(part, name):
    R, C = part.shape

    def body(p_ref, o_ref, buf, s_sem, r_sem):
        x, y, c, _ = _place()
        my_id = 4 * x + 2 * y + c
        buf[my_id] = p_ref[...]
        cps = []
        for k in range(1, N_DEV):
            fx, fy, fc = (k >> 2) & 1, (k >> 1) & 1, k & 1
            peer = (x ^ fx, y ^ fy, c ^ fc)
            cps.append(_remote(p_ref, buf.at[my_id], s_sem.at[k - 1], r_sem.at[k - 1], peer))
        for cp in cps:
            cp.start()
        for cp in cps:
            cp.wait()
        acc = buf[0]
        for d in range(1, N_DEV):
            acc = acc + buf[d]
        o_ref[...] = acc

    vm = pl.BlockSpec(memory_space=pltpu.VMEM)
    return _pcall(body, name=name, in_specs=[vm], out_specs=vm, out_shape=_S((R, C), F32),
                  scratch_shapes=[pltpu.VMEM((N_DEV, R, C), F32)] + _dma_sems(N_DEV - 1),
                  compiler_params=pltpu.CompilerParams(has_side_effects=True, vmem_limit_bytes=VMEM_LIMIT))(part)


SHARD_STEPS = 4


def _own_shard_buffers(ws, l, group, place, name):
    nw = len(group)

    def body(p_ref, *refs):
        for (_, tr), i_ref, o_ref in zip(group, refs[:nw], refs[nw:]):
            v = i_ref[...]
            o_ref[...] = (v.T if tr else v).astype(o_ref.dtype)

    in_specs, out_specs, out_shape, sizes = [], [], [], []
    for wname, tr in group:
        _, K, n = ws[wname].shape
        in_specs.append(pl.BlockSpec((None, K // SHARD_STEPS, n), lambda i, p: (l, i, 0)))
        if tr:
            out_specs.append(pl.BlockSpec((n, K // SHARD_STEPS), lambda i, p: (p[1], i)))
            out_shape.append(_S((N_CHIPS * n, K), MXU))
            sizes.append(n)
        else:
            out_specs.append(pl.BlockSpec((K // SHARD_STEPS, n), lambda i, p: (p[1] * SHARD_STEPS + i, 0)))
            out_shape.append(_S((N_CHIPS * K, n), MXU))
            sizes.append(K)
    gs = pltpu.PrefetchScalarGridSpec(num_scalar_prefetch=1, grid=(SHARD_STEPS,), in_specs=in_specs, out_specs=out_specs)
    bufs = _pcall(body, name=name, grid_spec=gs, out_shape=out_shape,
                  compiler_params=_cp("parallel"))(place, *[ws[wname] for wname, _ in group])
    return list(bufs), sizes


def _after(xs, ys):
    return lax.optimization_barrier((xs, ys))[0]


def _small_rows(v):
    flat = v.reshape(-1)
    pad = (-flat.shape[0]) % 1024
    return jnp.pad(flat, (0, pad)).reshape(-1, 1024)


def _pack_small(vals):
    rows = [_small_rows(vals[n]) for n in SMALL]
    cat = jnp.concatenate(rows, axis=0)
    pad = (-cat.shape[0]) % 8
    return jnp.pad(cat, ((0, pad), (0, 0)))


def _unpack_small(packed, like):
    out, r = {}, 0
    for n in SMALL:
        size = like[n].size
        nr = -(-size // 1024)
        out[n] = packed[r:r + nr].reshape(-1)[:size].reshape(like[n].shape)
        r += nr
    return out


def _rope_tables(positions):
    inv_freq = ROPE_THETA ** (-jnp.arange(0, HEAD_DIM, 2, dtype=F32) / HEAD_DIM)
    ang = positions.astype(F32)[:, None] * inv_freq
    cos, sin = jnp.cos(ang), jnp.sin(ang)
    return jnp.concatenate([cos, cos, cos, cos], axis=1), jnp.concatenate([-sin, sin, -sin, sin], axis=1)


def _layer_fwd(l, x, h1, memv, W, P, cos2, sin2, next_norm):
    t = f"l{l}"
    sv = {"x0": x, "h1": h1}
    sv["a1"], sv["b1"], sv["s1"] = _ffn_up(h1, W["ffn1_w_gate"], W["ffn1_w_up"], t + "_ffn1_up")
    sv["x1"], sv["h2"] = _mm([(sv["s1"], W["ffn1_w_down"])], nt=False, out_dtype=F32, res=x, res_scale=FFN_RES,
                             name=t + "_ffn1_down", norm_g=P["mix_norm"])

    sv["q"], sv["k"], sv["v"], sv["u"] = _in_proj(sv["h2"], W["w_in"], cos2, sin2, t + "_in_proj")
    sv["oa"] = _swa_fwd(sv["q"], sv["k"], sv["v"], P["attn_sinks"], t + "_swa")
    sv["ob"], sv["mg"] = _pool_fwd(sv["u"], sv["oa"], P["pool_w"], P["pool_scale"], P["attn_out_norm"],
                                   P["pool_out_norm"], t + "_pool")
    sv["x2"], sv["h3"] = _mm([(sv["mg"], W["w_out"])], nt=False, out_dtype=F32, res=sv["x1"], name=t + "_out_proj",
                             norm_g=P["xattn_norm"])

    sv["memn"] = _rms_fwd(memv, P["mem_norm"], t + "_mem_norm")
    sv["q3"] = _mm([(sv["h3"], W["xattn_wq"])], nt=False, out_dtype=MXU, name=t + "_xq")
    sv["kv"] = _mm([(sv["memn"], W["xattn_wkv"])], nt=True, out_dtype=MXU, name=t + "_xkv")
    sv["o3"] = _xattn_fwd(sv["q3"], sv["kv"], t + "_xattn")
    sv["x3"], sv["h4"] = _mm([(sv["o3"], W["xattn_wo"])], nt=False, out_dtype=F32, res=sv["x2"], name=t + "_xo",
                             norm_g=P["ffn2_norm"])

    sv["a2"], sv["b2"], sv["s2"] = _ffn_up(sv["h4"], W["ffn2_w_gate"], W["ffn2_w_up"], t + "_ffn2_up")
    out = _mm([(sv["s2"], W["ffn2_w_down"])], nt=False, out_dtype=F32, res=sv["x3"], res_scale=FFN_RES,
              name=t + "_ffn2_down", norm_g=next_norm)
    x4, h_next = out if next_norm is not None else (out, None)
    return x4, h_next, sv


def _ffn_bwd(t, dx, x_in, g, h, a, b, s, wgT, wuT, wd, behind=None, begin=None):
    d_wd = _mm_tn(s, dx, t + "_dwd", r_scale=FFN_RES)
    da, db = _ffn_mid_bwd(dx, wd, a, b, t + "_mid")
    if behind is not None:
        da = _after(da, behind)
    d_wg = _mm_tn(da, h, t + "_dwg")
    d_wu = _mm_tn(db, h, t + "_dwu")
    if begin is not None:
        da = _after(da, begin(d_wg, d_wu, d_wd))
    (dx_in,), (dg,) = _mm_norm_bwd([(da, wgT), (db, wuT)], [(x_in, g, dx)], nt=False, name=t + "_dh",
                                   tm=TOKEN_TILE // 2)
    return dx_in, dg, d_wg, d_wu, d_wd


def _layer_bwd(l, dx, sv, memv, W, P, cos2, sin2, begin_rest, begin_ffn1):
    t = f"l{l}b"
    GW, GP, marks = {}, {}, {}
    dx, GP["ffn2_norm"], GW["ffn2_w_gate"], GW["ffn2_w_up"], GW["ffn2_w_down"] = _ffn_bwd(
        t + "_ffn2", dx, sv["x3"], P["ffn2_norm"], sv["h4"], sv["a2"], sv["b2"], sv["s2"],
        W["ffn2_w_gate"], W["ffn2_w_up"], W["ffn2_w_down"])
    marks["ffn2"] = dx

    GW["xattn_wo"] = _mm_tn(sv["o3"], dx, t + "_dwo")
    do3 = _mm([(dx, W["xattn_wo"])], nt=True, out_dtype=MXU, name=t + "_do3")
    dq3, dkv = _xattn_bwd(sv["q3"], sv["kv"], do3, t + "_xattn")
    GW["xattn_wq"] = _mm_tn(sv["h3"], dq3, t + "_dwq")
    GW["xattn_wkv"] = _mm_tn(dkv, sv["memn"], t + "_dwkv")
    dmemn = _mm([(dkv, W["xattn_wkv"])], nt=False, out_dtype=F32, name=t + "_dmemn")
    _, GP["mem_norm"] = _rms_bwd(memv, P["mem_norm"], dmemn, t + "_mem_norm_bwd")
    (dx,), (GP["xattn_norm"],) = _mm_norm_bwd([(dq3, W["xattn_wq"])], [(sv["x2"], P["xattn_norm"], dx)], nt=True,
                                             name=t + "_dh3")
    marks["xattn"] = dx

    GW["w_out"] = _mm_tn(sv["mg"], dx, t + "_dwout")
    (doa, dob), (GP["attn_out_norm"], GP["pool_out_norm"]) = _mm_norm_bwd(
        [(dx, W["w_out"])], [(sv["oa"], P["attn_out_norm"], None), (sv["ob"], P["pool_out_norm"], None)], nt=True,
        name=t + "_dmg")
    du, GP["pool_w"], GP["pool_scale"] = _pool_bwd(sv["u"], dob, P["pool_w"], P["pool_scale"], t + "_pool")
    dq, dko, dkp, dvo, dvp, dsk = _swa_bwd(sv["q"], sv["k"], sv["v"], doa, P["attn_sinks"], t + "_swa")
    GP["attn_sinks"] = dsk[:, 0]
    dpj = _dproj(dq, dko, dkp, dvo, dvp, du, cos2, sin2, t + "_dproj")
    GW["w_in"] = _mm_tn(dpj, sv["h2"], t + "_dwin")
    (dx,), (GP["mix_norm"],) = _mm_norm_bwd([(dpj, W["w_in"])], [(sv["x1"], P["mix_norm"], dx)], nt=False,
                                           name=t + "_dh2")

    dx, GP["ffn1_norm"], _, _, _ = _ffn_bwd(
        t + "_ffn1", dx, sv["x0"], P["ffn1_norm"], sv["h1"], sv["a1"], sv["b1"], sv["s1"],
        W["ffn1_w_gate"], W["ffn1_w_up"], W["ffn1_w_down"], behind=begin_rest(GW), begin=begin_ffn1)
    return dx, GP, marks


def _reduce_begin(t, grads, sizes, place, seq_ids):
    gots = _rs_pair_exchange(grads, sizes, t + "_pair")
    sums = _pair_sum(grads, gots, sizes, place, t + "_pair_sum")
    recvs = _rs_chip_exchange(sums, sizes, t + "_chips", seq_id=next(seq_ids))
    return dict(t=t, grads=grads, gots=gots, sums=sums, recvs=recvs, sizes=sizes)


def _reduce_end(st, place, seq_ids, late=None):
    recvs = st["recvs"] if late is None else _after(st["recvs"], late)
    reds = _final_sum(st["grads"], st["gots"], recvs, st["sizes"], place, st["t"] + "_final_sum")
    return _rs_share_halves(reds, st["sizes"], st["t"] + "_share")


def _adamw_layer(l, w3, g, m3, v3, transposed, prev, name):
    _, K, n = w3.shape
    if transposed:
        tr = K // SHARD_STEPS
        g_spec = pl.BlockSpec((n, tr), lambda i: (0, i))
    else:
        tr = _rows_tile(K)
        g_spec = pl.BlockSpec((tr, n), lambda i: (i, 0))
    c1 = 1.0 / (1.0 - ADAM_B1 ** ADAM_STEP)
    c2 = 1.0 / (1.0 - ADAM_B2 ** ADAM_STEP)

    def body(w_ref, g_ref, m_ref, v_ref, *rest):
        go_ref, d_ref, nm_ref, nv_ref = rest[-4:]
        gv = g_ref[...].T if transposed else g_ref[...]
        nm = ADAM_B1 * m_ref[...] + (1.0 - ADAM_B1) * gv
        nv = ADAM_B2 * v_ref[...] + (1.0 - ADAM_B2) * (gv * gv)
        go_ref[...] = gv
        d_ref[...] = -ADAM_LR * ((nm * c1) / (jnp.sqrt(nv * c2) + ADAM_EPS) + ADAM_WD * w_ref[...])
        nm_ref[...] = nm
        nv_ref[...] = nv

    slab = pl.BlockSpec((None, tr, n), lambda i: (l, i, 0))
    in_specs, args, aliases = [slab, g_spec, slab, slab], [w3, g, m3, v3], {}
    if prev is not None:
        in_specs += [ANY] * 4
        args += list(prev)
        aliases = {4 + j: j for j in range(4)}
    return _pcall(body, name=name, grid=(K // tr,), in_specs=in_specs, out_specs=[slab] * 4,
                  out_shape=[_S(w3.shape, F32)] * 4, input_output_aliases=aliases,
                  compiler_params=_cp("parallel"))(*args)


def kernel(x, mem, positions, ffn1_norm, ffn1_w_gate, ffn1_w_up, ffn1_w_down, mix_norm, w_in, attn_sinks, pool_w, pool_scale, attn_out_norm, pool_out_norm, w_out, xattn_norm, mem_norm, xattn_wq, xattn_wkv, xattn_wo, ffn2_norm, ffn2_w_gate, ffn2_w_up, ffn2_w_down, final_norm, loss_target, m_ffn1_norm, m_ffn1_w_gate, m_ffn1_w_up, m_ffn1_w_down, m_mix_norm, m_w_in, m_attn_sinks, m_pool_w, m_pool_scale, m_attn_out_norm, m_pool_out_norm, m_w_out, m_xattn_norm, m_mem_norm, m_xattn_wq, m_xattn_wkv, m_xattn_wo, m_ffn2_norm, m_ffn2_w_gate, m_ffn2_w_up, m_ffn2_w_down, m_final_norm, v_ffn1_norm, v_ffn1_w_gate, v_ffn1_w_up, v_ffn1_w_down, v_mix_norm, v_w_in, v_attn_sinks, v_pool_w, v_pool_scale, v_attn_out_norm, v_pool_out_norm, v_w_out, v_xattn_norm, v_mem_norm, v_xattn_wq, v_xattn_wkv, v_xattn_wo, v_ffn2_norm, v_ffn2_w_gate, v_ffn2_w_up, v_ffn2_w_down, v_final_norm):
    ws = dict(ffn1_norm=ffn1_norm, ffn1_w_gate=ffn1_w_gate, ffn1_w_up=ffn1_w_up, ffn1_w_down=ffn1_w_down,
              mix_norm=mix_norm, w_in=w_in, attn_sinks=attn_sinks, pool_w=pool_w, pool_scale=pool_scale,
              attn_out_norm=attn_out_norm, pool_out_norm=pool_out_norm, w_out=w_out, xattn_norm=xattn_norm,
              mem_norm=mem_norm, xattn_wq=xattn_wq, xattn_wkv=xattn_wkv, xattn_wo=xattn_wo, ffn2_norm=ffn2_norm,
              ffn2_w_gate=ffn2_w_gate, ffn2_w_up=ffn2_w_up, ffn2_w_down=ffn2_w_down, final_norm=final_norm)
    ms = dict(ffn1_norm=m_ffn1_norm, ffn1_w_gate=m_ffn1_w_gate, ffn1_w_up=m_ffn1_w_up, ffn1_w_down=m_ffn1_w_down,
              mix_norm=m_mix_norm, w_in=m_w_in, attn_sinks=m_attn_sinks, pool_w=m_pool_w, pool_scale=m_pool_scale,
              attn_out_norm=m_attn_out_norm, pool_out_norm=m_pool_out_norm, w_out=m_w_out, xattn_norm=m_xattn_norm,
              mem_norm=m_mem_norm, xattn_wq=m_xattn_wq, xattn_wkv=m_xattn_wkv, xattn_wo=m_xattn_wo,
              ffn2_norm=m_ffn2_norm, ffn2_w_gate=m_ffn2_w_gate, ffn2_w_up=m_ffn2_w_up, ffn2_w_down=m_ffn2_w_down,
              final_norm=m_final_norm)
    vs = dict(ffn1_norm=v_ffn1_norm, ffn1_w_gate=v_ffn1_w_gate, ffn1_w_up=v_ffn1_w_up, ffn1_w_down=v_ffn1_w_down,
              mix_norm=v_mix_norm, w_in=v_w_in, attn_sinks=v_attn_sinks, pool_w=v_pool_w, pool_scale=v_pool_scale,
              attn_out_norm=v_attn_out_norm, pool_out_norm=v_pool_out_norm, w_out=v_w_out, xattn_norm=v_xattn_norm,
              mem_norm=v_mem_norm, xattn_wq=v_xattn_wq, xattn_wkv=v_xattn_wkv, xattn_wo=v_xattn_wo,
              ffn2_norm=v_ffn2_norm, ffn2_w_gate=v_ffn2_w_gate, ffn2_w_up=v_ffn2_w_up, ffn2_w_down=v_ffn2_w_down,
              final_norm=v_final_norm)
    depth = ffn1_norm.shape[0]
    T, D = x.shape[1], x.shape[2]
    xv = x.reshape(T, D)
    memv = mem.reshape(mem.shape[1], D)
    tgt = loss_target.reshape(T, D)
    cos2, sin2 = _rope_tables(positions.reshape(T))
    in_kernel = {name: tr and ws[name].shape[2] % LANES == 0 for name, tr in BIG}
    swapped = [name for name, tr in BIG if tr and not in_kernel[name]]
    rows = lambda d: {name: (jnp.swapaxes(d[name], 1, 2) if name in swapped else d[name]) for name, _ in BIG}
    wr, mr, vr = rows(ws), rows(ms), rows(vs)
    groups = [[(name, in_kernel[name]) for name, _ in g] for g in GROUPS]

    q_me = 2 * lax.axis_index("x") + lax.axis_index("y")
    place = jnp.stack([lax.axis_index("c"), q_me]).astype(jnp.int32)
    seq_ids = iter(range(1, 1 + 8 * depth))
    Ws, sizes, first = [dict() for _ in range(depth)], {}, None
    for l in range(depth):
        for gi, group in enumerate(groups):
            t = f"l{l}g{gi}"
            bufs, sizes[gi] = _own_shard_buffers(wr, l, group, place, t + "_shard")
            if first is None:
                full = first = _allgather_weights(bufs, sizes[gi], t + "_allgather")
            else:
                full = _allgather_weights(_after(bufs, first), sizes[gi], t + "_allgather", seq_id=next(seq_ids))
            Ws[l].update({name: f for (name, _), f in zip(group, full)})
    Ps = [{n: (ws[n][l].reshape(1, -1) if n != "pool_w" else ws[n][l]) for n in SMALL if n != "final_norm"}
          for l in range(depth)]

    saved = []
    h, hn = xv, _rms_fwd(xv, Ps[0]["ffn1_norm"], "l0_ffn1_norm")
    for l in range(depth):
        next_norm = Ps[l + 1]["ffn1_norm"] if l + 1 < depth else None
        h, hn, sv = _layer_fwd(l, h, hn, memv, Ws[l], Ps[l], cos2, sin2, next_norm)
        saved.append(sv)
    loss_row, dx, d_final = _loss_head(h, tgt, final_norm.reshape(1, D), "loss_head")
    GPs, marks, begun = [None] * depth, [None] * depth, {}
    for l in reversed(range(depth)):
        def begin_rest(GW, l=l):
            begun[l, 1] = _reduce_begin(f"l{l}g1r", [GW[name] for name, _ in groups[1]], sizes[1], place, seq_ids)
            return begun[l, 1]["sums"]

        def begin_ffn1(*gs, l=l):
            begun[l, 0] = _reduce_begin(f"l{l}g0r", list(gs), sizes[0], place, seq_ids)
            return begun[l, 0]["sums"]

        dx, GPs[l], marks[l] = _layer_bwd(l, dx, saved[l], memv, Ws[l], Ps[l], cos2, sin2, begin_rest, begin_ffn1)

    stacked = {}
    for l in reversed(range(depth)):
        for gi, group in reversed(list(enumerate(groups))):
            if l > 0:
                late = marks[l - 1]["ffn2" if gi == 1 else "xattn"]
            else:
                late = dx if gi == 1 else None
            reds = _reduce_end(begun[l, gi], place, seq_ids, late)
            for (name, tr), red in zip(group, reds):
                stacked[name] = _adamw_layer(l, wr[name], red, mr[name], vr[name], tr, stacked.get(name),
                                             f"l{l}_adamw_{name}")
    for name in swapped:
        stacked[name] = [jnp.swapaxes(a, 1, 2) for a in stacked[name]]
    small_part = {n: jnp.stack([GPs[l][n].reshape(ws[n].shape[1:]) for l in range(depth)]) for n in SMALL if n != "final_norm"}
    small_part["final_norm"] = d_final.reshape(D)
    small_g = _unpack_small(_allreduce_small(_pack_small(small_part), "small_allreduce"), ws)
    loss = lax.psum(loss_row[0, 0], ("x", "y", "c"))

    grads, deltas, new_m, new_v = {}, {}, {}, {}
    for name, _ in BIG:
        grads[name], deltas[name], new_m[name], new_v[name] = stacked[name]
    d, nm, nv = _adamw(_pack_small(ws), _pack_small(small_g), _pack_small(ms), _pack_small(vs), "adamw_small")
    grads.update(small_g)
    deltas.update(_unpack_small(d, ws))
    new_m.update(_unpack_small(nm, ws))
    new_v.update(_unpack_small(nv, ws))

    grad_x = dx.reshape(x.shape)
    return (loss, grad_x, *[grads[n] for n in WEIGHTS], *[deltas[n] for n in WEIGHTS],
            *[new_m[n] for n in WEIGHTS], *[new_v[n] for n in WEIGHTS])
```

```python
import functools

import jax
import jax.numpy as jnp
from jax import lax
from jax.experimental import pallas as pl
from jax.experimental.pallas import tpu as pltpu
from jax.experimental.pallas import tpu_sc as plsc

F32 = jnp.float32
MXU = jnp.bfloat16

EPS = 1e-6
HEAD_DIM = 64
N_Q_HEADS = 8
N_KV_HEADS = 2
Q_PER_KV = N_Q_HEADS // N_KV_HEADS
ATTN_WIDTH = N_Q_HEADS * HEAD_DIM
KV_WIDTH = N_KV_HEADS * HEAD_DIM
BLOCK = 128
ROPE_THETA = 10000.0
POOL_WINDOWS = (2, 4, 8, 16)
POOL_GROUP = 128
POOL_WIDTH = len(POOL_WINDOWS) * POOL_GROUP
POOL_HALO = 16
X_HEADS = 4
X_HEAD_DIM = 256
FFN_RES = 0.5
NEG = -1e30
ADAM_LR = 0.001
ADAM_B1 = 0.9
ADAM_B2 = 0.999
ADAM_EPS = 1e-08
ADAM_WD = 0.01
ADAM_STEP = 10

N_CHIPS = 4
N_DEV = 8
V7X_VMEM_BYTES = 64 * 1024 * 1024
VMEM_LIMIT = V7X_VMEM_BYTES - 8 * 1024 * 1024
LANES = 128
TOKEN_TILE = 512
MESH = pl.DeviceIdType.MESH

BIG = (("ffn1_w_gate", True), ("ffn1_w_up", True), ("ffn1_w_down", False), ("w_in", True), ("w_out", False),
       ("xattn_wq", False), ("xattn_wkv", True), ("xattn_wo", False),
       ("ffn2_w_gate", True), ("ffn2_w_up", True), ("ffn2_w_down", False))
GROUPS = (BIG[:3], BIG[3:])
SMALL = ("ffn1_norm", "mix_norm", "attn_sinks", "pool_w", "pool_scale", "attn_out_norm", "pool_out_norm",
         "xattn_norm", "mem_norm", "ffn2_norm", "final_norm")
WEIGHTS = ("ffn1_norm", "ffn1_w_gate", "ffn1_w_up", "ffn1_w_down", "mix_norm", "w_in", "attn_sinks", "pool_w",
           "pool_scale", "attn_out_norm", "pool_out_norm", "w_out", "xattn_norm", "mem_norm", "xattn_wq",
           "xattn_wkv", "xattn_wo", "ffn2_norm", "ffn2_w_gate", "ffn2_w_up", "ffn2_w_down", "final_norm")


def _S(shape, dtype):
    return jax.ShapeDtypeStruct(tuple(shape), dtype)


def _pcall(body, **kw):
    return pl.pallas_call(body, **kw)


def _cp(*sem):
    return pltpu.CompilerParams(dimension_semantics=tuple(sem), vmem_limit_bytes=VMEM_LIMIT)


def _nt(a, b):
    return lax.dot_general(a, b, (((1,), (1,)), ((), ())), preferred_element_type=F32)


def _nn(a, b):
    return lax.dot_general(a, b, (((1,), (0,)), ((), ())), preferred_element_type=F32)


def _tn(a, b):
    return lax.dot_general(a, b, (((0,), (0,)), ((), ())), preferred_element_type=F32)


def _tile(n, want):
    t = min(n, want)
    assert n % t == 0, (n, want)
    return t


def _resident(shape):
    nd = len(shape)
    return pl.BlockSpec(tuple(shape), lambda *_: (0,) * nd)


def _rms_fwd(x, g, name):
    T, C = x.shape
    tm = _tile(T, TOKEN_TILE)

    def body(x_ref, g_ref, o_ref):
        xv = x_ref[...]
        r = lax.rsqrt(jnp.mean(xv * xv, axis=-1, keepdims=True) + EPS)
        o_ref[...] = (xv * r * g_ref[...]).astype(o_ref.dtype)

    return _pcall(body, name=name, grid=(T // tm,),
                  in_specs=[pl.BlockSpec((tm, C), lambda i: (i, 0)), _resident((1, C))],
                  out_specs=pl.BlockSpec((tm, C), lambda i: (i, 0)),
                  out_shape=_S((T, C), MXU), compiler_params=_cp("parallel"))(x, g)


def _rms_bwd(x, g, dh, name, dres=None):
    T, C = x.shape
    tm = _tile(T, TOKEN_TILE)

    def body(*refs):
        if dres is None:
            x_ref, g_ref, dh_ref, dx_ref, dg_ref = refs
        else:
            x_ref, g_ref, dh_ref, dres_ref, dx_ref, dg_ref = refs
        xv = x_ref[...]
        r = lax.rsqrt(jnp.mean(xv * xv, axis=-1, keepdims=True) + EPS)
        xh = xv * r
        dhv = dh_ref[...].astype(F32)
        dxn = dhv * g_ref[...]
        dx = r * (dxn - xh * jnp.mean(dxn * xh, axis=-1, keepdims=True))
        if dres is not None:
            dx = dx + dres_ref[...]
        dx_ref[...] = dx

        @pl.when(pl.program_id(0) == 0)
        def _():
            dg_ref[...] = jnp.zeros_like(dg_ref)

        dg_ref[...] += jnp.sum(dhv * xh, axis=0, keepdims=True)

    tok = pl.BlockSpec((tm, C), lambda i: (i, 0))
    in_specs = [tok, _resident((1, C)), tok]
    args = [x, g, dh]
    if dres is not None:
        in_specs.append(tok)
        args.append(dres)
    return _pcall(body, name=name, grid=(T // tm,), in_specs=in_specs,
                  out_specs=[tok, _resident((1, C))],
                  out_shape=[_S((T, C), F32), _S((1, C), F32)], compiler_params=_cp("arbitrary"))(*args)


def _mm(pairs, *, nt, out_dtype, name, res=None, res_scale=1.0, tm=TOKEN_TILE, norm_g=None):
    M = pairs[0][0].shape[0]
    N = pairs[0][1].shape[0] if nt else pairs[0][1].shape[1]
    tm = _tile(M, tm)
    n = len(pairs)

    def body(*refs):
        a_refs, w_refs = refs[:n], refs[n:2 * n]
        o_ref = refs[-1] if norm_g is None else refs[-2]
        acc = None
        for a_ref, w_ref in zip(a_refs, w_refs):
            a = a_ref[...]
            a = a.astype(MXU)
            p = _nt(a, w_ref[...]) if nt else _nn(a, w_ref[...])
            acc = p if acc is None else acc + p
        if res is not None:
            acc = refs[2 * n][...] + res_scale * acc
        o_ref[...] = acc.astype(o_ref.dtype)
        if norm_g is not None:
            r = lax.rsqrt(jnp.mean(acc * acc, axis=-1, keepdims=True) + EPS)
            refs[-1][...] = (acc * r * refs[-3][...]).astype(MXU)

    tok = pl.BlockSpec((tm, N), lambda i: (i, 0))
    in_specs = [pl.BlockSpec((tm, a.shape[1]), lambda i: (i, 0)) for a, _ in pairs]
    in_specs += [_resident(w.shape) for _, w in pairs]
    args = [a for a, _ in pairs] + [w for _, w in pairs]
    if res is not None:
        in_specs.append(tok)
        args.append(res)
    if norm_g is None:
        return _pcall(body, name=name, grid=(M // tm,), in_specs=in_specs, out_specs=tok,
                      out_shape=_S((M, N), out_dtype), compiler_params=_cp("parallel"))(*args)
    return _pcall(body, name=name, grid=(M // tm,), in_specs=in_specs + [_resident(norm_g.shape)],
                  out_specs=[tok, tok], out_shape=[_S((M, N), out_dtype), _S((M, N), MXU)],
                  compiler_params=_cp("parallel"))(*args, norm_g)


def _mm_norm_bwd(pairs, norms, *, nt, name, tm=TOKEN_TILE, half=False):
    M = pairs[0][0].shape[0]
    tm = _tile(M, tm)
    n, k = len(pairs), len(norms)
    has_res = [d is not None for _, _, d in norms]

    def body(*refs):
        a_refs, w_refs = refs[:n], refs[n:2 * n]
        rest = list(refs[2 * n:])
        acc = None
        for a_ref, w_ref in zip(a_refs, w_refs):
            p = _nt(a_ref[...].astype(MXU), w_ref[...]) if nt else _nn(a_ref[...].astype(MXU), w_ref[...])
            acc = p if acc is None else acc + p
        ins = [(rest.pop(0), rest.pop(0), rest.pop(0) if has_res[j] else None) for j in range(k)]
        dx_refs, dg_refs = rest[:k], rest[k:2 * k]
        c0 = 0
        for j, ((x_ref, g_ref, d_ref), dx_ref, dg_ref) in enumerate(zip(ins, dx_refs, dg_refs)):
            xv = x_ref[...]
            dhv = acc[:, c0:c0 + xv.shape[1]]
            c0 += xv.shape[1]
            r = lax.rsqrt(jnp.mean(xv * xv, axis=-1, keepdims=True) + EPS)
            xh = xv * r
            dxn = dhv * g_ref[...]
            dx = r * (dxn - xh * jnp.mean(dxn * xh, axis=-1, keepdims=True))
            if d_ref is not None:
                dx = dx + d_ref[...]
            dx_ref[...] = dx
            if half and j == 0:
                rest[2 * k][...] = (dx * FFN_RES).astype(MXU)

            @pl.when(pl.program_id(0) == 0)
            def _():
                dg_ref[...] = jnp.zeros_like(dg_ref)

            dg_ref[...] += jnp.sum(dhv * xh, axis=0, keepdims=True)

    tok = lambda c: pl.BlockSpec((tm, c), lambda i: (i, 0))
    in_specs = [tok(a.shape[1]) for a, _ in pairs] + [_resident(w.shape) for _, w in pairs]
    args = [a for a, _ in pairs] + [w for _, w in pairs]
    for x, g, d in norms:
        in_specs += [tok(x.shape[1]), _resident(g.shape)] + ([tok(x.shape[1])] if d is not None else [])
        args += [x, g] + ([d] if d is not None else [])
    out_specs = [tok(x.shape[1]) for x, _, _ in norms] + [_resident((1, x.shape[1])) for x, _, _ in norms]
    out_shape = [_S((M, x.shape[1]), F32) for x, _, _ in norms] + [_S((1, x.shape[1]), F32) for x, _, _ in norms]
    if half:
        out_specs.append(tok(norms[0][0].shape[1]))
        out_shape.append(_S(norms[0][0].shape, MXU))
    outs = _pcall(body, name=name, grid=(M // tm,), in_specs=in_specs, out_specs=out_specs, out_shape=out_shape,
                  compiler_params=_cp("arbitrary"))(*args)
    if half:
        return list(outs[:k]), list(outs[k:2 * k]), outs[2 * k]
    return list(outs[:k]), list(outs[k:])


def _mm_tn(l, r, name, *, tr=1408, tt=TOKEN_TILE):
    T, R = l.shape
    C = r.shape[1]
    tt = _tile(T, tt)
    tr = tr if R % tr == 0 else (1024 if R % 1024 == 0 and R > 1280 else R)

    def body(l_ref, r_ref, o_ref):
        lv, rv = l_ref[...].astype(MXU), r_ref[...].astype(MXU)

        @pl.when(pl.program_id(1) == 0)
        def _():
            o_ref[...] = jnp.zeros_like(o_ref)

        o_ref[...] += _tn(lv, rv)

    return _pcall(body, name=name, grid=(R // tr, T // tt),
                  in_specs=[pl.BlockSpec((tt, tr), lambda i, t: (t, i)), pl.BlockSpec((tt, C), lambda i, t: (t, 0))],
                  out_specs=pl.BlockSpec((tr, C), lambda i, t: (i, 0)),
                  out_shape=_S((R, C), F32), compiler_params=_cp("parallel", "arbitrary"))(l, r)


FFN_COL_TILE = 1408


def _sigmoid(a):
    return 0.5 * (jnp.tanh(0.5 * a) + 1.0)


def _ffn_up(h, wgT, wuT, name):
    T, D = h.shape
    Fd = wgT.shape[0]
    tm, tn = _tile(T, TOKEN_TILE), _tile(Fd, FFN_COL_TILE)

    def body(h_ref, wg_ref, wu_ref, a_ref, b_ref, s_ref):
        hv = h_ref[...]
        a = _nt(hv, wg_ref[...])
        b = _nt(hv, wu_ref[...])
        s = a * _sigmoid(a) * b
        a_ref[...] = a.astype(a_ref.dtype)
        b_ref[...] = b.astype(b_ref.dtype)
        s_ref[...] = s.astype(s_ref.dtype)

    wspec = pl.BlockSpec((tn, D), lambda j, i: (j, 0))
    ospec = pl.BlockSpec((tm, tn), lambda j, i: (i, j))
    return _pcall(body, name=name, grid=(Fd // tn, T // tm),
                  in_specs=[pl.BlockSpec((tm, D), lambda j, i: (i, 0)), wspec, wspec],
                  out_specs=[ospec, ospec, ospec], out_shape=[_S((T, Fd), MXU)] * 3,
                  compiler_params=_cp("parallel", "parallel"))(h, wgT, wuT)


def _ffn_mid_bwd(dy, wd, a, b, name):
    T, D = dy.shape
    Fd = wd.shape[0]
    tm, tn = _tile(T, TOKEN_TILE), _tile(Fd, FFN_COL_TILE)

    def body(dy_ref, wd_ref, a_ref, b_ref, da_ref, db_ref):
        ds = _nt(dy_ref[...], wd_ref[...])
        av, bv = a_ref[...].astype(F32), b_ref[...].astype(F32)
        sg = _sigmoid(av)
        da_ref[...] = (ds * bv * (sg * (1.0 + av * (1.0 - sg)))).astype(da_ref.dtype)
        db_ref[...] = (ds * (av * sg)).astype(db_ref.dtype)

    aspec = pl.BlockSpec((tm, tn), lambda j, i: (i, j))
    return _pcall(body, name=name, grid=(Fd // tn, T // tm),
                  in_specs=[pl.BlockSpec((tm, D), lambda j, i: (i, 0)), pl.BlockSpec((tn, D), lambda j, i: (j, 0)),
                            aspec, aspec],
                  out_specs=[aspec, aspec], out_shape=[_S((T, Fd), MXU)] * 2,
                  compiler_params=_cp("parallel", "parallel"))(dy, wd, a, b)


def _swap_halves(t):
    w = t.shape[1]
    lane = lax.broadcasted_iota(jnp.int32, t.shape, 1)
    first = (lane % HEAD_DIM) < (HEAD_DIM // 2)
    return jnp.where(first, pltpu.roll(t, w - HEAD_DIM // 2, 1), pltpu.roll(t, HEAD_DIM // 2, 1))


def _rope(t, cos2, sin2):
    reps = t.shape[1] // LANES
    c = jnp.tile(cos2, (1, reps)) if reps > 1 else cos2
    s = jnp.tile(sin2, (1, reps)) if reps > 1 else sin2
    return t * c + _swap_halves(t) * s


def _rope_bwd(dt, cos2, sin2):
    reps = dt.shape[1] // LANES
    c = jnp.tile(cos2, (1, reps)) if reps > 1 else cos2
    s = jnp.tile(sin2, (1, reps)) if reps > 1 else sin2
    return dt * c + _swap_halves(dt * s)


def _in_proj(h, winT, cos2, sin2, name):
    T, D = h.shape
    tm = _tile(T, TOKEN_TILE)
    qe, ke, ve = ATTN_WIDTH, ATTN_WIDTH + KV_WIDTH, ATTN_WIDTH + 2 * KV_WIDTH

    def body(h_ref, w_ref, c_ref, s_ref, q_ref, k_ref, v_ref, u_ref):
        proj = _nt(h_ref[...], w_ref[...])
        cv, sv = c_ref[...], s_ref[...]
        q_ref[...] = _rope(proj[:, :qe], cv, sv).astype(q_ref.dtype)
        k_ref[...] = _rope(proj[:, qe:ke], cv, sv).astype(k_ref.dtype)
        v_ref[...] = proj[:, ke:ve].astype(v_ref.dtype)
        u_ref[...] = proj[:, ve:]

    def tok(w):
        return pl.BlockSpec((tm, w), lambda i: (i, 0))

    return _pcall(body, name=name, grid=(T // tm,),
                  in_specs=[tok(D), _resident(winT.shape), tok(LANES), tok(LANES)],
                  out_specs=[tok(ATTN_WIDTH), tok(KV_WIDTH), tok(KV_WIDTH), tok(POOL_WIDTH)],
                  out_shape=[_S((T, ATTN_WIDTH), MXU), _S((T, KV_WIDTH), MXU), _S((T, KV_WIDTH), MXU),
                             _S((T, POOL_WIDTH), F32)],
                  compiler_params=_cp("parallel"))(h, winT, cos2, sin2)


SWA_TILE_BLOCKS = 4
SM_SCALE = HEAD_DIM ** -0.5


def _swa_bias(first_tile):
    cols = Q_PER_KV * BLOCK
    kj = lax.broadcasted_iota(jnp.int32, (2 * BLOCK, cols), 0)
    qi = lax.broadcasted_iota(jnp.int32, (2 * BLOCK, cols), 1) % BLOCK
    diff = qi + BLOCK - kj
    bias = jnp.where((diff >= 0) & (diff < BLOCK), 0.0, NEG)
    return bias, jnp.where(kj < jnp.where(first_tile, BLOCK, 0), NEG, bias)


def _swa_probs(kh, qs, sink_row, bias):
    s = _nt(kh, qs) + bias
    m = jnp.maximum(jnp.max(s, axis=0, keepdims=True), sink_row)
    e = jnp.exp(s - m)
    es = jnp.exp(sink_row - m)
    inv = 1.0 / (jnp.sum(e, axis=0, keepdims=True) + es)
    return e * inv, es * inv


def _sink_row(sinks_ref, kv):
    return jnp.concatenate([jnp.full((1, BLOCK), sinks_ref[0, kv * Q_PER_KV + g], F32) for g in range(Q_PER_KV)], axis=1)


def _stack_heads(t, kv):
    return jnp.concatenate([t[:, (kv * Q_PER_KV + g) * HEAD_DIM:(kv * Q_PER_KV + g + 1) * HEAD_DIM]
                            for g in range(Q_PER_KV)], axis=0)


def _swa_specs(T):
    tq = _tile(T, SWA_TILE_BLOCKS * BLOCK)
    nbt = tq // BLOCK
    cur = lambda w: pl.BlockSpec((tq, w), lambda i: (i, 0))
    prev = lambda w: pl.BlockSpec((BLOCK, w), lambda i: (jnp.maximum(i * nbt - 1, 0), 0))
    return tq, nbt, cur, prev


def _rows(b):
    return slice(b * BLOCK, (b + 1) * BLOCK)


def _swa_fwd(q, k, v, sinks, name):
    T = q.shape[0]
    tq, nbt, cur, prev = _swa_specs(T)

    def body(sinks_ref, q_ref, k_ref, kp_ref, v_ref, vp_ref, o_ref):
        bias, bias0 = _swa_bias(pl.program_id(0) == 0)
        sink = [_sink_row(sinks_ref, kv) for kv in range(N_KV_HEADS)]
        kx = jnp.concatenate([kp_ref[...], k_ref[...]], axis=0)
        vx = jnp.concatenate([vp_ref[...], v_ref[...]], axis=0)
        for b in range(nbt):
            qv = q_ref[_rows(b), :] * SM_SCALE
            kk, vv = kx[b * BLOCK:(b + 2) * BLOCK], vx[b * BLOCK:(b + 2) * BLOCK]
            for kv in range(N_KV_HEADS):
                hs = slice(kv * HEAD_DIM, (kv + 1) * HEAD_DIM)
                p, _ = _swa_probs(kk[:, hs], _stack_heads(qv, kv), sink[kv], bias0 if b == 0 else bias)
                o_t = _tn(vv[:, hs], p.astype(MXU))
                for g in range(Q_PER_KV):
                    c0 = (kv * Q_PER_KV + g) * HEAD_DIM
                    o_ref[_rows(b), c0:c0 + HEAD_DIM] = o_t[:, _rows(g)].T

    return _pcall(body, name=name, grid=(T // tq,),
                  in_specs=[pl.BlockSpec(memory_space=pltpu.SMEM), cur(ATTN_WIDTH), cur(KV_WIDTH), prev(KV_WIDTH),
                            cur(KV_WIDTH), prev(KV_WIDTH)],
                  out_specs=cur(ATTN_WIDTH), out_shape=_S((T, ATTN_WIDTH), F32),
                  compiler_params=_cp("parallel"))(sinks, q, k, k, v, v)


def _swa_bwd(q, k, v, do, sinks, name):
    T = q.shape[0]
    tq, nbt, cur, prev = _swa_specs(T)
    per_tile = lambda w: pl.BlockSpec((BLOCK, w), lambda i: (i, 0))

    def add(acc, t):
        return t if acc is None else acc + t

    def body(sinks_ref, q_ref, k_ref, kp_ref, v_ref, vp_ref, do_ref,
             dq_ref, dk_ref, dkp_ref, dv_ref, dvp_ref, dsk_ref):
        bias, bias0 = _swa_bias(pl.program_id(0) == 0)
        sink = [_sink_row(sinks_ref, kv) for kv in range(N_KV_HEADS)]
        kx = jnp.concatenate([kp_ref[...], k_ref[...]], axis=0)
        vx = jnp.concatenate([vp_ref[...], v_ref[...]], axis=0)

        @pl.when(pl.program_id(0) == 0)
        def _():
            dsk_ref[...] = jnp.zeros_like(dsk_ref)

        dk_acc, dv_acc = [None] * (nbt + 1), [None] * (nbt + 1)
        dsk_acc = [None] * N_Q_HEADS
        for b in range(nbt):
            qv, dov = q_ref[_rows(b), :] * SM_SCALE, do_ref[_rows(b), :].astype(MXU)
            kk, vv = kx[b * BLOCK:(b + 2) * BLOCK], vx[b * BLOCK:(b + 2) * BLOCK]
            dks, dvs = [], []
            for kv in range(N_KV_HEADS):
                hs = slice(kv * HEAD_DIM, (kv + 1) * HEAD_DIM)
                qs, dos = _stack_heads(qv, kv), _stack_heads(dov, kv)
                p, ps = _swa_probs(kk[:, hs], qs, sink[kv], bias0 if b == 0 else bias)
                dp = _nt(vv[:, hs], dos)
                delta = jnp.sum(p * dp, axis=0, keepdims=True)
                ds = (p * (dp - delta)).astype(MXU)
                dq_t = _tn(kk[:, hs], ds) * SM_SCALE
                dks.append(_nn(ds, qs))
                dvs.append(_nn(p.astype(MXU), dos))
                dsink = -ps * delta
                for g in range(Q_PER_KV):
                    h = kv * Q_PER_KV + g
                    dq_ref[_rows(b), h * HEAD_DIM:(h + 1) * HEAD_DIM] = dq_t[:, _rows(g)].T
                    dsk_acc[h] = add(dsk_acc[h], jnp.sum(dsink[:, _rows(g)], axis=1, keepdims=True))
            dk, dv = jnp.concatenate(dks, axis=1), jnp.concatenate(dvs, axis=1)
            dk_acc[b], dk_acc[b + 1] = add(dk_acc[b], dk[:BLOCK]), add(dk_acc[b + 1], dk[BLOCK:])
            dv_acc[b], dv_acc[b + 1] = add(dv_acc[b], dv[:BLOCK]), add(dv_acc[b + 1], dv[BLOCK:])
        dkp_ref[...], dvp_ref[...] = dk_acc[0], dv_acc[0]
        dk_ref[...] = jnp.concatenate(dk_acc[1:], axis=0)
        dv_ref[...] = jnp.concatenate(dv_acc[1:], axis=0)
        for h in range(N_Q_HEADS):
            dsk_ref[h:h + 1, :] += jnp.broadcast_to(dsk_acc[h], (1, LANES))

    kvs, kvp = _S((T, KV_WIDTH), F32), _S((T // tq * BLOCK, KV_WIDTH), F32)
    return _pcall(body, name=name, grid=(T // tq,),
                  in_specs=[pl.BlockSpec(memory_space=pltpu.SMEM), cur(ATTN_WIDTH), cur(KV_WIDTH), prev(KV_WIDTH),
                            cur(KV_WIDTH), prev(KV_WIDTH), cur(ATTN_WIDTH)],
                  out_specs=[cur(ATTN_WIDTH), cur(KV_WIDTH), per_tile(KV_WIDTH), cur(KV_WIDTH), per_tile(KV_WIDTH),
                             _resident((N_Q_HEADS, LANES))],
                  out_shape=[_S((T, ATTN_WIDTH), F32), kvs, kvp, kvs, kvp, _S((N_Q_HEADS, LANES), F32)],
                  compiler_params=_cp("arbitrary"))(sinks, q, k, k, v, v, do)


def _dproj(dq, dk, dkp, dv, dvp, du, cos2, sin2, name):
    T = dq.shape[0]
    tq, nbt, cur, _ = _swa_specs(T)
    nt = T // tq
    nxt = lambda w: pl.BlockSpec((BLOCK, w), lambda i: (jnp.minimum(i + 1, nt - 1), 0))

    def body(dq_ref, dk_ref, dkp_ref, dv_ref, dvp_ref, du_ref, c_ref, s_ref, o_ref):
        more = (pl.program_id(0) < nt - 1).astype(F32)
        cv, sv = c_ref[...], s_ref[...]

        def whole(t_ref, p_ref):
            t, last = t_ref[...], t_ref[tq - BLOCK:, :] + more * p_ref[...]
            return last if nbt == 1 else jnp.concatenate([t[:tq - BLOCK], last], axis=0)

        o_ref[...] = jnp.concatenate(
            [_rope_bwd(dq_ref[...], cv, sv), _rope_bwd(whole(dk_ref, dkp_ref), cv, sv), whole(dv_ref, dvp_ref),
             du_ref[...]], axis=1).astype(o_ref.dtype)

    width = ATTN_WIDTH + 2 * KV_WIDTH + POOL_WIDTH
    return _pcall(body, name=name, grid=(nt,),
                  in_specs=[cur(ATTN_WIDTH), cur(KV_WIDTH), nxt(KV_WIDTH), cur(KV_WIDTH), nxt(KV_WIDTH),
                            cur(POOL_WIDTH), cur(LANES), cur(LANES)],
                  out_specs=cur(width), out_shape=_S((T, width), MXU),
                  compiler_params=_cp("parallel"))(dq, dk, dkp, dv, dvp, du, cos2, sin2)


def _pool_specs(T):
    tm = _tile(T, TOKEN_TILE)
    hb = tm // POOL_HALO
    nh = T // POOL_HALO
    tok = lambda w: pl.BlockSpec((tm, w), lambda i: (i, 0))
    before = pl.BlockSpec((POOL_HALO, POOL_WIDTH), lambda i: (jnp.maximum(i * hb - 1, 0), 0))
    after = pl.BlockSpec((POOL_HALO, POOL_WIDTH), lambda i: (jnp.minimum((i + 1) * hb, nh - 1), 0))
    return tm, tok, before, after


def _window_counts(i, tm, rows, w):
    t = i * tm + lax.broadcasted_iota(jnp.int32, (rows, 1), 0)
    return jnp.minimum(t + 1, w).astype(F32)


def _pooled(u_ext, i, tm):
    out = []
    for g, w in enumerate(POOL_WINDOWS):
        acc = u_ext[:, g * POOL_GROUP:(g + 1) * POOL_GROUP]
        tok = acc[POOL_HALO:, :]
        sh = 1
        while sh < w:
            acc = acc + pltpu.roll(acc, sh, 0)
            sh *= 2
        out.append(acc[POOL_HALO:, :] / _window_counts(i, tm, tm, w) - tok)
    return out


def _pool_fwd(u, out_a, pool_w, pool_scale, ga, gb, name):
    T = u.shape[0]
    tm, tok, before, _ = _pool_specs(T)

    def body(u_ref, halo_ref, oa_ref, pw_ref, sc_ref, ga_ref, gb_ref, ob_ref, mg_ref):
        i = pl.program_id(0)
        halo = halo_ref[...] * (i > 0).astype(F32)
        pooled = _pooled(jnp.concatenate([halo, u_ref[...]], axis=0), i, tm)
        mixed = [_nn(pooled[g].astype(MXU), pw_ref[g].astype(MXU)) for g in range(len(POOL_WINDOWS))]
        ob = jnp.concatenate(mixed, axis=1) * sc_ref[...]
        ob_ref[...] = ob
        oa = oa_ref[...]
        ra = lax.rsqrt(jnp.mean(oa * oa, axis=-1, keepdims=True) + EPS)
        rb = lax.rsqrt(jnp.mean(ob * ob, axis=-1, keepdims=True) + EPS)
        mg_ref[...] = jnp.concatenate([oa * ra * ga_ref[...], ob * rb * gb_ref[...]], axis=1).astype(mg_ref.dtype)

    vec = _resident((1, POOL_WIDTH))
    return _pcall(body, name=name, grid=(T // tm,),
                  in_specs=[tok(POOL_WIDTH), before, tok(ATTN_WIDTH), _resident(pool_w.shape), vec, vec, vec],
                  out_specs=[tok(POOL_WIDTH), tok(ATTN_WIDTH + POOL_WIDTH)],
                  out_shape=[_S((T, POOL_WIDTH), F32), _S((T, ATTN_WIDTH + POOL_WIDTH), MXU)],
                  compiler_params=_cp("parallel"))(u, u, out_a, pool_w, pool_scale, ga, gb)


def _pool_bwd(u, dob, pool_w, pool_scale, name):
    T = u.shape[0]
    tm, tok, before, after = _pool_specs(T)
    nt = T // tm
    G = len(POOL_WINDOWS)

    def body(u_ref, halo_ref, dob_ref, dnext_ref, pw_ref, sc_ref, du_ref, dpw_ref, dsc_ref):
        i = pl.program_id(0)
        halo = halo_ref[...] * (i > 0).astype(F32)
        pooled = _pooled(jnp.concatenate([halo, u_ref[...]], axis=0), i, tm)
        dnext = dnext_ref[...] * (i < nt - 1).astype(F32)
        dext = jnp.concatenate([dob_ref[...], dnext], axis=0) * sc_ref[...]

        @pl.when(i == 0)
        def _():
            dpw_ref[...] = jnp.zeros_like(dpw_ref)
            dsc_ref[...] = jnp.zeros_like(dsc_ref)

        dus, dscs = [], []
        for g, w in enumerate(POOL_WINDOWS):
            gs = slice(g * POOL_GROUP, (g + 1) * POOL_GROUP)
            pw = pw_ref[g].astype(MXU)
            pg = pooled[g].astype(MXU)
            dmix = dext[:, gs].astype(MXU)
            dscs.append(jnp.sum(dob_ref[:, gs] * _nn(pg, pw), axis=0, keepdims=True))
            dpw_ref[g] += _tn(pg, dmix[:tm, :])
            dpooled = _nt(dmix, pw)
            acc = dpooled / _window_counts(i, tm, tm + POOL_HALO, w)
            sh = 1
            while sh < w:
                acc = acc + pltpu.roll(acc, tm + POOL_HALO - sh, 0)
                sh *= 2
            dus.append(acc[:tm, :] - dpooled[:tm, :])
        du_ref[...] = jnp.concatenate(dus, axis=1)
        dsc_ref[...] += jnp.concatenate(dscs, axis=1)

    vec = _resident((1, POOL_WIDTH))
    return _pcall(body, name=name, grid=(nt,),
                  in_specs=[tok(POOL_WIDTH), before, tok(POOL_WIDTH), after, _resident(pool_w.shape), vec],
                  out_specs=[tok(POOL_WIDTH), _resident(pool_w.shape), vec],
                  out_shape=[_S((T, POOL_WIDTH), F32), _S(pool_w.shape, F32), _S((1, POOL_WIDTH), F32)],
                  compiler_params=_cp("arbitrary"))(u, u, dob, dob, pool_w, pool_scale)


def _xattn_probs(qh, kh):
    s = _nt(qh, kh) * (X_HEAD_DIM ** -0.5)
    e = jnp.exp(s - jnp.max(s, axis=1, keepdims=True))
    return e / jnp.sum(e, axis=1, keepdims=True)


def _xattn_fwd(q, kvm, name):
    T, XW = q.shape
    tm = _tile(T, TOKEN_TILE)

    def body(q_ref, kv_ref, o_ref):
        for h in range(X_HEADS):
            hs = slice(h * X_HEAD_DIM, (h + 1) * X_HEAD_DIM)
            vs = slice(XW + h * X_HEAD_DIM, XW + (h + 1) * X_HEAD_DIM)
            p = _xattn_probs(q_ref[:, hs], kv_ref[:, hs])
            o_ref[:, hs] = _nn(p.astype(MXU), kv_ref[:, vs]).astype(o_ref.dtype)

    return _pcall(body, name=name, grid=(T // tm,),
                  in_specs=[pl.BlockSpec((tm, XW), lambda i: (i, 0)), _resident(kvm.shape)],
                  out_specs=pl.BlockSpec((tm, XW), lambda i: (i, 0)), out_shape=_S((T, XW), MXU),
                  compiler_params=_cp("parallel"))(q, kvm)


def _xattn_bwd(q, kvm, do, name):
    T, XW = q.shape
    tm = _tile(T, TOKEN_TILE)

    def body(q_ref, kv_ref, do_ref, dq_ref, dkv_ref):
        @pl.when(pl.program_id(0) == 0)
        def _():
            dkv_ref[...] = jnp.zeros_like(dkv_ref)

        for h in range(X_HEADS):
            hs = slice(h * X_HEAD_DIM, (h + 1) * X_HEAD_DIM)
            vs = slice(XW + h * X_HEAD_DIM, XW + (h + 1) * X_HEAD_DIM)
            qh, doh = q_ref[:, hs], do_ref[:, hs]
            p = _xattn_probs(qh, kv_ref[:, hs])
            dp = _nt(doh, kv_ref[:, vs])
            ds = (p * (dp - jnp.sum(p * dp, axis=1, keepdims=True)) * (X_HEAD_DIM ** -0.5)).astype(MXU)
            dq_ref[:, hs] = _nn(ds, kv_ref[:, hs]).astype(dq_ref.dtype)
            dkv_ref[:, hs] += _tn(ds, qh)
            dkv_ref[:, vs] += _tn(p.astype(MXU), doh)

    tok = pl.BlockSpec((tm, XW), lambda i: (i, 0))
    return _pcall(body, name=name, grid=(T // tm,),
                  in_specs=[tok, _resident(kvm.shape), tok],
                  out_specs=[tok, _resident(kvm.shape)],
                  out_shape=[_S((T, XW), MXU), _S(kvm.shape, F32)],
                  compiler_params=_cp("arbitrary"))(q, kvm, do)


def _loss_head(x, tgt, g, name):
    T, D = x.shape
    tm = _tile(T, TOKEN_TILE)

    def body(x_ref, t_ref, g_ref, loss_ref, dx_ref, half_ref, dg_ref):
        xv, gv = x_ref[...], g_ref[...]
        r = lax.rsqrt(jnp.mean(xv * xv, axis=-1, keepdims=True) + EPS)
        xh = xv * r
        e = xh * gv - t_ref[...]
        dy = e * (1.0 / D)
        dxn = dy * gv
        dx = r * (dxn - xh * jnp.mean(dxn * xh, axis=-1, keepdims=True))
        dx_ref[...] = dx
        half_ref[...] = (dx * FFN_RES).astype(MXU)

        @pl.when(pl.program_id(0) == 0)
        def _():
            loss_ref[...] = jnp.zeros_like(loss_ref)
            dg_ref[...] = jnp.zeros_like(dg_ref)

        part = jnp.sum(jnp.sum(e * e, axis=1, keepdims=True), axis=0, keepdims=True) * (0.5 / D)
        loss_ref[...] += jnp.broadcast_to(part, (1, LANES))
        dg_ref[...] += jnp.sum(dy * xh, axis=0, keepdims=True)

    tok = pl.BlockSpec((tm, D), lambda i: (i, 0))
    return _pcall(body, name=name, grid=(T // tm,),
                  in_specs=[tok, tok, _resident((1, D))],
                  out_specs=[_resident((1, LANES)), tok, tok, _resident((1, D))],
                  out_shape=[_S((1, LANES), F32), _S((T, D), F32), _S((T, D), MXU), _S((1, D), F32)],
                  compiler_params=_cp("arbitrary"))(x, tgt, g)


def _rows_tile(rows):
    for t in (512, 416, 352, 256, 128, 64, 32, 16, 8):
        if rows % t == 0:
            return t
    return rows


def _pair_sum(grads, gots, sizes, place, name):
    nw = len(sizes)
    C = grads[0].shape[1]

    def body(p_ref, *refs):
        g, got, out = refs[:nw], refs[nw:2 * nw], refs[2 * nw:]
        for w in range(nw):
            out[w][...] = (g[w][...] + got[w][...]).astype(out[w].dtype)

    def blk(w):
        return (sizes[w] // 4, C)

    in_specs = [pl.BlockSpec(blk(w), lambda q, s, p: (4 * q + 2 * p[0] + s, 0)) for w in range(nw)]
    in_specs += [pl.BlockSpec(blk(w), lambda q, s, p: (2 * q + s, 0)) for w in range(nw)]
    out_specs = [pl.BlockSpec(blk(w), lambda q, s, p: (2 * q + s, 0)) for w in range(nw)]
    gs = pltpu.PrefetchScalarGridSpec(num_scalar_prefetch=1, grid=(N_CHIPS, 2), in_specs=in_specs, out_specs=out_specs)
    return _pcall(body, name=name, grid_spec=gs, out_shape=[_S((2 * n, C), MXU) for n in sizes],
                  compiler_params=_cp("parallel", "parallel"))(place, *grads, *gots)


def _final_sum(grads, gots, recvs, sizes, place, name):
    nw = len(sizes)
    C = grads[0].shape[1]

    def body(p_ref, *refs):
        g, got, rv, out = refs[:nw], refs[nw:2 * nw], refs[2 * nw:5 * nw], refs[5 * nw:]
        for w in range(nw):
            acc = g[w][...] + got[w][...]
            for j in range(3):
                acc = acc + rv[3 * w + j][...].astype(F32)
            out[w][...] = acc

    def blk(w):
        return (sizes[w] // 4, C)

    in_specs = [pl.BlockSpec(blk(w), lambda s, p: (4 * p[1] + 2 * p[0] + s, 0)) for w in range(nw)]
    in_specs += [pl.BlockSpec(blk(w), lambda s, p: (2 * p[1] + s, 0)) for w in range(nw)]
    args = list(grads) + list(gots)
    for w in range(nw):
        for j in range(3):
            in_specs.append(pl.BlockSpec(blk(w), lambda s, p, j=j: (2 * j + s, 0)))
            args.append(recvs[w])
    out_specs = [pl.BlockSpec(blk(w), lambda s, p: (2 * p[0] + s, 0)) for w in range(nw)]
    gs = pltpu.PrefetchScalarGridSpec(num_scalar_prefetch=1, grid=(2,), in_specs=in_specs, out_specs=out_specs)
    return _pcall(body, name=name, grid_spec=gs, out_shape=[_S((n, C), F32) for n in sizes],
                  compiler_params=_cp("parallel"))(place, *args)


def _adamw(w, g, m, v, name):
    R, C = w.shape
    tr = _rows_tile(R)
    c1 = 1.0 / (1.0 - ADAM_B1 ** ADAM_STEP)
    c2 = 1.0 / (1.0 - ADAM_B2 ** ADAM_STEP)

    def body(w_ref, g_ref, m_ref, v_ref, d_ref, nm_ref, nv_ref):
        gv = g_ref[...]
        nm = ADAM_B1 * m_ref[...] + (1.0 - ADAM_B1) * gv
        nv = ADAM_B2 * v_ref[...] + (1.0 - ADAM_B2) * (gv * gv)
        d_ref[...] = -ADAM_LR * ((nm * c1) / (jnp.sqrt(nv * c2) + ADAM_EPS) + ADAM_WD * w_ref[...])
        nm_ref[...] = nm
        nv_ref[...] = nv

    spec = pl.BlockSpec((tr, C), lambda i: (i, 0))
    return _pcall(body, name=name, grid=(R // tr,), in_specs=[spec] * 4, out_specs=[spec] * 3,
                  out_shape=[_S((R, C), F32)] * 3, compiler_params=_cp("parallel"))(w, g, m, v)


ANY = pl.BlockSpec(memory_space=pl.ANY)


def _place():
    x, y, c = lax.axis_index("x"), lax.axis_index("y"), lax.axis_index("c")
    chips = [(1 - x, y), (x, 1 - y), (1 - x, 1 - y)]
    return x, y, c, chips


def _remote(src, dst, send_sem, recv_sem, dev):
    return pltpu.make_async_remote_copy(src_ref=src, dst_ref=dst, send_sem=send_sem, recv_sem=recv_sem,
                                        device_id=dev, device_id_type=MESH)


def _drain(like, send_sem, recv_sem, me, *, send=False, recv=False):
    d = _remote(like, like, send_sem, recv_sem, me)
    if send:
        d.wait_send()
    if recv:
        d.wait_recv()


def _dma_sems(n):
    return [pltpu.SemaphoreType.DMA((n,)), pltpu.SemaphoreType.DMA((n,))]


def _comm_params():
    return pltpu.CompilerParams(has_side_effects=True)


def _on_sequencer(exchange, refs, sem_types, peers_of, name, seq_id):
    def launch(*sems):
        x, y, c, chips = _place()
        barrier = pltpu.get_barrier_semaphore()
        peers = peers_of(x, y, c, chips)
        for peer in peers:
            pl.semaphore_signal(barrier, inc=1, device_id=peer, device_id_type=MESH)
        pl.semaphore_wait(barrier, len(peers))
        exchange(refs, *sems)

    pl.kernel(launch, mesh=plsc.ScalarSubcoreMesh(axis_name="seq", num_cores=1), name=name,
              scratch_types=tuple(sem_types), compiler_params=pltpu.CompilerParams(collective_id=seq_id))()


def _hbm_ref(a):
    return jax.new_ref(a, memory_space=pltpu.MemorySpace.HBM)


def _allgather_weights(bufs, sizes, name, seq_id=None):
    nw = len(sizes)

    def exchange(out, s_ici, r_ici, s_fwd, r_fwd):
        x, y, c, chips = _place()
        me, sib = (x, y, c), (x, y, 1 - c)
        q_me = 2 * x + y

        def rows(w, q):
            hw = sizes[w] // 2
            return out[w].at[pl.ds(q * sizes[w] + c * hw, hw)]

        def three(w):
            return out[w].at[pl.ds(0, 3 * (sizes[w] // 2))]

        for w in range(nw):
            for px, py in chips:
                _remote(rows(w, q_me), rows(w, q_me), s_ici.at[w], r_ici.at[w], (px, py, c)).start()
        for w in range(nw):
            _drain(three(w), s_ici.at[w], r_ici.at[w], me, recv=True)
            for px, py in chips:
                got = rows(w, 2 * px + py)
                _remote(got, got, s_fwd.at[w], r_fwd.at[w], sib).start()
        for w in range(nw):
            _drain(three(w), s_fwd.at[w], r_fwd.at[w], me, recv=True)
        for w in range(nw):
            _drain(three(w), s_ici.at[w], r_ici.at[w], me, send=True)
            _drain(three(w), s_fwd.at[w], r_fwd.at[w], me, send=True)

    if seq_id is not None:
        refs = [_hbm_ref(b) for b in bufs]
        _on_sequencer(exchange, refs, _dma_sems(nw) + _dma_sems(nw),
                      lambda x, y, c, chips: [(x, y, 1 - c)] + [(px, py, c) for px, py in chips], name, seq_id)
        return [r[...] for r in refs]

    def body(*refs):
        exchange(refs[nw:2 * nw], *refs[2 * nw:])

    return _pcall(body, name=name, in_specs=[ANY] * nw, out_specs=[ANY] * nw,
                  out_shape=[_S(b.shape, b.dtype) for b in bufs],
                  input_output_aliases={w: w for w in range(nw)},
                  scratch_shapes=_dma_sems(nw) + _dma_sems(nw), compiler_params=_comm_params())(*bufs)


def _sibling_only(x, y, c, chips):
    return [(x, y, 1 - c)]


def _rs_pair_exchange(grads, sizes, name, seq_id=None):
    C = grads[0].shape[1]
    nw = len(sizes)
    out_shape = [_S((2 * n, C), F32) for n in sizes]

    def exchange(refs, s_sem, r_sem):
        g, got = refs[:nw], refs[nw:2 * nw]
        x, y, c, _ = _place()
        me, sib = (x, y, c), (x, y, 1 - c)
        for w in range(nw):
            hw = sizes[w] // 2
            for q in range(N_CHIPS):
                _remote(g[w].at[pl.ds(q * sizes[w] + (1 - c) * hw, hw)], got[w].at[pl.ds(q * hw, hw)],
                        s_sem.at[w], r_sem.at[w], sib).start()
        for w in range(nw):
            _drain(got[w], s_sem.at[w], r_sem.at[w], me, send=True, recv=True)

    if seq_id is not None:
        gots = [jax.empty_ref(s, memory_space=pltpu.MemorySpace.HBM) for s in out_shape]
        _on_sequencer(exchange, [_hbm_ref(g) for g in grads] + gots, _dma_sems(nw), _sibling_only, name, seq_id)
        return [r[...] for r in gots]

    def body(*refs):
        exchange(refs[:2 * nw], *refs[2 * nw:])

    return _pcall(body, name=name, in_specs=[ANY] * nw, out_specs=[ANY] * nw, out_shape=out_shape,
                  scratch_shapes=_dma_sems(nw), compiler_params=_comm_params())(*grads)


def _rs_chip_exchange(sums, sizes, name, seq_id=None):
    C = sums[0].shape[1]
    nw = len(sizes)
    out_shape = [_S((3 * (n // 2), C), sums[0].dtype) for n in sizes]

    def exchange(refs, s_sem, r_sem):
        sm, got = refs[:nw], refs[nw:2 * nw]
        x, y, c, chips = _place()
        for w in range(nw):
            hw = sizes[w] // 2
            for j, (px, py) in enumerate(chips):
                _remote(sm[w].at[pl.ds((2 * px + py) * hw, hw)], got[w].at[pl.ds(j * hw, hw)],
                        s_sem.at[w], r_sem.at[w], (px, py, c)).start()
        for w in range(nw):
            _drain(got[w], s_sem.at[w], r_sem.at[w], (x, y, c), send=True, recv=True)

    if seq_id is not None:
        gots = [jax.empty_ref(s, memory_space=pltpu.MemorySpace.HBM) for s in out_shape]
        _on_sequencer(exchange, [_hbm_ref(s) for s in sums] + gots, _dma_sems(nw),
                      lambda x, y, c, chips: [(px, py, c) for px, py in chips], name, seq_id)
        return [r[...] for r in gots]

    def body(*refs):
        exchange(refs[:2 * nw], *refs[2 * nw:])

    return _pcall(body, name=name, in_specs=[ANY] * nw, out_specs=[ANY] * nw, out_shape=out_shape,
                  scratch_shapes=_dma_sems(nw), compiler_params=_comm_params())(*sums)


def _rs_share_halves(reds, sizes, name, seq_id=None):
    nw = len(sizes)

    def exchange(out, s_sem, r_sem):
        x, y, c, _ = _place()
        for w in range(nw):
            hw = sizes[w] // 2
            rows = out[w].at[pl.ds(c * hw, hw)]
            _remote(rows, rows, s_sem.at[w], r_sem.at[w], (x, y, 1 - c)).start()
        for w in range(nw):
            _drain(out[w].at[pl.ds(0, sizes[w] // 2)], s_sem.at[w], r_sem.at[w], (x, y, c), send=True, recv=True)

    if seq_id is not None:
        refs = [_hbm_ref(r) for r in reds]
        _on_sequencer(exchange, refs, _dma_sems(nw), _sibling_only, name, seq_id)
        return [r[...] for r in refs]

    def body(*refs):
        exchange(refs[nw:2 * nw], *refs[2 * nw:])

    return _pcall(body, name=name, in_specs=[ANY] * nw, out_specs=[ANY] * nw,
                  out_shape=[_S(r.shape, r.dtype) for r in reds],
                  input_output_aliases={w: w for w in range(nw)},
                  scratch_shapes=_dma_sems(nw), compiler_params=_comm_params())(*reds)


def _allreduce_small(part, name):
    R, C = part.shape

    def body(p_ref, o_ref, buf, s_sem, r_sem):
        x, y, c, _ = _place()
        my_id = 4 * x + 2 * y + c
        buf[my_id] = p_ref[...]
        cps = []
        for k in range(1, N_DEV):
            fx, fy, fc = (k >> 2) & 1, (k >> 1) & 1, k & 1
            peer = (x ^ fx, y ^ fy, c ^ fc)
            cps.append(_remote(p_ref, buf.at[my_id], s_sem.at[k - 1], r_sem.at[k - 1], peer))
        for cp in cps:
            cp.start()
        for cp in cps:
            cp.wait()
        acc = buf[0]
        for d in range(1, N_DEV):
            acc = acc + buf[d]
        o_ref[...] = acc

    vm = pl.BlockSpec(memory_space=pltpu.VMEM)
    return _pcall(body, name=name, in_specs=[vm], out_specs=vm, out_shape=_S((R, C), F32),
                  scratch_shapes=[pltpu.VMEM((N_DEV, R, C), F32)] + _dma_sems(N_DEV - 1),
                  compiler_params=pltpu.CompilerParams(has_side_effects=True, vmem_limit_bytes=VMEM_LIMIT))(part)


SHARD_STEPS = 4


def _own_shard_buffers(ws, l, group, place, name):
    nw = len(group)

    def body(p_ref, *refs):
        for (_, tr), i_ref, o_ref in zip(group, refs[:nw], refs[nw:]):
            v = i_ref[...]
            o_ref[...] = (v.T if tr else v).astype(o_ref.dtype)

    in_specs, out_specs, out_shape, sizes = [], [], [], []
    for wname, tr in group:
        _, K, n = ws[wname].shape
        in_specs.append(pl.BlockSpec((None, K // SHARD_STEPS, n), lambda i, p: (l, i, 0)))
        if tr:
            out_specs.append(pl.BlockSpec((n, K // SHARD_STEPS), lambda i, p: (p[1], i)))
            out_shape.append(_S((N_CHIPS * n, K), MXU))
            sizes.append(n)
        else:
            out_specs.append(pl.BlockSpec((K // SHARD_STEPS, n), lambda i, p: (p[1] * SHARD_STEPS + i, 0)))
            out_shape.append(_S((N_CHIPS * K, n), MXU))
            sizes.append(K)
    gs = pltpu.PrefetchScalarGridSpec(num_scalar_prefetch=1, grid=(SHARD_STEPS,), in_specs=in_specs, out_specs=out_specs)
    bufs = _pcall(body, name=name, grid_spec=gs, out_shape=out_shape,
                  compiler_params=_cp("parallel"))(place, *[ws[wname] for wname, _ in group])
    return list(bufs), sizes


def _after(xs, ys):
    return lax.optimization_barrier((xs, ys))[0]


def _small_rows(v):
    flat = v.reshape(-1)
    pad = (-flat.shape[0]) % 1024
    return jnp.pad(flat, (0, pad)).reshape(-1, 1024)


def _pack_small(vals):
    rows = [_small_rows(vals[n]) for n in SMALL]
    cat = jnp.concatenate(rows, axis=0)
    pad = (-cat.shape[0]) % 8
    return jnp.pad(cat, ((0, pad), (0, 0)))


def _unpack_small(packed, like):
    out, r = {}, 0
    for n in SMALL:
        size = like[n].size
        nr = -(-size // 1024)
        out[n] = packed[r:r + nr].reshape(-1)[:size].reshape(like[n].shape)
        r += nr
    return out


def _rope_tables(positions):
    inv_freq = ROPE_THETA ** (-jnp.arange(0, HEAD_DIM, 2, dtype=F32) / HEAD_DIM)
    ang = positions.astype(F32)[:, None] * inv_freq
    cos, sin = jnp.cos(ang), jnp.sin(ang)
    return jnp.concatenate([cos, cos, cos, cos], axis=1), jnp.concatenate([-sin, sin, -sin, sin], axis=1)


def _layer_fwd(l, x, h1, memv, W, P, cos2, sin2, next_norm):
    t = f"l{l}"
    sv = {"x0": x, "h1": h1}
    sv["a1"], sv["b1"], sv["s1"] = _ffn_up(h1, W["ffn1_w_gate"], W["ffn1_w_up"], t + "_ffn1_up")
    sv["x1"], sv["h2"] = _mm([(sv["s1"], W["ffn1_w_down"])], nt=False, out_dtype=F32, res=x, res_scale=FFN_RES,
                             name=t + "_ffn1_down", norm_g=P["mix_norm"])

    sv["q"], sv["k"], sv["v"], sv["u"] = _in_proj(sv["h2"], W["w_in"], cos2, sin2, t + "_in_proj")
    sv["oa"] = _swa_fwd(sv["q"], sv["k"], sv["v"], P["attn_sinks"], t + "_swa")
    sv["ob"], sv["mg"] = _pool_fwd(sv["u"], sv["oa"], P["pool_w"], P["pool_scale"], P["attn_out_norm"],
                                   P["pool_out_norm"], t + "_pool")
    sv["x2"], sv["h3"] = _mm([(sv["mg"], W["w_out"])], nt=False, out_dtype=F32, res=sv["x1"], name=t + "_out_proj",
                             norm_g=P["xattn_norm"])

    sv["memn"] = _rms_fwd(memv, P["mem_norm"], t + "_mem_norm")
    sv["q3"] = _mm([(sv["h3"], W["xattn_wq"])], nt=False, out_dtype=MXU, name=t + "_xq")
    sv["kv"] = _mm([(sv["memn"], W["xattn_wkv"])], nt=True, out_dtype=MXU, name=t + "_xkv")
    sv["o3"] = _xattn_fwd(sv["q3"], sv["kv"], t + "_xattn")
    sv["x3"], sv["h4"] = _mm([(sv["o3"], W["xattn_wo"])], nt=False, out_dtype=F32, res=sv["x2"], name=t + "_xo",
                             norm_g=P["ffn2_norm"])

    sv["a2"], sv["b2"], sv["s2"] = _ffn_up(sv["h4"], W["ffn2_w_gate"], W["ffn2_w_up"], t + "_ffn2_up")
    out = _mm([(sv["s2"], W["ffn2_w_down"])], nt=False, out_dtype=F32, res=sv["x3"], res_scale=FFN_RES,
              name=t + "_ffn2_down", norm_g=next_norm)
    x4, h_next = out if next_norm is not None else (out, None)
    return x4, h_next, sv


def _ffn_bwd(t, dx, dy, x_in, g, h, a, b, s, wgT, wuT, wd, behind=None, begin=None, half=False):
    d_wd = _mm_tn(s, dy, t + "_dwd")
    da, db = _ffn_mid_bwd(dy, wd, a, b, t + "_mid")
    if behind is not None:
        da = _after(da, behind)
    d_wg = _mm_tn(da, h, t + "_dwg")
    d_wu = _mm_tn(db, h, t + "_dwu")
    if begin is not None:
        da = _after(da, begin(d_wg, d_wu, d_wd))
    out = _mm_norm_bwd([(da, wgT), (db, wuT)], [(x_in, g, dx)], nt=False, name=t + "_dh", tm=TOKEN_TILE // 2,
                       half=half)
    return out[0][0], (out[2] if half else None), out[1][0], d_wg, d_wu, d_wd


def _layer_bwd(l, dx, dy, sv, memv, W, P, cos2, sin2, begin_rest, begin_ffn1):
    t = f"l{l}b"
    GW, GP, marks = {}, {}, {}
    dx, _, GP["ffn2_norm"], GW["ffn2_w_gate"], GW["ffn2_w_up"], GW["ffn2_w_down"] = _ffn_bwd(
        t + "_ffn2", dx, dy, sv["x3"], P["ffn2_norm"], sv["h4"], sv["a2"], sv["b2"], sv["s2"],
        W["ffn2_w_gate"], W["ffn2_w_up"], W["ffn2_w_down"])
    marks["ffn2"] = dx

    GW["xattn_wo"] = _mm_tn(sv["o3"], dx, t + "_dwo")
    do3 = _mm([(dx, W["xattn_wo"])], nt=True, out_dtype=MXU, name=t + "_do3")
    dq3, dkv = _xattn_bwd(sv["q3"], sv["kv"], do3, t + "_xattn")
    GW["xattn_wq"] = _mm_tn(sv["h3"], dq3, t + "_dwq")
    GW["xattn_wkv"] = _mm_tn(dkv, sv["memn"], t + "_dwkv")
    dmemn = _mm([(dkv, W["xattn_wkv"])], nt=False, out_dtype=F32, name=t + "_dmemn")
    _, GP["mem_norm"] = _rms_bwd(memv, P["mem_norm"], dmemn, t + "_mem_norm_bwd")
    (dx,), (GP["xattn_norm"],) = _mm_norm_bwd([(dq3, W["xattn_wq"])], [(sv["x2"], P["xattn_norm"], dx)], nt=True,
                                             name=t + "_dh3")
    marks["xattn"] = dx

    GW["w_out"] = _mm_tn(sv["mg"], dx, t + "_dwout")
    (doa, dob), (GP["attn_out_norm"], GP["pool_out_norm"]) = _mm_norm_bwd(
        [(dx, W["w_out"])], [(sv["oa"], P["attn_out_norm"], None), (sv["ob"], P["pool_out_norm"], None)], nt=True,
        name=t + "_dmg")
    du, GP["pool_w"], GP["pool_scale"] = _pool_bwd(sv["u"], dob, P["pool_w"], P["pool_scale"], t + "_pool")
    dq, dko, dkp, dvo, dvp, dsk = _swa_bwd(sv["q"], sv["k"], sv["v"], doa, P["attn_sinks"], t + "_swa")
    GP["attn_sinks"] = dsk[:, 0]
    dpj = _dproj(dq, dko, dkp, dvo, dvp, du, cos2, sin2, t + "_dproj")
    GW["w_in"] = _mm_tn(dpj, sv["h2"], t + "_dwin")
    (dx,), (GP["mix_norm"],), dy = _mm_norm_bwd([(dpj, W["w_in"])], [(sv["x1"], P["mix_norm"], dx)], nt=False,
                                               name=t + "_dh2", half=True)

    dx, dy, GP["ffn1_norm"], _, _, _ = _ffn_bwd(
        t + "_ffn1", dx, dy, sv["x0"], P["ffn1_norm"], sv["h1"], sv["a1"], sv["b1"], sv["s1"],
        W["ffn1_w_gate"], W["ffn1_w_up"], W["ffn1_w_down"], behind=begin_rest(GW), begin=begin_ffn1, half=l > 0)
    return dx, dy, GP, marks


def _reduce_begin(t, grads, sizes, place, seq_ids):
    gots = _rs_pair_exchange(grads, sizes, t + "_pair")
    sums = _pair_sum(grads, gots, sizes, place, t + "_pair_sum")
    recvs = _rs_chip_exchange(sums, sizes, t + "_chips", seq_id=next(seq_ids))
    return dict(t=t, grads=grads, gots=gots, sums=sums, recvs=recvs, sizes=sizes)


def _reduce_end(st, place, seq_ids, late=None):
    recvs = st["recvs"] if late is None else _after(st["recvs"], late)
    reds = _final_sum(st["grads"], st["gots"], recvs, st["sizes"], place, st["t"] + "_final_sum")
    return _rs_share_halves(reds, st["sizes"], st["t"] + "_share")


def _adamw_layer(l, w3, g, m3, v3, transposed, prev, name):
    _, K, n = w3.shape
    if transposed:
        tr = K // SHARD_STEPS
        g_spec = pl.BlockSpec((n, tr), lambda i: (0, i))
    else:
        tr = _rows_tile(K)
        g_spec = pl.BlockSpec((tr, n), lambda i: (i, 0))
    c1 = 1.0 / (1.0 - ADAM_B1 ** ADAM_STEP)
    c2 = 1.0 / (1.0 - ADAM_B2 ** ADAM_STEP)

    def body(w_ref, g_ref, m_ref, v_ref, *rest):
        go_ref, d_ref, nm_ref, nv_ref = rest[-4:]
        gv = g_ref[...].T if transposed else g_ref[...]
        nm = ADAM_B1 * m_ref[...] + (1.0 - ADAM_B1) * gv
        nv = ADAM_B2 * v_ref[...] + (1.0 - ADAM_B2) * (gv * gv)
        go_ref[...] = gv
        d_ref[...] = -ADAM_LR * ((nm * c1) / (jnp.sqrt(nv * c2) + ADAM_EPS) + ADAM_WD * w_ref[...])
        nm_ref[...] = nm
        nv_ref[...] = nv

    slab = pl.BlockSpec((None, tr, n), lambda i: (l, i, 0))
    in_specs, args, aliases = [slab, g_spec, slab, slab], [w3, g, m3, v3], {}
    if prev is not None:
        in_specs += [ANY] * 4
        args += list(prev)
        aliases = {4 + j: j for j in range(4)}
    return _pcall(body, name=name, grid=(K // tr,), in_specs=in_specs, out_specs=[slab] * 4,
                  out_shape=[_S(w3.shape, F32)] * 4, input_output_aliases=aliases,
                  compiler_params=_cp("parallel"))(*args)


def kernel(x, mem, positions, ffn1_norm, ffn1_w_gate, ffn1_w_up, ffn1_w_down, mix_norm, w_in, attn_sinks, pool_w, pool_scale, attn_out_norm, pool_out_norm, w_out, xattn_norm, mem_norm, xattn_wq, xattn_wkv, xattn_wo, ffn2_norm, ffn2_w_gate, ffn2_w_up, ffn2_w_down, final_norm, loss_target, m_ffn1_norm, m_ffn1_w_gate, m_ffn1_w_up, m_ffn1_w_down, m_mix_norm, m_w_in, m_attn_sinks, m_pool_w, m_pool_scale, m_attn_out_norm, m_pool_out_norm, m_w_out, m_xattn_norm, m_mem_norm, m_xattn_wq, m_xattn_wkv, m_xattn_wo, m_ffn2_norm, m_ffn2_w_gate, m_ffn2_w_up, m_ffn2_w_down, m_final_norm, v_ffn1_norm, v_ffn1_w_gate, v_ffn1_w_up, v_ffn1_w_down, v_mix_norm, v_w_in, v_attn_sinks, v_pool_w, v_pool_scale, v_attn_out_norm, v_pool_out_norm, v_w_out, v_xattn_norm, v_mem_norm, v_xattn_wq, v_xattn_wkv, v_xattn_wo, v_ffn2_norm, v_ffn2_w_gate, v_ffn2_w_up, v_ffn2_w_down, v_final_norm):
    ws = dict(ffn1_norm=ffn1_norm, ffn1_w_gate=ffn1_w_gate, ffn1_w_up=ffn1_w_up, ffn1_w_down=ffn1_w_down,
              mix_norm=mix_norm, w_in=w_in, attn_sinks=attn_sinks, pool_w=pool_w, pool_scale=pool_scale,
              attn_out_norm=attn_out_norm, pool_out_norm=pool_out_norm, w_out=w_out, xattn_norm=xattn_norm,
              mem_norm=mem_norm, xattn_wq=xattn_wq, xattn_wkv=xattn_wkv, xattn_wo=xattn_wo, ffn2_norm=ffn2_norm,
              ffn2_w_gate=ffn2_w_gate, ffn2_w_up=ffn2_w_up, ffn2_w_down=ffn2_w_down, final_norm=final_norm)
    ms = dict(ffn1_norm=m_ffn1_norm, ffn1_w_gate=m_ffn1_w_gate, ffn1_w_up=m_ffn1_w_up, ffn1_w_down=m_ffn1_w_down,
              mix_norm=m_mix_norm, w_in=m_w_in, attn_sinks=m_attn_sinks, pool_w=m_pool_w, pool_scale=m_pool_scale,
              attn_out_norm=m_attn_out_norm, pool_out_norm=m_pool_out_norm, w_out=m_w_out, xattn_norm=m_xattn_norm,
              mem_norm=m_mem_norm, xattn_wq=m_xattn_wq, xattn_wkv=m_xattn_wkv, xattn_wo=m_xattn_wo,
              ffn2_norm=m_ffn2_norm, ffn2_w_gate=m_ffn2_w_gate, ffn2_w_up=m_ffn2_w_up, ffn2_w_down=m_ffn2_w_down,
              final_norm=m_final_norm)
    vs = dict(ffn1_norm=v_ffn1_norm, ffn1_w_gate=v_ffn1_w_gate, ffn1_w_up=v_ffn1_w_up, ffn1_w_down=v_ffn1_w_down,
              mix_norm=v_mix_norm, w_in=v_w_in, attn_sinks=v_attn_sinks, pool_w=v_pool_w, pool_scale=v_pool_scale,
              attn_out_norm=v_attn_out_norm, pool_out_norm=v_pool_out_norm, w_out=v_w_out, xattn_norm=v_xattn_norm,
              mem_norm=v_mem_norm, xattn_wq=v_xattn_wq, xattn_wkv=v_xattn_wkv, xattn_wo=v_xattn_wo,
              ffn2_norm=v_ffn2_norm, ffn2_w_gate=v_ffn2_w_gate, ffn2_w_up=v_ffn2_w_up, ffn2_w_down=v_ffn2_w_down,
              final_norm=v_final_norm)
    depth = ffn1_norm.shape[0]
    T, D = x.shape[1], x.shape[2]
    xv = x.reshape(T, D)
    memv = mem.reshape(mem.shape[1], D)
    tgt = loss_target.reshape(T, D)
    cos2, sin2 = _rope_tables(positions.reshape(T))
    in_kernel = {name: tr and ws[name].shape[2] % LANES == 0 for name, tr in BIG}
    swapped = [name for name, tr in BIG if tr and not in_kernel[name]]
    rows = lambda d: {name: (jnp.swapaxes(d[name], 1, 2) if name in swapped else d[name]) for name, _ in BIG}
    wr, mr, vr = rows(ws), rows(ms), rows(vs)
    groups = [[(name, in_kernel[name]) for name, _ in g] for g in GROUPS]

    q_me = 2 * lax.axis_index("x") + lax.axis_index("y")
    place = jnp.stack([lax.axis_index("c"), q_me]).astype(jnp.int32)
    seq_ids = iter(range(1, 1 + 8 * depth))
    Ws, sizes, first = [dict() for _ in range(depth)], {}, None
    for l in range(depth):
        for gi, group in enumerate(groups):
            t = f"l{l}g{gi}"
            bufs, sizes[gi] = _own_shard_buffers(wr, l, group, place, t + "_shard")
            if first is None:
                full = first = _allgather_weights(bufs, sizes[gi], t + "_allgather")
            else:
                full = _allgather_weights(_after(bufs, first), sizes[gi], t + "_allgather", seq_id=next(seq_ids))
            Ws[l].update({name: f for (name, _), f in zip(group, full)})
    Ps = [{n: (ws[n][l].reshape(1, -1) if n != "pool_w" else ws[n][l]) for n in SMALL if n != "final_norm"}
          for l in range(depth)]

    saved = []
    h, hn = xv, _rms_fwd(xv, Ps[0]["ffn1_norm"], "l0_ffn1_norm")
    for l in range(depth):
        next_norm = Ps[l + 1]["ffn1_norm"] if l + 1 < depth else None
        h, hn, sv = _layer_fwd(l, h, hn, memv, Ws[l], Ps[l], cos2, sin2, next_norm)
        saved.append(sv)
    loss_row, dx, dy, d_final = _loss_head(h, tgt, final_norm.reshape(1, D), "loss_head")
    GPs, marks, begun = [None] * depth, [None] * depth, {}
    for l in reversed(range(depth)):
        def begin_rest(GW, l=l):
            begun[l, 1] = _reduce_begin(f"l{l}g1r", [GW[name] for name, _ in groups[1]], sizes[1], place, seq_ids)
            return begun[l, 1]["sums"]

        def begin_ffn1(*gs, l=l):
            begun[l, 0] = _reduce_begin(f"l{l}g0r", list(gs), sizes[0], place, seq_ids)
            return begun[l, 0]["sums"]

        dx, dy, GPs[l], marks[l] = _layer_bwd(l, dx, dy, saved[l], memv, Ws[l], Ps[l], cos2, sin2, begin_rest,
                                              begin_ffn1)

    stacked = {}
    for l in reversed(range(depth)):
        for gi, group in reversed(list(enumerate(groups))):
            if l > 0:
                late = marks[l - 1]["ffn2" if gi == 1 else "xattn"]
            else:
                late = dx if gi == 1 else None
            reds = _reduce_end(begun[l, gi], place, seq_ids, late)
            for (name, tr), red in zip(group, reds):
                stacked[name] = _adamw_layer(l, wr[name], red, mr[name], vr[name], tr, stacked.get(name),
                                             f"l{l}_adamw_{name}")
    for name in swapped:
        stacked[name] = [jnp.swapaxes(a, 1, 2) for a in stacked[name]]
    small_part = {n: jnp.stack([GPs[l][n].reshape(ws[n].shape[1:]) for l in range(depth)]) for n in SMALL if n != "final_norm"}
    small_part["final_norm"] = d_final.reshape(D)
    small_g = _unpack_small(_allreduce_small(_pack_small(small_part), "small_allreduce"), ws)
    loss = lax.psum(loss_row[0, 0], ("x", "y", "c"))

    grads, deltas, new_m, new_v = {}, {}, {}, {}
    for name, _ in BIG:
        grads[name], deltas[name], new_m[name], new_v[name] = stacked[name]
    d, nm, nv = _adamw(_pack_small(ws), _pack_small(small_g), _pack_small(ms), _pack_small(vs), "adamw_small")
    grads.update(small_g)
    deltas.update(_unpack_small(d, ws))
    new_m.update(_unpack_small(nm, ws))
    new_v.update(_unpack_small(nv, ws))

    grad_x = dx.reshape(x.shape)
    return (loss, grad_x, *[grads[n] for n in WEIGHTS], *[deltas[n] for n in WEIGHTS],
            *[new_m[n] for n in WEIGHTS], *[new_v[n] for n in WEIGHTS])
```

```python
import functools

import jax
import jax.numpy as jnp
from jax import lax
from jax.experimental import pallas as pl
from jax.experimental.pallas import tpu as pltpu
from jax.experimental.pallas import tpu_sc as plsc

F32 = jnp.float32
MXU = jnp.bfloat16

EPS = 1e-6
HEAD_DIM = 64
N_Q_HEADS = 8
N_KV_HEADS = 2
Q_PER_KV = N_Q_HEADS // N_KV_HEADS
ATTN_WIDTH = N_Q_HEADS * HEAD_DIM
KV_WIDTH = N_KV_HEADS * HEAD_DIM
BLOCK = 128
ROPE_THETA = 10000.0
POOL_WINDOWS = (2, 4, 8, 16)
POOL_GROUP = 128
POOL_WIDTH = len(POOL_WINDOWS) * POOL_GROUP
POOL_HALO = 16
X_HEADS = 4
X_HEAD_DIM = 256
FFN_RES = 0.5
NEG = -1e30
ADAM_LR = 0.001
ADAM_B1 = 0.9
ADAM_B2 = 0.999
ADAM_EPS = 1e-08
ADAM_WD = 0.01
ADAM_STEP = 10

N_CHIPS = 4
N_DEV = 8
V7X_VMEM_BYTES = 64 * 1024 * 1024
VMEM_LIMIT = V7X_VMEM_BYTES - 8 * 1024 * 1024
LANES = 128
TOKEN_TILE = 512
MESH = pl.DeviceIdType.MESH

BIG = (("ffn1_w_gate", True), ("ffn1_w_up", True), ("ffn1_w_down", False), ("w_in", True), ("w_out", False),
       ("xattn_wq", False), ("xattn_wkv", True), ("xattn_wo", False),
       ("ffn2_w_gate", True), ("ffn2_w_up", True), ("ffn2_w_down", False))
GROUPS = (BIG[:3], BIG[3:])
SMALL = ("ffn1_norm", "mix_norm", "attn_sinks", "pool_w", "pool_scale", "attn_out_norm", "pool_out_norm",
         "xattn_norm", "mem_norm", "ffn2_norm", "final_norm")
WEIGHTS = ("ffn1_norm", "ffn1_w_gate", "ffn1_w_up", "ffn1_w_down", "mix_norm", "w_in", "attn_sinks", "pool_w",
           "pool_scale", "attn_out_norm", "pool_out_norm", "w_out", "xattn_norm", "mem_norm", "xattn_wq",
           "xattn_wkv", "xattn_wo", "ffn2_norm", "ffn2_w_gate", "ffn2_w_up", "ffn2_w_down", "final_norm")


def _S(shape, dtype):
    return jax.ShapeDtypeStruct(tuple(shape), dtype)


def _pcall(body, **kw):
    return pl.pallas_call(body, **kw)


def _cp(*sem):
    return pltpu.CompilerParams(dimension_semantics=tuple(sem), vmem_limit_bytes=VMEM_LIMIT)


def _nt(a, b):
    return lax.dot_general(a, b, (((1,), (1,)), ((), ())), preferred_element_type=F32)


def _nn(a, b):
    return lax.dot_general(a, b, (((1,), (0,)), ((), ())), preferred_element_type=F32)


def _tn(a, b):
    return lax.dot_general(a, b, (((0,), (0,)), ((), ())), preferred_element_type=F32)


def _tile(n, want):
    t = min(n, want)
    assert n % t == 0, (n, want)
    return t


def _resident(shape):
    nd = len(shape)
    return pl.BlockSpec(tuple(shape), lambda *_: (0,) * nd)


def _rms_fwd(x, g, name):
    T, C = x.shape
    tm = _tile(T, TOKEN_TILE)

    def body(x_ref, g_ref, o_ref):
        xv = x_ref[...]
        r = lax.rsqrt(jnp.mean(xv * xv, axis=-1, keepdims=True) + EPS)
        o_ref[...] = (xv * r * g_ref[...]).astype(o_ref.dtype)

    return _pcall(body, name=name, grid=(T // tm,),
                  in_specs=[pl.BlockSpec((tm, C), lambda i: (i, 0)), _resident((1, C))],
                  out_specs=pl.BlockSpec((tm, C), lambda i: (i, 0)),
                  out_shape=_S((T, C), MXU), compiler_params=_cp("parallel"))(x, g)


def _rms_bwd(x, g, dh, name, dres=None):
    T, C = x.shape
    tm = _tile(T, TOKEN_TILE)

    def body(*refs):
        if dres is None:
            x_ref, g_ref, dh_ref, dx_ref, dg_ref = refs
        else:
            x_ref, g_ref, dh_ref, dres_ref, dx_ref, dg_ref = refs
        xv = x_ref[...]
        r = lax.rsqrt(jnp.mean(xv * xv, axis=-1, keepdims=True) + EPS)
        xh = xv * r
        dhv = dh_ref[...].astype(F32)
        dxn = dhv * g_ref[...]
        dx = r * (dxn - xh * jnp.mean(dxn * xh, axis=-1, keepdims=True))
        if dres is not None:
            dx = dx + dres_ref[...]
        dx_ref[...] = dx

        @pl.when(pl.program_id(0) == 0)
        def _():
            dg_ref[...] = jnp.zeros_like(dg_ref)

        dg_ref[...] += jnp.sum(dhv * xh, axis=0, keepdims=True)

    tok = pl.BlockSpec((tm, C), lambda i: (i, 0))
    in_specs = [tok, _resident((1, C)), tok]
    args = [x, g, dh]
    if dres is not None:
        in_specs.append(tok)
        args.append(dres)
    return _pcall(body, name=name, grid=(T // tm,), in_specs=in_specs,
                  out_specs=[tok, _resident((1, C))],
                  out_shape=[_S((T, C), F32), _S((1, C), F32)], compiler_params=_cp("arbitrary"))(*args)


def _mm(pairs, *, nt, out_dtype, name, res=None, res_scale=1.0, tm=TOKEN_TILE, norm_g=None):
    M = pairs[0][0].shape[0]
    N = pairs[0][1].shape[0] if nt else pairs[0][1].shape[1]
    tm = _tile(M, tm)
    n = len(pairs)

    def body(*refs):
        a_refs, w_refs = refs[:n], refs[n:2 * n]
        o_ref = refs[-1] if norm_g is None else refs[-2]
        acc = None
        for a_ref, w_ref in zip(a_refs, w_refs):
            a = a_ref[...]
            a = a.astype(MXU)
            p = _nt(a, w_ref[...]) if nt else _nn(a, w_ref[...])
            acc = p if acc is None else acc + p
        if res is not None:
            acc = refs[2 * n][...] + res_scale * acc
        o_ref[...] = acc.astype(o_ref.dtype)
        if norm_g is not None:
            r = lax.rsqrt(jnp.mean(acc * acc, axis=-1, keepdims=True) + EPS)
            refs[-1][...] = (acc * r * refs[-3][...]).astype(MXU)

    tok = pl.BlockSpec((tm, N), lambda i: (i, 0))
    in_specs = [pl.BlockSpec((tm, a.shape[1]), lambda i: (i, 0)) for a, _ in pairs]
    in_specs += [_resident(w.shape) for _, w in pairs]
    args = [a for a, _ in pairs] + [w for _, w in pairs]
    if res is not None:
        in_specs.append(tok)
        args.append(res)
    if norm_g is None:
        return _pcall(body, name=name, grid=(M // tm,), in_specs=in_specs, out_specs=tok,
                      out_shape=_S((M, N), out_dtype), compiler_params=_cp("parallel"))(*args)
    return _pcall(body, name=name, grid=(M // tm,), in_specs=in_specs + [_resident(norm_g.shape)],
                  out_specs=[tok, tok], out_shape=[_S((M, N), out_dtype), _S((M, N), MXU)],
                  compiler_params=_cp("parallel"))(*args, norm_g)


def _mm_norm_bwd(pairs, norms, *, nt, name, tm=TOKEN_TILE, half=False):
    M = pairs[0][0].shape[0]
    tm = _tile(M, tm)
    n, k = len(pairs), len(norms)
    has_res = [d is not None for _, _, d in norms]

    def body(*refs):
        a_refs, w_refs = refs[:n], refs[n:2 * n]
        rest = list(refs[2 * n:])
        acc = None
        for a_ref, w_ref in zip(a_refs, w_refs):
            p = _nt(a_ref[...].astype(MXU), w_ref[...]) if nt else _nn(a_ref[...].astype(MXU), w_ref[...])
            acc = p if acc is None else acc + p
        ins = [(rest.pop(0), rest.pop(0), rest.pop(0) if has_res[j] else None) for j in range(k)]
        dx_refs, dg_refs = rest[:k], rest[k:2 * k]
        c0 = 0
        for j, ((x_ref, g_ref, d_ref), dx_ref, dg_ref) in enumerate(zip(ins, dx_refs, dg_refs)):
            xv = x_ref[...]
            dhv = acc[:, c0:c0 + xv.shape[1]]
            c0 += xv.shape[1]
            r = lax.rsqrt(jnp.mean(xv * xv, axis=-1, keepdims=True) + EPS)
            xh = xv * r
            dxn = dhv * g_ref[...]
            dx = r * (dxn - xh * jnp.mean(dxn * xh, axis=-1, keepdims=True))
            if d_ref is not None:
                dx = dx + d_ref[...]
            dx_ref[...] = dx
            if half and j == 0:
                rest[2 * k][...] = (dx * FFN_RES).astype(MXU)

            @pl.when(pl.program_id(0) == 0)
            def _():
                dg_ref[...] = jnp.zeros_like(dg_ref)

            dg_ref[...] += jnp.sum(dhv * xh, axis=0, keepdims=True)

    tok = lambda c: pl.BlockSpec((tm, c), lambda i: (i, 0))
    in_specs = [tok(a.shape[1]) for a, _ in pairs] + [_resident(w.shape) for _, w in pairs]
    args = [a for a, _ in pairs] + [w for _, w in pairs]
    for x, g, d in norms:
        in_specs += [tok(x.shape[1]), _resident(g.shape)] + ([tok(x.shape[1])] if d is not None else [])
        args += [x, g] + ([d] if d is not None else [])
    out_specs = [tok(x.shape[1]) for x, _, _ in norms] + [_resident((1, x.shape[1])) for x, _, _ in norms]
    out_shape = [_S((M, x.shape[1]), F32) for x, _, _ in norms] + [_S((1, x.shape[1]), F32) for x, _, _ in norms]
    if half:
        out_specs.append(tok(norms[0][0].shape[1]))
        out_shape.append(_S(norms[0][0].shape, MXU))
    outs = _pcall(body, name=name, grid=(M // tm,), in_specs=in_specs, out_specs=out_specs, out_shape=out_shape,
                  compiler_params=_cp("arbitrary"))(*args)
    if half:
        return list(outs[:k]), list(outs[k:2 * k]), outs[2 * k]
    return list(outs[:k]), list(outs[k:])


def _mm_tn(l, r, name, *, tr=1408, tt=2 * TOKEN_TILE):
    T, R = l.shape
    C = r.shape[1]
    tt = _tile(T, tt)
    tr = tr if R % tr == 0 else (1024 if R % 1024 == 0 and R > 1280 else R)

    def body(l_ref, r_ref, o_ref):
        lv, rv = l_ref[...].astype(MXU), r_ref[...].astype(MXU)

        @pl.when(pl.program_id(1) == 0)
        def _():
            o_ref[...] = jnp.zeros_like(o_ref)

        o_ref[...] += _tn(lv, rv)

    return _pcall(body, name=name, grid=(R // tr, T // tt),
                  in_specs=[pl.BlockSpec((tt, tr), lambda i, t: (t, i)), pl.BlockSpec((tt, C), lambda i, t: (t, 0))],
                  out_specs=pl.BlockSpec((tr, C), lambda i, t: (i, 0)),
                  out_shape=_S((R, C), F32), compiler_params=_cp("parallel", "arbitrary"))(l, r)


FFN_COL_TILE = 1408


def _sigmoid(a):
    return 0.5 * (jnp.tanh(0.5 * a) + 1.0)


def _ffn_up(h, wgT, wuT, name):
    T, D = h.shape
    Fd = wgT.shape[0]
    tm, tn = _tile(T, TOKEN_TILE), _tile(Fd, FFN_COL_TILE)

    def body(h_ref, wg_ref, wu_ref, a_ref, b_ref, s_ref):
        hv = h_ref[...]
        a = _nt(hv, wg_ref[...])
        b = _nt(hv, wu_ref[...])
        s = a * _sigmoid(a) * b
        a_ref[...] = a.astype(a_ref.dtype)
        b_ref[...] = b.astype(b_ref.dtype)
        s_ref[...] = s.astype(s_ref.dtype)

    wspec = pl.BlockSpec((tn, D), lambda j, i: (j, 0))
    ospec = pl.BlockSpec((tm, tn), lambda j, i: (i, j))
    return _pcall(body, name=name, grid=(Fd // tn, T // tm),
                  in_specs=[pl.BlockSpec((tm, D), lambda j, i: (i, 0)), wspec, wspec],
                  out_specs=[ospec, ospec, ospec], out_shape=[_S((T, Fd), MXU)] * 3,
                  compiler_params=_cp("parallel", "parallel"))(h, wgT, wuT)


def _ffn_mid_bwd(dy, wd, a, b, name):
    T, D = dy.shape
    Fd = wd.shape[0]
    tm, tn = _tile(T, TOKEN_TILE), _tile(Fd, FFN_COL_TILE)

    def body(dy_ref, wd_ref, a_ref, b_ref, da_ref, db_ref):
        ds = _nt(dy_ref[...], wd_ref[...])
        av, bv = a_ref[...].astype(F32), b_ref[...].astype(F32)
        sg = _sigmoid(av)
        da_ref[...] = (ds * bv * (sg * (1.0 + av * (1.0 - sg)))).astype(da_ref.dtype)
        db_ref[...] = (ds * (av * sg)).astype(db_ref.dtype)

    aspec = pl.BlockSpec((tm, tn), lambda j, i: (i, j))
    return _pcall(body, name=name, grid=(Fd // tn, T // tm),
                  in_specs=[pl.BlockSpec((tm, D), lambda j, i: (i, 0)), pl.BlockSpec((tn, D), lambda j, i: (j, 0)),
                            aspec, aspec],
                  out_specs=[aspec, aspec], out_shape=[_S((T, Fd), MXU)] * 2,
                  compiler_params=_cp("parallel", "parallel"))(dy, wd, a, b)


def _swap_halves(t):
    w = t.shape[1]
    lane = lax.broadcasted_iota(jnp.int32, t.shape, 1)
    first = (lane % HEAD_DIM) < (HEAD_DIM // 2)
    return jnp.where(first, pltpu.roll(t, w - HEAD_DIM // 2, 1), pltpu.roll(t, HEAD_DIM // 2, 1))


def _rope(t, cos2, sin2):
    reps = t.shape[1] // LANES
    c = jnp.tile(cos2, (1, reps)) if reps > 1 else cos2
    s = jnp.tile(sin2, (1, reps)) if reps > 1 else sin2
    return t * c + _swap_halves(t) * s


def _rope_bwd(dt, cos2, sin2):
    reps = dt.shape[1] // LANES
    c = jnp.tile(cos2, (1, reps)) if reps > 1 else cos2
    s = jnp.tile(sin2, (1, reps)) if reps > 1 else sin2
    return dt * c + _swap_halves(dt * s)


def _in_proj(h, winT, cos2, sin2, name):
    T, D = h.shape
    tm = _tile(T, TOKEN_TILE)
    qe, ke, ve = ATTN_WIDTH, ATTN_WIDTH + KV_WIDTH, ATTN_WIDTH + 2 * KV_WIDTH

    def body(h_ref, w_ref, c_ref, s_ref, q_ref, k_ref, v_ref, u_ref):
        proj = _nt(h_ref[...], w_ref[...])
        cv, sv = c_ref[...], s_ref[...]
        q_ref[...] = _rope(proj[:, :qe], cv, sv).astype(q_ref.dtype)
        k_ref[...] = _rope(proj[:, qe:ke], cv, sv).astype(k_ref.dtype)
        v_ref[...] = proj[:, ke:ve].astype(v_ref.dtype)
        u_ref[...] = proj[:, ve:]

    def tok(w):
        return pl.BlockSpec((tm, w), lambda i: (i, 0))

    return _pcall(body, name=name, grid=(T // tm,),
                  in_specs=[tok(D), _resident(winT.shape), tok(LANES), tok(LANES)],
                  out_specs=[tok(ATTN_WIDTH), tok(KV_WIDTH), tok(KV_WIDTH), tok(POOL_WIDTH)],
                  out_shape=[_S((T, ATTN_WIDTH), MXU), _S((T, KV_WIDTH), MXU), _S((T, KV_WIDTH), MXU),
                             _S((T, POOL_WIDTH), F32)],
                  compiler_params=_cp("parallel"))(h, winT, cos2, sin2)


SWA_TILE_BLOCKS = 4
SM_SCALE = HEAD_DIM ** -0.5


def _swa_bias(first_tile):
    cols = Q_PER_KV * BLOCK
    kj = lax.broadcasted_iota(jnp.int32, (2 * BLOCK, cols), 0)
    qi = lax.broadcasted_iota(jnp.int32, (2 * BLOCK, cols), 1) % BLOCK
    diff = qi + BLOCK - kj
    bias = jnp.where((diff >= 0) & (diff < BLOCK), 0.0, NEG)
    return bias, jnp.where(kj < jnp.where(first_tile, BLOCK, 0), NEG, bias)


def _swa_probs(kh, qs, sink_row, bias):
    s = _nt(kh, qs) + bias
    m = jnp.maximum(jnp.max(s, axis=0, keepdims=True), sink_row)
    e = jnp.exp(s - m)
    es = jnp.exp(sink_row - m)
    inv = 1.0 / (jnp.sum(e, axis=0, keepdims=True) + es)
    return e * inv, es * inv


def _sink_row(sinks_ref, kv):
    return jnp.concatenate([jnp.full((1, BLOCK), sinks_ref[0, kv * Q_PER_KV + g], F32) for g in range(Q_PER_KV)], axis=1)


def _stack_heads(t, kv):
    return jnp.concatenate([t[:, (kv * Q_PER_KV + g) * HEAD_DIM:(kv * Q_PER_KV + g + 1) * HEAD_DIM]
                            for g in range(Q_PER_KV)], axis=0)


def _swa_specs(T):
    tq = _tile(T, SWA_TILE_BLOCKS * BLOCK)
    nbt = tq // BLOCK
    cur = lambda w: pl.BlockSpec((tq, w), lambda i: (i, 0))
    prev = lambda w: pl.BlockSpec((BLOCK, w), lambda i: (jnp.maximum(i * nbt - 1, 0), 0))
    return tq, nbt, cur, prev


def _rows(b):
    return slice(b * BLOCK, (b + 1) * BLOCK)


def _swa_fwd(q, k, v, sinks, name):
    T = q.shape[0]
    tq, nbt, cur, prev = _swa_specs(T)

    def body(sinks_ref, q_ref, k_ref, kp_ref, v_ref, vp_ref, o_ref):
        bias, bias0 = _swa_bias(pl.program_id(0) == 0)
        sink = [_sink_row(sinks_ref, kv) for kv in range(N_KV_HEADS)]
        kx = jnp.concatenate([kp_ref[...], k_ref[...]], axis=0)
        vx = jnp.concatenate([vp_ref[...], v_ref[...]], axis=0)
        for b in range(nbt):
            qv = q_ref[_rows(b), :] * SM_SCALE
            kk, vv = kx[b * BLOCK:(b + 2) * BLOCK], vx[b * BLOCK:(b + 2) * BLOCK]
            for kv in range(N_KV_HEADS):
                hs = slice(kv * HEAD_DIM, (kv + 1) * HEAD_DIM)
                p, _ = _swa_probs(kk[:, hs], _stack_heads(qv, kv), sink[kv], bias0 if b == 0 else bias)
                o_t = _tn(vv[:, hs], p.astype(MXU))
                for g in range(Q_PER_KV):
                    c0 = (kv * Q_PER_KV + g) * HEAD_DIM
                    o_ref[_rows(b), c0:c0 + HEAD_DIM] = o_t[:, _rows(g)].T

    return _pcall(body, name=name, grid=(T // tq,),
                  in_specs=[pl.BlockSpec(memory_space=pltpu.SMEM), cur(ATTN_WIDTH), cur(KV_WIDTH), prev(KV_WIDTH),
                            cur(KV_WIDTH), prev(KV_WIDTH)],
                  out_specs=cur(ATTN_WIDTH), out_shape=_S((T, ATTN_WIDTH), F32),
                  compiler_params=_cp("parallel"))(sinks, q, k, k, v, v)


def _swa_bwd(q, k, v, do, sinks, name):
    T = q.shape[0]
    tq, nbt, cur, prev = _swa_specs(T)
    per_tile = lambda w: pl.BlockSpec((BLOCK, w), lambda i: (i, 0))

    def add(acc, t):
        return t if acc is None else acc + t

    def body(sinks_ref, q_ref, k_ref, kp_ref, v_ref, vp_ref, do_ref,
             dq_ref, dk_ref, dkp_ref, dv_ref, dvp_ref, dsk_ref):
        bias, bias0 = _swa_bias(pl.program_id(0) == 0)
        sink = [_sink_row(sinks_ref, kv) for kv in range(N_KV_HEADS)]
        kx = jnp.concatenate([kp_ref[...], k_ref[...]], axis=0)
        vx = jnp.concatenate([vp_ref[...], v_ref[...]], axis=0)

        @pl.when(pl.program_id(0) == 0)
        def _():
            dsk_ref[...] = jnp.zeros_like(dsk_ref)

        dk_acc, dv_acc = [None] * (nbt + 1), [None] * (nbt + 1)
        dsk_acc = [None] * N_Q_HEADS
        for b in range(nbt):
            qv, dov = q_ref[_rows(b), :] * SM_SCALE, do_ref[_rows(b), :].astype(MXU)
            kk, vv = kx[b * BLOCK:(b + 2) * BLOCK], vx[b * BLOCK:(b + 2) * BLOCK]
            dks, dvs = [], []
            for kv in range(N_KV_HEADS):
                hs = slice(kv * HEAD_DIM, (kv + 1) * HEAD_DIM)
                qs, dos = _stack_heads(qv, kv), _stack_heads(dov, kv)
                p, ps = _swa_probs(kk[:, hs], qs, sink[kv], bias0 if b == 0 else bias)
                dp = _nt(vv[:, hs], dos)
                delta = jnp.sum(p * dp, axis=0, keepdims=True)
                ds = (p * (dp - delta)).astype(MXU)
                dq_t = _tn(kk[:, hs], ds) * SM_SCALE
                dks.append(_nn(ds, qs))
                dvs.append(_nn(p.astype(MXU), dos))
                dsink = -ps * delta
                for g in range(Q_PER_KV):
                    h = kv * Q_PER_KV + g
                    dq_ref[_rows(b), h * HEAD_DIM:(h + 1) * HEAD_DIM] = dq_t[:, _rows(g)].T
                    dsk_acc[h] = add(dsk_acc[h], jnp.sum(dsink[:, _rows(g)], axis=1, keepdims=True))
            dk, dv = jnp.concatenate(dks, axis=1), jnp.concatenate(dvs, axis=1)
            dk_acc[b], dk_acc[b + 1] = add(dk_acc[b], dk[:BLOCK]), add(dk_acc[b + 1], dk[BLOCK:])
            dv_acc[b], dv_acc[b + 1] = add(dv_acc[b], dv[:BLOCK]), add(dv_acc[b + 1], dv[BLOCK:])
        dkp_ref[...], dvp_ref[...] = dk_acc[0], dv_acc[0]
        dk_ref[...] = jnp.concatenate(dk_acc[1:], axis=0)
        dv_ref[...] = jnp.concatenate(dv_acc[1:], axis=0)
        for h in range(N_Q_HEADS):
            dsk_ref[h:h + 1, :] += jnp.broadcast_to(dsk_acc[h], (1, LANES))

    kvs, kvp = _S((T, KV_WIDTH), F32), _S((T // tq * BLOCK, KV_WIDTH), F32)
    return _pcall(body, name=name, grid=(T // tq,),
                  in_specs=[pl.BlockSpec(memory_space=pltpu.SMEM), cur(ATTN_WIDTH), cur(KV_WIDTH), prev(KV_WIDTH),
                            cur(KV_WIDTH), prev(KV_WIDTH), cur(ATTN_WIDTH)],
                  out_specs=[cur(ATTN_WIDTH), cur(KV_WIDTH), per_tile(KV_WIDTH), cur(KV_WIDTH), per_tile(KV_WIDTH),
                             _resident((N_Q_HEADS, LANES))],
                  out_shape=[_S((T, ATTN_WIDTH), F32), kvs, kvp, kvs, kvp, _S((N_Q_HEADS, LANES), F32)],
                  compiler_params=_cp("arbitrary"))(sinks, q, k, k, v, v, do)


def _dproj(dq, dk, dkp, dv, dvp, du, cos2, sin2, name):
    T = dq.shape[0]
    tq, nbt, cur, _ = _swa_specs(T)
    nt = T // tq
    nxt = lambda w: pl.BlockSpec((BLOCK, w), lambda i: (jnp.minimum(i + 1, nt - 1), 0))

    def body(dq_ref, dk_ref, dkp_ref, dv_ref, dvp_ref, du_ref, c_ref, s_ref, o_ref):
        more = (pl.program_id(0) < nt - 1).astype(F32)
        cv, sv = c_ref[...], s_ref[...]

        def whole(t_ref, p_ref):
            t, last = t_ref[...], t_ref[tq - BLOCK:, :] + more * p_ref[...]
            return last if nbt == 1 else jnp.concatenate([t[:tq - BLOCK], last], axis=0)

        o_ref[...] = jnp.concatenate(
            [_rope_bwd(dq_ref[...], cv, sv), _rope_bwd(whole(dk_ref, dkp_ref), cv, sv), whole(dv_ref, dvp_ref),
             du_ref[...]], axis=1).astype(o_ref.dtype)

    width = ATTN_WIDTH + 2 * KV_WIDTH + POOL_WIDTH
    return _pcall(body, name=name, grid=(nt,),
                  in_specs=[cur(ATTN_WIDTH), cur(KV_WIDTH), nxt(KV_WIDTH), cur(KV_WIDTH), nxt(KV_WIDTH),
                            cur(POOL_WIDTH), cur(LANES), cur(LANES)],
                  out_specs=cur(width), out_shape=_S((T, width), MXU),
                  compiler_params=_cp("parallel"))(dq, dk, dkp, dv, dvp, du, cos2, sin2)


def _pool_specs(T):
    tm = _tile(T, TOKEN_TILE)
    hb = tm // POOL_HALO
    nh = T // POOL_HALO
    tok = lambda w: pl.BlockSpec((tm, w), lambda i: (i, 0))
    before = pl.BlockSpec((POOL_HALO, POOL_WIDTH), lambda i: (jnp.maximum(i * hb - 1, 0), 0))
    after = pl.BlockSpec((POOL_HALO, POOL_WIDTH), lambda i: (jnp.minimum((i + 1) * hb, nh - 1), 0))
    return tm, tok, before, after


def _window_counts(i, tm, rows, w):
    t = i * tm + lax.broadcasted_iota(jnp.int32, (rows, 1), 0)
    return jnp.minimum(t + 1, w).astype(F32)


def _pooled(u_ext, i, tm):
    out = []
    for g, w in enumerate(POOL_WINDOWS):
        acc = u_ext[:, g * POOL_GROUP:(g + 1) * POOL_GROUP]
        tok = acc[POOL_HALO:, :]
        sh = 1
        while sh < w:
            acc = acc + pltpu.roll(acc, sh, 0)
            sh *= 2
        out.append(acc[POOL_HALO:, :] / _window_counts(i, tm, tm, w) - tok)
    return out


def _pool_fwd(u, out_a, pool_w, pool_scale, ga, gb, name):
    T = u.shape[0]
    tm, tok, before, _ = _pool_specs(T)

    def body(u_ref, halo_ref, oa_ref, pw_ref, sc_ref, ga_ref, gb_ref, ob_ref, mg_ref):
        i = pl.program_id(0)
        halo = halo_ref[...] * (i > 0).astype(F32)
        pooled = _pooled(jnp.concatenate([halo, u_ref[...]], axis=0), i, tm)
        mixed = [_nn(pooled[g].astype(MXU), pw_ref[g].astype(MXU)) for g in range(len(POOL_WINDOWS))]
        ob = jnp.concatenate(mixed, axis=1) * sc_ref[...]
        ob_ref[...] = ob
        oa = oa_ref[...]
        ra = lax.rsqrt(jnp.mean(oa * oa, axis=-1, keepdims=True) + EPS)
        rb = lax.rsqrt(jnp.mean(ob * ob, axis=-1, keepdims=True) + EPS)
        mg_ref[...] = jnp.concatenate([oa * ra * ga_ref[...], ob * rb * gb_ref[...]], axis=1).astype(mg_ref.dtype)

    vec = _resident((1, POOL_WIDTH))
    return _pcall(body, name=name, grid=(T // tm,),
                  in_specs=[tok(POOL_WIDTH), before, tok(ATTN_WIDTH), _resident(pool_w.shape), vec, vec, vec],
                  out_specs=[tok(POOL_WIDTH), tok(ATTN_WIDTH + POOL_WIDTH)],
                  out_shape=[_S((T, POOL_WIDTH), F32), _S((T, ATTN_WIDTH + POOL_WIDTH), MXU)],
                  compiler_params=_cp("parallel"))(u, u, out_a, pool_w, pool_scale, ga, gb)


def _pool_bwd(u, dob, pool_w, pool_scale, name):
    T = u.shape[0]
    tm, tok, before, after = _pool_specs(T)
    nt = T // tm
    G = len(POOL_WINDOWS)

    def body(u_ref, halo_ref, dob_ref, dnext_ref, pw_ref, sc_ref, du_ref, dpw_ref, dsc_ref):
        i = pl.program_id(0)
        halo = halo_ref[...] * (i > 0).astype(F32)
        pooled = _pooled(jnp.concatenate([halo, u_ref[...]], axis=0), i, tm)
        dnext = dnext_ref[...] * (i < nt - 1).astype(F32)
        dext = jnp.concatenate([dob_ref[...], dnext], axis=0) * sc_ref[...]

        @pl.when(i == 0)
        def _():
            dpw_ref[...] = jnp.zeros_like(dpw_ref)
            dsc_ref[...] = jnp.zeros_like(dsc_ref)

        dus, dscs = [], []
        for g, w in enumerate(POOL_WINDOWS):
            gs = slice(g * POOL_GROUP, (g + 1) * POOL_GROUP)
            pw = pw_ref[g].astype(MXU)
            pg = pooled[g].astype(MXU)
            dmix = dext[:, gs].astype(MXU)
            dscs.append(jnp.sum(dob_ref[:, gs] * _nn(pg, pw), axis=0, keepdims=True))
            dpw_ref[g] += _tn(pg, dmix[:tm, :])
            dpooled = _nt(dmix, pw)
            acc = dpooled / _window_counts(i, tm, tm + POOL_HALO, w)
            sh = 1
            while sh < w:
                acc = acc + pltpu.roll(acc, tm + POOL_HALO - sh, 0)
                sh *= 2
            dus.append(acc[:tm, :] - dpooled[:tm, :])
        du_ref[...] = jnp.concatenate(dus, axis=1)
        dsc_ref[...] += jnp.concatenate(dscs, axis=1)

    vec = _resident((1, POOL_WIDTH))
    return _pcall(body, name=name, grid=(nt,),
                  in_specs=[tok(POOL_WIDTH), before, tok(POOL_WIDTH), after, _resident(pool_w.shape), vec],
                  out_specs=[tok(POOL_WIDTH), _resident(pool_w.shape), vec],
                  out_shape=[_S((T, POOL_WIDTH), F32), _S(pool_w.shape, F32), _S((1, POOL_WIDTH), F32)],
                  compiler_params=_cp("arbitrary"))(u, u, dob, dob, pool_w, pool_scale)


def _xattn_probs(qh, kh):
    s = _nt(qh, kh) * (X_HEAD_DIM ** -0.5)
    e = jnp.exp(s - jnp.max(s, axis=1, keepdims=True))
    return e / jnp.sum(e, axis=1, keepdims=True)


def _xattn_fwd(q, kvm, name):
    T, XW = q.shape
    tm = _tile(T, TOKEN_TILE)

    def body(q_ref, kv_ref, o_ref):
        for h in range(X_HEADS):
            hs = slice(h * X_HEAD_DIM, (h + 1) * X_HEAD_DIM)
            vs = slice(XW + h * X_HEAD_DIM, XW + (h + 1) * X_HEAD_DIM)
            p = _xattn_probs(q_ref[:, hs], kv_ref[:, hs])
            o_ref[:, hs] = _nn(p.astype(MXU), kv_ref[:, vs]).astype(o_ref.dtype)

    return _pcall(body, name=name, grid=(T // tm,),
                  in_specs=[pl.BlockSpec((tm, XW), lambda i: (i, 0)), _resident(kvm.shape)],
                  out_specs=pl.BlockSpec((tm, XW), lambda i: (i, 0)), out_shape=_S((T, XW), MXU),
                  compiler_params=_cp("parallel"))(q, kvm)


def _xattn_bwd(q, kvm, do, name):
    T, XW = q.shape
    tm = _tile(T, TOKEN_TILE)

    def body(q_ref, kv_ref, do_ref, dq_ref, dkv_ref):
        @pl.when(pl.program_id(0) == 0)
        def _():
            dkv_ref[...] = jnp.zeros_like(dkv_ref)

        for h in range(X_HEADS):
            hs = slice(h * X_HEAD_DIM, (h + 1) * X_HEAD_DIM)
            vs = slice(XW + h * X_HEAD_DIM, XW + (h + 1) * X_HEAD_DIM)
            qh, doh = q_ref[:, hs], do_ref[:, hs]
            p = _xattn_probs(qh, kv_ref[:, hs])
            dp = _nt(doh, kv_ref[:, vs])
            ds = (p * (dp - jnp.sum(p * dp, axis=1, keepdims=True)) * (X_HEAD_DIM ** -0.5)).astype(MXU)
            dq_ref[:, hs] = _nn(ds, kv_ref[:, hs]).astype(dq_ref.dtype)
            dkv_ref[:, hs] += _tn(ds, qh)
            dkv_ref[:, vs] += _tn(p.astype(MXU), doh)

    tok = pl.BlockSpec((tm, XW), lambda i: (i, 0))
    return _pcall(body, name=name, grid=(T // tm,),
                  in_specs=[tok, _resident(kvm.shape), tok],
                  out_specs=[tok, _resident(kvm.shape)],
                  out_shape=[_S((T, XW), MXU), _S(kvm.shape, F32)],
                  compiler_params=_cp("arbitrary"))(q, kvm, do)


def _loss_head(x, tgt, g, name):
    T, D = x.shape
    tm = _tile(T, TOKEN_TILE)

    def body(x_ref, t_ref, g_ref, loss_ref, dx_ref, half_ref, dg_ref):
        xv, gv = x_ref[...], g_ref[...]
        r = lax.rsqrt(jnp.mean(xv * xv, axis=-1, keepdims=True) + EPS)
        xh = xv * r
        e = xh * gv - t_ref[...]
        dy = e * (1.0 / D)
        dxn = dy * gv
        dx = r * (dxn - xh * jnp.mean(dxn * xh, axis=-1, keepdims=True))
        dx_ref[...] = dx
        half_ref[...] = (dx * FFN_RES).astype(MXU)

        @pl.when(pl.program_id(0) == 0)
        def _():
            loss_ref[...] = jnp.zeros_like(loss_ref)
            dg_ref[...] = jnp.zeros_like(dg_ref)

        part = jnp.sum(jnp.sum(e * e, axis=1, keepdims=True), axis=0, keepdims=True) * (0.5 / D)
        loss_ref[...] += jnp.broadcast_to(part, (1, LANES))
        dg_ref[...] += jnp.sum(dy * xh, axis=0, keepdims=True)

    tok = pl.BlockSpec((tm, D), lambda i: (i, 0))
    return _pcall(body, name=name, grid=(T // tm,),
                  in_specs=[tok, tok, _resident((1, D))],
                  out_specs=[_resident((1, LANES)), tok, tok, _resident((1, D))],
                  out_shape=[_S((1, LANES), F32), _S((T, D), F32), _S((T, D), MXU), _S((1, D), F32)],
                  compiler_params=_cp("arbitrary"))(x, tgt, g)


def _rows_tile(rows):
    for t in (512, 416, 352, 256, 128, 64, 32, 16, 8):
        if rows % t == 0:
            return t
    return rows


def _pair_sum(grads, gots, sizes, place, name):
    nw = len(sizes)
    C = grads[0].shape[1]

    def body(p_ref, *refs):
        g, got, out = refs[:nw], refs[nw:2 * nw], refs[2 * nw:]
        for w in range(nw):
            out[w][...] = (g[w][...] + got[w][...]).astype(out[w].dtype)

    def blk(w):
        return (sizes[w] // 4, C)

    in_specs = [pl.BlockSpec(blk(w), lambda q, s, p: (4 * q + 2 * p[0] + s, 0)) for w in range(nw)]
    in_specs += [pl.BlockSpec(blk(w), lambda q, s, p: (2 * q + s, 0)) for w in range(nw)]
    out_specs = [pl.BlockSpec(blk(w), lambda q, s, p: (2 * q + s, 0)) for w in range(nw)]
    gs = pltpu.PrefetchScalarGridSpec(num_scalar_prefetch=1, grid=(N_CHIPS, 2), in_specs=in_specs, out_specs=out_specs)
    return _pcall(body, name=name, grid_spec=gs, out_shape=[_S((2 * n, C), MXU) for n in sizes],
                  compiler_params=_cp("parallel", "parallel"))(place, *grads, *gots)


def _final_sum(grads, gots, recvs, sizes, place, name):
    nw = len(sizes)
    C = grads[0].shape[1]

    def body(p_ref, *refs):
        g, got, rv, out = refs[:nw], refs[nw:2 * nw], refs[2 * nw:5 * nw], refs[5 * nw:]
        for w in range(nw):
            acc = g[w][...] + got[w][...]
            for j in range(3):
                acc = acc + rv[3 * w + j][...].astype(F32)
            out[w][...] = acc

    def blk(w):
        return (sizes[w] // 4, C)

    in_specs = [pl.BlockSpec(blk(w), lambda s, p: (4 * p[1] + 2 * p[0] + s, 0)) for w in range(nw)]
    in_specs += [pl.BlockSpec(blk(w), lambda s, p: (2 * p[1] + s, 0)) for w in range(nw)]
    args = list(grads) + list(gots)
    for w in range(nw):
        for j in range(3):
            in_specs.append(pl.BlockSpec(blk(w), lambda s, p, j=j: (2 * j + s, 0)))
            args.append(recvs[w])
    out_specs = [pl.BlockSpec(blk(w), lambda s, p: (2 * p[0] + s, 0)) for w in range(nw)]
    gs = pltpu.PrefetchScalarGridSpec(num_scalar_prefetch=1, grid=(2,), in_specs=in_specs, out_specs=out_specs)
    return _pcall(body, name=name, grid_spec=gs, out_shape=[_S((n, C), F32) for n in sizes],
                  compiler_params=_cp("parallel"))(place, *args)


def _adamw(w, g, m, v, name):
    R, C = w.shape
    tr = _rows_tile(R)
    c1 = 1.0 / (1.0 - ADAM_B1 ** ADAM_STEP)
    c2 = 1.0 / (1.0 - ADAM_B2 ** ADAM_STEP)

    def body(w_ref, g_ref, m_ref, v_ref, d_ref, nm_ref, nv_ref):
        gv = g_ref[...]
        nm = ADAM_B1 * m_ref[...] + (1.0 - ADAM_B1) * gv
        nv = ADAM_B2 * v_ref[...] + (1.0 - ADAM_B2) * (gv * gv)
        d_ref[...] = -ADAM_LR * ((nm * c1) / (jnp.sqrt(nv * c2) + ADAM_EPS) + ADAM_WD * w_ref[...])
        nm_ref[...] = nm
        nv_ref[...] = nv

    spec = pl.BlockSpec((tr, C), lambda i: (i, 0))
    return _pcall(body, name=name, grid=(R // tr,), in_specs=[spec] * 4, out_specs=[spec] * 3,
                  out_shape=[_S((R, C), F32)] * 3, compiler_params=_cp("parallel"))(w, g, m, v)


ANY = pl.BlockSpec(memory_space=pl.ANY)


def _place():
    x, y, c = lax.axis_index("x"), lax.axis_index("y"), lax.axis_index("c")
    chips = [(1 - x, y), (x, 1 - y), (1 - x, 1 - y)]
    return x, y, c, chips


def _remote(src, dst, send_sem, recv_sem, dev):
    return pltpu.make_async_remote_copy(src_ref=src, dst_ref=dst, send_sem=send_sem, recv_sem=recv_sem,
                                        device_id=dev, device_id_type=MESH)


def _drain(like, send_sem, recv_sem, me, *, send=False, recv=False):
    d = _remote(like, like, send_sem, recv_sem, me)
    if send:
        d.wait_send()
    if recv:
        d.wait_recv()


def _dma_sems(n):
    return [pltpu.SemaphoreType.DMA((n,)), pltpu.SemaphoreType.DMA((n,))]


def _comm_params():
    return pltpu.CompilerParams(has_side_effects=True)


def _on_sequencer(exchange, refs, sem_types, peers_of, name, seq_id):
    def launch(*sems):
        x, y, c, chips = _place()
        barrier = pltpu.get_barrier_semaphore()
        peers = peers_of(x, y, c, chips)
        for peer in peers:
            pl.semaphore_signal(barrier, inc=1, device_id=peer, device_id_type=MESH)
        pl.semaphore_wait(barrier, len(peers))
        exchange(refs, *sems)

    pl.kernel(launch, mesh=plsc.ScalarSubcoreMesh(axis_name="seq", num_cores=1), name=name,
              scratch_types=tuple(sem_types), compiler_params=pltpu.CompilerParams(collective_id=seq_id))()


def _hbm_ref(a):
    return jax.new_ref(a, memory_space=pltpu.MemorySpace.HBM)


def _allgather_weights(bufs, sizes, name, seq_id=None):
    nw = len(sizes)

    def exchange(out, s_ici, r_ici, s_fwd, r_fwd):
        x, y, c, chips = _place()
        me, sib = (x, y, c), (x, y, 1 - c)
        q_me = 2 * x + y

        def rows(w, q):
            hw = sizes[w] // 2
            return out[w].at[pl.ds(q * sizes[w] + c * hw, hw)]

        def three(w):
            return out[w].at[pl.ds(0, 3 * (sizes[w] // 2))]

        for w in range(nw):
            for px, py in chips:
                _remote(rows(w, q_me), rows(w, q_me), s_ici.at[w], r_ici.at[w], (px, py, c)).start()
        for w in range(nw):
            _drain(three(w), s_ici.at[w], r_ici.at[w], me, recv=True)
            for px, py in chips:
                got = rows(w, 2 * px + py)
                _remote(got, got, s_fwd.at[w], r_fwd.at[w], sib).start()
        for w in range(nw):
            _drain(three(w), s_fwd.at[w], r_fwd.at[w], me, recv=True)
        for w in range(nw):
            _drain(three(w), s_ici.at[w], r_ici.at[w], me, send=True)
            _drain(three(w), s_fwd.at[w], r_fwd.at[w], me, send=True)

    if seq_id is not None:
        refs = [_hbm_ref(b) for b in bufs]
        _on_sequencer(exchange, refs, _dma_sems(nw) + _dma_sems(nw),
                      lambda x, y, c, chips: [(x, y, 1 - c)] + [(px, py, c) for px, py in chips], name, seq_id)
        return [r[...] for r in refs]

    def body(*refs):
        exchange(refs[nw:2 * nw], *refs[2 * nw:])

    return _pcall(body, name=name, in_specs=[ANY] * nw, out_specs=[ANY] * nw,
                  out_shape=[_S(b.shape, b.dtype) for b in bufs],
                  input_output_aliases={w: w for w in range(nw)},
                  scratch_shapes=_dma_sems(nw) + _dma_sems(nw), compiler_params=_comm_params())(*bufs)


def _sibling_only(x, y, c, chips):
    return [(x, y, 1 - c)]


def _rs_pair_exchange(grads, sizes, name, seq_id=None):
    C = grads[0].shape[1]
    nw = len(sizes)
    out_shape = [_S((2 * n, C), F32) for n in sizes]

    def exchange(refs, s_sem, r_sem):
        g, got = refs[:nw], refs[nw:2 * nw]
        x, y, c, _ = _place()
        me, sib = (x, y, c), (x, y, 1 - c)
        for w in range(nw):
            hw = sizes[w] // 2
            for q in range(N_CHIPS):
                _remote(g[w].at[pl.ds(q * sizes[w] + (1 - c) * hw, hw)], got[w].at[pl.ds(q * hw, hw)],
                        s_sem.at[w], r_sem.at[w], sib).start()
        for w in range(nw):
            _drain(got[w], s_sem.at[w], r_sem.at[w], me, send=True, recv=True)

    if seq_id is not None:
        gots = [jax.empty_ref(s, memory_space=pltpu.MemorySpace.HBM) for s in out_shape]
        _on_sequencer(exchange, [_hbm_ref(g) for g in grads] + gots, _dma_sems(nw), _sibling_only, name, seq_id)
        return [r[...] for r in gots]

    def body(*refs):
        exchange(refs[:2 * nw], *refs[2 * nw:])

    return _pcall(body, name=name, in_specs=[ANY] * nw, out_specs=[ANY] * nw, out_shape=out_shape,
                  scratch_shapes=_dma_sems(nw), compiler_params=_comm_params())(*grads)


def _rs_chip_exchange(sums, sizes, name, seq_id=None):
    C = sums[0].shape[1]
    nw = len(sizes)
    out_shape = [_S((3 * (n // 2), C), sums[0].dtype) for n in sizes]

    def exchange(refs, s_sem, r_sem):
        sm, got = refs[:nw], refs[nw:2 * nw]
        x, y, c, chips = _place()
        for w in range(nw):
            hw = sizes[w] // 2
            for j, (px, py) in enumerate(chips):
                _remote(sm[w].at[pl.ds((2 * px + py) * hw, hw)], got[w].at[pl.ds(j * hw, hw)],
                        s_sem.at[w], r_sem.at[w], (px, py, c)).start()
        for w in range(nw):
            _drain(got[w], s_sem.at[w], r_sem.at[w], (x, y, c), send=True, recv=True)

    if seq_id is not None:
        gots = [jax.empty_ref(s, memory_space=pltpu.MemorySpace.HBM) for s in out_shape]
        _on_sequencer(exchange, [_hbm_ref(s) for s in sums] + gots, _dma_sems(nw),
                      lambda x, y, c, chips: [(px, py, c) for px, py in chips], name, seq_id)
        return [r[...] for r in gots]

    def body(*refs):
        exchange(refs[:2 * nw], *refs[2 * nw:])

    return _pcall(body, name=name, in_specs=[ANY] * nw, out_specs=[ANY] * nw, out_shape=out_shape,
                  scratch_shapes=_dma_sems(nw), compiler_params=_comm_params())(*sums)


def _rs_share_halves(reds, sizes, name, seq_id=None):
    nw = len(sizes)

    def exchange(out, s_sem, r_sem):
        x, y, c, _ = _place()
        for w in range(nw):
            hw = sizes[w] // 2
            rows = out[w].at[pl.ds(c * hw, hw)]
            _remote(rows, rows, s_sem.at[w], r_sem.at[w], (x, y, 1 - c)).start()
        for w in range(nw):
            _drain(out[w].at[pl.ds(0, sizes[w] // 2)], s_sem.at[w], r_sem.at[w], (x, y, c), send=True, recv=True)

    if seq_id is not None:
        refs = [_hbm_ref(r) for r in reds]
        _on_sequencer(exchange, refs, _dma_sems(nw), _sibling_only, name, seq_id)
        return [r[...] for r in refs]

    def body(*refs):
        exchange(refs[nw:2 * nw], *refs[2 * nw:])

    return _pcall(body, name=name, in_specs=[ANY] * nw, out_specs=[ANY] * nw,
                  out_shape=[_S(r.shape, r.dtype) for r in reds],
                  input_output_aliases={w: w for w in range(nw)},
                  scratch_shapes=_dma_sems(nw), compiler_params=_comm_params())(*reds)


def _allreduce_small(part, name):
    R, C = part.shape

    def body(p_ref, o_ref, buf, s_sem, r_sem):
        x, y, c, _ = _place()
        my_id = 4 * x + 2 * y + c
        buf[my_id] = p_ref[...]
        cps = []
        for k in range(1, N_DEV):
            fx, fy, fc = (k >> 2) & 1, (k >> 1) & 1, k & 1
            peer = (x ^ fx, y ^ fy, c ^ fc)
            cps.append(_remote(p_ref, buf.at[my_id], s_sem.at[k - 1], r_sem.at[k - 1], peer))
        for cp in cps:
            cp.start()
        for cp in cps:
            cp.wait()
        acc = buf[0]
        for d in range(1, N_DEV):
            acc = acc + buf[d]
        o_ref[...] = acc

    vm = pl.BlockSpec(memory_space=pltpu.VMEM)
    return _pcall(body, name=name, in_specs=[vm], out_specs=vm, out_shape=_S((R, C), F32),
                  scratch_shapes=[pltpu.VMEM((N_DEV, R, C), F32)] + _dma_sems(N_DEV - 1),
                  compiler_params=pltpu.CompilerParams(has_side_effects=True, vmem_limit_bytes=VMEM_LIMIT))(part)


SHARD_STEPS = 4


def _own_shard_buffers(ws, l, group, place, name):
    nw = len(group)

    def body(p_ref, *refs):
        for (_, tr), i_ref, o_ref in zip(group, refs[:nw], refs[nw:]):
            v = i_ref[...]
            o_ref[...] = (v.T if tr else v).astype(o_ref.dtype)

    in_specs, out_specs, out_shape, sizes = [], [], [], []
    for wname, tr in group:
        _, K, n = ws[wname].shape
        in_specs.append(pl.BlockSpec((None, K // SHARD_STEPS, n), lambda i, p: (l, i, 0)))
        if tr:
            out_specs.append(pl.BlockSpec((n, K // SHARD_STEPS), lambda i, p: (p[1], i)))
            out_shape.append(_S((N_CHIPS * n, K), MXU))
            sizes.append(n)
        else:
            out_specs.append(pl.BlockSpec((K // SHARD_STEPS, n), lambda i, p: (p[1] * SHARD_STEPS + i, 0)))
            out_shape.append(_S((N_CHIPS * K, n), MXU))
            sizes.append(K)
    gs = pltpu.PrefetchScalarGridSpec(num_scalar_prefetch=1, grid=(SHARD_STEPS,), in_specs=in_specs, out_specs=out_specs)
    bufs = _pcall(body, name=name, grid_spec=gs, out_shape=out_shape,
                  compiler_params=_cp("parallel"))(place, *[ws[wname] for wname, _ in group])
    return list(bufs), sizes


def _after(xs, ys):
    return lax.optimization_barrier((xs, ys))[0]


def _small_rows(v):
    flat = v.reshape(-1)
    pad = (-flat.shape[0]) % 1024
    return jnp.pad(flat, (0, pad)).reshape(-1, 1024)


def _pack_small(vals):
    rows = [_small_rows(vals[n]) for n in SMALL]
    cat = jnp.concatenate(rows, axis=0)
    pad = (-cat.shape[0]) % 8
    return jnp.pad(cat, ((0, pad), (0, 0)))


def _unpack_small(packed, like):
    out, r = {}, 0
    for n in SMALL:
        size = like[n].size
        nr = -(-size // 1024)
        out[n] = packed[r:r + nr].reshape(-1)[:size].reshape(like[n].shape)
        r += nr
    return out


def _rope_tables(positions):
    inv_freq = ROPE_THETA ** (-jnp.arange(0, HEAD_DIM, 2, dtype=F32) / HEAD_DIM)
    ang = positions.astype(F32)[:, None] * inv_freq
    cos, sin = jnp.cos(ang), jnp.sin(ang)
    return jnp.concatenate([cos, cos, cos, cos], axis=1), jnp.concatenate([-sin, sin, -sin, sin], axis=1)


def _layer_fwd(l, x, h1, memv, W, P, cos2, sin2, next_norm):
    t = f"l{l}"
    sv = {"x0": x, "h1": h1}
    sv["a1"], sv["b1"], sv["s1"] = _ffn_up(h1, W["ffn1_w_gate"], W["ffn1_w_up"], t + "_ffn1_up")
    sv["x1"], sv["h2"] = _mm([(sv["s1"], W["ffn1_w_down"])], nt=False, out_dtype=F32, res=x, res_scale=FFN_RES,
                             name=t + "_ffn1_down", norm_g=P["mix_norm"])

    sv["q"], sv["k"], sv["v"], sv["u"] = _in_proj(sv["h2"], W["w_in"], cos2, sin2, t + "_in_proj")
    sv["oa"] = _swa_fwd(sv["q"], sv["k"], sv["v"], P["attn_sinks"], t + "_swa")
    sv["ob"], sv["mg"] = _pool_fwd(sv["u"], sv["oa"], P["pool_w"], P["pool_scale"], P["attn_out_norm"],
                                   P["pool_out_norm"], t + "_pool")
    sv["x2"], sv["h3"] = _mm([(sv["mg"], W["w_out"])], nt=False, out_dtype=F32, res=sv["x1"], name=t + "_out_proj",
                             norm_g=P["xattn_norm"])

    sv["memn"] = _rms_fwd(memv, P["mem_norm"], t + "_mem_norm")
    sv["q3"] = _mm([(sv["h3"], W["xattn_wq"])], nt=False, out_dtype=MXU, name=t + "_xq")
    sv["kv"] = _mm([(sv["memn"], W["xattn_wkv"])], nt=True, out_dtype=MXU, name=t + "_xkv")
    sv["o3"] = _xattn_fwd(sv["q3"], sv["kv"], t + "_xattn")
    sv["x3"], sv["h4"] = _mm([(sv["o3"], W["xattn_wo"])], nt=False, out_dtype=F32, res=sv["x2"], name=t + "_xo",
                             norm_g=P["ffn2_norm"])

    sv["a2"], sv["b2"], sv["s2"] = _ffn_up(sv["h4"], W["ffn2_w_gate"], W["ffn2_w_up"], t + "_ffn2_up")
    out = _mm([(sv["s2"], W["ffn2_w_down"])], nt=False, out_dtype=F32, res=sv["x3"], res_scale=FFN_RES,
              name=t + "_ffn2_down", norm_g=next_norm)
    x4, h_next = out if next_norm is not None else (out, None)
    return x4, h_next, sv


def _ffn_bwd(t, dx, dy, x_in, g, h, a, b, s, wgT, wuT, wd, behind=None, begin=None, half=False):
    d_wd = _mm_tn(s, dy, t + "_dwd")
    da, db = _ffn_mid_bwd(dy, wd, a, b, t + "_mid")
    if behind is not None:
        da = _after(da, behind)
    d_wg = _mm_tn(da, h, t + "_dwg")
    d_wu = _mm_tn(db, h, t + "_dwu")
    if begin is not None:
        da = _after(da, begin(d_wg, d_wu, d_wd))
    out = _mm_norm_bwd([(da, wgT), (db, wuT)], [(x_in, g, dx)], nt=False, name=t + "_dh", tm=TOKEN_TILE // 2,
                       half=half)
    return out[0][0], (out[2] if half else None), out[1][0], d_wg, d_wu, d_wd


def _layer_bwd(l, dx, dy, sv, memv, W, P, cos2, sin2, begin_rest, begin_ffn1):
    t = f"l{l}b"
    GW, GP, marks = {}, {}, {}
    dx, _, GP["ffn2_norm"], GW["ffn2_w_gate"], GW["ffn2_w_up"], GW["ffn2_w_down"] = _ffn_bwd(
        t + "_ffn2", dx, dy, sv["x3"], P["ffn2_norm"], sv["h4"], sv["a2"], sv["b2"], sv["s2"],
        W["ffn2_w_gate"], W["ffn2_w_up"], W["ffn2_w_down"])
    marks["ffn2"] = dx

    GW["xattn_wo"] = _mm_tn(sv["o3"], dx, t + "_dwo")
    do3 = _mm([(dx, W["xattn_wo"])], nt=True, out_dtype=MXU, name=t + "_do3")
    dq3, dkv = _xattn_bwd(sv["q3"], sv["kv"], do3, t + "_xattn")
    GW["xattn_wq"] = _mm_tn(sv["h3"], dq3, t + "_dwq")
    GW["xattn_wkv"] = _mm_tn(dkv, sv["memn"], t + "_dwkv")
    dmemn = _mm([(dkv, W["xattn_wkv"])], nt=False, out_dtype=F32, name=t + "_dmemn")
    _, GP["mem_norm"] = _rms_bwd(memv, P["mem_norm"], dmemn, t + "_mem_norm_bwd")
    (dx,), (GP["xattn_norm"],) = _mm_norm_bwd([(dq3, W["xattn_wq"])], [(sv["x2"], P["xattn_norm"], dx)], nt=True,
                                             name=t + "_dh3")
    marks["xattn"] = dx

    GW["w_out"] = _mm_tn(sv["mg"], dx, t + "_dwout")
    (doa, dob), (GP["attn_out_norm"], GP["pool_out_norm"]) = _mm_norm_bwd(
        [(dx, W["w_out"])], [(sv["oa"], P["attn_out_norm"], None), (sv["ob"], P["pool_out_norm"], None)], nt=True,
        name=t + "_dmg")
    du, GP["pool_w"], GP["pool_scale"] = _pool_bwd(sv["u"], dob, P["pool_w"], P["pool_scale"], t + "_pool")
    dq, dko, dkp, dvo, dvp, dsk = _swa_bwd(sv["q"], sv["k"], sv["v"], doa, P["attn_sinks"], t + "_swa")
    GP["attn_sinks"] = dsk[:, 0]
    dpj = _dproj(dq, dko, dkp, dvo, dvp, du, cos2, sin2, t + "_dproj")
    GW["w_in"] = _mm_tn(dpj, sv["h2"], t + "_dwin")
    (dx,), (GP["mix_norm"],), dy = _mm_norm_bwd([(dpj, W["w_in"])], [(sv["x1"], P["mix_norm"], dx)], nt=False,
                                               name=t + "_dh2", half=True)

    dx, dy, GP["ffn1_norm"], _, _, _ = _ffn_bwd(
        t + "_ffn1", dx, dy, sv["x0"], P["ffn1_norm"], sv["h1"], sv["a1"], sv["b1"], sv["s1"],
        W["ffn1_w_gate"], W["ffn1_w_up"], W["ffn1_w_down"], behind=begin_rest(GW), begin=begin_ffn1, half=l > 0)
    return dx, dy, GP, marks


def _reduce_begin(t, grads, sizes, place, seq_ids):
    gots = _rs_pair_exchange(grads, sizes, t + "_pair")
    sums = _pair_sum(grads, gots, sizes, place, t + "_pair_sum")
    recvs = _rs_chip_exchange(sums, sizes, t + "_chips", seq_id=next(seq_ids))
    return dict(t=t, grads=grads, gots=gots, sums=sums, recvs=recvs, sizes=sizes)


def _reduce_end(st, place, seq_ids, late=None):
    recvs = st["recvs"] if late is None else _after(st["recvs"], late)
    reds = _final_sum(st["grads"], st["gots"], recvs, st["sizes"], place, st["t"] + "_final_sum")
    return _rs_share_halves(reds, st["sizes"], st["t"] + "_share")


def _adamw_layer(l, w3, g, m3, v3, transposed, prev, name):
    _, K, n = w3.shape
    if transposed:
        tr = K // SHARD_STEPS
        g_spec = pl.BlockSpec((n, tr), lambda i: (0, i))
    else:
        tr = _rows_tile(K)
        g_spec = pl.BlockSpec((tr, n), lambda i: (i, 0))
    c1 = 1.0 / (1.0 - ADAM_B1 ** ADAM_STEP)
    c2 = 1.0 / (1.0 - ADAM_B2 ** ADAM_STEP)

    def body(w_ref, g_ref, m_ref, v_ref, *rest):
        go_ref, d_ref, nm_ref, nv_ref = rest[-4:]
        gv = g_ref[...].T if transposed else g_ref[...]
        nm = ADAM_B1 * m_ref[...] + (1.0 - ADAM_B1) * gv
        nv = ADAM_B2 * v_ref[...] + (1.0 - ADAM_B2) * (gv * gv)
        go_ref[...] = gv
        d_ref[...] = -ADAM_LR * ((nm * c1) / (jnp.sqrt(nv * c2) + ADAM_EPS) + ADAM_WD * w_ref[...])
        nm_ref[...] = nm
        nv_ref[...] = nv

    slab = pl.BlockSpec((None, tr, n), lambda i: (l, i, 0))
    in_specs, args, aliases = [slab, g_spec, slab, slab], [w3, g, m3, v3], {}
    if prev is not None:
        in_specs += [ANY] * 4
        args += list(prev)
        aliases = {4 + j: j for j in range(4)}
    return _pcall(body, name=name, grid=(K // tr,), in_specs=in_specs, out_specs=[slab] * 4,
                  out_shape=[_S(w3.shape, F32)] * 4, input_output_aliases=aliases,
                  compiler_params=_cp("parallel"))(*args)


def kernel(x, mem, positions, ffn1_norm, ffn1_w_gate, ffn1_w_up, ffn1_w_down, mix_norm, w_in, attn_sinks, pool_w, pool_scale, attn_out_norm, pool_out_norm, w_out, xattn_norm, mem_norm, xattn_wq, xattn_wkv, xattn_wo, ffn2_norm, ffn2_w_gate, ffn2_w_up, ffn2_w_down, final_norm, loss_target, m_ffn1_norm, m_ffn1_w_gate, m_ffn1_w_up, m_ffn1_w_down, m_mix_norm, m_w_in, m_attn_sinks, m_pool_w, m_pool_scale, m_attn_out_norm, m_pool_out_norm, m_w_out, m_xattn_norm, m_mem_norm, m_xattn_wq, m_xattn_wkv, m_xattn_wo, m_ffn2_norm, m_ffn2_w_gate, m_ffn2_w_up, m_ffn2_w_down, m_final_norm, v_ffn1_norm, v_ffn1_w_gate, v_ffn1_w_up, v_ffn1_w_down, v_mix_norm, v_w_in, v_attn_sinks, v_pool_w, v_pool_scale, v_attn_out_norm, v_pool_out_norm, v_w_out, v_xattn_norm, v_mem_norm, v_xattn_wq, v_xattn_wkv, v_xattn_wo, v_ffn2_norm, v_ffn2_w_gate, v_ffn2_w_up, v_ffn2_w_down, v_final_norm):
    ws = dict(ffn1_norm=ffn1_norm, ffn1_w_gate=ffn1_w_gate, ffn1_w_up=ffn1_w_up, ffn1_w_down=ffn1_w_down,
              mix_norm=mix_norm, w_in=w_in, attn_sinks=attn_sinks, pool_w=pool_w, pool_scale=pool_scale,
              attn_out_norm=attn_out_norm, pool_out_norm=pool_out_norm, w_out=w_out, xattn_norm=xattn_norm,
              mem_norm=mem_norm, xattn_wq=xattn_wq, xattn_wkv=xattn_wkv, xattn_wo=xattn_wo, ffn2_norm=ffn2_norm,
              ffn2_w_gate=ffn2_w_gate, ffn2_w_up=ffn2_w_up, ffn2_w_down=ffn2_w_down, final_norm=final_norm)
    ms = dict(ffn1_norm=m_ffn1_norm, ffn1_w_gate=m_ffn1_w_gate, ffn1_w_up=m_ffn1_w_up, ffn1_w_down=m_ffn1_w_down,
              mix_norm=m_mix_norm, w_in=m_w_in, attn_sinks=m_attn_sinks, pool_w=m_pool_w, pool_scale=m_pool_scale,
              attn_out_norm=m_attn_out_norm, pool_out_norm=m_pool_out_norm, w_out=m_w_out, xattn_norm=m_xattn_norm,
              mem_norm=m_mem_norm, xattn_wq=m_xattn_wq, xattn_wkv=m_xattn_wkv, xattn_wo=m_xattn_wo,
              ffn2_norm=m_ffn2_norm, ffn2_w_gate=m_ffn2_w_gate, ffn2_w_up=m_ffn2_w_up, ffn2_w_down=m_ffn2_w_down,
              final_norm=m_final_norm)
    vs = dict(ffn1_norm=v_ffn1_norm, ffn1_w_gate=v_ffn1_w_gate, ffn1_w_up=v_ffn1_w_up, ffn1_w_down=v_ffn1_w_down,
              mix_norm=v_mix_norm, w_in=v_w_in, attn_sinks=v_attn_sinks, pool_w=v_pool_w, pool_scale=v_pool_scale,
              attn_out_norm=v_attn_out_norm, pool_out_norm=v_pool_out_norm, w_out=v_w_out, xattn_norm=v_xattn_norm,
              mem_norm=v_mem_norm, xattn_wq=v_xattn_wq, xattn_wkv=v_xattn_wkv, xattn_wo=v_xattn_wo,
              ffn2_norm=v_ffn2_norm, ffn2_w_gate=v_ffn2_w_gate, ffn2_w_up=v_ffn2_w_up, ffn2_w_down=v_ffn2_w_down,
              final_norm=v_final_norm)
    depth = ffn1_norm.shape[0]
    T, D = x.shape[1], x.shape[2]
    xv = x.reshape(T, D)
    memv = mem.reshape(mem.shape[1], D)
    tgt = loss_target.reshape(T, D)
    cos2, sin2 = _rope_tables(positions.reshape(T))
    in_kernel = {name: tr and ws[name].shape[2] % LANES == 0 for name, tr in BIG}
    swapped = [name for name, tr in BIG if tr and not in_kernel[name]]
    rows = lambda d: {name: (jnp.swapaxes(d[name], 1, 2) if name in swapped else d[name]) for name, _ in BIG}
    wr, mr, vr = rows(ws), rows(ms), rows(vs)
    groups = [[(name, in_kernel[name]) for name, _ in g] for g in GROUPS]

    q_me = 2 * lax.axis_index("x") + lax.axis_index("y")
    place = jnp.stack([lax.axis_index("c"), q_me]).astype(jnp.int32)
    seq_ids = iter(range(1, 1 + 8 * depth))
    Ws, sizes, first = [dict() for _ in range(depth)], {}, None
    for l in range(depth):
        for gi, group in enumerate(groups):
            t = f"l{l}g{gi}"
            bufs, sizes[gi] = _own_shard_buffers(wr, l, group, place, t + "_shard")
            if first is None:
                full = first = _allgather_weights(bufs, sizes[gi], t + "_allgather")
            else:
                full = _allgather_weights(_after(bufs, first), sizes[gi], t + "_allgather", seq_id=next(seq_ids))
            Ws[l].update({name: f for (name, _), f in zip(group, full)})
    Ps = [{n: (ws[n][l].reshape(1, -1) if n != "pool_w" else ws[n][l]) for n in SMALL if n != "final_norm"}
          for l in range(depth)]

    saved = []
    h, hn = xv, _rms_fwd(xv, Ps[0]["ffn1_norm"], "l0_ffn1_norm")
    for l in range(depth):
        next_norm = Ps[l + 1]["ffn1_norm"] if l + 1 < depth else None
        h, hn, sv = _layer_fwd(l, h, hn, memv, Ws[l], Ps[l], cos2, sin2, next_norm)
        saved.append(sv)
    loss_row, dx, dy, d_final = _loss_head(h, tgt, final_norm.reshape(1, D), "loss_head")
    GPs, marks, begun = [None] * depth, [None] * depth, {}
    for l in reversed(range(depth)):
        def begin_rest(GW, l=l):
            begun[l, 1] = _reduce_begin(f"l{l}g1r", [GW[name] for name, _ in groups[1]], sizes[1], place, seq_ids)
            return begun[l, 1]["sums"]

        def begin_ffn1(*gs, l=l):
            begun[l, 0] = _reduce_begin(f"l{l}g0r", list(gs), sizes[0], place, seq_ids)
            return begun[l, 0]["sums"]

        dx, dy, GPs[l], marks[l] = _layer_bwd(l, dx, dy, saved[l], memv, Ws[l], Ps[l], cos2, sin2, begin_rest,
                                              begin_ffn1)

    stacked = {}
    for l in reversed(range(depth)):
        for gi, group in reversed(list(enumerate(groups))):
            if l > 0:
                late = marks[l - 1]["ffn2" if gi == 1 else "xattn"]
            else:
                late = dx if gi == 1 else None
            reds = _reduce_end(begun[l, gi], place, seq_ids, late)
            for (name, tr), red in zip(group, reds):
                stacked[name] = _adamw_layer(l, wr[name], red, mr[name], vr[name], tr, stacked.get(name),
                                             f"l{l}_adamw_{name}")
    for name in swapped:
        stacked[name] = [jnp.swapaxes(a, 1, 2) for a in stacked[name]]
    small_part = {n: jnp.stack([GPs[l][n].reshape(ws[n].shape[1:]) for l in range(depth)]) for n in SMALL if n != "final_norm"}
    small_part["final_norm"] = d_final.reshape(D)
    small_g = _unpack_small(_allreduce_small(_pack_small(small_part), "small_allreduce"), ws)
    loss = lax.psum(loss_row[0, 0], ("x", "y", "c"))

    grads, deltas, new_m, new_v = {}, {}, {}, {}
    for name, _ in BIG:
        grads[name], deltas[name], new_m[name], new_v[name] = stacked[name]
    d, nm, nv = _adamw(_pack_small(ws), _pack_small(small_g), _pack_small(ms), _pack_small(vs), "adamw_small")
    grads.update(small_g)
    deltas.update(_unpack_small(d, ws))
    new_m.update(_unpack_small(nm, ws))
    new_v.update(_unpack_small(nv, ws))

    grad_x = dx.reshape(x.shape)
    return (loss, grad_x, *[grads[n] for n in WEIGHTS], *[deltas[n] for n in WEIGHTS],
            *[new_m[n] for n in WEIGHTS], *[new_v[n] for n in WEIGHTS])
```

```python
import functools

import jax
import jax.numpy as jnp
from jax import lax
from jax.experimental import pallas as pl
from jax.experimental.pallas import tpu as pltpu
from jax.experimental.pallas import tpu_sc as plsc

F32 = jnp.float32
MXU = jnp.bfloat16

EPS = 1e-6
HEAD_DIM = 64
N_Q_HEADS = 8
N_KV_HEADS = 2
Q_PER_KV = N_Q_HEADS // N_KV_HEADS
ATTN_WIDTH = N_Q_HEADS * HEAD_DIM
KV_WIDTH = N_KV_HEADS * HEAD_DIM
BLOCK = 128
ROPE_THETA = 10000.0
POOL_WINDOWS = (2, 4, 8, 16)
POOL_GROUP = 128
POOL_WIDTH = len(POOL_WINDOWS) * POOL_GROUP
POOL_HALO = 16
X_HEADS = 4
X_HEAD_DIM = 256
FFN_RES = 0.5
NEG = -1e30
ADAM_LR = 0.001
ADAM_B1 = 0.9
ADAM_B2 = 0.999
ADAM_EPS = 1e-08
ADAM_WD = 0.01
ADAM_STEP = 10

N_CHIPS = 4
N_DEV = 8
V7X_VMEM_BYTES = 64 * 1024 * 1024
VMEM_LIMIT = V7X_VMEM_BYTES - 8 * 1024 * 1024
LANES = 128
TOKEN_TILE = 512
MESH = pl.DeviceIdType.MESH

BIG = (("ffn1_w_gate", True), ("ffn1_w_up", True), ("ffn1_w_down", False), ("w_in", True), ("w_out", False),
       ("xattn_wq", False), ("xattn_wkv", True), ("xattn_wo", False),
       ("ffn2_w_gate", True), ("ffn2_w_up", True), ("ffn2_w_down", False))
GROUPS = (BIG[:3], BIG[3:])
SMALL = ("ffn1_norm", "mix_norm", "attn_sinks", "pool_w", "pool_scale", "attn_out_norm", "pool_out_norm",
         "xattn_norm", "mem_norm", "ffn2_norm", "final_norm")
WEIGHTS = ("ffn1_norm", "ffn1_w_gate", "ffn1_w_up", "ffn1_w_down", "mix_norm", "w_in", "attn_sinks", "pool_w",
           "pool_scale", "attn_out_norm", "pool_out_norm", "w_out", "xattn_norm", "mem_norm", "xattn_wq",
           "xattn_wkv", "xattn_wo", "ffn2_norm", "ffn2_w_gate", "ffn2_w_up", "ffn2_w_down", "final_norm")


def _S(shape, dtype):
    return jax.ShapeDtypeStruct(tuple(shape), dtype)


def _pcall(body, **kw):
    return pl.pallas_call(body, **kw)


def _cp(*sem):
    return pltpu.CompilerParams(dimension_semantics=tuple(sem), vmem_limit_bytes=VMEM_LIMIT)


def _nt(a, b):
    return lax.dot_general(a, b, (((1,), (1,)), ((), ())), preferred_element_type=F32)


def _nn(a, b):
    return lax.dot_general(a, b, (((1,), (0,)), ((), ())), preferred_element_type=F32)


def _tn(a, b):
    return lax.dot_general(a, b, (((0,), (0,)), ((), ())), preferred_element_type=F32)


def _tile(n, want):
    t = min(n, want)
    assert n % t == 0, (n, want)
    return t


def _resident(shape):
    nd = len(shape)
    return pl.BlockSpec(tuple(shape), lambda *_: (0,) * nd)


def _rms_fwd(x, g, name):
    T, C = x.shape
    tm = _tile(T, TOKEN_TILE)

    def body(x_ref, g_ref, o_ref):
        xv = x_ref[...]
        r = lax.rsqrt(jnp.mean(xv * xv, axis=-1, keepdims=True) + EPS)
        o_ref[...] = (xv * r * g_ref[...]).astype(o_ref.dtype)

    return _pcall(body, name=name, grid=(T // tm,),
                  in_specs=[pl.BlockSpec((tm, C), lambda i: (i, 0)), _resident((1, C))],
                  out_specs=pl.BlockSpec((tm, C), lambda i: (i, 0)),
                  out_shape=_S((T, C), MXU), compiler_params=_cp("parallel"))(x, g)


def _rms_bwd(x, g, dh, name, dres=None):
    T, C = x.shape
    tm = _tile(T, TOKEN_TILE)

    def body(*refs):
        if dres is None:
            x_ref, g_ref, dh_ref, dx_ref, dg_ref = refs
        else:
            x_ref, g_ref, dh_ref, dres_ref, dx_ref, dg_ref = refs
        xv = x_ref[...]
        r = lax.rsqrt(jnp.mean(xv * xv, axis=-1, keepdims=True) + EPS)
        xh = xv * r
        dhv = dh_ref[...].astype(F32)
        dxn = dhv * g_ref[...]
        dx = r * (dxn - xh * jnp.mean(dxn * xh, axis=-1, keepdims=True))
        if dres is not None:
            dx = dx + dres_ref[...]
        dx_ref[...] = dx

        @pl.when(pl.program_id(0) == 0)
        def _():
            dg_ref[...] = jnp.zeros_like(dg_ref)

        dg_ref[...] += jnp.sum(dhv * xh, axis=0, keepdims=True)

    tok = pl.BlockSpec((tm, C), lambda i: (i, 0))
    in_specs = [tok, _resident((1, C)), tok]
    args = [x, g, dh]
    if dres is not None:
        in_specs.append(tok)
        args.append(dres)
    return _pcall(body, name=name, grid=(T // tm,), in_specs=in_specs,
                  out_specs=[tok, _resident((1, C))],
                  out_shape=[_S((T, C), F32), _S((1, C), F32)], compiler_params=_cp("arbitrary"))(*args)


def _mm(pairs, *, nt, out_dtype, name, res=None, res_scale=1.0, tm=TOKEN_TILE, norm_g=None):
    M = pairs[0][0].shape[0]
    N = pairs[0][1].shape[0] if nt else pairs[0][1].shape[1]
    tm = _tile(M, tm)
    n = len(pairs)

    def body(*refs):
        a_refs, w_refs = refs[:n], refs[n:2 * n]
        o_ref = refs[-1] if norm_g is None else refs[-2]
        acc = None
        for a_ref, w_ref in zip(a_refs, w_refs):
            a = a_ref[...]
            a = a.astype(MXU)
            p = _nt(a, w_ref[...]) if nt else _nn(a, w_ref[...])
            acc = p if acc is None else acc + p
        if res is not None:
            acc = refs[2 * n][...] + res_scale * acc
        o_ref[...] = acc.astype(o_ref.dtype)
        if norm_g is not None:
            r = lax.rsqrt(jnp.mean(acc * acc, axis=-1, keepdims=True) + EPS)
            refs[-1][...] = (acc * r * refs[-3][...]).astype(MXU)

    tok = pl.BlockSpec((tm, N), lambda i: (i, 0))
    in_specs = [pl.BlockSpec((tm, a.shape[1]), lambda i: (i, 0)) for a, _ in pairs]
    in_specs += [_resident(w.shape) for _, w in pairs]
    args = [a for a, _ in pairs] + [w for _, w in pairs]
    if res is not None:
        in_specs.append(tok)
        args.append(res)
    if norm_g is None:
        return _pcall(body, name=name, grid=(M // tm,), in_specs=in_specs, out_specs=tok,
                      out_shape=_S((M, N), out_dtype), compiler_params=_cp("parallel"))(*args)
    return _pcall(body, name=name, grid=(M // tm,), in_specs=in_specs + [_resident(norm_g.shape)],
                  out_specs=[tok, tok], out_shape=[_S((M, N), out_dtype), _S((M, N), MXU)],
                  compiler_params=_cp("parallel"))(*args, norm_g)


def _mm_norm_bwd(pairs, norms, *, nt, name, tm=TOKEN_TILE, half=False):
    M = pairs[0][0].shape[0]
    tm = _tile(M, tm)
    n, k = len(pairs), len(norms)
    has_res = [d is not None for _, _, d in norms]

    def body(*refs):
        a_refs, w_refs = refs[:n], refs[n:2 * n]
        rest = list(refs[2 * n:])
        acc = None
        for a_ref, w_ref in zip(a_refs, w_refs):
            p = _nt(a_ref[...].astype(MXU), w_ref[...]) if nt else _nn(a_ref[...].astype(MXU), w_ref[...])
            acc = p if acc is None else acc + p
        ins = [(rest.pop(0), rest.pop(0), rest.pop(0) if has_res[j] else None) for j in range(k)]
        dx_refs, dg_refs = rest[:k], rest[k:2 * k]
        c0 = 0
        for j, ((x_ref, g_ref, d_ref), dx_ref, dg_ref) in enumerate(zip(ins, dx_refs, dg_refs)):
            xv = x_ref[...]
            dhv = acc[:, c0:c0 + xv.shape[1]]
            c0 += xv.shape[1]
            r = lax.rsqrt(jnp.mean(xv * xv, axis=-1, keepdims=True) + EPS)
            xh = xv * r
            dxn = dhv * g_ref[...]
            dx = r * (dxn - xh * jnp.mean(dxn * xh, axis=-1, keepdims=True))
            if d_ref is not None:
                dx = dx + d_ref[...]
            dx_ref[...] = dx
            if half and j == 0:
                rest[2 * k][...] = (dx * FFN_RES).astype(MXU)

            @pl.when(pl.program_id(0) == 0)
            def _():
                dg_ref[...] = jnp.zeros_like(dg_ref)

            dg_ref[...] += jnp.sum(dhv * xh, axis=0, keepdims=True)

    tok = lambda c: pl.BlockSpec((tm, c), lambda i: (i, 0))
    in_specs = [tok(a.shape[1]) for a, _ in pairs] + [_resident(w.shape) for _, w in pairs]
    args = [a for a, _ in pairs] + [w for _, w in pairs]
    for x, g, d in norms:
        in_specs += [tok(x.shape[1]), _resident(g.shape)] + ([tok(x.shape[1])] if d is not None else [])
        args += [x, g] + ([d] if d is not None else [])
    out_specs = [tok(x.shape[1]) for x, _, _ in norms] + [_resident((1, x.shape[1])) for x, _, _ in norms]
    out_shape = [_S((M, x.shape[1]), F32) for x, _, _ in norms] + [_S((1, x.shape[1]), F32) for x, _, _ in norms]
    if half:
        out_specs.append(tok(norms[0][0].shape[1]))
        out_shape.append(_S(norms[0][0].shape, MXU))
    outs = _pcall(body, name=name, grid=(M // tm,), in_specs=in_specs, out_specs=out_specs, out_shape=out_shape,
                  compiler_params=_cp("arbitrary"))(*args)
    if half:
        return list(outs[:k]), list(outs[k:2 * k]), outs[2 * k]
    return list(outs[:k]), list(outs[k:])


def _mm_tn(l, r, name, *, tr=1408, tt=4 * TOKEN_TILE):
    T, R = l.shape
    C = r.shape[1]
    tt = _tile(T, tt)
    tr = tr if R % tr == 0 else (1024 if R % 1024 == 0 and R > 1280 else R)

    def body(l_ref, r_ref, o_ref):
        lv, rv = l_ref[...].astype(MXU), r_ref[...].astype(MXU)

        @pl.when(pl.program_id(1) == 0)
        def _():
            o_ref[...] = jnp.zeros_like(o_ref)

        o_ref[...] += _tn(lv, rv)

    return _pcall(body, name=name, grid=(R // tr, T // tt),
                  in_specs=[pl.BlockSpec((tt, tr), lambda i, t: (t, i)), pl.BlockSpec((tt, C), lambda i, t: (t, 0))],
                  out_specs=pl.BlockSpec((tr, C), lambda i, t: (i, 0)),
                  out_shape=_S((R, C), F32), compiler_params=_cp("parallel", "arbitrary"))(l, r)


FFN_COL_TILE = 1408


def _sigmoid(a):
    return 0.5 * (jnp.tanh(0.5 * a) + 1.0)


def _ffn_up(h, wgT, wuT, name):
    T, D = h.shape
    Fd = wgT.shape[0]
    tm, tn = _tile(T, TOKEN_TILE), _tile(Fd, FFN_COL_TILE)

    def body(h_ref, wg_ref, wu_ref, a_ref, b_ref, s_ref):
        hv = h_ref[...]
        a = _nt(hv, wg_ref[...])
        b = _nt(hv, wu_ref[...])
        s = a * _sigmoid(a) * b
        a_ref[...] = a.astype(a_ref.dtype)
        b_ref[...] = b.astype(b_ref.dtype)
        s_ref[...] = s.astype(s_ref.dtype)

    wspec = pl.BlockSpec((tn, D), lambda j, i: (j, 0))
    ospec = pl.BlockSpec((tm, tn), lambda j, i: (i, j))
    return _pcall(body, name=name, grid=(Fd // tn, T // tm),
                  in_specs=[pl.BlockSpec((tm, D), lambda j, i: (i, 0)), wspec, wspec],
                  out_specs=[ospec, ospec, ospec], out_shape=[_S((T, Fd), MXU)] * 3,
                  compiler_params=_cp("parallel", "parallel"))(h, wgT, wuT)


def _ffn_mid_bwd(dy, wd, a, b, name):
    T, D = dy.shape
    Fd = wd.shape[0]
    tm, tn = _tile(T, TOKEN_TILE), _tile(Fd, FFN_COL_TILE)

    def body(dy_ref, wd_ref, a_ref, b_ref, da_ref, db_ref):
        ds = _nt(dy_ref[...], wd_ref[...])
        av, bv = a_ref[...].astype(F32), b_ref[...].astype(F32)
        sg = _sigmoid(av)
        da_ref[...] = (ds * bv * (sg * (1.0 + av * (1.0 - sg)))).astype(da_ref.dtype)
        db_ref[...] = (ds * (av * sg)).astype(db_ref.dtype)

    aspec = pl.BlockSpec((tm, tn), lambda j, i: (i, j))
    return _pcall(body, name=name, grid=(Fd // tn, T // tm),
                  in_specs=[pl.BlockSpec((tm, D), lambda j, i: (i, 0)), pl.BlockSpec((tn, D), lambda j, i: (j, 0)),
                            aspec, aspec],
                  out_specs=[aspec, aspec], out_shape=[_S((T, Fd), MXU)] * 2,
                  compiler_params=_cp("parallel", "parallel"))(dy, wd, a, b)


def _swap_halves(t):
    w = t.shape[1]
    lane = lax.broadcasted_iota(jnp.int32, t.shape, 1)
    first = (lane % HEAD_DIM) < (HEAD_DIM // 2)
    return jnp.where(first, pltpu.roll(t, w - HEAD_DIM // 2, 1), pltpu.roll(t, HEAD_DIM // 2, 1))


def _rope(t, cos2, sin2):
    reps = t.shape[1] // LANES
    c = jnp.tile(cos2, (1, reps)) if reps > 1 else cos2
    s = jnp.tile(sin2, (1, reps)) if reps > 1 else sin2
    return t * c + _swap_halves(t) * s


def _rope_bwd(dt, cos2, sin2):
    reps = dt.shape[1] // LANES
    c = jnp.tile(cos2, (1, reps)) if reps > 1 else cos2
    s = jnp.tile(sin2, (1, reps)) if reps > 1 else sin2
    return dt * c + _swap_halves(dt * s)


def _in_proj(h, winT, cos2, sin2, name):
    T, D = h.shape
    tm = _tile(T, TOKEN_TILE)
    qe, ke, ve = ATTN_WIDTH, ATTN_WIDTH + KV_WIDTH, ATTN_WIDTH + 2 * KV_WIDTH

    def body(h_ref, w_ref, c_ref, s_ref, q_ref, k_ref, v_ref, u_ref):
        proj = _nt(h_ref[...], w_ref[...])
        cv, sv = c_ref[...], s_ref[...]
        q_ref[...] = _rope(proj[:, :qe], cv, sv).astype(q_ref.dtype)
        k_ref[...] = _rope(proj[:, qe:ke], cv, sv).astype(k_ref.dtype)
        v_ref[...] = proj[:, ke:ve].astype(v_ref.dtype)
        u_ref[...] = proj[:, ve:]

    def tok(w):
        return pl.BlockSpec((tm, w), lambda i: (i, 0))

    return _pcall(body, name=name, grid=(T // tm,),
                  in_specs=[tok(D), _resident(winT.shape), tok(LANES), tok(LANES)],
                  out_specs=[tok(ATTN_WIDTH), tok(KV_WIDTH), tok(KV_WIDTH), tok(POOL_WIDTH)],
                  out_shape=[_S((T, ATTN_WIDTH), MXU), _S((T, KV_WIDTH), MXU), _S((T, KV_WIDTH), MXU),
                             _S((T, POOL_WIDTH), F32)],
                  compiler_params=_cp("parallel"))(h, winT, cos2, sin2)


SWA_TILE_BLOCKS = 4
SM_SCALE = HEAD_DIM ** -0.5


def _swa_bias(first_tile):
    cols = Q_PER_KV * BLOCK
    kj = lax.broadcasted_iota(jnp.int32, (2 * BLOCK, cols), 0)
    qi = lax.broadcasted_iota(jnp.int32, (2 * BLOCK, cols), 1) % BLOCK
    diff = qi + BLOCK - kj
    bias = jnp.where((diff >= 0) & (diff < BLOCK), 0.0, NEG)
    return bias, jnp.where(kj < jnp.where(first_tile, BLOCK, 0), NEG, bias)


def _swa_probs(kh, qs, sink_row, bias):
    s = _nt(kh, qs) + bias
    m = jnp.maximum(jnp.max(s, axis=0, keepdims=True), sink_row)
    e = jnp.exp(s - m)
    es = jnp.exp(sink_row - m)
    inv = 1.0 / (jnp.sum(e, axis=0, keepdims=True) + es)
    return e * inv, es * inv


def _sink_row(sinks_ref, kv):
    return jnp.concatenate([jnp.full((1, BLOCK), sinks_ref[0, kv * Q_PER_KV + g], F32) for g in range(Q_PER_KV)], axis=1)


def _stack_heads(t, kv):
    return jnp.concatenate([t[:, (kv * Q_PER_KV + g) * HEAD_DIM:(kv * Q_PER_KV + g + 1) * HEAD_DIM]
                            for g in range(Q_PER_KV)], axis=0)


def _swa_specs(T):
    tq = _tile(T, SWA_TILE_BLOCKS * BLOCK)
    nbt = tq // BLOCK
    cur = lambda w: pl.BlockSpec((tq, w), lambda i: (i, 0))
    prev = lambda w: pl.BlockSpec((BLOCK, w), lambda i: (jnp.maximum(i * nbt - 1, 0), 0))
    return tq, nbt, cur, prev


def _rows(b):
    return slice(b * BLOCK, (b + 1) * BLOCK)


def _swa_fwd(q, k, v, sinks, name):
    T = q.shape[0]
    tq, nbt, cur, prev = _swa_specs(T)

    def body(sinks_ref, q_ref, k_ref, kp_ref, v_ref, vp_ref, o_ref):
        bias, bias0 = _swa_bias(pl.program_id(0) == 0)
        sink = [_sink_row(sinks_ref, kv) for kv in range(N_KV_HEADS)]
        kx = jnp.concatenate([kp_ref[...], k_ref[...]], axis=0)
        vx = jnp.concatenate([vp_ref[...], v_ref[...]], axis=0)
        for b in range(nbt):
            qv = q_ref[_rows(b), :] * SM_SCALE
            kk, vv = kx[b * BLOCK:(b + 2) * BLOCK], vx[b * BLOCK:(b + 2) * BLOCK]
            for kv in range(N_KV_HEADS):
                hs = slice(kv * HEAD_DIM, (kv + 1) * HEAD_DIM)
                p, _ = _swa_probs(kk[:, hs], _stack_heads(qv, kv), sink[kv], bias0 if b == 0 else bias)
                o_t = _tn(vv[:, hs], p.astype(MXU))
                for g in range(Q_PER_KV):
                    c0 = (kv * Q_PER_KV + g) * HEAD_DIM
                    o_ref[_rows(b), c0:c0 + HEAD_DIM] = o_t[:, _rows(g)].T

    return _pcall(body, name=name, grid=(T // tq,),
                  in_specs=[pl.BlockSpec(memory_space=pltpu.SMEM), cur(ATTN_WIDTH), cur(KV_WIDTH), prev(KV_WIDTH),
                            cur(KV_WIDTH), prev(KV_WIDTH)],
                  out_specs=cur(ATTN_WIDTH), out_shape=_S((T, ATTN_WIDTH), F32),
                  compiler_params=_cp("parallel"))(sinks, q, k, k, v, v)


def _swa_bwd(q, k, v, do, sinks, name):
    T = q.shape[0]
    tq, nbt, cur, prev = _swa_specs(T)
    per_tile = lambda w: pl.BlockSpec((BLOCK, w), lambda i: (i, 0))

    def add(acc, t):
        return t if acc is None else acc + t

    def body(sinks_ref, q_ref, k_ref, kp_ref, v_ref, vp_ref, do_ref,
             dq_ref, dk_ref, dkp_ref, dv_ref, dvp_ref, dsk_ref):
        bias, bias0 = _swa_bias(pl.program_id(0) == 0)
        sink = [_sink_row(sinks_ref, kv) for kv in range(N_KV_HEADS)]
        kx = jnp.concatenate([kp_ref[...], k_ref[...]], axis=0)
        vx = jnp.concatenate([vp_ref[...], v_ref[...]], axis=0)

        @pl.when(pl.program_id(0) == 0)
        def _():
            dsk_ref[...] = jnp.zeros_like(dsk_ref)

        dk_acc, dv_acc = [None] * (nbt + 1), [None] * (nbt + 1)
        dsk_acc = [None] * N_Q_HEADS
        for b in range(nbt):
            qv, dov = q_ref[_rows(b), :] * SM_SCALE, do_ref[_rows(b), :].astype(MXU)
            kk, vv = kx[b * BLOCK:(b + 2) * BLOCK], vx[b * BLOCK:(b + 2) * BLOCK]
            dks, dvs = [], []
            for kv in range(N_KV_HEADS):
                hs = slice(kv * HEAD_DIM, (kv + 1) * HEAD_DIM)
                qs, dos = _stack_heads(qv, kv), _stack_heads(dov, kv)
                p, ps = _swa_probs(kk[:, hs], qs, sink[kv], bias0 if b == 0 else bias)
                dp = _nt(vv[:, hs], dos)
                delta = jnp.sum(p * dp, axis=0, keepdims=True)
                ds = (p * (dp - delta)).astype(MXU)
                dq_t = _tn(kk[:, hs], ds) * SM_SCALE
                dks.append(_nn(ds, qs))
                dvs.append(_nn(p.astype(MXU), dos))
                dsink = -ps * delta
                for g in range(Q_PER_KV):
                    h = kv * Q_PER_KV + g
                    dq_ref[_rows(b), h * HEAD_DIM:(h + 1) * HEAD_DIM] = dq_t[:, _rows(g)].T
                    dsk_acc[h] = add(dsk_acc[h], jnp.sum(dsink[:, _rows(g)], axis=1, keepdims=True))
            dk, dv = jnp.concatenate(dks, axis=1), jnp.concatenate(dvs, axis=1)
            dk_acc[b], dk_acc[b + 1] = add(dk_acc[b], dk[:BLOCK]), add(dk_acc[b + 1], dk[BLOCK:])
            dv_acc[b], dv_acc[b + 1] = add(dv_acc[b], dv[:BLOCK]), add(dv_acc[b + 1], dv[BLOCK:])
        dkp_ref[...], dvp_ref[...] = dk_acc[0], dv_acc[0]
        dk_ref[...] = jnp.concatenate(dk_acc[1:], axis=0)
        dv_ref[...] = jnp.concatenate(dv_acc[1:], axis=0)
        for h in range(N_Q_HEADS):
            dsk_ref[h:h + 1, :] += jnp.broadcast_to(dsk_acc[h], (1, LANES))

    kvs, kvp = _S((T, KV_WIDTH), F32), _S((T // tq * BLOCK, KV_WIDTH), F32)
    return _pcall(body, name=name, grid=(T // tq,),
                  in_specs=[pl.BlockSpec(memory_space=pltpu.SMEM), cur(ATTN_WIDTH), cur(KV_WIDTH), prev(KV_WIDTH),
                            cur(KV_WIDTH), prev(KV_WIDTH), cur(ATTN_WIDTH)],
                  out_specs=[cur(ATTN_WIDTH), cur(KV_WIDTH), per_tile(KV_WIDTH), cur(KV_WIDTH), per_tile(KV_WIDTH),
                             _resident((N_Q_HEADS, LANES))],
                  out_shape=[_S((T, ATTN_WIDTH), F32), kvs, kvp, kvs, kvp, _S((N_Q_HEADS, LANES), F32)],
                  compiler_params=_cp("arbitrary"))(sinks, q, k, k, v, v, do)


def _dproj(dq, dk, dkp, dv, dvp, du, cos2, sin2, name):
    T = dq.shape[0]
    tq, nbt, cur, _ = _swa_specs(T)
    nt = T // tq
    nxt = lambda w: pl.BlockSpec((BLOCK, w), lambda i: (jnp.minimum(i + 1, nt - 1), 0))

    def body(dq_ref, dk_ref, dkp_ref, dv_ref, dvp_ref, du_ref, c_ref, s_ref, o_ref):
        more = (pl.program_id(0) < nt - 1).astype(F32)
        cv, sv = c_ref[...], s_ref[...]

        def whole(t_ref, p_ref):
            t, last = t_ref[...], t_ref[tq - BLOCK:, :] + more * p_ref[...]
            return last if nbt == 1 else jnp.concatenate([t[:tq - BLOCK], last], axis=0)

        o_ref[...] = jnp.concatenate(
            [_rope_bwd(dq_ref[...], cv, sv), _rope_bwd(whole(dk_ref, dkp_ref), cv, sv), whole(dv_ref, dvp_ref),
             du_ref[...]], axis=1).astype(o_ref.dtype)

    width = ATTN_WIDTH + 2 * KV_WIDTH + POOL_WIDTH
    return _pcall(body, name=name, grid=(nt,),
                  in_specs=[cur(ATTN_WIDTH), cur(KV_WIDTH), nxt(KV_WIDTH), cur(KV_WIDTH), nxt(KV_WIDTH),
                            cur(POOL_WIDTH), cur(LANES), cur(LANES)],
                  out_specs=cur(width), out_shape=_S((T, width), MXU),
                  compiler_params=_cp("parallel"))(dq, dk, dkp, dv, dvp, du, cos2, sin2)


def _pool_specs(T):
    tm = _tile(T, TOKEN_TILE)
    hb = tm // POOL_HALO
    nh = T // POOL_HALO
    tok = lambda w: pl.BlockSpec((tm, w), lambda i: (i, 0))
    before = pl.BlockSpec((POOL_HALO, POOL_WIDTH), lambda i: (jnp.maximum(i * hb - 1, 0), 0))
    after = pl.BlockSpec((POOL_HALO, POOL_WIDTH), lambda i: (jnp.minimum((i + 1) * hb, nh - 1), 0))
    return tm, tok, before, after


def _window_counts(i, tm, rows, w):
    t = i * tm + lax.broadcasted_iota(jnp.int32, (rows, 1), 0)
    return jnp.minimum(t + 1, w).astype(F32)


def _pooled(u_ext, i, tm):
    out = []
    for g, w in enumerate(POOL_WINDOWS):
        acc = u_ext[:, g * POOL_GROUP:(g + 1) * POOL_GROUP]
        tok = acc[POOL_HALO:, :]
        sh = 1
        while sh < w:
            acc = acc + pltpu.roll(acc, sh, 0)
            sh *= 2
        out.append(acc[POOL_HALO:, :] / _window_counts(i, tm, tm, w) - tok)
    return out


def _pool_fwd(u, out_a, pool_w, pool_scale, ga, gb, name):
    T = u.shape[0]
    tm, tok, before, _ = _pool_specs(T)

    def body(u_ref, halo_ref, oa_ref, pw_ref, sc_ref, ga_ref, gb_ref, ob_ref, mg_ref):
        i = pl.program_id(0)
        halo = halo_ref[...] * (i > 0).astype(F32)
        pooled = _pooled(jnp.concatenate([halo, u_ref[...]], axis=0), i, tm)
        mixed = [_nn(pooled[g].astype(MXU), pw_ref[g].astype(MXU)) for g in range(len(POOL_WINDOWS))]
        ob = jnp.concatenate(mixed, axis=1) * sc_ref[...]
        ob_ref[...] = ob
        oa = oa_ref[...]
        ra = lax.rsqrt(jnp.mean(oa * oa, axis=-1, keepdims=True) + EPS)
        rb = lax.rsqrt(jnp.mean(ob * ob, axis=-1, keepdims=True) + EPS)
        mg_ref[...] = jnp.concatenate([oa * ra * ga_ref[...], ob * rb * gb_ref[...]], axis=1).astype(mg_ref.dtype)

    vec = _resident((1, POOL_WIDTH))
    return _pcall(body, name=name, grid=(T // tm,),
                  in_specs=[tok(POOL_WIDTH), before, tok(ATTN_WIDTH), _resident(pool_w.shape), vec, vec, vec],
                  out_specs=[tok(POOL_WIDTH), tok(ATTN_WIDTH + POOL_WIDTH)],
                  out_shape=[_S((T, POOL_WIDTH), F32), _S((T, ATTN_WIDTH + POOL_WIDTH), MXU)],
                  compiler_params=_cp("parallel"))(u, u, out_a, pool_w, pool_scale, ga, gb)


def _pool_bwd(u, dob, pool_w, pool_scale, name):
    T = u.shape[0]
    tm, tok, before, after = _pool_specs(T)
    nt = T // tm
    G = len(POOL_WINDOWS)

    def body(u_ref, halo_ref, dob_ref, dnext_ref, pw_ref, sc_ref, du_ref, dpw_ref, dsc_ref):
        i = pl.program_id(0)
        halo = halo_ref[...] * (i > 0).astype(F32)
        pooled = _pooled(jnp.concatenate([halo, u_ref[...]], axis=0), i, tm)
        dnext = dnext_ref[...] * (i < nt - 1).astype(F32)
        dext = jnp.concatenate([dob_ref[...], dnext], axis=0) * sc_ref[...]

        @pl.when(i == 0)
        def _():
            dpw_ref[...] = jnp.zeros_like(dpw_ref)
            dsc_ref[...] = jnp.zeros_like(dsc_ref)

        dus, dscs = [], []
        for g, w in enumerate(POOL_WINDOWS):
            gs = slice(g * POOL_GROUP, (g + 1) * POOL_GROUP)
            pw = pw_ref[g].astype(MXU)
            pg = pooled[g].astype(MXU)
            dmix = dext[:, gs].astype(MXU)
            dscs.append(jnp.sum(dob_ref[:, gs] * _nn(pg, pw), axis=0, keepdims=True))
            dpw_ref[g] += _tn(pg, dmix[:tm, :])
            dpooled = _nt(dmix, pw)
            acc = dpooled / _window_counts(i, tm, tm + POOL_HALO, w)
            sh = 1
            while sh < w:
                acc = acc + pltpu.roll(acc, tm + POOL_HALO - sh, 0)
                sh *= 2
            dus.append(acc[:tm, :] - dpooled[:tm, :])
        du_ref[...] = jnp.concatenate(dus, axis=1)
        dsc_ref[...] += jnp.concatenate(dscs, axis=1)

    vec = _resident((1, POOL_WIDTH))
    return _pcall(body, name=name, grid=(nt,),
                  in_specs=[tok(POOL_WIDTH), before, tok(POOL_WIDTH), after, _resident(pool_w.shape), vec],
                  out_specs=[tok(POOL_WIDTH), _resident(pool_w.shape), vec],
                  out_shape=[_S((T, POOL_WIDTH), F32), _S(pool_w.shape, F32), _S((1, POOL_WIDTH), F32)],
                  compiler_params=_cp("arbitrary"))(u, u, dob, dob, pool_w, pool_scale)


def _xattn_probs(qh, kh):
    s = _nt(qh, kh) * (X_HEAD_DIM ** -0.5)
    e = jnp.exp(s - jnp.max(s, axis=1, keepdims=True))
    return e / jnp.sum(e, axis=1, keepdims=True)


def _xattn_fwd(q, kvm, name):
    T, XW = q.shape
    tm = _tile(T, TOKEN_TILE)

    def body(q_ref, kv_ref, o_ref):
        for h in range(X_HEADS):
            hs = slice(h * X_HEAD_DIM, (h + 1) * X_HEAD_DIM)
            vs = slice(XW + h * X_HEAD_DIM, XW + (h + 1) * X_HEAD_DIM)
            p = _xattn_probs(q_ref[:, hs], kv_ref[:, hs])
            o_ref[:, hs] = _nn(p.astype(MXU), kv_ref[:, vs]).astype(o_ref.dtype)

    return _pcall(body, name=name, grid=(T // tm,),
                  in_specs=[pl.BlockSpec((tm, XW), lambda i: (i, 0)), _resident(kvm.shape)],
                  out_specs=pl.BlockSpec((tm, XW), lambda i: (i, 0)), out_shape=_S((T, XW), MXU),
                  compiler_params=_cp("parallel"))(q, kvm)


def _xattn_bwd(q, kvm, do, name):
    T, XW = q.shape
    tm = _tile(T, TOKEN_TILE)

    def body(q_ref, kv_ref, do_ref, dq_ref, dkv_ref):
        @pl.when(pl.program_id(0) == 0)
        def _():
            dkv_ref[...] = jnp.zeros_like(dkv_ref)

        for h in range(X_HEADS):
            hs = slice(h * X_HEAD_DIM, (h + 1) * X_HEAD_DIM)
            vs = slice(XW + h * X_HEAD_DIM, XW + (h + 1) * X_HEAD_DIM)
            qh, doh = q_ref[:, hs], do_ref[:, hs]
            p = _xattn_probs(qh, kv_ref[:, hs])
            dp = _nt(doh, kv_ref[:, vs])
            ds = (p * (dp - jnp.sum(p * dp, axis=1, keepdims=True)) * (X_HEAD_DIM ** -0.5)).astype(MXU)
            dq_ref[:, hs] = _nn(ds, kv_ref[:, hs]).astype(dq_ref.dtype)
            dkv_ref[:, hs] += _tn(ds, qh)
            dkv_ref[:, vs] += _tn(p.astype(MXU), doh)

    tok = pl.BlockSpec((tm, XW), lambda i: (i, 0))
    return _pcall(body, name=name, grid=(T // tm,),
                  in_specs=[tok, _resident(kvm.shape), tok],
                  out_specs=[tok, _resident(kvm.shape)],
                  out_shape=[_S((T, XW), MXU), _S(kvm.shape, F32)],
                  compiler_params=_cp("arbitrary"))(q, kvm, do)


def _loss_head(x, tgt, g, name):
    T, D = x.shape
    tm = _tile(T, TOKEN_TILE)

    def body(x_ref, t_ref, g_ref, loss_ref, dx_ref, half_ref, dg_ref):
        xv, gv = x_ref[...], g_ref[...]
        r = lax.rsqrt(jnp.mean(xv * xv, axis=-1, keepdims=True) + EPS)
        xh = xv * r
        e = xh * gv - t_ref[...]
        dy = e * (1.0 / D)
        dxn = dy * gv
        dx = r * (dxn - xh * jnp.mean(dxn * xh, axis=-1, keepdims=True))
        dx_ref[...] = dx
        half_ref[...] = (dx * FFN_RES).astype(MXU)

        @pl.when(pl.program_id(0) == 0)
        def _():
            loss_ref[...] = jnp.zeros_like(loss_ref)
            dg_ref[...] = jnp.zeros_like(dg_ref)

        part = jnp.sum(jnp.sum(e * e, axis=1, keepdims=True), axis=0, keepdims=True) * (0.5 / D)
        loss_ref[...] += jnp.broadcast_to(part, (1, LANES))
        dg_ref[...] += jnp.sum(dy * xh, axis=0, keepdims=True)

    tok = pl.BlockSpec((tm, D), lambda i: (i, 0))
    return _pcall(body, name=name, grid=(T // tm,),
                  in_specs=[tok, tok, _resident((1, D))],
                  out_specs=[_resident((1, LANES)), tok, tok, _resident((1, D))],
                  out_shape=[_S((1, LANES), F32), _S((T, D), F32), _S((T, D), MXU), _S((1, D), F32)],
                  compiler_params=_cp("arbitrary"))(x, tgt, g)


def _rows_tile(rows):
    for t in (512, 416, 352, 256, 128, 64, 32, 16, 8):
        if rows % t == 0:
            return t
    return rows


def _pair_sum(grads, gots, sizes, place, name):
    nw = len(sizes)
    C = grads[0].shape[1]

    def body(p_ref, *refs):
        g, got, out = refs[:nw], refs[nw:2 * nw], refs[2 * nw:]
        for w in range(nw):
            out[w][...] = (g[w][...] + got[w][...]).astype(out[w].dtype)

    def blk(w):
        return (sizes[w] // 4, C)

    in_specs = [pl.BlockSpec(blk(w), lambda q, s, p: (4 * q + 2 * p[0] + s, 0)) for w in range(nw)]
    in_specs += [pl.BlockSpec(blk(w), lambda q, s, p: (2 * q + s, 0)) for w in range(nw)]
    out_specs = [pl.BlockSpec(blk(w), lambda q, s, p: (2 * q + s, 0)) for w in range(nw)]
    gs = pltpu.PrefetchScalarGridSpec(num_scalar_prefetch=1, grid=(N_CHIPS, 2), in_specs=in_specs, out_specs=out_specs)
    return _pcall(body, name=name, grid_spec=gs, out_shape=[_S((2 * n, C), MXU) for n in sizes],
                  compiler_params=_cp("parallel", "parallel"))(place, *grads, *gots)


def _final_sum(grads, gots, recvs, sizes, place, name):
    nw = len(sizes)
    C = grads[0].shape[1]

    def body(p_ref, *refs):
        g, got, rv, out = refs[:nw], refs[nw:2 * nw], refs[2 * nw:5 * nw], refs[5 * nw:]
        for w in range(nw):
            acc = g[w][...] + got[w][...]
            for j in range(3):
                acc = acc + rv[3 * w + j][...].astype(F32)
            out[w][...] = acc

    def blk(w):
        return (sizes[w] // 4, C)

    in_specs = [pl.BlockSpec(blk(w), lambda s, p: (4 * p[1] + 2 * p[0] + s, 0)) for w in range(nw)]
    in_specs += [pl.BlockSpec(blk(w), lambda s, p: (2 * p[1] + s, 0)) for w in range(nw)]
    args = list(grads) + list(gots)
    for w in range(nw):
        for j in range(3):
            in_specs.append(pl.BlockSpec(blk(w), lambda s, p, j=j: (2 * j + s, 0)))
            args.append(recvs[w])
    out_specs = [pl.BlockSpec(blk(w), lambda s, p: (2 * p[0] + s, 0)) for w in range(nw)]
    gs = pltpu.PrefetchScalarGridSpec(num_scalar_prefetch=1, grid=(2,), in_specs=in_specs, out_specs=out_specs)
    return _pcall(body, name=name, grid_spec=gs, out_shape=[_S((n, C), F32) for n in sizes],
                  compiler_params=_cp("parallel"))(place, *args)


def _adamw(w, g, m, v, name):
    R, C = w.shape
    tr = _rows_tile(R)
    c1 = 1.0 / (1.0 - ADAM_B1 ** ADAM_STEP)
    c2 = 1.0 / (1.0 - ADAM_B2 ** ADAM_STEP)

    def body(w_ref, g_ref, m_ref, v_ref, d_ref, nm_ref, nv_ref):
        gv = g_ref[...]
        nm = ADAM_B1 * m_ref[...] + (1.0 - ADAM_B1) * gv
        nv = ADAM_B2 * v_ref[...] + (1.0 - ADAM_B2) * (gv * gv)
        d_ref[...] = -ADAM_LR * ((nm * c1) / (jnp.sqrt(nv * c2) + ADAM_EPS) + ADAM_WD * w_ref[...])
        nm_ref[...] = nm
        nv_ref[...] = nv

    spec = pl.BlockSpec((tr, C), lambda i: (i, 0))
    return _pcall(body, name=name, grid=(R // tr,), in_specs=[spec] * 4, out_specs=[spec] * 3,
                  out_shape=[_S((R, C), F32)] * 3, compiler_params=_cp("parallel"))(w, g, m, v)


ANY = pl.BlockSpec(memory_space=pl.ANY)


def _place():
    x, y, c = lax.axis_index("x"), lax.axis_index("y"), lax.axis_index("c")
    chips = [(1 - x, y), (x, 1 - y), (1 - x, 1 - y)]
    return x, y, c, chips


def _remote(src, dst, send_sem, recv_sem, dev):
    return pltpu.make_async_remote_copy(src_ref=src, dst_ref=dst, send_sem=send_sem, recv_sem=recv_sem,
                                        device_id=dev, device_id_type=MESH)


def _drain(like, send_sem, recv_sem, me, *, send=False, recv=False):
    d = _remote(like, like, send_sem, recv_sem, me)
    if send:
        d.wait_send()
    if recv:
        d.wait_recv()


def _dma_sems(n):
    return [pltpu.SemaphoreType.DMA((n,)), pltpu.SemaphoreType.DMA((n,))]


def _comm_params():
    return pltpu.CompilerParams(has_side_effects=True)


def _on_sequencer(exchange, refs, sem_types, peers_of, name, seq_id):
    def launch(*sems):
        x, y, c, chips = _place()
        barrier = pltpu.get_barrier_semaphore()
        peers = peers_of(x, y, c, chips)
        for peer in peers:
            pl.semaphore_signal(barrier, inc=1, device_id=peer, device_id_type=MESH)
        pl.semaphore_wait(barrier, len(peers))
        exchange(refs, *sems)

    pl.kernel(launch, mesh=plsc.ScalarSubcoreMesh(axis_name="seq", num_cores=1), name=name,
              scratch_types=tuple(sem_types), compiler_params=pltpu.CompilerParams(collective_id=seq_id))()


def _hbm_ref(a):
    return jax.new_ref(a, memory_space=pltpu.MemorySpace.HBM)


def _allgather_weights(bufs, sizes, name, seq_id=None):
    nw = len(sizes)

    def exchange(out, s_ici, r_ici, s_fwd, r_fwd):
        x, y, c, chips = _place()
        me, sib = (x, y, c), (x, y, 1 - c)
        q_me = 2 * x + y

        def rows(w, q):
            hw = sizes[w] // 2
            return out[w].at[pl.ds(q * sizes[w] + c * hw, hw)]

        def three(w):
            return out[w].at[pl.ds(0, 3 * (sizes[w] // 2))]

        for w in range(nw):
            for px, py in chips:
                _remote(rows(w, q_me), rows(w, q_me), s_ici.at[w], r_ici.at[w], (px, py, c)).start()
        for w in range(nw):
            _drain(three(w), s_ici.at[w], r_ici.at[w], me, recv=True)
            for px, py in chips:
                got = rows(w, 2 * px + py)
                _remote(got, got, s_fwd.at[w], r_fwd.at[w], sib).start()
        for w in range(nw):
            _drain(three(w), s_fwd.at[w], r_fwd.at[w], me, recv=True)
        for w in range(nw):
            _drain(three(w), s_ici.at[w], r_ici.at[w], me, send=True)
            _drain(three(w), s_fwd.at[w], r_fwd.at[w], me, send=True)

    if seq_id is not None:
        refs = [_hbm_ref(b) for b in bufs]
        _on_sequencer(exchange, refs, _dma_sems(nw) + _dma_sems(nw),
                      lambda x, y, c, chips: [(x, y, 1 - c)] + [(px, py, c) for px, py in chips], name, seq_id)
        return [r[...] for r in refs]

    def body(*refs):
        exchange(refs[nw:2 * nw], *refs[2 * nw:])

    return _pcall(body, name=name, in_specs=[ANY] * nw, out_specs=[ANY] * nw,
                  out_shape=[_S(b.shape, b.dtype) for b in bufs],
                  input_output_aliases={w: w for w in range(nw)},
                  scratch_shapes=_dma_sems(nw) + _dma_sems(nw), compiler_params=_comm_params())(*bufs)


def _sibling_only(x, y, c, chips):
    return [(x, y, 1 - c)]


def _rs_pair_exchange(grads, sizes, name, seq_id=None):
    C = grads[0].shape[1]
    nw = len(sizes)
    out_shape = [_S((2 * n, C), F32) for n in sizes]

    def exchange(refs, s_sem, r_sem):
        g, got = refs[:nw], refs[nw:2 * nw]
        x, y, c, _ = _place()
        me, sib = (x, y, c), (x, y, 1 - c)
        for w in range(nw):
            hw = sizes[w] // 2
            for q in range(N_CHIPS):
                _remote(g[w].at[pl.ds(q * sizes[w] + (1 - c) * hw, hw)], got[w].at[pl.ds(q * hw, hw)],
                        s_sem.at[w], r_sem.at[w], sib).start()
        for w in range(nw):
            _drain(got[w], s_sem.at[w], r_sem.at[w], me, send=True, recv=True)

    if seq_id is not None:
        gots = [jax.empty_ref(s, memory_space=pltpu.MemorySpace.HBM) for s in out_shape]
        _on_sequencer(exchange, [_hbm_ref(g) for g in grads] + gots, _dma_sems(nw), _sibling_only, name, seq_id)
        return [r[...] for r in gots]

    def body(*refs):
        exchange(refs[:2 * nw], *refs[2 * nw:])

    return _pcall(body, name=name, in_specs=[ANY] * nw, out_specs=[ANY] * nw, out_shape=out_shape,
                  scratch_shapes=_dma_sems(nw), compiler_params=_comm_params())(*grads)


def _rs_chip_exchange(sums, sizes, name, seq_id=None):
    C = sums[0].shape[1]
    nw = len(sizes)
    out_shape = [_S((3 * (n // 2), C), sums[0].dtype) for n in sizes]

    def exchange(refs, s_sem, r_sem):
        sm, got = refs[:nw], refs[nw:2 * nw]
        x, y, c, chips = _place()
        for w in range(nw):
            hw = sizes[w] // 2
            for j, (px, py) in enumerate(chips):
                _remote(sm[w].at[pl.ds((2 * px + py) * hw, hw)], got[w].at[pl.ds(j * hw, hw)],
                        s_sem.at[w], r_sem.at[w], (px, py, c)).start()
        for w in range(nw):
            _drain(got[w], s_sem.at[w], r_sem.at[w], (x, y, c), send=True, recv=True)

    if seq_id is not None:
        gots = [jax.empty_ref(s, memory_space=pltpu.MemorySpace.HBM) for s in out_shape]
        _on_sequencer(exchange, [_hbm_ref(s) for s in sums] + gots, _dma_sems(nw),
                      lambda x, y, c, chips: [(px, py, c) for px, py in chips], name, seq_id)
        return [r[...] for r in gots]

    def body(*refs):
        exchange(refs[:2 * nw], *refs[2 * nw:])

    return _pcall(body, name=name, in_specs=[ANY] * nw, out_specs=[ANY] * nw, out_shape=out_shape,
                  scratch_shapes=_dma_sems(nw), compiler_params=_comm_params())(*sums)


def _rs_share_halves(reds, sizes, name, seq_id=None):
    nw = len(sizes)

    def exchange(out, s_sem, r_sem):
        x, y, c, _ = _place()
        for w in range(nw):
            hw = sizes[w] // 2
            rows = out[w].at[pl.ds(c * hw, hw)]
            _remote(rows, rows, s_sem.at[w], r_sem.at[w], (x, y, 1 - c)).start()
        for w in range(nw):
            _drain(out[w].at[pl.ds(0, sizes[w] // 2)], s_sem.at[w], r_sem.at[w], (x, y, c), send=True, recv=True)

    if seq_id is not None:
        refs = [_hbm_ref(r) for r in reds]
        _on_sequencer(exchange, refs, _dma_sems(nw), _sibling_only, name, seq_id)
        return [r[...] for r in refs]

    def body(*refs):
        exchange(refs[nw:2 * nw], *refs[2 * nw:])

    return _pcall(body, name=name, in_specs=[ANY] * nw, out_specs=[ANY] * nw,
                  out_shape=[_S(r.shape, r.dtype) for r in reds],
                  input_output_aliases={w: w for w in range(nw)},
                  scratch_shapes=_dma_sems(nw), compiler_params=_comm_params())(*reds)


def _allreduce_small(part, name):
    R, C = part.shape

    def body(p_ref, o_ref, buf, s_sem, r_sem):
        x, y, c, _ = _place()
        my_id = 4 * x + 2 * y + c
        buf[my_id] = p_ref[...]
        cps = []
        for k in range(1, N_DEV):
            fx, fy, fc = (k >> 2) & 1, (k >> 1) & 1, k & 1
            peer = (x ^ fx, y ^ fy, c ^ fc)
            cps.append(_remote(p_ref, buf.at[my_id], s_sem.at[k - 1], r_sem.at[k - 1], peer))
        for cp in cps:
            cp.start()
        for cp in cps:
            cp.wait()
        acc = buf[0]
        for d in range(1, N_DEV):
            acc = acc + buf[d]
        o_ref[...] = acc

    vm = pl.BlockSpec(memory_space=pltpu.VMEM)
    return _pcall(body, name=name, in_specs=[vm], out_specs=vm, out_shape=_S((R, C), F32),
                  scratch_shapes=[pltpu.VMEM((N_DEV, R, C), F32)] + _dma_sems(N_DEV - 1),
                  compiler_params=pltpu.CompilerParams(has_side_effects=True, vmem_limit_bytes=VMEM_LIMIT))(part)


SHARD_STEPS = 4


def _own_shard_buffers(ws, l, group, place, name):
    nw = len(group)

    def body(p_ref, *refs):
        for (_, tr), i_ref, o_ref in zip(group, refs[:nw], refs[nw:]):
            v = i_ref[...]
            o_ref[...] = (v.T if tr else v).astype(o_ref.dtype)

    in_specs, out_specs, out_shape, sizes = [], [], [], []
    for wname, tr in group:
        _, K, n = ws[wname].shape
        in_specs.append(pl.BlockSpec((None, K // SHARD_STEPS, n), lambda i, p: (l, i, 0)))
        if tr:
            out_specs.append(pl.BlockSpec((n, K // SHARD_STEPS), lambda i, p: (p[1], i)))
            out_shape.append(_S((N_CHIPS * n, K), MXU))
            sizes.append(n)
        else:
            out_specs.append(pl.BlockSpec((K // SHARD_STEPS, n), lambda i, p: (p[1] * SHARD_STEPS + i, 0)))
            out_shape.append(_S((N_CHIPS * K, n), MXU))
            sizes.append(K)
    gs = pltpu.PrefetchScalarGridSpec(num_scalar_prefetch=1, grid=(SHARD_STEPS,), in_specs=in_specs, out_specs=out_specs)
    bufs = _pcall(body, name=name, grid_spec=gs, out_shape=out_shape,
                  compiler_params=_cp("parallel"))(place, *[ws[wname] for wname, _ in group])
    return list(bufs), sizes


def _after(xs, ys):
    return lax.optimization_barrier((xs, ys))[0]


def _small_rows(v):
    flat = v.reshape(-1)
    pad = (-flat.shape[0]) % 1024
    return jnp.pad(flat, (0, pad)).reshape(-1, 1024)


def _pack_small(vals):
    rows = [_small_rows(vals[n]) for n in SMALL]
    cat = jnp.concatenate(rows, axis=0)
    pad = (-cat.shape[0]) % 8
    return jnp.pad(cat, ((0, pad), (0, 0)))


def _unpack_small(packed, like):
    out, r = {}, 0
    for n in SMALL:
        size = like[n].size
        nr = -(-size // 1024)
        out[n] = packed[r:r + nr].reshape(-1)[:size].reshape(like[n].shape)
        r += nr
    return out


def _rope_tables(positions):
    inv_freq = ROPE_THETA ** (-jnp.arange(0, HEAD_DIM, 2, dtype=F32) / HEAD_DIM)
    ang = positions.astype(F32)[:, None] * inv_freq
    cos, sin = jnp.cos(ang), jnp.sin(ang)
    return jnp.concatenate([cos, cos, cos, cos], axis=1), jnp.concatenate([-sin, sin, -sin, sin], axis=1)


def _layer_fwd(l, x, h1, memv, W, P, cos2, sin2, next_norm):
    t = f"l{l}"
    sv = {"x0": x, "h1": h1}
    sv["a1"], sv["b1"], sv["s1"] = _ffn_up(h1, W["ffn1_w_gate"], W["ffn1_w_up"], t + "_ffn1_up")
    sv["x1"], sv["h2"] = _mm([(sv["s1"], W["ffn1_w_down"])], nt=False, out_dtype=F32, res=x, res_scale=FFN_RES,
                             name=t + "_ffn1_down", norm_g=P["mix_norm"])

    sv["q"], sv["k"], sv["v"], sv["u"] = _in_proj(sv["h2"], W["w_in"], cos2, sin2, t + "_in_proj")
    sv["oa"] = _swa_fwd(sv["q"], sv["k"], sv["v"], P["attn_sinks"], t + "_swa")
    sv["ob"], sv["mg"] = _pool_fwd(sv["u"], sv["oa"], P["pool_w"], P["pool_scale"], P["attn_out_norm"],
                                   P["pool_out_norm"], t + "_pool")
    sv["x2"], sv["h3"] = _mm([(sv["mg"], W["w_out"])], nt=False, out_dtype=F32, res=sv["x1"], name=t + "_out_proj",
                             norm_g=P["xattn_norm"])

    sv["memn"] = _rms_fwd(memv, P["mem_norm"], t + "_mem_norm")
    sv["q3"] = _mm([(sv["h3"], W["xattn_wq"])], nt=False, out_dtype=MXU, name=t + "_xq")
    sv["kv"] = _mm([(sv["memn"], W["xattn_wkv"])], nt=True, out_dtype=MXU, name=t + "_xkv")
    sv["o3"] = _xattn_fwd(sv["q3"], sv["kv"], t + "_xattn")
    sv["x3"], sv["h4"] = _mm([(sv["o3"], W["xattn_wo"])], nt=False, out_dtype=F32, res=sv["x2"], name=t + "_xo",
                             norm_g=P["ffn2_norm"])

    sv["a2"], sv["b2"], sv["s2"] = _ffn_up(sv["h4"], W["ffn2_w_gate"], W["ffn2_w_up"], t + "_ffn2_up")
    out = _mm([(sv["s2"], W["ffn2_w_down"])], nt=False, out_dtype=F32, res=sv["x3"], res_scale=FFN_RES,
              name=t + "_ffn2_down", norm_g=next_norm)
    x4, h_next = out if next_norm is not None else (out, None)
    return x4, h_next, sv


def _ffn_bwd(t, dx, dy, x_in, g, h, a, b, s, wgT, wuT, wd, behind=None, begin=None, half=False):
    d_wd = _mm_tn(s, dy, t + "_dwd")
    da, db = _ffn_mid_bwd(dy, wd, a, b, t + "_mid")
    if behind is not None:
        da = _after(da, behind)
    d_wg = _mm_tn(da, h, t + "_dwg")
    d_wu = _mm_tn(db, h, t + "_dwu")
    if begin is not None:
        da = _after(da, begin(d_wg, d_wu, d_wd))
    out = _mm_norm_bwd([(da, wgT), (db, wuT)], [(x_in, g, dx)], nt=False, name=t + "_dh", tm=TOKEN_TILE // 2,
                       half=half)
    return out[0][0], (out[2] if half else None), out[1][0], d_wg, d_wu, d_wd


def _layer_bwd(l, dx, dy, sv, memv, W, P, cos2, sin2, begin_rest, begin_ffn1):
    t = f"l{l}b"
    GW, GP, marks = {}, {}, {}
    dx, _, GP["ffn2_norm"], GW["ffn2_w_gate"], GW["ffn2_w_up"], GW["ffn2_w_down"] = _ffn_bwd(
        t + "_ffn2", dx, dy, sv["x3"], P["ffn2_norm"], sv["h4"], sv["a2"], sv["b2"], sv["s2"],
        W["ffn2_w_gate"], W["ffn2_w_up"], W["ffn2_w_down"])
    marks["ffn2"] = dx

    GW["xattn_wo"] = _mm_tn(sv["o3"], dx, t + "_dwo")
    do3 = _mm([(dx, W["xattn_wo"])], nt=True, out_dtype=MXU, name=t + "_do3")
    dq3, dkv = _xattn_bwd(sv["q3"], sv["kv"], do3, t + "_xattn")
    GW["xattn_wq"] = _mm_tn(sv["h3"], dq3, t + "_dwq")
    GW["xattn_wkv"] = _mm_tn(dkv, sv["memn"], t + "_dwkv")
    dmemn = _mm([(dkv, W["xattn_wkv"])], nt=False, out_dtype=F32, name=t + "_dmemn")
    _, GP["mem_norm"] = _rms_bwd(memv, P["mem_norm"], dmemn, t + "_mem_norm_bwd")
    (dx,), (GP["xattn_norm"],) = _mm_norm_bwd([(dq3, W["xattn_wq"])], [(sv["x2"], P["xattn_norm"], dx)], nt=True,
                                             name=t + "_dh3")
    marks["xattn"] = dx

    GW["w_out"] = _mm_tn(sv["mg"], dx, t + "_dwout")
    (doa, dob), (GP["attn_out_norm"], GP["pool_out_norm"]) = _mm_norm_bwd(
        [(dx, W["w_out"])], [(sv["oa"], P["attn_out_norm"], None), (sv["ob"], P["pool_out_norm"], None)], nt=True,
        name=t + "_dmg")
    du, GP["pool_w"], GP["pool_scale"] = _pool_bwd(sv["u"], dob, P["pool_w"], P["pool_scale"], t + "_pool")
    dq, dko, dkp, dvo, dvp, dsk = _swa_bwd(sv["q"], sv["k"], sv["v"], doa, P["attn_sinks"], t + "_swa")
    GP["attn_sinks"] = dsk[:, 0]
    dpj = _dproj(dq, dko, dkp, dvo, dvp, du, cos2, sin2, t + "_dproj")
    GW["w_in"] = _mm_tn(dpj, sv["h2"], t + "_dwin")
    (dx,), (GP["mix_norm"],), dy = _mm_norm_bwd([(dpj, W["w_in"])], [(sv["x1"], P["mix_norm"], dx)], nt=False,
                                               name=t + "_dh2", half=True)

    dx, dy, GP["ffn1_norm"], _, _, _ = _ffn_bwd(
        t + "_ffn1", dx, dy, sv["x0"], P["ffn1_norm"], sv["h1"], sv["a1"], sv["b1"], sv["s1"],
        W["ffn1_w_gate"], W["ffn1_w_up"], W["ffn1_w_down"], behind=begin_rest(GW), begin=begin_ffn1, half=l > 0)
    return dx, dy, GP, marks


def _reduce_begin(t, grads, sizes, place, seq_ids):
    gots = _rs_pair_exchange(grads, sizes, t + "_pair")
    sums = _pair_sum(grads, gots, sizes, place, t + "_pair_sum")
    recvs = _rs_chip_exchange(sums, sizes, t + "_chips", seq_id=next(seq_ids))
    return dict(t=t, grads=grads, gots=gots, sums=sums, recvs=recvs, sizes=sizes)


def _reduce_end(st, place, seq_ids, late=None):
    recvs = st["recvs"] if late is None else _after(st["recvs"], late)
    reds = _final_sum(st["grads"], st["gots"], recvs, st["sizes"], place, st["t"] + "_final_sum")
    return _rs_share_halves(reds, st["sizes"], st["t"] + "_share")


def _adamw_layer(l, w3, g, m3, v3, transposed, prev, name):
    _, K, n = w3.shape
    if transposed:
        tr = K // SHARD_STEPS
        g_spec = pl.BlockSpec((n, tr), lambda i: (0, i))
    else:
        tr = _rows_tile(K)
        g_spec = pl.BlockSpec((tr, n), lambda i: (i, 0))
    c1 = 1.0 / (1.0 - ADAM_B1 ** ADAM_STEP)
    c2 = 1.0 / (1.0 - ADAM_B2 ** ADAM_STEP)

    def body(w_ref, g_ref, m_ref, v_ref, *rest):
        go_ref, d_ref, nm_ref, nv_ref = rest[-4:]
        gv = g_ref[...].T if transposed else g_ref[...]
        nm = ADAM_B1 * m_ref[...] + (1.0 - ADAM_B1) * gv
        nv = ADAM_B2 * v_ref[...] + (1.0 - ADAM_B2) * (gv * gv)
        go_ref[...] = gv
        d_ref[...] = -ADAM_LR * ((nm * c1) / (jnp.sqrt(nv * c2) + ADAM_EPS) + ADAM_WD * w_ref[...])
        nm_ref[...] = nm
        nv_ref[...] = nv

    slab = pl.BlockSpec((None, tr, n), lambda i: (l, i, 0))
    in_specs, args, aliases = [slab, g_spec, slab, slab], [w3, g, m3, v3], {}
    if prev is not None:
        in_specs += [ANY] * 4
        args += list(prev)
        aliases = {4 + j: j for j in range(4)}
    return _pcall(body, name=name, grid=(K // tr,), in_specs=in_specs, out_specs=[slab] * 4,
                  out_shape=[_S(w3.shape, F32)] * 4, input_output_aliases=aliases,
                  compiler_params=_cp("parallel"))(*args)


def kernel(x, mem, positions, ffn1_norm, ffn1_w_gate, ffn1_w_up, ffn1_w_down, mix_norm, w_in, attn_sinks, pool_w, pool_scale, attn_out_norm, pool_out_norm, w_out, xattn_norm, mem_norm, xattn_wq, xattn_wkv, xattn_wo, ffn2_norm, ffn2_w_gate, ffn2_w_up, ffn2_w_down, final_norm, loss_target, m_ffn1_norm, m_ffn1_w_gate, m_ffn1_w_up, m_ffn1_w_down, m_mix_norm, m_w_in, m_attn_sinks, m_pool_w, m_pool_scale, m_attn_out_norm, m_pool_out_norm, m_w_out, m_xattn_norm, m_mem_norm, m_xattn_wq, m_xattn_wkv, m_xattn_wo, m_ffn2_norm, m_ffn2_w_gate, m_ffn2_w_up, m_ffn2_w_down, m_final_norm, v_ffn1_norm, v_ffn1_w_gate, v_ffn1_w_up, v_ffn1_w_down, v_mix_norm, v_w_in, v_attn_sinks, v_pool_w, v_pool_scale, v_attn_out_norm, v_pool_out_norm, v_w_out, v_xattn_norm, v_mem_norm, v_xattn_wq, v_xattn_wkv, v_xattn_wo, v_ffn2_norm, v_ffn2_w_gate, v_ffn2_w_up, v_ffn2_w_down, v_final_norm):
    ws = dict(ffn1_norm=ffn1_norm, ffn1_w_gate=ffn1_w_gate, ffn1_w_up=ffn1_w_up, ffn1_w_down=ffn1_w_down,
              mix_norm=mix_norm, w_in=w_in, attn_sinks=attn_sinks, pool_w=pool_w, pool_scale=pool_scale,
              attn_out_norm=attn_out_norm, pool_out_norm=pool_out_norm, w_out=w_out, xattn_norm=xattn_norm,
              mem_norm=mem_norm, xattn_wq=xattn_wq, xattn_wkv=xattn_wkv, xattn_wo=xattn_wo, ffn2_norm=ffn2_norm,
              ffn2_w_gate=ffn2_w_gate, ffn2_w_up=ffn2_w_up, ffn2_w_down=ffn2_w_down, final_norm=final_norm)
    ms = dict(ffn1_norm=m_ffn1_norm, ffn1_w_gate=m_ffn1_w_gate, ffn1_w_up=m_ffn1_w_up, ffn1_w_down=m_ffn1_w_down,
              mix_norm=m_mix_norm, w_in=m_w_in, attn_sinks=m_attn_sinks, pool_w=m_pool_w, pool_scale=m_pool_scale,
              attn_out_norm=m_attn_out_norm, pool_out_norm=m_pool_out_norm, w_out=m_w_out, xattn_norm=m_xattn_norm,
              mem_norm=m_mem_norm, xattn_wq=m_xattn_wq, xattn_wkv=m_xattn_wkv, xattn_wo=m_xattn_wo,
              ffn2_norm=m_ffn2_norm, ffn2_w_gate=m_ffn2_w_gate, ffn2_w_up=m_ffn2_w_up, ffn2_w_down=m_ffn2_w_down,
              final_norm=m_final_norm)
    vs = dict(ffn1_norm=v_ffn1_norm, ffn1_w_gate=v_ffn1_w_gate, ffn1_w_up=v_ffn1_w_up, ffn1_w_down=v_ffn1_w_down,
              mix_norm=v_mix_norm, w_in=v_w_in, attn_sinks=v_attn_sinks, pool_w=v_pool_w, pool_scale=v_pool_scale,
              attn_out_norm=v_attn_out_norm, pool_out_norm=v_pool_out_norm, w_out=v_w_out, xattn_norm=v_xattn_norm,
              mem_norm=v_mem_norm, xattn_wq=v_xattn_wq, xattn_wkv=v_xattn_wkv, xattn_wo=v_xattn_wo,
              ffn2_norm=v_ffn2_norm, ffn2_w_gate=v_ffn2_w_gate, ffn2_w_up=v_ffn2_w_up, ffn2_w_down=v_ffn2_w_down,
              final_norm=v_final_norm)
    depth = ffn1_norm.shape[0]
    T, D = x.shape[1], x.shape[2]
    xv = x.reshape(T, D)
    memv = mem.reshape(mem.shape[1], D)
    tgt = loss_target.reshape(T, D)
    cos2, sin2 = _rope_tables(positions.reshape(T))
    in_kernel = {name: tr and ws[name].shape[2] % LANES == 0 for name, tr in BIG}
    swapped = [name for name, tr in BIG if tr and not in_kernel[name]]
    rows = lambda d: {name: (jnp.swapaxes(d[name], 1, 2) if name in swapped else d[name]) for name, _ in BIG}
    wr, mr, vr = rows(ws), rows(ms), rows(vs)
    groups = [[(name, in_kernel[name]) for name, _ in g] for g in GROUPS]

    q_me = 2 * lax.axis_index("x") + lax.axis_index("y")
    place = jnp.stack([lax.axis_index("c"), q_me]).astype(jnp.int32)
    seq_ids = iter(range(1, 1 + 8 * depth))
    Ws, sizes, first = [dict() for _ in range(depth)], {}, None
    for l in range(depth):
        for gi, group in enumerate(groups):
            t = f"l{l}g{gi}"
            bufs, sizes[gi] = _own_shard_buffers(wr, l, group, place, t + "_shard")
            if first is None:
                full = first = _allgather_weights(bufs, sizes[gi], t + "_allgather")
            else:
                full = _allgather_weights(_after(bufs, first), sizes[gi], t + "_allgather", seq_id=next(seq_ids))
            Ws[l].update({name: f for (name, _), f in zip(group, full)})
    Ps = [{n: (ws[n][l].reshape(1, -1) if n != "pool_w" else ws[n][l]) for n in SMALL if n != "final_norm"}
          for l in range(depth)]

    saved = []
    h, hn = xv, _rms_fwd(xv, Ps[0]["ffn1_norm"], "l0_ffn1_norm")
    for l in range(depth):
        next_norm = Ps[l + 1]["ffn1_norm"] if l + 1 < depth else None
        h, hn, sv = _layer_fwd(l, h, hn, memv, Ws[l], Ps[l], cos2, sin2, next_norm)
        saved.append(sv)
    loss_row, dx, dy, d_final = _loss_head(h, tgt, final_norm.reshape(1, D), "loss_head")
    GPs, marks, begun = [None] * depth, [None] * depth, {}
    for l in reversed(range(depth)):
        def begin_rest(GW, l=l):
            begun[l, 1] = _reduce_begin(f"l{l}g1r", [GW[name] for name, _ in groups[1]], sizes[1], place, seq_ids)
            return begun[l, 1]["sums"]

        def begin_ffn1(*gs, l=l):
            begun[l, 0] = _reduce_begin(f"l{l}g0r", list(gs), sizes[0], place, seq_ids)
            return begun[l, 0]["sums"]

        dx, dy, GPs[l], marks[l] = _layer_bwd(l, dx, dy, saved[l], memv, Ws[l], Ps[l], cos2, sin2, begin_rest,
                                              begin_ffn1)

    stacked = {}
    for l in reversed(range(depth)):
        for gi, group in reversed(list(enumerate(groups))):
            if l > 0:
                late = marks[l - 1]["ffn2" if gi == 1 else "xattn"]
            else:
                late = dx if gi == 1 else None
            reds = _reduce_end(begun[l, gi], place, seq_ids, late)
            for (name, tr), red in zip(group, reds):
                stacked[name] = _adamw_layer(l, wr[name], red, mr[name], vr[name], tr, stacked.get(name),
                                             f"l{l}_adamw_{name}")
    for name in swapped:
        stacked[name] = [jnp.swapaxes(a, 1, 2) for a in stacked[name]]
    small_part = {n: jnp.stack([GPs[l][n].reshape(ws[n].shape[1:]) for l in range(depth)]) for n in SMALL if n != "final_norm"}
    small_part["final_norm"] = d_final.reshape(D)
    small_g = _unpack_small(_allreduce_small(_pack_small(small_part), "small_allreduce"), ws)
    loss = lax.psum(loss_row[0, 0], ("x", "y", "c"))

    grads, deltas, new_m, new_v = {}, {}, {}, {}
    for name, _ in BIG:
        grads[name], deltas[name], new_m[name], new_v[name] = stacked[name]
    d, nm, nv = _adamw(_pack_small(ws), _pack_small(small_g), _pack_small(ms), _pack_small(vs), "adamw_small")
    grads.update(small_g)
    deltas.update(_unpack_small(d, ws))
    new_m.update(_unpack_small(nm, ws))
    new_v.update(_unpack_small(nv, ws))

    grad_x = dx.reshape(x.shape)
    return (loss, grad_x, *[grads[n] for n in WEIGHTS], *[deltas[n] for n in WEIGHTS],
            *[new_m[n] for n in WEIGHTS], *[new_v[n] for n in WEIGHTS])
```
